```python
import math
import jax, jax.numpy as jnp
from jax import lax
import numpy as np

D_MODEL = 1024
BATCH = 16
SEQ = 2048
DEPTH = 2
DEC_BATCH = 32
DEC_SEQ = 8
PAST_LEN = 16384
PAGE_SIZE = 128

NORM_EPS = 1e-6
N_MOD = 9
D_FF = 2816
A_GROUPS = ((128, 1), (512, 4), (2048, 16))
A_N_GROUPS = 3
A_HEADS = 8
A_HEAD_DIM = 64
A_WIDTH = A_HEADS * A_HEAD_DIM
A_KEYS = 128
A_BLOCK = 128
REL_BUCKETS = 32
REL_MAX_EXACT = 16
REL_MAX_DISTANCE = 2048
B_WIDTH = 1024
B_HEAD_DIM = 64
B_HEADS = B_WIDTH // B_HEAD_DIM
B_GROUPS = 2
B_STATE = 128
B_CONV = 4
B_CONV_CH = B_WIDTH + 2 * B_GROUPS * B_STATE
B_CHUNK = 128
C_WIDTH = 1024
C_BLOCKS = 8
C_BLOCK_DIM = C_WIDTH // C_BLOCKS
C_CONV = 4
C_POW = 8.0
IN_SPLIT = (A_N_GROUPS * A_WIDTH, A_N_GROUPS * A_WIDTH, A_N_GROUPS * A_WIDTH,
            B_WIDTH, B_CONV_CH, B_HEADS, C_WIDTH, C_WIDTH, 3 * D_MODEL)
N_IN = 3 * A_N_GROUPS * A_WIDTH + B_WIDTH + B_CONV_CH + B_HEADS + 2 * C_WIDTH + 3 * D_MODEL

kernel_name = "hybrid_gated_dilated_ssd_lru_decoder_step"


def rmsnorm(x, g):
    x32 = x.astype(jnp.float32)
    y = x32 * lax.rsqrt(jnp.mean(x32 * x32, axis=-1, keepdims=True) + NORM_EPS)
    return (y * g.astype(jnp.float32)).astype(x.dtype)


def swiglu(h, w_in, w_out):
    u, v = jnp.split(h @ w_in, 2, axis=-1)
    return (jax.nn.silu(u) * v) @ w_out


def t5_bucket(dist):
    dist = np.asarray(dist)
    large = REL_MAX_EXACT + (np.log(np.maximum(dist, 1) / REL_MAX_EXACT)
                             / math.log(REL_MAX_DISTANCE / REL_MAX_EXACT)
                             * (REL_BUCKETS - REL_MAX_EXACT)).astype(np.int64)
    large = np.minimum(large, REL_BUCKETS - 1)
    return np.where(dist < REL_MAX_EXACT, dist, large).astype(np.int32)


def group_bias(rel_bias, g):
    dil = A_GROUPS[g][1]
    buckets = t5_bucket(np.arange(A_KEYS + 1) * dil)
    return rel_bias[buckets][:, g * A_HEADS:(g + 1) * A_HEADS].T.astype(jnp.float32)


def dilated_window_prompt(q, k, v, bias, dil):
    b, s, h, dh = q.shape
    m = s // dil
    mp = -(-m // A_BLOCK) * A_BLOCK
    nb = mp // A_BLOCK

    def strided(t):
        t = t.reshape(b, m, dil, h, dh).transpose(0, 2, 1, 3, 4).reshape(b * dil, m, h, dh)
        return jnp.pad(t, ((0, 0), (0, mp - m), (0, 0), (0, 0)))

    def band(t):
        tb = t.reshape(b * dil, nb, A_BLOCK, h, dh)
        prev = jnp.pad(tb, ((0, 0), (1, 0), (0, 0), (0, 0), (0, 0)))[:, :-1]
        return jnp.concatenate([prev, tb], axis=2)

    qb = strided(q).reshape(b * dil, nb, A_BLOCK, h, dh)
    kb, vb = band(strided(k)), band(strided(v))
    scores = jnp.einsum('bnqhd,bnkhd->bnhqk', qb, kb,
                        preferred_element_type=jnp.float32) * (A_HEAD_DIM ** -0.5)
    qi = np.arange(A_BLOCK)[:, None]
    kj = np.arange(2 * A_BLOCK)[None, :]
    dist = A_BLOCK + qi - kj
    valid = (dist >= 0) & (dist <= A_KEYS)
    first = valid & (kj >= A_BLOCK)
    valid_nb = np.concatenate([first[None], np.broadcast_to(valid, (nb - 1,) + valid.shape)], 0)
    bias_full = bias[:, np.clip(dist, 0, A_KEYS)]
    logits = jnp.where(valid_nb[None, :, None], scores + bias_full[None, None], -jnp.inf)
    lse = jax.nn.logsumexp(logits, axis=-1)
    p = jnp.exp(logits - lse[..., None])
    o = jnp.einsum('bnhqk,bnkhd->bnqhd', p.astype(vb.dtype), vb)
    o = o.reshape(b, dil, mp, h, dh)[:, :, :m].transpose(0, 2, 1, 3, 4).reshape(b, s, h, dh)
    lse = lse.transpose(0, 1, 3, 2).reshape(b, dil, mp, h)[:, :, :m]
    lse = lse.transpose(0, 2, 1, 3).reshape(b, s, h)
    return o, lse


def dilated_window_sample(q, k_new, v_new, kv_buf, bias, dil):
    b, t, h, dh = q.shape
    wb = kv_buf.shape[1]
    keys = jnp.concatenate([kv_buf[:, :, 0].astype(k_new.dtype), k_new], axis=1)
    vals = jnp.concatenate([kv_buf[:, :, 1].astype(v_new.dtype), v_new], axis=1)
    idx = wb + np.arange(t)[:, None] - dil * np.arange(A_KEYS + 1)[None, :]
    valid = idx >= 0
    idx = np.maximum(idx, 0)
    kg, vg = keys[:, idx], vals[:, idx]
    scores = jnp.einsum('bthd,btkhd->bhtk', q, kg,
                        preferred_element_type=jnp.float32) * (A_HEAD_DIM ** -0.5)
    logits = jnp.where(valid[None, None], scores + bias[:, None, :], -jnp.inf)
    lse = jax.nn.logsumexp(logits, axis=-1)
    p = jnp.exp(logits - lse[..., None])
    o = jnp.einsum('bhtk,btkhd->bthd', p.astype(vg.dtype), vg)
    return o, lse.transpose(0, 2, 1)


def causal_conv(u, buf, w, bias):
    kw = w.shape[0]
    full = jnp.concatenate([buf.astype(u.dtype), u], axis=1)
    out = lax.conv_general_dilated(full, w[:, None, :].astype(u.dtype), window_strides=(1,),
                                   padding='VALID', dimension_numbers=('NWC', 'WIO', 'NWC'),
                                   feature_group_count=u.shape[-1])
    return out + bias, full[:, full.shape[1] - (kw - 1):]


def ssd_scan(x, dt, a, bm, cm, h0):
    f32 = jnp.float32
    b, L, nh, p = x.shape
    g, n = bm.shape[2], bm.shape[3]
    r = nh // g
    q = B_CHUNK if L % B_CHUNK == 0 else L
    nc = L // q
    xr = x.astype(f32).reshape(b, nc, q, g, r, p)
    dtr = dt.reshape(b, nc, q, g, r)
    br = bm.astype(f32).reshape(b, nc, q, g, n)
    cr = cm.astype(f32).reshape(b, nc, q, g, n)
    acum = jnp.cumsum(dtr * a.reshape(g, r), axis=2)
    xdt = xr * dtr[..., None]
    causal = np.tril(np.ones((q, q), bool))[:, :, None, None]
    seg = acum[:, :, :, None] - acum[:, :, None, :]
    decay_ls = jnp.exp(jnp.where(causal, seg, -jnp.inf))
    cb = jnp.einsum('bclgn,bcsgn->bclsg', cr, br)
    y_diag = jnp.einsum('bclsgr,bcsgrp->bclgrp', cb[..., None] * decay_ls, xdt)
    to_end = jnp.exp(acum[:, :, -1:] - acum)
    chunk_states = jnp.einsum('bclgn,bclgrp->bcgrpn', br, xdt * to_end[..., None])
    chunk_decay = jnp.exp(acum[:, :, -1])

    def step(hc, inp):
        dec, st = inp
        return dec[..., None, None] * hc + st, hc

    h_last, h_in = lax.scan(step, h0.astype(f32).reshape(b, g, r, p, n),
                            (jnp.moveaxis(chunk_decay, 1, 0), jnp.moveaxis(chunk_states, 1, 0)))
    h_in = jnp.moveaxis(h_in, 0, 1)
    y_off = jnp.einsum('bclgn,bcgrpn->bclgrp', cr, h_in) * jnp.exp(acum)[..., None]
    y = (y_diag + y_off).reshape(b, L, nh, p)
    return y.astype(x.dtype), h_last.reshape(b, nh, p, n)


def rg_lru(xc, h0, w_r, b_r, w_i, b_i, lam):
    f32 = jnp.float32
    b, L, _ = xc.shape
    x32 = xc.astype(f32)
    xb = x32.reshape(b, L, C_BLOCKS, C_BLOCK_DIM)
    rg = jax.nn.sigmoid(jnp.einsum('blhi,hij->blhj', xb, w_r.astype(f32)).reshape(b, L, C_WIDTH) + b_r)
    ig = jax.nn.sigmoid(jnp.einsum('blhi,hij->blhj', xb, w_i.astype(f32)).reshape(b, L, C_WIDTH) + b_i)
    log_a = -C_POW * rg * jax.nn.softplus(-lam.astype(f32))
    a = jnp.exp(log_a)
    u = x32 * ig * jnp.sqrt(-jnp.expm1(2.0 * log_a))
    u = u.at[:, 0].add(a[:, 0] * h0.astype(f32))

    def comb(e1, e2):
        a1, b1 = e1
        a2, b2 = e2
        return a1 * a2, a2 * b1 + b2

    _, hs = lax.associative_scan(comb, (a, u), axis=1)
    return hs.astype(xc.dtype), hs[:, -1]


def mixer(h, lw, st, rel_bias, prompt):
    b, L, _ = h.shape
    f32 = jnp.float32
    offs = [int(o) for o in np.cumsum(IN_SPLIT)[:-1]]
    qa, ka, va, zb, xbc, dtb, xc, gc, gates = jnp.split(h @ lw['w_in'], offs, axis=-1)
    if prompt:
        conv_b0 = jnp.zeros((b, B_CONV - 1, B_CONV_CH), h.dtype)
        ssm0 = jnp.zeros((b, B_HEADS, B_HEAD_DIM, B_STATE), f32)
        conv_c0 = jnp.zeros((b, C_CONV - 1, C_WIDTH), h.dtype)
        lru0 = jnp.zeros((b, C_WIDTH), f32)
    else:
        conv_b0, ssm0, conv_c0, lru0 = st['conv_b'], st['ssm'], st['conv_c'], st['lru']
    shp = (b, L, A_N_GROUPS, A_HEADS, A_HEAD_DIM)
    qa, ka, va = qa.reshape(shp), ka.reshape(shp), va.reshape(shp)
    outs, lses, new_kv = [], [], []
    for g, (win, dil) in enumerate(A_GROUPS):
        bias = group_bias(rel_bias, g)
        if prompt:
            o, lse = dilated_window_prompt(qa[:, :, g], ka[:, :, g], va[:, :, g], bias, dil)
            keep = min(win, L)
            new_kv.append(jnp.stack([ka[:, L - keep:, g], va[:, L - keep:, g]], axis=2))
        else:
            o, lse = dilated_window_sample(qa[:, :, g], ka[:, :, g], va[:, :, g], st['kv'][g], bias, dil)
            new_kv.append(jnp.stack([ka[:, :, g], va[:, :, g]], axis=2))
        outs.append(o)
        lses.append(lse)
    wgt = jax.nn.softmax(jnp.stack(lses, 0), axis=0)
    oa = jnp.einsum('gblh,gblhd->blhd', wgt, jnp.stack(outs, 0).astype(f32))
    ya = oa.reshape(b, L, A_WIDTH).astype(h.dtype) @ lw['w_a_proj']
    xbc, conv_b_new = causal_conv(xbc, conv_b0, lw['conv_b_w'], lw['conv_b_b'])
    xbc = jax.nn.silu(xbc)
    xs, bm, cm = jnp.split(xbc, [B_WIDTH, B_WIDTH + B_GROUPS * B_STATE], axis=-1)
    xs = xs.reshape(b, L, B_HEADS, B_HEAD_DIM)
    dt = jax.nn.softplus(dtb.astype(f32) + lw['dt_bias'].astype(f32))
    a_neg = -jnp.exp(lw['a_log'].astype(f32))
    y, ssm_new = ssd_scan(xs, dt, a_neg, bm.reshape(b, L, B_GROUPS, B_STATE),
                          cm.reshape(b, L, B_GROUPS, B_STATE), ssm0)
    y = y + lw['d_skip'][:, None].astype(y.dtype) * xs
    y = y.reshape(b, L, B_WIDTH) * jax.nn.silu(zb)
    y = rmsnorm(y.reshape(b, L, B_GROUPS, B_WIDTH // B_GROUPS),
                lw['g_ssm_norm'].reshape(B_GROUPS, B_WIDTH // B_GROUPS)).reshape(b, L, B_WIDTH)
    yb = y @ lw['w_b_proj']
    xc, conv_c_new = causal_conv(xc, conv_c0, lw['conv_c_w'], lw['conv_c_b'])
    hc, lru_new = rg_lru(xc, lru0, lw['w_rgate'], lw['b_rgate'], lw['w_igate'], lw['b_igate'],
                         lw['lru_lambda'])
    yc = (hc * jax.nn.gelu(gc)) @ lw['w_c_proj']
    ga, gb, gcc = jnp.split(jax.nn.sigmoid(gates), 3, axis=-1)
    out = (ga * ya + gb * yb + gcc * yc) @ lw['w_out']
    return out, (new_kv[0], new_kv[1], new_kv[2], conv_b_new, ssm_new, conv_c_new, lru_new)


def block(x, c, lw, st, rel_bias, prompt):
    mod = jax.nn.silu(c) @ lw['w_ada'] + lw['b_ada']
    sh1, sc1, g1, sh2, sc2, g2, sh3, sc3, g3 = jnp.split(mod[:, None, :], N_MOD, axis=-1)
    h = rmsnorm(x, lw['g_ff1']) * (1 + sc1) + sh1
    x = x + 0.5 * g1 * swiglu(h, lw['w_ff1_in'], lw['w_ff1_out'])
    h = rmsnorm(x, lw['g_mix']) * (1 + sc2) + sh2
    m, new_st = mixer(h, lw, st, rel_bias, prompt)
    x = x + g2 * m
    h = rmsnorm(x, lw['g_ff2']) * (1 + sc3) + sh3
    x = x + 0.5 * g3 * swiglu(h, lw['w_ff2_in'], lw['w_ff2_out'])
    return x, new_st


def setup_inputs(seed: int = 0) -> dict:
    key = jax.random.key(seed)
    ks = iter(jax.random.split(key, 64))
    f32 = jnp.float32

    def nrm(shape, scale):
        return jax.random.normal(next(ks), shape, f32) * scale

    wb = [min(w, PAST_LEN) for w, _ in A_GROUPS]
    dt0 = jnp.exp(jax.random.uniform(next(ks), (DEPTH, B_HEADS), f32, math.log(1e-3), math.log(1e-1)))
    a_init = jax.random.uniform(next(ks), (DEPTH, B_HEADS), f32, 1.0, 16.0)
    a0 = jax.random.uniform(next(ks), (DEPTH, C_WIDTH), f32, 0.9, 0.999)
    sig = a0 ** (1.0 / C_POW)
    D = D_MODEL
    return {
        "x_prompt": nrm((BATCH, SEQ, D), 1.0),
        "x_sample": nrm((DEC_BATCH, DEC_SEQ, D), 1.0),
        "c_prompt": nrm((BATCH, D), 1.0),
        "c_sample": nrm((DEC_BATCH, D), 1.0),
        "cache_win1_kv": nrm((DEPTH, DEC_BATCH, wb[0], 2, A_HEADS, A_HEAD_DIM), 1.0),
        "cache_win2_kv": nrm((DEPTH, DEC_BATCH, wb[1], 2, A_HEADS, A_HEAD_DIM), 1.0),
        "cache_win3_kv": nrm((DEPTH, DEC_BATCH, wb[2], 2, A_HEADS, A_HEAD_DIM), 1.0),
        "state_conv_b": nrm((DEPTH, DEC_BATCH, B_CONV - 1, B_CONV_CH), 1.0),
        "state_ssm": nrm((DEPTH, DEC_BATCH, B_HEADS, B_HEAD_DIM, B_STATE), 0.1),
        "state_conv_c": nrm((DEPTH, DEC_BATCH, C_CONV - 1, C_WIDTH), 1.0),
        "state_lru": nrm((DEPTH, DEC_BATCH, C_WIDTH), 0.5),
        "rel_bias": nrm((REL_BUCKETS, A_N_GROUPS * A_HEADS), 0.2),
        "w_ada": nrm((DEPTH, D, N_MOD * D), 0.5 * D ** -0.5),
        "b_ada": nrm((DEPTH, N_MOD * D), 0.02),
        "g_ff1": 1.0 + nrm((DEPTH, D), 0.05),
        "w_ff1_in": nrm((DEPTH, D, 2 * D_FF), D ** -0.5),
        "w_ff1_out": nrm((DEPTH, D_FF, D), D_FF ** -0.5),
        "g_mix": 1.0 + nrm((DEPTH, D), 0.05),
        "w_in": nrm((DEPTH, D, N_IN), D ** -0.5),
        "w_a_proj": nrm((DEPTH, A_WIDTH, D), A_WIDTH ** -0.5),
        "conv_b_w": nrm((DEPTH, B_CONV, B_CONV_CH), B_CONV ** -0.5),
        "conv_b_b": nrm((DEPTH, B_CONV_CH), 0.02),
        "dt_bias": dt0 + jnp.log(-jnp.expm1(-dt0)),
        "a_log": jnp.log(a_init),
        "d_skip": 1.0 + nrm((DEPTH, B_HEADS), 0.05),
        "g_ssm_norm": 1.0 + nrm((DEPTH, B_WIDTH), 0.05),
        "w_b_proj": nrm((DEPTH, B_WIDTH, D), B_WIDTH ** -0.5),
        "conv_c_w": nrm((DEPTH, C_CONV, C_WIDTH), C_CONV ** -0.5),
        "conv_c_b": nrm((DEPTH, C_WIDTH), 0.02),
        "w_rgate": nrm((DEPTH, C_BLOCKS, C_BLOCK_DIM, C_BLOCK_DIM), C_BLOCK_DIM ** -0.5),
        "b_rgate": nrm((DEPTH, C_WIDTH), 0.02),
        "w_igate": nrm((DEPTH, C_BLOCKS, C_BLOCK_DIM, C_BLOCK_DIM), C_BLOCK_DIM ** -0.5),
        "b_igate": nrm((DEPTH, C_WIDTH), 0.02),
        "lru_lambda": jnp.log(sig) - jnp.log1p(-sig),
        "w_c_proj": nrm((DEPTH, C_WIDTH, D), C_WIDTH ** -0.5),
        "w_out": nrm((DEPTH, D, D), D ** -0.5),
        "g_ff2": 1.0 + nrm((DEPTH, D), 0.05),
        "w_ff2_in": nrm((DEPTH, D, 2 * D_FF), D ** -0.5),
        "w_ff2_out": nrm((DEPTH, D_FF, D), D_FF ** -0.5),
        "g_final": 1.0 + nrm((D,), 0.05),
    }


def reference(x_prompt, x_sample, c_prompt, c_sample, cache_win1_kv, cache_win2_kv, cache_win3_kv,
              state_conv_b, state_ssm, state_conv_c, state_lru, rel_bias, w_ada, b_ada, g_ff1,
              w_ff1_in, w_ff1_out, g_mix, w_in, w_a_proj, conv_b_w, conv_b_b, dt_bias, a_log, d_skip,
              g_ssm_norm, w_b_proj, conv_c_w, conv_c_b, w_rgate, b_rgate, w_igate, b_igate, lru_lambda,
              w_c_proj, w_out, g_ff2, w_ff2_in, w_ff2_out, g_final):
    yp, ys = x_prompt, x_sample
    new_p = [[] for _ in range(7)]
    new_s = [[] for _ in range(7)]
    for l in range(DEPTH):
        lw = dict(w_ada=w_ada[l], b_ada=b_ada[l], g_ff1=g_ff1[l], w_ff1_in=w_ff1_in[l],
                  w_ff1_out=w_ff1_out[l], g_mix=g_mix[l], w_in=w_in[l], w_a_proj=w_a_proj[l],
                  conv_b_w=conv_b_w[l], conv_b_b=conv_b_b[l], dt_bias=dt_bias[l], a_log=a_log[l],
                  d_skip=d_skip[l], g_ssm_norm=g_ssm_norm[l], w_b_proj=w_b_proj[l],
                  conv_c_w=conv_c_w[l], conv_c_b=conv_c_b[l], w_rgate=w_rgate[l], b_rgate=b_rgate[l],
                  w_igate=w_igate[l], b_igate=b_igate[l], lru_lambda=lru_lambda[l],
                  w_c_proj=w_c_proj[l], w_out=w_out[l], g_ff2=g_ff2[l], w_ff2_in=w_ff2_in[l],
                  w_ff2_out=w_ff2_out[l])
        st = dict(kv=(cache_win1_kv[l], cache_win2_kv[l], cache_win3_kv[l]), conv_b=state_conv_b[l],
                  ssm=state_ssm[l], conv_c=state_conv_c[l], lru=state_lru[l])
        yp, stp = block(yp, c_prompt, lw, None, rel_bias, True)
        ys, sts = block(ys, c_sample, lw, st, rel_bias, False)
        for i in range(7):
            new_p[i].append(stp[i])
            new_s[i].append(sts[i])
    yp = rmsnorm(yp, g_final)
    ys = rmsnorm(ys, g_final)
    p_kv1, p_kv2, p_kv3, p_conv_b, p_ssm, p_conv_c, p_lru = [jnp.stack(v, 0) for v in new_p]
    s_kv1, s_kv2, s_kv3, s_conv_b, s_ssm, s_conv_c, s_lru = [jnp.stack(v, 0) for v in new_s]
    return (yp, ys, p_kv1, p_kv2, p_kv3, p_conv_b, p_ssm, p_conv_c, p_lru,
            s_kv1, s_kv2, s_kv3, s_conv_b, s_ssm, s_conv_c, s_lru)
```

```python
import functools
import math

import jax
import jax.numpy as jnp
import numpy as np
from jax import lax
from jax.experimental import pallas as pl
from jax.experimental.pallas import tpu as pltpu

F32 = jnp.float32
BF16 = jnp.bfloat16

NORM_EPS = 1e-6
N_MOD = 9
A_GROUPS = ((128, 1), (512, 4), (2048, 16))
A_HEADS = 8
A_HEAD_DIM = 64
A_WIDTH = A_HEADS * A_HEAD_DIM
A_KEYS = 128
A_BLOCK = 128
REL_BUCKETS = 32
REL_MAX_EXACT = 16
REL_MAX_DISTANCE = 2048
B_HEAD_DIM = 64
B_GROUPS = 2
B_STATE = 128
B_CONV = 4
C_BLOCKS = 8
C_CONV = 4
C_POW = 8.0

LANES = 128
SUBLANES = 8
SSD_CHUNK = 128
VMEM_LIMIT = 56 * 1024 * 1024


def _cparams(*sem):
    return pltpu.CompilerParams(dimension_semantics=sem, vmem_limit_bytes=VMEM_LIMIT)


def _resident(shape):
    nd = len(shape)
    return pl.BlockSpec(shape, lambda *_: (0,) * nd, pipeline_mode=pl.Buffered(1))


def _ld(ref):
    return ref[0] if len(ref.shape) == 3 else ref[...]


def _norm_mod(x, g, sc, sh):
    ms = jnp.mean(x * x, axis=-1, keepdims=True)
    return (x * lax.rsqrt(ms + NORM_EPS) * g) * (1.0 + sc) + sh


def _softplus(x):
    return jnp.maximum(x, 0.0) + jnp.log1p(jnp.exp(-jnp.abs(x)))


def _split_bf16(v, n):
    parts = []
    r = v
    for _ in range(n):
        p = r.astype(BF16)
        parts.append(p)
        r = r - p.astype(F32)
    return parts


def _dot(a, b):
    return jnp.dot(a, b, preferred_element_type=F32)


def _dot_nt(a, b):
    return lax.dot_general(a, b, (((1,), (1,)), ((), ())), preferred_element_type=F32)


def _ada_kernel(c_ref, w_ref, b_ref, o_ref):
    c = c_ref[...]
    a = (c * jax.nn.sigmoid(c)).astype(BF16)
    o_ref[...] = _dot(a, w_ref[...].astype(BF16)) + b_ref[...]


def _ada(c, w, b):
    rows, d = c.shape
    n = w.shape[1]
    tn = 1024
    return pl.pallas_call(
        _ada_kernel,
        grid=(n // tn,),
        in_specs=[pl.BlockSpec((rows, d), lambda j: (0, 0)),
                  pl.BlockSpec((d, tn), lambda j: (0, j)),
                  pl.BlockSpec((1, tn), lambda j: (0, j))],
        out_specs=pl.BlockSpec((rows, tn), lambda j: (0, j)),
        out_shape=jax.ShapeDtypeStruct((rows, n), F32),
        compiler_params=_cparams("parallel"),
        name="ada",
    )(c, w, b.reshape(1, n))


class _Mod:
    def __init__(self, mod, batch, seq, per_row):
        d = mod.shape[1] // N_MOD
        self.d = d
        self.seq = seq
        self.per_row = per_row
        if per_row:
            self.arr = jnp.repeat(mod, seq, axis=0)
        else:
            self.arr = mod.reshape(batch * N_MOD, 1, d)

    def spec(self, k, tm, grid_rank):
        d = self.d
        if self.per_row:
            if grid_rank == 1:
                return pl.BlockSpec((tm, d), lambda i: (i, k))
            return pl.BlockSpec((tm, d), lambda i, j: (i, k))
        per = self.seq // tm
        if grid_rank == 1:
            return pl.BlockSpec((1, 1, d), lambda i: ((i // per) * N_MOD + k, 0, 0))
        return pl.BlockSpec((1, 1, d), lambda i, j: ((i // per) * N_MOD + k, 0, 0))


def _row_tile(m, seq, per_row, want):
    tm = min(want, m if per_row else seq)
    assert m % tm == 0 and (per_row or seq % tm == 0)
    return tm


def _ffn_kernel(x_ref, sh_ref, sc_ref, gt_ref, gn_ref, win_ref, wout_ref, *rest, d_ff, fc, final):
    if final:
        gf_ref, o_ref, acc_ref = rest
    else:
        o_ref, acc_ref = rest
    x = x_ref[...]
    h = _norm_mod(x, gn_ref[...], _ld(sc_ref), _ld(sh_ref)).astype(BF16)
    for c in range(d_ff // fc):
        u = _dot(h, win_ref[:, c * fc:(c + 1) * fc])
        v = _dot(h, win_ref[:, d_ff + c * fc:d_ff + (c + 1) * fc])
        a = (u * jax.nn.sigmoid(u) * v).astype(BF16)
        part = _dot(a, wout_ref[c * fc:(c + 1) * fc, :])
        if c == 0:
            acc_ref[...] = part
        else:
            acc_ref[...] += part
    y = x + 0.5 * _ld(gt_ref) * acc_ref[...]
    if final:
        ms = jnp.mean(y * y, axis=-1, keepdims=True)
        y = y * lax.rsqrt(ms + NORM_EPS) * gf_ref[...]
    o_ref[...] = y


def _ffn(x, mod, k0, gn, w_in, w_out, g_final=None):
    m, d = x.shape
    d_ff = w_out.shape[0]
    tm = _row_tile(m, mod.seq, mod.per_row, 512)
    final = g_final is not None
    in_specs = [pl.BlockSpec((tm, d), lambda i: (i, 0)),
                mod.spec(k0, tm, 1), mod.spec(k0 + 1, tm, 1), mod.spec(k0 + 2, tm, 1),
                _resident((1, d)), _resident(w_in.shape), _resident(w_out.shape)]
    args = [x, mod.arr, mod.arr, mod.arr, gn.reshape(1, d), w_in, w_out]
    if final:
        in_specs.append(_resident((1, d)))
        args.append(g_final.reshape(1, d))
    return pl.pallas_call(
        functools.partial(_ffn_kernel, d_ff=d_ff, fc=256, final=final),
        grid=(m // tm,),
        in_specs=in_specs,
        out_specs=pl.BlockSpec((tm, d), lambda i: (i, 0)),
        out_shape=jax.ShapeDtypeStruct((m, d), F32),
        scratch_shapes=[pltpu.VMEM((tm, d), F32)],
        compiler_params=_cparams("parallel"),
        name="ffn",
    )(*args)


def _proj_kernel(x_ref, sh_ref, sc_ref, gn_ref, w_ref, o_ref, h_ref):
    @pl.when(pl.program_id(1) == 0)
    def _():
        h_ref[...] = _norm_mod(x_ref[...], gn_ref[...], _ld(sc_ref), _ld(sh_ref)).astype(BF16)

    o_ref[...] = _dot(h_ref[...], w_ref[...]).astype(o_ref.dtype)


def _proj(x, mod, gn, w, out_dtype):
    m, d = x.shape
    n = w.shape[1]
    tm = _row_tile(m, mod.seq, mod.per_row, 1024)
    tn = 512
    return pl.pallas_call(
        _proj_kernel,
        grid=(m // tm, n // tn),
        in_specs=[pl.BlockSpec((tm, d), lambda i, j: (i, 0)),
                  mod.spec(3, tm, 2), mod.spec(4, tm, 2),
                  pl.BlockSpec((1, d), lambda i, j: (0, 0)),
                  pl.BlockSpec((d, tn), lambda i, j: (0, j))],
        out_specs=pl.BlockSpec((tm, tn), lambda i, j: (i, j)),
        out_shape=jax.ShapeDtypeStruct((m, n), out_dtype),
        scratch_shapes=[pltpu.VMEM((tm, d), BF16)],
        compiler_params=_cparams("parallel", "arbitrary"),
        name="proj",
    )(x, mod.arr, mod.arr, gn.reshape(1, d), w)


def _projkv_kernel(x_ref, sh_ref, sc_ref, gn_ref, w_ref, kv1_ref, kv2_ref, kv3_ref, dt_ref):
    h = _norm_mod(x_ref[...], gn_ref[...], _ld(sc_ref), _ld(sh_ref)).astype(BF16)
    kvw = 2 * A_WIDTH
    for g, ref in enumerate((kv1_ref, kv2_ref, kv3_ref)):
        ref[...] = _dot(h, w_ref[:, g * kvw:(g + 1) * kvw])
    dt_ref[...] = _dot(h, w_ref[:, 3 * kvw:3 * kvw + LANES])


def _projkv(x, mod, gn, w):
    m, d = x.shape
    kvw = 2 * A_WIDTH
    tm = _row_tile(m, mod.seq, mod.per_row, 512)
    return pl.pallas_call(
        _projkv_kernel,
        grid=(m // tm,),
        in_specs=[pl.BlockSpec((tm, d), lambda i: (i, 0)),
                  mod.spec(3, tm, 1), mod.spec(4, tm, 1),
                  _resident((1, d)), _resident(w.shape)],
        out_specs=[pl.BlockSpec((tm, kvw), lambda i: (i, 0))] * 3
        + [pl.BlockSpec((tm, LANES), lambda i: (i, 0))],
        out_shape=[jax.ShapeDtypeStruct((m, kvw), F32)] * 3 + [jax.ShapeDtypeStruct((m, LANES), F32)],
        compiler_params=_cparams("parallel"),
        name="projkv",
    )(x, mod.arr, mod.arr, gn.reshape(1, d), w)


def _attn_pair(q2, k2, v2, bias_a, bias_b, lo):
    outs = []
    for qh, bias in ((jnp.where(lo, q2, jnp.zeros_like(q2)), bias_a),
                     (jnp.where(lo, jnp.zeros_like(q2), q2), bias_b)):
        s = _dot_nt(qh, k2) + bias
        mx = jnp.max(s, axis=-1, keepdims=True)
        p = jnp.exp(s - mx)
        l = jnp.sum(p, axis=-1, keepdims=True)
        o = _dot(p.astype(BF16), v2) / l
        outs.append((o, mx + jnp.log(l)))
    (oa, la), (ob, lb) = outs
    return jnp.where(lo, oa, ob), jnp.where(lo, la, lb)


def _attn_prompt_kernel(q_ref, k_ref, v_ref, bias_ref, o_ref, lse_ref, *, nb):
    blk = A_BLOCK
    lo = lax.broadcasted_iota(jnp.int32, (blk, LANES), 1) < A_HEAD_DIM

    def block(qs, ks, kw):
        for hp in range(A_HEADS // 2):
            cols = slice(hp * LANES, (hp + 1) * LANES)
            q2 = q_ref[pl.ds(qs, blk), cols].astype(BF16)
            k2 = k_ref[pl.ds(ks, kw), cols].astype(BF16)
            v2 = v_ref[pl.ds(ks, kw), cols].astype(BF16)
            if kw == blk:
                ba = bias_ref[2 * hp, :, blk:2 * blk]
                bb = bias_ref[2 * hp + 1, :, blk:2 * blk]
            else:
                ba = bias_ref[2 * hp]
                bb = bias_ref[2 * hp + 1]
            o2, l2 = _attn_pair(q2, k2, v2, ba, bb, lo)
            o_ref[pl.ds(qs, blk), cols] = o2.astype(o_ref.dtype)
            lse_ref[pl.ds(qs, blk), cols] = l2

    block(0, 0, blk)
    if nb > 1:
        def body(b, carry):
            qs = pl.multiple_of(b * blk, blk)
            ks = pl.multiple_of((b - 1) * blk, blk)
            block(qs, ks, 2 * blk)
            return carry
        lax.fori_loop(1, nb, body, 0)


def _attn_prompt(big, kv, bias_tab, g, dil, batch, seq, o_dtype):
    m = seq // dil
    nbf = big.shape[1]
    kvw = kv.shape[1]
    aw = A_WIDTH
    qv = big.reshape(batch * m, dil * nbf)
    kvv = kv.reshape(batch * m, dil * kvw)
    qstep, kstep = nbf // aw, kvw // aw
    out_sd = jax.ShapeDtypeStruct((batch * m, dil * aw), o_dtype)
    lse_sd = jax.ShapeDtypeStruct((batch * m, dil * aw), F32)
    o, lse = pl.pallas_call(
        functools.partial(_attn_prompt_kernel, nb=m // A_BLOCK),
        grid=(batch, dil),
        in_specs=[pl.BlockSpec((m, aw), lambda b, r: (b, r * qstep + g)),
                  pl.BlockSpec((m, aw), lambda b, r: (b, r * kstep)),
                  pl.BlockSpec((m, aw), lambda b, r: (b, r * kstep + 1)),
                  pl.BlockSpec(bias_tab.shape, lambda b, r: (0, 0, 0))],
        out_specs=[pl.BlockSpec((m, aw), lambda b, r: (b, r))] * 2,
        out_shape=[out_sd, lse_sd],
        compiler_params=_cparams("parallel", "parallel"),
        name=f"attn_prompt{g}",
    )(qv, kvv, kvv, bias_tab)
    return o.reshape(batch * seq, aw), lse.reshape(batch * seq, aw)


def _attn_sample_kernel(q_ref, kn_ref, vn_ref, buf_ref, bbuf_ref, bnew_ref, o_ref, lse_ref,
                        knew_scr, vnew_scr):
    t = q_ref.shape[0]
    aw = A_WIDTH
    lo = lax.broadcasted_iota(jnp.int32, (t, LANES), 1) < A_HEAD_DIM
    knew_scr[...] = jnp.zeros_like(knew_scr)
    vnew_scr[...] = jnp.zeros_like(vnew_scr)
    knew_scr[0:t, :] = kn_ref[...]
    vnew_scr[0:t, :] = vn_ref[...]
    for hp in range(A_HEADS // 2):
        cols = slice(hp * LANES, (hp + 1) * LANES)
        q2 = q_ref[:, cols].astype(BF16)
        kb = buf_ref[:, hp * LANES:(hp + 1) * LANES].astype(BF16)
        vb = buf_ref[:, aw + hp * LANES:aw + (hp + 1) * LANES].astype(BF16)
        kn = knew_scr[:, cols].astype(BF16)
        vn = vnew_scr[:, cols].astype(BF16)
        outs = []
        for qh, h in ((jnp.where(lo, q2, jnp.zeros_like(q2)), 2 * hp),
                      (jnp.where(lo, jnp.zeros_like(q2), q2), 2 * hp + 1)):
            s1 = _dot_nt(qh, kb) + bbuf_ref[h]
            s2 = _dot_nt(qh, kn) + bnew_ref[h]
            mx = jnp.maximum(jnp.max(s1, axis=-1, keepdims=True), jnp.max(s2, axis=-1, keepdims=True))
            p1 = jnp.exp(s1 - mx)
            p2 = jnp.exp(s2 - mx)
            l = jnp.sum(p1, axis=-1, keepdims=True) + jnp.sum(p2, axis=-1, keepdims=True)
            o = (_dot(p1.astype(BF16), vb) + _dot(p2.astype(BF16), vn)) / l
            outs.append((o, mx + jnp.log(l)))
        (oa, la), (ob, lb) = outs
        o_ref[:, cols] = jnp.where(lo, oa, ob)
        lse_ref[:, cols] = jnp.where(lo, la, lb)


def _attn_sample(big, kv, cache, layer, bias_buf, bias_new, g, batch, t):
    aw = A_WIDTH
    wb = cache.shape[2]
    bufv = cache.reshape(cache.shape[0], batch, wb, 2 * aw)
    sd = jax.ShapeDtypeStruct((batch * t, aw), F32)
    return pl.pallas_call(
        _attn_sample_kernel,
        grid=(batch,),
        in_specs=[pl.BlockSpec((t, aw), lambda b: (b, g)),
                  pl.BlockSpec((t, aw), lambda b: (b, 0)),
                  pl.BlockSpec((t, aw), lambda b: (b, 1)),
                  pl.BlockSpec((None, None, wb, 2 * aw), lambda b: (layer, b, 0, 0)),
                  pl.BlockSpec(bias_buf.shape, lambda b: (0, 0, 0)),
                  pl.BlockSpec(bias_new.shape, lambda b: (0, 0, 0))],
        out_specs=[pl.BlockSpec((t, aw), lambda b: (b, 0))] * 2,
        out_shape=[sd, sd],
        scratch_shapes=[pltpu.VMEM((LANES, aw), F32), pltpu.VMEM((LANES, aw), F32)],
        compiler_params=_cparams("parallel"),
        name=f"attn_sample{g}",
    )(big, kv, kv, bufv, bias_buf, bias_new)


def _ssd_kernel(xbc_ref, z_ref, dt_ref, conv0_ref, st0_ref, cw_ref, cb_ref, dtb_ref, aneg_ref,
                dsk_ref, gn_ref, e_ref, y_ref, convo_ref, sto_ref, xpad, st_t, y_scr, *, lv, nch):
    q = SSD_CHUNK
    c = pl.program_id(1)
    width = y_ref.shape[1]
    nst = B_STATE
    hpg = width // B_HEAD_DIM // B_GROUPS // 2

    @pl.when(c == 0)
    def _():
        xpad[...] = jnp.zeros_like(xpad)
        xpad[0:SUBLANES, :] = conv0_ref[0]
        for j in range(width // LANES):
            st_t[:, j * LANES:(j + 1) * LANES] = st0_ref[0, j * LANES:(j + 1) * LANES, :].T

    @pl.when(c > 0)
    def _():
        xpad[0:SUBLANES, :] = xpad[q:q + SUBLANES, :]

    xpad[SUBLANES:SUBLANES + lv, :] = xbc_ref[...].astype(F32)

    w = cw_ref[...]
    conv = cb_ref[...]
    for k in range(B_CONV):
        off = SUBLANES - (B_CONV - 1) + k
        conv = conv + xpad[off:off + q, :] * w[k:k + 1, :]
    act = conv * jax.nn.sigmoid(conv)
    xs = act[:, :width]
    bm = [act[:, width + gi * nst:width + (gi + 1) * nst] for gi in range(B_GROUPS)]
    cm = [act[:, width + (B_GROUPS + gi) * nst:width + (B_GROUPS + gi + 1) * nst] for gi in range(B_GROUPS)]

    dt_raw = dt_ref[...]
    if lv < q:
        dt_raw = jnp.concatenate([dt_raw, jnp.zeros((q - lv, LANES), F32)], axis=0)
    dt = _softplus(dt_raw + dtb_ref[...])
    if lv < q:
        dt = jnp.where(lax.broadcasted_iota(jnp.int32, (q, LANES), 0) < lv, dt, 0.0)
    dta = dt * aneg_ref[...]

    row = lax.broadcasted_iota(jnp.int32, (q, q), 0)
    col = lax.broadcasted_iota(jnp.int32, (q, q), 1)
    causal = row >= col
    tri = jnp.where(causal, 1.0, 0.0).astype(BF16)
    acum = sum(_dot(tri, p) for p in _split_bf16(dta, 3))
    acum_t = acum.T
    e = e_ref[...]
    dt_x = sum(_dot(p, e) for p in _split_bf16(dt, 2))
    acum_x = sum(_dot(p, e) for p in _split_bf16(acum, 2))
    last_x = acum_x[q - 1:q, :]
    ea_x = jnp.exp(acum_x)
    xdt = xs * dt_x
    xdt_te = (xdt * jnp.exp(last_x - acum_x)).astype(BF16)
    xdt_b = xdt.astype(BF16)
    chunk_decay = jnp.exp(last_x)
    lo = lax.broadcasted_iota(jnp.int32, (q, LANES), 1) < B_HEAD_DIM

    for gi in range(B_GROUPS):
        cmb = cm[gi].astype(BF16)
        cb = _dot_nt(cmb, bm[gi].astype(BF16))
        bm_t = bm[gi].T.astype(BF16)
        for hp in range(gi * hpg, (gi + 1) * hpg):
            cols = slice(hp * LANES, (hp + 1) * LANES)
            ax = acum_x[:, cols]
            ax_r = pltpu.roll(ax, B_HEAD_DIM, axis=1)
            ys = []
            for col_v, h in ((jnp.where(lo, ax, ax_r), 2 * hp), (jnp.where(lo, ax_r, ax), 2 * hp + 1)):
                seg = col_v - acum_t[h:h + 1, :]
                dec = jnp.exp(jnp.where(causal, seg, -jnp.inf))
                ys.append(_dot((cb * dec).astype(BF16), xdt_b[:, cols]))
            st_old = st_t[:, cols]
            y_off = _dot(cmb, st_old.astype(BF16)) * ea_x[:, cols]
            st_t[:, cols] = st_old * chunk_decay[:, cols] + _dot(bm_t, xdt_te[:, cols])
            y_scr[:, cols] = jnp.where(lo, ys[0], ys[1]) + y_off + dsk_ref[:, cols] * xs[:, cols]

    z = z_ref[...].astype(F32)
    if lv < q:
        z = jnp.concatenate([z, jnp.zeros((q - lv, width), F32)], axis=0)
    y = y_scr[...] * (z * jax.nn.sigmoid(z))
    gw = width // B_GROUPS
    for gi in range(B_GROUPS):
        yg = y[:, gi * gw:(gi + 1) * gw]
        ms = jnp.mean(yg * yg, axis=-1, keepdims=True)
        yn = yg * lax.rsqrt(ms + NORM_EPS) * gn_ref[:, gi * gw:(gi + 1) * gw]
        y_ref[:, gi * gw:(gi + 1) * gw] = yn[0:lv].astype(y_ref.dtype)

    @pl.when(c == nch - 1)
    def _():
        convo_ref[0] = xpad[lv:lv + SUBLANES, :]
        for j in range(width // LANES):
            sto_ref[0, j * LANES:(j + 1) * LANES, :] = st_t[:, j * LANES:(j + 1) * LANES].T


def _ssd(big, dtp, conv0, st0, lw, batch, seq, o_dtype):
    width = lw["w_b_proj"].shape[0]
    cch = lw["conv_b_w"].shape[1]
    nheads = width // B_HEAD_DIM
    lv = min(seq, SSD_CHUNK)
    nch = seq // lv
    assert lv == SSD_CHUNK or nch == 1
    pad = lambda v: jnp.pad(v.astype(F32), (0, LANES - nheads)).reshape(1, LANES)
    e = (np.arange(LANES)[:, None] == (np.arange(width)[None, :] // B_HEAD_DIM)).astype(np.float32)
    conv0p = jnp.pad(conv0, ((0, 0), (SUBLANES - (B_CONV - 1), 0), (0, 0)))
    y, convo, sto = pl.pallas_call(
        functools.partial(_ssd_kernel, lv=lv, nch=nch),
        grid=(batch, nch),
        in_specs=[pl.BlockSpec((lv, cch), lambda b, c: (b * nch + c, 1)),
                  pl.BlockSpec((lv, width), lambda b, c: (b * nch + c, 3)),
                  pl.BlockSpec((lv, LANES), lambda b, c: (b * nch + c, 0)),
                  pl.BlockSpec((1, SUBLANES, cch), lambda b, c: (b, 0, 0)),
                  pl.BlockSpec((1, width, B_STATE), lambda b, c: (b, 0, 0)),
                  pl.BlockSpec((B_CONV, cch), lambda b, c: (0, 0)),
                  pl.BlockSpec((1, cch), lambda b, c: (0, 0)),
                  pl.BlockSpec((1, LANES), lambda b, c: (0, 0)),
                  pl.BlockSpec((1, LANES), lambda b, c: (0, 0)),
                  pl.BlockSpec((1, width), lambda b, c: (0, 0)),
                  pl.BlockSpec((1, width), lambda b, c: (0, 0)),
                  pl.BlockSpec((LANES, width), lambda b, c: (0, 0))],
        out_specs=[pl.BlockSpec((lv, width), lambda b, c: (b * nch + c, 0)),
                   pl.BlockSpec((1, SUBLANES, cch), lambda b, c: (b, 0, 0)),
                   pl.BlockSpec((1, width, B_STATE), lambda b, c: (b, 0, 0))],
        out_shape=[jax.ShapeDtypeStruct((batch * seq, width), o_dtype),
                   jax.ShapeDtypeStruct((batch, SUBLANES, cch), F32),
                   jax.ShapeDtypeStruct((batch, width, B_STATE), F32)],
        scratch_shapes=[pltpu.VMEM((SUBLANES + SSD_CHUNK, cch), F32),
                        pltpu.VMEM((B_STATE, width), F32),
                        pltpu.VMEM((SSD_CHUNK, width), F32)],
        compiler_params=_cparams("parallel", "arbitrary"),
        name="ssd",
    )(big, big, dtp, conv0p, st0.reshape(batch, width, B_STATE),
      lw["conv_b_w"], lw["conv_b_b"].reshape(1, cch), pad(lw["dt_bias"]),
      pad(-jnp.exp(lw["a_log"].astype(F32))),
      jnp.repeat(lw["d_skip"].astype(F32), B_HEAD_DIM).reshape(1, width),
      lw["g_ssm_norm"].reshape(1, width), jnp.asarray(e, BF16))
    return (y, convo[:, SUBLANES - (B_CONV - 1):, :],
            sto.reshape(batch, nheads, B_HEAD_DIM, B_STATE))


def _lru_kernel(xc_ref, gc_ref, conv0_ref, h0_ref, cw_ref, cb_ref, wr_ref, wi_ref, br_ref, bi_ref,
                lam_ref, y_ref, convo_ref, ho_ref, xpad, hprev, hs_scr, *, q, nch):
    c = pl.program_id(1)
    width = y_ref.shape[1]

    @pl.when(c == 0)
    def _():
        xpad[0:SUBLANES, :] = conv0_ref[0]
        hprev[...] = h0_ref[0]

    @pl.when(c > 0)
    def _():
        xpad[0:SUBLANES, :] = xpad[q:q + SUBLANES, :]

    xpad[SUBLANES:SUBLANES + q, :] = xc_ref[...].astype(F32)
    w = cw_ref[...]
    x = cb_ref[...]
    for k in range(C_CONV):
        off = SUBLANES - (C_CONV - 1) + k
        x = x + xpad[off:off + q, :] * w[k:k + 1, :]

    bd = width // C_BLOCKS
    rs, gs = [], []
    for j in range(C_BLOCKS):
        xb = x[:, j * bd:(j + 1) * bd].astype(BF16)
        rs.append(_dot(xb, wr_ref[j]))
        gs.append(_dot(xb, wi_ref[j]))
    rg = jax.nn.sigmoid(jnp.concatenate(rs, axis=1) + br_ref[...])
    ig = jax.nn.sigmoid(jnp.concatenate(gs, axis=1) + bi_ref[...])
    log_a = -C_POW * rg * _softplus(-lam_ref[...])
    a = jnp.exp(log_a)
    u = x * ig * jnp.sqrt(-jnp.tanh(log_a) * (a * a + 1.0))

    rowi = lax.broadcasted_iota(jnp.int32, (SUBLANES, width), 0)
    h = hprev[...]
    for g in range(q // SUBLANES):
        ag = a[g * SUBLANES:(g + 1) * SUBLANES]
        bg = u[g * SUBLANES:(g + 1) * SUBLANES]
        s = 1
        while s < SUBLANES:
            keep = rowi >= s
            a_sh = jnp.where(keep, pltpu.roll(ag, s, axis=0), 1.0)
            b_sh = jnp.where(keep, pltpu.roll(bg, s, axis=0), 0.0)
            bg = ag * b_sh + bg
            ag = ag * a_sh
            s *= 2
        hg = bg + ag * h
        hs_scr[g * SUBLANES:(g + 1) * SUBLANES, :] = hg
        h = hg[SUBLANES - 1:SUBLANES]
    hprev[...] = h
    y_ref[...] = (hs_scr[...] * jax.nn.gelu(gc_ref[...].astype(F32))).astype(y_ref.dtype)

    @pl.when(c == nch - 1)
    def _():
        convo_ref[0] = xpad[q:q + SUBLANES, :]
        ho_ref[0] = h


def _lru(big, conv0, h0, lw, batch, seq, o_dtype):
    width = lw["w_c_proj"].shape[0]
    q = min(seq, 128)
    nch = seq // q
    bd = width // C_BLOCKS
    conv0p = jnp.pad(conv0, ((0, 0), (SUBLANES - (C_CONV - 1), 0), (0, 0)))
    vec = lambda v: v.astype(F32).reshape(1, width)
    y, convo, ho = pl.pallas_call(
        functools.partial(_lru_kernel, q=q, nch=nch),
        grid=(batch, nch),
        in_specs=[pl.BlockSpec((q, width), lambda b, c: (b * nch + c, 4)),
                  pl.BlockSpec((q, width), lambda b, c: (b * nch + c, 5)),
                  pl.BlockSpec((1, SUBLANES, width), lambda b, c: (b, 0, 0)),
                  pl.BlockSpec((1, 1, width), lambda b, c: (b, 0, 0)),
                  pl.BlockSpec((C_CONV, width), lambda b, c: (0, 0)),
                  pl.BlockSpec((1, width), lambda b, c: (0, 0)),
                  pl.BlockSpec((C_BLOCKS, bd, bd), lambda b, c: (0, 0, 0)),
                  pl.BlockSpec((C_BLOCKS, bd, bd), lambda b, c: (0, 0, 0)),
                  pl.BlockSpec((1, width), lambda b, c: (0, 0)),
                  pl.BlockSpec((1, width), lambda b, c: (0, 0)),
                  pl.BlockSpec((1, width), lambda b, c: (0, 0))],
        out_specs=[pl.BlockSpec((q, width), lambda b, c: (b * nch + c, 0)),
                   pl.BlockSpec((1, SUBLANES, width), lambda b, c: (b, 0, 0)),
                   pl.BlockSpec((1, 1, width), lambda b, c: (b, 0, 0))],
        out_shape=[jax.ShapeDtypeStruct((batch * seq, width), o_dtype),
                   jax.ShapeDtypeStruct((batch, SUBLANES, width), F32),
                   jax.ShapeDtypeStruct((batch, 1, width), F32)],
        scratch_shapes=[pltpu.VMEM((SUBLANES + q, width), F32),
                        pltpu.VMEM((1, width), F32),
                        pltpu.VMEM((q, width), F32)],
        compiler_params=_cparams("parallel", "arbitrary"),
        name="lru",
    )(big, big, conv0p, h0.reshape(batch, 1, width), lw["conv_c_w"], vec(lw["conv_c_b"]),
      lw["w_rgate"].astype(BF16), lw["w_igate"].astype(BF16), vec(lw["b_rgate"]), vec(lw["b_igate"]),
      vec(lw["lru_lambda"]))
    return y, convo[:, SUBLANES - (C_CONV - 1):, :], ho.reshape(batch, width)


def _merge_kernel(x_ref, gt_ref, o1_ref, o2_ref, o3_ref, l1_ref, l2_ref, l3_ref, yb_ref, yc_ref,
                  gates_ref, wa_ref, wb_ref, wc_ref, wo_ref, out_ref):
    d = x_ref.shape[1]
    l1, l2, l3 = l1_ref[...], l2_ref[...], l3_ref[...]
    mx = jnp.maximum(jnp.maximum(l1, l2), l3)
    e1, e2, e3 = jnp.exp(l1 - mx), jnp.exp(l2 - mx), jnp.exp(l3 - mx)
    den = e1 + e2 + e3
    oa = (e1 / den) * o1_ref[...].astype(F32) + (e2 / den) * o2_ref[...].astype(F32) \
        + (e3 / den) * o3_ref[...].astype(F32)
    ya = _dot(oa.astype(BF16), wa_ref[...])
    yb = _dot(yb_ref[...].astype(BF16), wb_ref[...])
    yc = _dot(yc_ref[...].astype(BF16), wc_ref[...])
    sg = jax.nn.sigmoid(gates_ref[...].astype(F32))
    mixed = sg[:, :d] * ya + sg[:, d:2 * d] * yb + sg[:, 2 * d:] * yc
    out_ref[...] = x_ref[...] + _ld(gt_ref) * _dot(mixed.astype(BF16), wo_ref[...])


def _merge(x, mod, os_, lses, yb, yc, big, lw):
    m, d = x.shape
    aw = A_WIDTH
    tm = _row_tile(m, mod.seq, mod.per_row, 512)
    row = lambda wd, j=0: pl.BlockSpec((tm, wd), lambda i: (i, j))
    return pl.pallas_call(
        _merge_kernel,
        grid=(m // tm,),
        in_specs=[row(d), mod.spec(5, tm, 1)] + [row(aw)] * 6
        + [row(yb.shape[1]), row(yc.shape[1]), row(3 * d, 2)]
        + [_resident(lw[k].shape) for k in ("w_a_proj", "w_b_proj", "w_c_proj", "w_out")],
        out_specs=row(d),
        out_shape=jax.ShapeDtypeStruct((m, d), F32),
        compiler_params=_cparams("parallel"),
        name="merge",
    )(x, mod.arr, *os_, *lses, yb, yc, big,
      lw["w_a_proj"], lw["w_b_proj"], lw["w_c_proj"], lw["w_out"])


def _t5_bucket(dist):
    dist = np.asarray(dist)
    large = REL_MAX_EXACT + (np.log(np.maximum(dist, 1) / REL_MAX_EXACT)
                             / math.log(REL_MAX_DISTANCE / REL_MAX_EXACT)
                             * (REL_BUCKETS - REL_MAX_EXACT)).astype(np.int64)
    large = np.minimum(large, REL_BUCKETS - 1)
    return np.where(dist < REL_MAX_EXACT, dist, large).astype(np.int32)


def _group_bias(rel_bias, g):
    dil = A_GROUPS[g][1]
    buckets = _t5_bucket(np.arange(A_KEYS + 1) * dil)
    return rel_bias[buckets][:, g * A_HEADS:(g + 1) * A_HEADS].T.astype(F32)


def _prompt_bias_table(bias):
    qi = np.arange(A_BLOCK)[:, None]
    kj = np.arange(2 * A_BLOCK)[None, :]
    dist = A_BLOCK + qi - kj
    valid = (dist >= 0) & (dist <= A_KEYS)
    tab = bias[:, np.clip(dist, 0, A_KEYS)]
    return jnp.where(valid[None], tab, -jnp.inf)


def _sample_bias_tables(bias, wb, dil, t):
    tq = np.arange(t)[:, None]
    dist = wb + tq - np.arange(wb)[None, :]
    valid = (dist % dil == 0) & (dist // dil <= A_KEYS)
    tab_buf = jnp.where(valid[None], bias[:, np.clip(dist // dil, 0, A_KEYS)], -jnp.inf)
    dist = tq - np.arange(LANES)[None, :]
    valid = (dist >= 0) & (dist % dil == 0) & (np.arange(LANES)[None, :] < t)
    tab_new = jnp.where(valid[None], bias[:, np.clip(dist // dil, 0, A_KEYS)], -jnp.inf)
    return tab_buf, tab_new


def _mixer_weights(w_in):
    na = len(A_GROUPS) * A_WIDTH
    d = w_in.shape[0]
    offs = np.cumsum([0, na, na, na, 1024, 1536, 16, 1024, 1024, 3 * d])
    seg = lambda i: w_in[:, offs[i]:offs[i + 1]]
    qa, ka, va, zb, xbc, dtb, xc, gc, gates = [seg(i) for i in range(9)]
    kv_cols = []
    for g in range(len(A_GROUPS)):
        kv_cols += [ka[:, g * A_WIDTH:(g + 1) * A_WIDTH], va[:, g * A_WIDTH:(g + 1) * A_WIDTH]]
    w_kv = jnp.concatenate(kv_cols + [dtb, jnp.zeros((d, LANES - dtb.shape[1]), w_in.dtype)], axis=1)
    w_big = jnp.concatenate([qa * (A_HEAD_DIM ** -0.5), xbc, zb, xc, gc, gates], axis=1)
    return w_kv.astype(BF16), w_big.astype(BF16)


def _mixer(x, mod, lw, st, bias, prompt, batch, seq):
    act_dtype = BF16 if prompt else F32
    kv1, kv2, kv3, dtp = _projkv(x, mod, lw["g_mix"], lw["w_kv"])
    kvs = (kv1, kv2, kv3)
    big = _proj(x, mod, lw["g_mix"], lw["w_big"], act_dtype)
    os_, lses, new_kv = [], [], []
    for g, (win, dil) in enumerate(A_GROUPS):
        if prompt:
            o, lse = _attn_prompt(big, kvs[g], bias[g], g, dil, batch, seq, act_dtype)
            keep = min(win, seq)
            new_kv.append(kvs[g].reshape(batch, seq, 2, A_HEADS, A_HEAD_DIM)[:, seq - keep:])
        else:
            o, lse = _attn_sample(big, kvs[g], st["kv"][g], st["layer"], bias[g][0], bias[g][1], g, batch, seq)
            new_kv.append(kvs[g].reshape(batch, seq, 2, A_HEADS, A_HEAD_DIM))
        os_.append(o)
        lses.append(lse)
    yb, conv_b_new, ssm_new = _ssd(big, dtp, st["conv_b"], st["ssm"], lw, batch, seq, act_dtype)
    yc, conv_c_new, lru_new = _lru(big, st["conv_c"], st["lru"], lw, batch, seq, act_dtype)
    x = _merge(x, mod, os_, lses, yb, yc, big, lw)
    return x, (new_kv[0], new_kv[1], new_kv[2], conv_b_new, ssm_new, conv_c_new, lru_new)


def _block(x, mod, lw, st, bias, prompt, batch, seq, g_final):
    x = _ffn(x, mod, 0, lw["g_ff1"], lw["w_ff1_in"], lw["w_ff1_out"])
    x, new_st = _mixer(x, mod, lw, st, bias, prompt, batch, seq)
    x = _ffn(x, mod, 6, lw["g_ff2"], lw["w_ff2_in"], lw["w_ff2_out"], g_final)
    return x, new_st


@jax.jit
def _forward(x_prompt, x_sample, c_prompt, c_sample, cache_win1_kv, cache_win2_kv, cache_win3_kv,
             state_conv_b, state_ssm, state_conv_c, state_lru, rel_bias, w_ada, b_ada, g_ff1,
             w_ff1_in, w_ff1_out, g_mix, w_in, w_a_proj, conv_b_w, conv_b_b, dt_bias, a_log, d_skip,
             g_ssm_norm, w_b_proj, conv_c_w, conv_c_b, w_rgate, b_rgate, w_igate, b_igate, lru_lambda,
             w_c_proj, w_out, g_ff2, w_ff2_in, w_ff2_out, g_final):
    bp, lp, d = x_prompt.shape
    bs, ls, _ = x_sample.shape
    depth = w_ada.shape[0]
    caches = (cache_win1_kv, cache_win2_kv, cache_win3_kv)
    biases = [_group_bias(rel_bias, g) for g in range(len(A_GROUPS))]
    bias_p = [_prompt_bias_table(b) for b in biases]
    bias_s = [_sample_bias_tables(b, caches[g].shape[2], A_GROUPS[g][1], ls) for g, b in enumerate(biases)]

    yp = x_prompt.reshape(bp * lp, d)
    ys = x_sample.reshape(bs * ls, d)
    c_all = jnp.concatenate([c_prompt, c_sample], axis=0)
    new_p = [[] for _ in range(7)]
    new_s = [[] for _ in range(7)]
    for l in range(depth):
        w_kv, w_big = _mixer_weights(w_in[l])
        lw = dict(g_ff1=g_ff1[l], w_ff1_in=w_ff1_in[l].astype(BF16), w_ff1_out=w_ff1_out[l].astype(BF16),
                  g_mix=g_mix[l], w_kv=w_kv, w_big=w_big, w_a_proj=w_a_proj[l].astype(BF16),
                  conv_b_w=conv_b_w[l], conv_b_b=conv_b_b[l], dt_bias=dt_bias[l], a_log=a_log[l],
                  d_skip=d_skip[l], g_ssm_norm=g_ssm_norm[l], w_b_proj=w_b_proj[l].astype(BF16),
                  conv_c_w=conv_c_w[l], conv_c_b=conv_c_b[l], w_rgate=w_rgate[l], b_rgate=b_rgate[l],
                  w_igate=w_igate[l], b_igate=b_igate[l], lru_lambda=lru_lambda[l],
                  w_c_proj=w_c_proj[l].astype(BF16), w_out=w_out[l].astype(BF16),
                  g_ff2=g_ff2[l], w_ff2_in=w_ff2_in[l].astype(BF16), w_ff2_out=w_ff2_out[l].astype(BF16))
        mod_all = _ada(c_all, w_ada[l], b_ada[l])
        mod_p = _Mod(mod_all[:bp], bp, lp, per_row=False)
        mod_s = _Mod(mod_all[bp:], bs, ls, per_row=True)
        gf = g_final if l == depth - 1 else None
        st_p = dict(conv_b=jnp.zeros((bp, B_CONV - 1, conv_b_w.shape[2]), F32),
                    ssm=jnp.zeros((bp,) + state_ssm.shape[2:], F32),
                    conv_c=jnp.zeros((bp, C_CONV - 1, conv_c_w.shape[2]), F32),
                    lru=jnp.zeros((bp, state_lru.shape[2]), F32))
        st_s = dict(kv=caches, layer=l, conv_b=state_conv_b[l], ssm=state_ssm[l],
                    conv_c=state_conv_c[l], lru=state_lru[l])
        yp, stp = _block(yp, mod_p, lw, st_p, bias_p, True, bp, lp, gf)
        ys, sts = _block(ys, mod_s, lw, st_s, bias_s, False, bs, ls, gf)
        for i in range(7):
            new_p[i].append(stp[i])
            new_s[i].append(sts[i])
    outs_p = [jnp.stack(v, 0) for v in new_p]
    outs_s = [jnp.stack(v, 0) for v in new_s]
    return (yp.reshape(bp, lp, d), ys.reshape(bs, ls, d), *outs_p, *outs_s)


def kernel(x_prompt, x_sample, c_prompt, c_sample, cache_win1_kv, cache_win2_kv, cache_win3_kv,
           state_conv_b, state_ssm, state_conv_c, state_lru, rel_bias, w_ada, b_ada, g_ff1,
           w_ff1_in, w_ff1_out, g_mix, w_in, w_a_proj, conv_b_w, conv_b_b, dt_bias, a_log, d_skip,
           g_ssm_norm, w_b_proj, conv_c_w, conv_c_b, w_rgate, b_rgate, w_igate, b_igate, lru_lambda,
           w_c_proj, w_out, g_ff2, w_ff2_in, w_ff2_out, g_final):
    return _forward(x_prompt, x_sample, c_prompt, c_sample, cache_win1_kv, cache_win2_kv,
                    cache_win3_kv, state_conv_b, state_ssm, state_conv_c, state_lru, rel_bias, w_ada,
                    b_ada, g_ff1, w_ff1_in, w_ff1_out, g_mix, w_in, w_a_proj, conv_b_w, conv_b_b,
                    dt_bias, a_log, d_skip, g_ssm_norm, w_b_proj, conv_c_w, conv_c_b, w_rgate,
                    b_rgate, w_igate, b_igate, lru_lambda, w_c_proj, w_out, g_ff2, w_ff2_in,
                    w_ff2_out, g_final)
```

```python
import functools
import math

import jax
import jax.numpy as jnp
import numpy as np
from jax import lax
from jax.experimental import pallas as pl
from jax.experimental.pallas import tpu as pltpu

F32 = jnp.float32
BF16 = jnp.bfloat16

NORM_EPS = 1e-6
N_MOD = 9
A_GROUPS = ((128, 1), (512, 4), (2048, 16))
A_HEADS = 8
A_HEAD_DIM = 64
A_WIDTH = A_HEADS * A_HEAD_DIM
A_KEYS = 128
A_BLOCK = 128
REL_BUCKETS = 32
REL_MAX_EXACT = 16
REL_MAX_DISTANCE = 2048
B_HEAD_DIM = 64
B_GROUPS = 2
B_STATE = 128
B_CONV = 4
C_BLOCKS = 8
C_CONV = 4
C_POW = 8.0

LANES = 128
SUBLANES = 8
SSD_CHUNK = 128
PROJ_TN = 512
VMEM_LIMIT = 56 * 1024 * 1024


def _cparams(*sem):
    return pltpu.CompilerParams(dimension_semantics=sem, vmem_limit_bytes=VMEM_LIMIT)


def _resident(shape):
    nd = len(shape)
    return pl.BlockSpec(shape, lambda *_: (0,) * nd, pipeline_mode=pl.Buffered(1))


def _ld(ref):
    return ref[0] if len(ref.shape) == 3 else ref[...]


def _norm_mod(x, g, sc, sh):
    ms = jnp.mean(x * x, axis=-1, keepdims=True)
    return (x * lax.rsqrt(ms + NORM_EPS) * g) * (1.0 + sc) + sh


def _softplus(x):
    return jnp.maximum(x, 0.0) + jnp.log1p(jnp.exp(-jnp.abs(x)))


def _split_bf16(v, n):
    parts = []
    r = v
    for _ in range(n):
        p = r.astype(BF16)
        parts.append(p)
        r = r - p.astype(F32)
    return parts


def _dot(a, b):
    return jnp.dot(a, b, preferred_element_type=F32)


def _dot_nt(a, b):
    return lax.dot_general(a, b, (((1,), (1,)), ((), ())), preferred_element_type=F32)


def _ada_kernel(c_ref, w_ref, b_ref, o_ref):
    c = c_ref[...]
    a = (c * jax.nn.sigmoid(c)).astype(BF16)
    o_ref[...] = _dot(a, w_ref[...].astype(BF16)) + b_ref[...]


def _ada(c, w, b):
    rows, d = c.shape
    n = w.shape[1]
    tn = 1024
    return pl.pallas_call(
        _ada_kernel,
        grid=(n // tn,),
        in_specs=[pl.BlockSpec((rows, d), lambda j: (0, 0)),
                  pl.BlockSpec((d, tn), lambda j: (0, j)),
                  pl.BlockSpec((1, tn), lambda j: (0, j))],
        out_specs=pl.BlockSpec((rows, tn), lambda j: (0, j)),
        out_shape=jax.ShapeDtypeStruct((rows, n), F32),
        compiler_params=_cparams("parallel"),
        name="ada",
    )(c, w, b.reshape(1, n))


class _Mod:
    def __init__(self, mod, batch, seq, per_row):
        d = mod.shape[1] // N_MOD
        self.d = d
        self.seq = seq
        self.per_row = per_row
        if per_row:
            self.arr = jnp.repeat(mod, seq, axis=0)
        else:
            self.arr = mod.reshape(batch * N_MOD, 1, d)

    def spec(self, k, tm, grid_rank):
        d = self.d
        if self.per_row:
            if grid_rank == 1:
                return pl.BlockSpec((tm, d), lambda i: (i, k))
            return pl.BlockSpec((tm, d), lambda i, j: (i, k))
        per = self.seq // tm
        if grid_rank == 1:
            return pl.BlockSpec((1, 1, d), lambda i: ((i // per) * N_MOD + k, 0, 0))
        return pl.BlockSpec((1, 1, d), lambda i, j: ((i // per) * N_MOD + k, 0, 0))


def _row_tile(m, seq, per_row, want):
    tm = min(want, m if per_row else seq)
    assert m % tm == 0 and (per_row or seq % tm == 0)
    return tm


def _ffn_kernel(x_ref, sh_ref, sc_ref, gt_ref, gn_ref, win_ref, wout_ref, *rest, d_ff, fc, final):
    if final:
        gf_ref, o_ref, acc_ref = rest
    else:
        o_ref, acc_ref = rest
    x = x_ref[...]
    h = _norm_mod(x, gn_ref[...], _ld(sc_ref), _ld(sh_ref)).astype(BF16)
    for c in range(d_ff // fc):
        u = _dot(h, win_ref[:, c * fc:(c + 1) * fc])
        v = _dot(h, win_ref[:, d_ff + c * fc:d_ff + (c + 1) * fc])
        a = (u * jax.nn.sigmoid(u) * v).astype(BF16)
        part = _dot(a, wout_ref[c * fc:(c + 1) * fc, :])
        if c == 0:
            acc_ref[...] = part
        else:
            acc_ref[...] += part
    y = x + 0.5 * _ld(gt_ref) * acc_ref[...]
    if final:
        ms = jnp.mean(y * y, axis=-1, keepdims=True)
        y = y * lax.rsqrt(ms + NORM_EPS) * gf_ref[...]
    o_ref[...] = y


def _ffn(x, mod, k0, gn, w_in, w_out, g_final=None):
    m, d = x.shape
    d_ff = w_out.shape[0]
    tm = _row_tile(m, mod.seq, mod.per_row, 512)
    final = g_final is not None
    in_specs = [pl.BlockSpec((tm, d), lambda i: (i, 0)),
                mod.spec(k0, tm, 1), mod.spec(k0 + 1, tm, 1), mod.spec(k0 + 2, tm, 1),
                _resident((1, d)), _resident(w_in.shape), _resident(w_out.shape)]
    args = [x, mod.arr, mod.arr, mod.arr, gn.reshape(1, d), w_in, w_out]
    if final:
        in_specs.append(_resident((1, d)))
        args.append(g_final.reshape(1, d))
    return pl.pallas_call(
        functools.partial(_ffn_kernel, d_ff=d_ff, fc=256, final=final),
        grid=(m // tm,),
        in_specs=in_specs,
        out_specs=pl.BlockSpec((tm, d), lambda i: (i, 0)),
        out_shape=jax.ShapeDtypeStruct((m, d), F32),
        scratch_shapes=[pltpu.VMEM((tm, d), F32)],
        compiler_params=_cparams("parallel"),
        name="ffn",
    )(*args)


def _proj_kernel(x_ref, sh_ref, sc_ref, gn_ref, w_ref, o_ref, h_ref):
    @pl.when(pl.program_id(1) == 0)
    def _():
        h_ref[...] = _norm_mod(x_ref[...], gn_ref[...], _ld(sc_ref), _ld(sh_ref)).astype(BF16)

    o_ref[...] = _dot(h_ref[...], w_ref[...]).astype(o_ref.dtype)


def _proj(x, mod, gn, w, out_dtype):
    m, d = x.shape
    n = w.shape[1]
    tm = _row_tile(m, mod.seq, mod.per_row, 1024)
    tn = PROJ_TN
    return pl.pallas_call(
        _proj_kernel,
        grid=(m // tm, n // tn),
        in_specs=[pl.BlockSpec((tm, d), lambda i, j: (i, 0)),
                  mod.spec(3, tm, 2), mod.spec(4, tm, 2),
                  pl.BlockSpec((1, d), lambda i, j: (0, 0)),
                  pl.BlockSpec((d, tn), lambda i, j: (0, j))],
        out_specs=pl.BlockSpec((tm, tn), lambda i, j: (i, j)),
        out_shape=jax.ShapeDtypeStruct((m, n), out_dtype),
        scratch_shapes=[pltpu.VMEM((tm, d), BF16)],
        compiler_params=_cparams("parallel", "arbitrary"),
        name="proj",
    )(x, mod.arr, mod.arr, gn.reshape(1, d), w)


def _pair_softmax(q2, k2, v2, bias_of, lo):
    zero = jnp.zeros_like(q2)
    res = []
    for i, qh in enumerate((jnp.where(lo, q2, zero), jnp.where(lo, zero, q2))):
        s = _dot_nt(qh, k2) + bias_of(i)
        mx = jnp.max(s, axis=-1, keepdims=True)
        p = jnp.exp(s - mx)
        res.append((_dot(p.astype(BF16), v2), mx, jnp.sum(p, axis=-1, keepdims=True)))
    (oa, ma, la), (ob, mb, lb) = res
    return jnp.where(lo, oa, ob), jnp.where(lo, ma, mb), jnp.where(lo, la, lb)


def _merge_softmax(acc, new):
    ao, am, al = acc
    o, m, l = new
    mn = jnp.maximum(am, m)
    a1 = jnp.exp(am - mn)
    a2 = jnp.exp(m - mn)
    return ao * a1 + o * a2, mn, al * a1 + l * a2


def _attn_prompt_kernel(*refs, seq):
    qkv = refs[:9]
    bias_ref, o_ref, acc_o, acc_m, acc_l = refs[9:]
    blk = A_BLOCK
    hp = pl.program_id(1)
    lo = lax.broadcasted_iota(jnp.int32, (blk, LANES), 1) < A_HEAD_DIM

    def tile(g, qstart, kstart, kw):
        dil = A_GROUPS[g][1]
        q_ref, k_ref, v_ref = qkv[3 * g:3 * g + 3]

        def rows(start, n):
            return pl.ds(start, n) if dil == 1 else pl.ds(start, n, stride=dil)

        q2 = q_ref[rows(qstart, blk), :].astype(BF16)
        k2 = k_ref[rows(kstart, kw), :].astype(BF16)
        v2 = v_ref[rows(kstart, kw), :].astype(BF16)
        if kw == blk:
            bias_of = lambda i: bias_ref[g, 2 * hp + i, :, blk:2 * blk]
        else:
            bias_of = lambda i: bias_ref[g, 2 * hp + i]
        new = _pair_softmax(q2, k2, v2, bias_of, lo)
        sel = rows(qstart, blk)
        if g > 0:
            new = _merge_softmax((acc_o[sel, :], acc_m[sel, :], acc_l[sel, :]), new)
        acc_o[sel, :] = new[0]
        acc_m[sel, :] = new[1]
        acc_l[sel, :] = new[2]

    for g, (_, dil) in enumerate(A_GROUPS):
        nb = seq // dil // blk
        span = blk * dil

        def first(r, carry, g=g):
            tile(g, r, r, blk)
            return carry

        def rest(i, carry, g=g, nb=nb, span=span):
            r = i // (nb - 1)
            b = 1 + i % (nb - 1)
            tile(g, r + b * span, r + (b - 1) * span, 2 * blk)
            return carry

        lax.fori_loop(0, dil, first, 0)
        if nb > 1:
            lax.fori_loop(0, dil * (nb - 1), rest, 0)
    o_ref[...] = (acc_o[...] / acc_l[...]).astype(o_ref.dtype)


def _attn_prompt(qkv, bias_tab, batch, seq, o_dtype):
    npair = A_WIDTH // LANES
    ng = len(A_GROUPS)
    col = lambda j: pl.BlockSpec((seq, LANES), lambda b, h: (b, j * npair + h))
    return pl.pallas_call(
        functools.partial(_attn_prompt_kernel, seq=seq),
        grid=(batch, npair),
        in_specs=[col(j) for j in range(3 * ng)]
        + [pl.BlockSpec(bias_tab.shape, lambda b, h: (0, 0, 0, 0))],
        out_specs=pl.BlockSpec((seq, LANES), lambda b, h: (b, h)),
        out_shape=jax.ShapeDtypeStruct((batch * seq, A_WIDTH), o_dtype),
        scratch_shapes=[pltpu.VMEM((seq, LANES), F32)] * 3,
        compiler_params=_cparams("parallel", "parallel"),
        name="attn_prompt",
    )(*([qkv] * (3 * ng)), bias_tab)


def _attn_sample_kernel(qkv_ref, buf1_ref, buf2_ref, buf3_ref, bb1, bn1, bb2, bn2, bb3, bn3,
                        o_ref, knew_scr, vnew_scr):
    t = qkv_ref.shape[0]
    aw = A_WIDTH
    bufs = (buf1_ref, buf2_ref, buf3_ref)
    bias = ((bb1, bn1), (bb2, bn2), (bb3, bn3))
    lo = lax.broadcasted_iota(jnp.int32, (t, LANES), 1) < A_HEAD_DIM
    knew_scr[...] = jnp.zeros_like(knew_scr)
    vnew_scr[...] = jnp.zeros_like(vnew_scr)
    for g in range(len(A_GROUPS)):
        knew_scr[g, 0:t, :] = qkv_ref[:, (3 * g + 1) * aw:(3 * g + 2) * aw]
        vnew_scr[g, 0:t, :] = qkv_ref[:, (3 * g + 2) * aw:(3 * g + 3) * aw]
    for hp in range(A_HEADS // 2):
        cols = slice(hp * LANES, (hp + 1) * LANES)
        acc = None
        for g in range(len(A_GROUPS)):
            q2 = qkv_ref[:, 3 * g * aw + hp * LANES:3 * g * aw + (hp + 1) * LANES].astype(BF16)
            kb = bufs[g][:, hp * LANES:(hp + 1) * LANES].astype(BF16)
            vb = bufs[g][:, aw + hp * LANES:aw + (hp + 1) * LANES].astype(BF16)
            kn = knew_scr[g, :, cols].astype(BF16)
            vn = vnew_scr[g, :, cols].astype(BF16)
            bbuf, bnew = bias[g]
            new = _pair_softmax(q2, kb, vb, lambda i: bbuf[2 * hp + i], lo)
            new = _merge_softmax(new, _pair_softmax(q2, kn, vn, lambda i: bnew[2 * hp + i], lo))
            acc = new if acc is None else _merge_softmax(acc, new)
        o_ref[:, cols] = acc[0] / acc[2]


def _attn_sample(qkv, caches, layer, bias_s, batch, t):
    aw = A_WIDTH
    ng = len(A_GROUPS)
    bufs = [c.reshape(c.shape[0], batch, c.shape[2], 2 * aw) for c in caches]
    buf_specs = [pl.BlockSpec((None, None, bv.shape[2], 2 * aw), lambda b: (layer, b, 0, 0)) for bv in bufs]
    tabs = [tab for pair in bias_s for tab in pair]
    return pl.pallas_call(
        _attn_sample_kernel,
        grid=(batch,),
        in_specs=[pl.BlockSpec((t, 3 * ng * aw), lambda b: (b, 0))] + buf_specs
        + [pl.BlockSpec(tab.shape, lambda b: (0, 0, 0)) for tab in tabs],
        out_specs=pl.BlockSpec((t, aw), lambda b: (b, 0)),
        out_shape=jax.ShapeDtypeStruct((batch * t, aw), F32),
        scratch_shapes=[pltpu.VMEM((ng, LANES, aw), F32)] * 2,
        compiler_params=_cparams("parallel"),
        name="attn_sample",
    )(qkv, *bufs, *tabs)


def _ssd_kernel(xbc_ref, z_ref, dt_ref, conv0_ref, st0_ref, cw_ref, cb_ref, dtb_ref, aneg_ref,
                dsk_ref, gn_ref, e_ref, y_ref, convo_ref, sto_ref, xpad, st_t, y_scr, *, lv, nch):
    q = SSD_CHUNK
    c = pl.program_id(1)
    width = y_ref.shape[1]
    nst = B_STATE
    hpg = width // B_HEAD_DIM // B_GROUPS // 2

    @pl.when(c == 0)
    def _():
        xpad[...] = jnp.zeros_like(xpad)
        xpad[0:SUBLANES, :] = conv0_ref[0]
        for j in range(width // LANES):
            st_t[:, j * LANES:(j + 1) * LANES] = st0_ref[0, j * LANES:(j + 1) * LANES, :].T

    @pl.when(c > 0)
    def _():
        xpad[0:SUBLANES, :] = xpad[q:q + SUBLANES, :]

    xpad[SUBLANES:SUBLANES + lv, :] = xbc_ref[...].astype(F32)

    w = cw_ref[...]
    conv = cb_ref[...]
    for k in range(B_CONV):
        off = SUBLANES - (B_CONV - 1) + k
        conv = conv + xpad[off:off + q, :] * w[k:k + 1, :]
    act = conv * jax.nn.sigmoid(conv)
    xs = act[:, :width]
    bm = [act[:, width + gi * nst:width + (gi + 1) * nst] for gi in range(B_GROUPS)]
    cm = [act[:, width + (B_GROUPS + gi) * nst:width + (B_GROUPS + gi + 1) * nst] for gi in range(B_GROUPS)]

    dt_raw = dt_ref[...]
    if lv < q:
        dt_raw = jnp.concatenate([dt_raw, jnp.zeros((q - lv, LANES), F32)], axis=0)
    dt = _softplus(dt_raw + dtb_ref[...])
    if lv < q:
        dt = jnp.where(lax.broadcasted_iota(jnp.int32, (q, LANES), 0) < lv, dt, 0.0)
    dta = dt * aneg_ref[...]

    row = lax.broadcasted_iota(jnp.int32, (q, q), 0)
    col = lax.broadcasted_iota(jnp.int32, (q, q), 1)
    causal = row >= col
    tri = jnp.where(causal, 1.0, 0.0).astype(BF16)
    acum = sum(_dot(tri, p) for p in _split_bf16(dta, 3))
    acum_t = acum.T
    e = e_ref[...]
    dt_x = sum(_dot(p, e) for p in _split_bf16(dt, 2))
    acum_x = sum(_dot(p, e) for p in _split_bf16(acum, 2))
    last_x = acum_x[q - 1:q, :]
    ea_x = jnp.exp(acum_x)
    xdt = xs * dt_x
    xdt_te = (xdt * jnp.exp(last_x - acum_x)).astype(BF16)
    xdt_b = xdt.astype(BF16)
    chunk_decay = jnp.exp(last_x)
    lo = lax.broadcasted_iota(jnp.int32, (q, LANES), 1) < B_HEAD_DIM

    for gi in range(B_GROUPS):
        cmb = cm[gi].astype(BF16)
        cb = _dot_nt(cmb, bm[gi].astype(BF16))
        bm_t = bm[gi].T.astype(BF16)
        for hp in range(gi * hpg, (gi + 1) * hpg):
            cols = slice(hp * LANES, (hp + 1) * LANES)
            ax = acum_x[:, cols]
            ax_r = pltpu.roll(ax, B_HEAD_DIM, axis=1)
            ys = []
            for col_v, h in ((jnp.where(lo, ax, ax_r), 2 * hp), (jnp.where(lo, ax_r, ax), 2 * hp + 1)):
                seg = col_v - acum_t[h:h + 1, :]
                dec = jnp.exp(jnp.where(causal, seg, -jnp.inf))
                ys.append(_dot((cb * dec).astype(BF16), xdt_b[:, cols]))
            st_old = st_t[:, cols]
            y_off = _dot(cmb, st_old.astype(BF16)) * ea_x[:, cols]
            st_t[:, cols] = st_old * chunk_decay[:, cols] + _dot(bm_t, xdt_te[:, cols])
            y_scr[:, cols] = jnp.where(lo, ys[0], ys[1]) + y_off + dsk_ref[:, cols] * xs[:, cols]

    z = z_ref[...].astype(F32)
    if lv < q:
        z = jnp.concatenate([z, jnp.zeros((q - lv, width), F32)], axis=0)
    y = y_scr[...] * (z * jax.nn.sigmoid(z))
    gw = width // B_GROUPS
    for gi in range(B_GROUPS):
        yg = y[:, gi * gw:(gi + 1) * gw]
        ms = jnp.mean(yg * yg, axis=-1, keepdims=True)
        yn = yg * lax.rsqrt(ms + NORM_EPS) * gn_ref[:, gi * gw:(gi + 1) * gw]
        y_ref[:, gi * gw:(gi + 1) * gw] = yn[0:lv].astype(y_ref.dtype)

    @pl.when(c == nch - 1)
    def _():
        convo_ref[0] = xpad[lv:lv + SUBLANES, :]
        for j in range(width // LANES):
            sto_ref[0, j * LANES:(j + 1) * LANES, :] = st_t[:, j * LANES:(j + 1) * LANES].T


def _ssd(big, qkv, dt_col, conv0, st0, lw, batch, seq, o_dtype):
    width = lw["w_b_proj"].shape[0]
    cch = lw["conv_b_w"].shape[1]
    nheads = width // B_HEAD_DIM
    lv = min(seq, SSD_CHUNK)
    nch = seq // lv
    assert lv == SSD_CHUNK or nch == 1
    pad = lambda v: jnp.pad(v.astype(F32), (0, LANES - nheads)).reshape(1, LANES)
    e = (np.arange(LANES)[:, None] == (np.arange(width)[None, :] // B_HEAD_DIM)).astype(np.float32)
    conv0p = jnp.pad(conv0, ((0, 0), (SUBLANES - (B_CONV - 1), 0), (0, 0)))
    y, convo, sto = pl.pallas_call(
        functools.partial(_ssd_kernel, lv=lv, nch=nch),
        grid=(batch, nch),
        in_specs=[pl.BlockSpec((lv, cch), lambda b, c: (b * nch + c, 4)),
                  pl.BlockSpec((lv, width), lambda b, c: (b * nch + c, 0)),
                  pl.BlockSpec((lv, LANES), lambda b, c: (b * nch + c, dt_col)),
                  pl.BlockSpec((1, SUBLANES, cch), lambda b, c: (b, 0, 0)),
                  pl.BlockSpec((1, width, B_STATE), lambda b, c: (b, 0, 0)),
                  pl.BlockSpec((B_CONV, cch), lambda b, c: (0, 0)),
                  pl.BlockSpec((1, cch), lambda b, c: (0, 0)),
                  pl.BlockSpec((1, LANES), lambda b, c: (0, 0)),
                  pl.BlockSpec((1, LANES), lambda b, c: (0, 0)),
                  pl.BlockSpec((1, width), lambda b, c: (0, 0)),
                  pl.BlockSpec((1, width), lambda b, c: (0, 0)),
                  pl.BlockSpec((LANES, width), lambda b, c: (0, 0))],
        out_specs=[pl.BlockSpec((lv, width), lambda b, c: (b * nch + c, 0)),
                   pl.BlockSpec((1, SUBLANES, cch), lambda b, c: (b, 0, 0)),
                   pl.BlockSpec((1, width, B_STATE), lambda b, c: (b, 0, 0))],
        out_shape=[jax.ShapeDtypeStruct((batch * seq, width), o_dtype),
                   jax.ShapeDtypeStruct((batch, SUBLANES, cch), F32),
                   jax.ShapeDtypeStruct((batch, width, B_STATE), F32)],
        scratch_shapes=[pltpu.VMEM((SUBLANES + SSD_CHUNK, cch), F32),
                        pltpu.VMEM((B_STATE, width), F32),
                        pltpu.VMEM((SSD_CHUNK, width), F32)],
        compiler_params=_cparams("parallel", "arbitrary"),
        name="ssd",
    )(big, big, qkv, conv0p, st0.reshape(batch, width, B_STATE),
      lw["conv_b_w"], lw["conv_b_b"].reshape(1, cch), pad(lw["dt_bias"]),
      pad(-jnp.exp(lw["a_log"].astype(F32))),
      jnp.repeat(lw["d_skip"].astype(F32), B_HEAD_DIM).reshape(1, width),
      lw["g_ssm_norm"].reshape(1, width), jnp.asarray(e, BF16))
    return (y, convo[:, SUBLANES - (B_CONV - 1):, :],
            sto.reshape(batch, nheads, B_HEAD_DIM, B_STATE))


def _lru_kernel(xc_ref, gc_ref, conv0_ref, h0_ref, cw_ref, cb_ref, wr_ref, wi_ref, br_ref, bi_ref,
                lam_ref, y_ref, convo_ref, ho_ref, xpad, hprev, hs_scr, *, q, nch):
    c = pl.program_id(1)
    width = y_ref.shape[1]

    @pl.when(c == 0)
    def _():
        xpad[0:SUBLANES, :] = conv0_ref[0]
        hprev[...] = h0_ref[0]

    @pl.when(c > 0)
    def _():
        xpad[0:SUBLANES, :] = xpad[q:q + SUBLANES, :]

    xpad[SUBLANES:SUBLANES + q, :] = xc_ref[...].astype(F32)
    w = cw_ref[...]
    x = cb_ref[...]
    for k in range(C_CONV):
        off = SUBLANES - (C_CONV - 1) + k
        x = x + xpad[off:off + q, :] * w[k:k + 1, :]

    bd = width // C_BLOCKS
    rs, gs = [], []
    for j in range(C_BLOCKS):
        xb = x[:, j * bd:(j + 1) * bd].astype(BF16)
        rs.append(_dot(xb, wr_ref[j]))
        gs.append(_dot(xb, wi_ref[j]))
    rg = jax.nn.sigmoid(jnp.concatenate(rs, axis=1) + br_ref[...])
    ig = jax.nn.sigmoid(jnp.concatenate(gs, axis=1) + bi_ref[...])
    log_a = -C_POW * rg * _softplus(-lam_ref[...])
    a = jnp.exp(log_a)
    u = x * ig * jnp.sqrt(-jnp.tanh(log_a) * (a * a + 1.0))

    rowi = lax.broadcasted_iota(jnp.int32, (SUBLANES, width), 0)
    h = hprev[...]
    for g in range(q // SUBLANES):
        ag = a[g * SUBLANES:(g + 1) * SUBLANES]
        bg = u[g * SUBLANES:(g + 1) * SUBLANES]
        s = 1
        while s < SUBLANES:
            keep = rowi >= s
            a_sh = jnp.where(keep, pltpu.roll(ag, s, axis=0), 1.0)
            b_sh = jnp.where(keep, pltpu.roll(bg, s, axis=0), 0.0)
            bg = ag * b_sh + bg
            ag = ag * a_sh
            s *= 2
        hg = bg + ag * h
        hs_scr[g * SUBLANES:(g + 1) * SUBLANES, :] = hg
        h = hg[SUBLANES - 1:SUBLANES]
    hprev[...] = h
    y_ref[...] = (hs_scr[...] * jax.nn.gelu(gc_ref[...].astype(F32))).astype(y_ref.dtype)

    @pl.when(c == nch - 1)
    def _():
        convo_ref[0] = xpad[q:q + SUBLANES, :]
        ho_ref[0] = h


def _lru(big, conv0, h0, lw, batch, seq, o_dtype):
    width = lw["w_c_proj"].shape[0]
    q = min(seq, 128)
    nch = seq // q
    bd = width // C_BLOCKS
    conv0p = jnp.pad(conv0, ((0, 0), (SUBLANES - (C_CONV - 1), 0), (0, 0)))
    vec = lambda v: v.astype(F32).reshape(1, width)
    y, convo, ho = pl.pallas_call(
        functools.partial(_lru_kernel, q=q, nch=nch),
        grid=(batch, nch),
        in_specs=[pl.BlockSpec((q, width), lambda b, c: (b * nch + c, 1)),
                  pl.BlockSpec((q, width), lambda b, c: (b * nch + c, 2)),
                  pl.BlockSpec((1, SUBLANES, width), lambda b, c: (b, 0, 0)),
                  pl.BlockSpec((1, 1, width), lambda b, c: (b, 0, 0)),
                  pl.BlockSpec((C_CONV, width), lambda b, c: (0, 0)),
                  pl.BlockSpec((1, width), lambda b, c: (0, 0)),
                  pl.BlockSpec((C_BLOCKS, bd, bd), lambda b, c: (0, 0, 0)),
                  pl.BlockSpec((C_BLOCKS, bd, bd), lambda b, c: (0, 0, 0)),
                  pl.BlockSpec((1, width), lambda b, c: (0, 0)),
                  pl.BlockSpec((1, width), lambda b, c: (0, 0)),
                  pl.BlockSpec((1, width), lambda b, c: (0, 0))],
        out_specs=[pl.BlockSpec((q, width), lambda b, c: (b * nch + c, 0)),
                   pl.BlockSpec((1, SUBLANES, width), lambda b, c: (b, 0, 0)),
                   pl.BlockSpec((1, 1, width), lambda b, c: (b, 0, 0))],
        out_shape=[jax.ShapeDtypeStruct((batch * seq, width), o_dtype),
                   jax.ShapeDtypeStruct((batch, SUBLANES, width), F32),
                   jax.ShapeDtypeStruct((batch, 1, width), F32)],
        scratch_shapes=[pltpu.VMEM((SUBLANES + q, width), F32),
                        pltpu.VMEM((1, width), F32),
                        pltpu.VMEM((q, width), F32)],
        compiler_params=_cparams("parallel", "arbitrary"),
        name="lru",
    )(big, big, conv0p, h0.reshape(batch, 1, width), lw["conv_c_w"], vec(lw["conv_c_b"]),
      lw["w_rgate"].astype(BF16), lw["w_igate"].astype(BF16), vec(lw["b_rgate"]), vec(lw["b_igate"]),
      vec(lw["lru_lambda"]))
    return y, convo[:, SUBLANES - (C_CONV - 1):, :], ho.reshape(batch, width)


def _merge_kernel(x_ref, gt_ref, oa_ref, yb_ref, yc_ref, gates_ref, wa_ref, wb_ref, wc_ref, wo_ref,
                  out_ref):
    d = x_ref.shape[1]
    ya = _dot(oa_ref[...].astype(BF16), wa_ref[...])
    yb = _dot(yb_ref[...].astype(BF16), wb_ref[...])
    yc = _dot(yc_ref[...].astype(BF16), wc_ref[...])
    sg = jax.nn.sigmoid(gates_ref[...].astype(F32))
    mixed = sg[:, :d] * ya + sg[:, d:2 * d] * yb + sg[:, 2 * d:] * yc
    out_ref[...] = x_ref[...] + _ld(gt_ref) * _dot(mixed.astype(BF16), wo_ref[...])


def _merge(x, mod, oa, yb, yc, big, lw):
    m, d = x.shape
    aw = A_WIDTH
    tm = _row_tile(m, mod.seq, mod.per_row, 512)
    row = lambda wd, j=0: pl.BlockSpec((tm, wd), lambda i: (i, j))
    return pl.pallas_call(
        _merge_kernel,
        grid=(m // tm,),
        in_specs=[row(d), mod.spec(5, tm, 1), row(aw), row(yb.shape[1]), row(yc.shape[1]), row(3 * d, 1)]
        + [_resident(lw[k].shape) for k in ("w_a_proj", "w_b_proj", "w_c_proj", "w_out")],
        out_specs=row(d),
        out_shape=jax.ShapeDtypeStruct((m, d), F32),
        compiler_params=_cparams("parallel"),
        name="merge",
    )(x, mod.arr, oa, yb, yc, big,
      lw["w_a_proj"], lw["w_b_proj"], lw["w_c_proj"], lw["w_out"])


def _t5_bucket(dist):
    dist = np.asarray(dist)
    large = REL_MAX_EXACT + (np.log(np.maximum(dist, 1) / REL_MAX_EXACT)
                             / math.log(REL_MAX_DISTANCE / REL_MAX_EXACT)
                             * (REL_BUCKETS - REL_MAX_EXACT)).astype(np.int64)
    large = np.minimum(large, REL_BUCKETS - 1)
    return np.where(dist < REL_MAX_EXACT, dist, large).astype(np.int32)


def _group_bias(rel_bias, g):
    dil = A_GROUPS[g][1]
    buckets = _t5_bucket(np.arange(A_KEYS + 1) * dil)
    return rel_bias[buckets][:, g * A_HEADS:(g + 1) * A_HEADS].T.astype(F32)


def _prompt_bias_table(bias):
    qi = np.arange(A_BLOCK)[:, None]
    kj = np.arange(2 * A_BLOCK)[None, :]
    dist = A_BLOCK + qi - kj
    valid = (dist >= 0) & (dist <= A_KEYS)
    tab = bias[:, np.clip(dist, 0, A_KEYS)]
    return jnp.where(valid[None], tab, -jnp.inf)


def _sample_bias_tables(bias, wb, dil, t):
    tq = np.arange(t)[:, None]
    dist = wb + tq - np.arange(wb)[None, :]
    valid = (dist % dil == 0) & (dist // dil <= A_KEYS)
    tab_buf = jnp.where(valid[None], bias[:, np.clip(dist // dil, 0, A_KEYS)], -jnp.inf)
    dist = tq - np.arange(LANES)[None, :]
    valid = (dist >= 0) & (dist % dil == 0) & (np.arange(LANES)[None, :] < t)
    tab_new = jnp.where(valid[None], bias[:, np.clip(dist // dil, 0, A_KEYS)], -jnp.inf)
    return tab_buf, tab_new


def _mixer_weights(w_in):
    na = len(A_GROUPS) * A_WIDTH
    d = w_in.shape[0]
    offs = np.cumsum([0, na, na, na, 1024, 1536, 16, 1024, 1024, 3 * d])
    seg = lambda i: w_in[:, offs[i]:offs[i + 1]]
    qa, ka, va, zb, xbc, dtb, xc, gc, gates = [seg(i) for i in range(9)]
    cols = []
    for g in range(len(A_GROUPS)):
        sl = slice(g * A_WIDTH, (g + 1) * A_WIDTH)
        cols += [qa[:, sl] * (A_HEAD_DIM ** -0.5), ka[:, sl], va[:, sl]]
    w_qkv = jnp.concatenate(cols + [dtb, jnp.zeros((d, PROJ_TN - dtb.shape[1]), w_in.dtype)], axis=1)
    w_big = jnp.concatenate([zb, xc, gc, gates, xbc], axis=1)
    return w_qkv.astype(BF16), w_big.astype(BF16)


def _mixer(x, mod, lw, st, bias, prompt, batch, seq):
    act_dtype = BF16 if prompt else F32
    qkv = _proj(x, mod, lw["g_mix"], lw["w_qkv"], F32)
    big = _proj(x, mod, lw["g_mix"], lw["w_big"], act_dtype)
    ng = len(A_GROUPS)
    if prompt:
        oa = _attn_prompt(qkv, bias, batch, seq, act_dtype)
    else:
        oa = _attn_sample(qkv, st["kv"], st["layer"], bias, batch, seq)
    new_kv = []
    for g, (win, _) in enumerate(A_GROUPS):
        keep = min(win, seq) if prompt else seq
        kv = qkv[:, (3 * g + 1) * A_WIDTH:(3 * g + 3) * A_WIDTH]
        new_kv.append(kv.reshape(batch, seq, 2, A_HEADS, A_HEAD_DIM)[:, seq - keep:])
    dt_col = 3 * ng * A_WIDTH // LANES
    yb, conv_b_new, ssm_new = _ssd(big, qkv, dt_col, st["conv_b"], st["ssm"], lw, batch, seq, act_dtype)
    yc, conv_c_new, lru_new = _lru(big, st["conv_c"], st["lru"], lw, batch, seq, act_dtype)
    x = _merge(x, mod, oa, yb, yc, big, lw)
    return x, (new_kv[0], new_kv[1], new_kv[2], conv_b_new, ssm_new, conv_c_new, lru_new)


def _block(x, mod, lw, st, bias, prompt, batch, seq, g_final):
    x = _ffn(x, mod, 0, lw["g_ff1"], lw["w_ff1_in"], lw["w_ff1_out"])
    x, new_st = _mixer(x, mod, lw, st, bias, prompt, batch, seq)
    x = _ffn(x, mod, 6, lw["g_ff2"], lw["w_ff2_in"], lw["w_ff2_out"], g_final)
    return x, new_st


@jax.jit
def _forward(x_prompt, x_sample, c_prompt, c_sample, cache_win1_kv, cache_win2_kv, cache_win3_kv,
             state_conv_b, state_ssm, state_conv_c, state_lru, rel_bias, w_ada, b_ada, g_ff1,
             w_ff1_in, w_ff1_out, g_mix, w_in, w_a_proj, conv_b_w, conv_b_b, dt_bias, a_log, d_skip,
             g_ssm_norm, w_b_proj, conv_c_w, conv_c_b, w_rgate, b_rgate, w_igate, b_igate, lru_lambda,
             w_c_proj, w_out, g_ff2, w_ff2_in, w_ff2_out, g_final):
    bp, lp, d = x_prompt.shape
    bs, ls, _ = x_sample.shape
    depth = w_ada.shape[0]
    caches = (cache_win1_kv, cache_win2_kv, cache_win3_kv)
    biases = [_group_bias(rel_bias, g) for g in range(len(A_GROUPS))]
    bias_p = jnp.stack([_prompt_bias_table(b) for b in biases], 0)
    bias_s = [_sample_bias_tables(b, caches[g].shape[2], A_GROUPS[g][1], ls) for g, b in enumerate(biases)]

    yp = x_prompt.reshape(bp * lp, d)
    ys = x_sample.reshape(bs * ls, d)
    c_all = jnp.concatenate([c_prompt, c_sample], axis=0)
    new_p = [[] for _ in range(7)]
    new_s = [[] for _ in range(7)]
    for l in range(depth):
        w_qkv, w_big = _mixer_weights(w_in[l])
        lw = dict(g_ff1=g_ff1[l], w_ff1_in=w_ff1_in[l].astype(BF16), w_ff1_out=w_ff1_out[l].astype(BF16),
                  g_mix=g_mix[l], w_qkv=w_qkv, w_big=w_big, w_a_proj=w_a_proj[l].astype(BF16),
                  conv_b_w=conv_b_w[l], conv_b_b=conv_b_b[l], dt_bias=dt_bias[l], a_log=a_log[l],
                  d_skip=d_skip[l], g_ssm_norm=g_ssm_norm[l], w_b_proj=w_b_proj[l].astype(BF16),
                  conv_c_w=conv_c_w[l], conv_c_b=conv_c_b[l], w_rgate=w_rgate[l], b_rgate=b_rgate[l],
                  w_igate=w_igate[l], b_igate=b_igate[l], lru_lambda=lru_lambda[l],
                  w_c_proj=w_c_proj[l].astype(BF16), w_out=w_out[l].astype(BF16),
                  g_ff2=g_ff2[l], w_ff2_in=w_ff2_in[l].astype(BF16), w_ff2_out=w_ff2_out[l].astype(BF16))
        mod_all = _ada(c_all, w_ada[l], b_ada[l])
        mod_p = _Mod(mod_all[:bp], bp, lp, per_row=False)
        mod_s = _Mod(mod_all[bp:], bs, ls, per_row=True)
        gf = g_final if l == depth - 1 else None
        st_p = dict(conv_b=jnp.zeros((bp, B_CONV - 1, conv_b_w.shape[2]), F32),
                    ssm=jnp.zeros((bp,) + state_ssm.shape[2:], F32),
                    conv_c=jnp.zeros((bp, C_CONV - 1, conv_c_w.shape[2]), F32),
                    lru=jnp.zeros((bp, state_lru.shape[2]), F32))
        st_s = dict(kv=caches, layer=l, conv_b=state_conv_b[l], ssm=state_ssm[l],
                    conv_c=state_conv_c[l], lru=state_lru[l])
        yp, stp = _block(yp, mod_p, lw, st_p, bias_p, True, bp, lp, gf)
        ys, sts = _block(ys, mod_s, lw, st_s, bias_s, False, bs, ls, gf)
        for i in range(7):
            new_p[i].append(stp[i])
            new_s[i].append(sts[i])
    outs_p = [jnp.stack(v, 0) for v in new_p]
    outs_s = [jnp.stack(v, 0) for v in new_s]
    return (yp.reshape(bp, lp, d), ys.reshape(bs, ls, d), *outs_p, *outs_s)


def kernel(x_prompt, x_sample, c_prompt, c_sample, cache_win1_kv, cache_win2_kv, cache_win3_kv,
           state_conv_b, state_ssm, state_conv_c, state_lru, rel_bias, w_ada, b_ada, g_ff1,
           w_ff1_in, w_ff1_out, g_mix, w_in, w_a_proj, conv_b_w, conv_b_b, dt_bias, a_log, d_skip,
           g_ssm_norm, w_b_proj, conv_c_w, conv_c_b, w_rgate, b_rgate, w_igate, b_igate, lru_lambda,
           w_c_proj, w_out, g_ff2, w_ff2_in, w_ff2_out, g_final):
    return _forward(x_prompt, x_sample, c_prompt, c_sample, cache_win1_kv, cache_win2_kv,
                    cache_win3_kv, state_conv_b, state_ssm, state_conv_c, state_lru, rel_bias, w_ada,
                    b_ada, g_ff1, w_ff1_in, w_ff1_out, g_mix, w_in, w_a_proj, conv_b_w, conv_b_b,
                    dt_bias, a_log, d_skip, g_ssm_norm, w_b_proj, conv_c_w, conv_c_b, w_rgate,
                    b_rgate, w_igate, b_igate, lru_lambda, w_c_proj, w_out, g_ff2, w_ff2_in,
                    w_ff2_out, g_final)
```

```python
import functools
import math

import jax
import jax.numpy as jnp
import numpy as np
from jax import lax
from jax.experimental import pallas as pl
from jax.experimental.pallas import tpu as pltpu

F32 = jnp.float32
BF16 = jnp.bfloat16

NORM_EPS = 1e-6
N_MOD = 9
A_GROUPS = ((128, 1), (512, 4), (2048, 16))
A_HEADS = 8
A_HEAD_DIM = 64
A_WIDTH = A_HEADS * A_HEAD_DIM
A_KEYS = 128
A_BLOCK = 128
REL_BUCKETS = 32
REL_MAX_EXACT = 16
REL_MAX_DISTANCE = 2048
B_HEAD_DIM = 64
B_GROUPS = 2
B_STATE = 128
B_CONV = 4
C_BLOCKS = 8
C_CONV = 4
C_POW = 8.0

LANES = 128
SUBLANES = 8
SSD_CHUNK = 128
PROJ_TN = 512
ATTN_UNROLL = 4
VMEM_LIMIT = 56 * 1024 * 1024


def _cparams(*sem):
    return pltpu.CompilerParams(dimension_semantics=sem, vmem_limit_bytes=VMEM_LIMIT)


def _resident(shape):
    nd = len(shape)
    return pl.BlockSpec(shape, lambda *_: (0,) * nd, pipeline_mode=pl.Buffered(1))


def _ld(ref):
    return ref[0] if len(ref.shape) == 3 else ref[...]


def _norm_mod(x, g, sc, sh):
    ms = jnp.mean(x * x, axis=-1, keepdims=True)
    return (x * lax.rsqrt(ms + NORM_EPS) * g) * (1.0 + sc) + sh


def _softplus(x):
    return jnp.maximum(x, 0.0) + jnp.log1p(jnp.exp(-jnp.abs(x)))


def _split_bf16(v, n):
    parts = []
    r = v
    for _ in range(n):
        p = r.astype(BF16)
        parts.append(p)
        r = r - p.astype(F32)
    return parts


def _dot(a, b):
    return jnp.dot(a, b, preferred_element_type=F32)


def _dot_nt(a, b):
    return lax.dot_general(a, b, (((1,), (1,)), ((), ())), preferred_element_type=F32)


def _ada_kernel(c_ref, w_ref, b_ref, o_ref):
    c = c_ref[...]
    a = (c * jax.nn.sigmoid(c)).astype(BF16)
    o_ref[...] = _dot(a, w_ref[...].astype(BF16)) + b_ref[...]


def _ada(c, w, b, layer):
    rows, d = c.shape
    depth, _, n = w.shape
    tn = 1024
    return pl.pallas_call(
        _ada_kernel,
        grid=(n // tn,),
        in_specs=[pl.BlockSpec((rows, d), lambda j: (0, 0)),
                  pl.BlockSpec((None, d, tn), lambda j: (layer, 0, j)),
                  pl.BlockSpec((None, 1, tn), lambda j: (layer, 0, j))],
        out_specs=pl.BlockSpec((rows, tn), lambda j: (0, j)),
        out_shape=jax.ShapeDtypeStruct((rows, n), F32),
        compiler_params=_cparams("parallel"),
        name="ada",
    )(c, w, b.reshape(depth, 1, n))


class _Mod:
    def __init__(self, mod, batch, seq, per_row):
        d = mod.shape[1] // N_MOD
        self.d = d
        self.seq = seq
        self.per_row = per_row
        if per_row:
            self.arr = jnp.repeat(mod, seq, axis=0)
        else:
            self.arr = mod.reshape(batch * N_MOD, 1, d)

    def spec(self, k, tm, grid_rank):
        d = self.d
        if self.per_row:
            if grid_rank == 1:
                return pl.BlockSpec((tm, d), lambda i: (i, k))
            return pl.BlockSpec((tm, d), lambda i, j: (i, k))
        per = self.seq // tm
        if grid_rank == 1:
            return pl.BlockSpec((1, 1, d), lambda i: ((i // per) * N_MOD + k, 0, 0))
        return pl.BlockSpec((1, 1, d), lambda i, j: ((i // per) * N_MOD + k, 0, 0))


def _row_tile(m, seq, per_row, want):
    tm = min(want, m if per_row else seq)
    assert m % tm == 0 and (per_row or seq % tm == 0)
    return tm


def _ffn_kernel(x_ref, sh_ref, sc_ref, gt_ref, gn_ref, win_ref, wout_ref, *rest, d_ff, fc, final):
    if final:
        gf_ref, o_ref, acc_ref = rest
    else:
        o_ref, acc_ref = rest
    x = x_ref[...]
    h = _norm_mod(x, gn_ref[...], _ld(sc_ref), _ld(sh_ref)).astype(BF16)
    for c in range(d_ff // fc):
        u = _dot(h, win_ref[:, c * fc:(c + 1) * fc])
        v = _dot(h, win_ref[:, d_ff + c * fc:d_ff + (c + 1) * fc])
        a = (u * jax.nn.sigmoid(u) * v).astype(BF16)
        part = _dot(a, wout_ref[c * fc:(c + 1) * fc, :])
        if c == 0:
            acc_ref[...] = part
        else:
            acc_ref[...] += part
    y = x + 0.5 * _ld(gt_ref) * acc_ref[...]
    if final:
        ms = jnp.mean(y * y, axis=-1, keepdims=True)
        y = y * lax.rsqrt(ms + NORM_EPS) * gf_ref[...]
    o_ref[...] = y


def _ffn(x, mod, k0, gn, w_in, w_out, g_final=None):
    m, d = x.shape
    d_ff = w_out.shape[0]
    tm = _row_tile(m, mod.seq, mod.per_row, 512)
    final = g_final is not None
    in_specs = [pl.BlockSpec((tm, d), lambda i: (i, 0)),
                mod.spec(k0, tm, 1), mod.spec(k0 + 1, tm, 1), mod.spec(k0 + 2, tm, 1),
                _resident((1, d)), _resident(w_in.shape), _resident(w_out.shape)]
    args = [x, mod.arr, mod.arr, mod.arr, gn.reshape(1, d), w_in, w_out]
    if final:
        in_specs.append(_resident((1, d)))
        args.append(g_final.reshape(1, d))
    return pl.pallas_call(
        functools.partial(_ffn_kernel, d_ff=d_ff, fc=256, final=final),
        grid=(m // tm,),
        in_specs=in_specs,
        out_specs=pl.BlockSpec((tm, d), lambda i: (i, 0)),
        out_shape=jax.ShapeDtypeStruct((m, d), F32),
        scratch_shapes=[pltpu.VMEM((tm, d), F32)],
        compiler_params=_cparams("parallel"),
        name="ffn",
    )(*args)


def _proj_kernel(x_ref, sh_ref, sc_ref, gn_ref, w_ref, o_ref, h_ref):
    @pl.when(pl.program_id(1) == 0)
    def _():
        h_ref[...] = _norm_mod(x_ref[...], gn_ref[...], _ld(sc_ref), _ld(sh_ref)).astype(BF16)

    o_ref[...] = _dot(h_ref[...], w_ref[...]).astype(o_ref.dtype)


def _proj(x, mod, gn, w, out_dtype):
    m, d = x.shape
    n = w.shape[1]
    tm = _row_tile(m, mod.seq, mod.per_row, 1024)
    tn = PROJ_TN
    return pl.pallas_call(
        _proj_kernel,
        grid=(m // tm, n // tn),
        in_specs=[pl.BlockSpec((tm, d), lambda i, j: (i, 0)),
                  mod.spec(3, tm, 2), mod.spec(4, tm, 2),
                  pl.BlockSpec((1, d), lambda i, j: (0, 0)),
                  pl.BlockSpec((d, tn), lambda i, j: (0, j))],
        out_specs=pl.BlockSpec((tm, tn), lambda i, j: (i, j)),
        out_shape=jax.ShapeDtypeStruct((m, n), out_dtype),
        scratch_shapes=[pltpu.VMEM((tm, d), BF16)],
        compiler_params=_cparams("parallel", "arbitrary"),
        name="proj",
    )(x, mod.arr, mod.arr, gn.reshape(1, d), w)


def _pair_softmax(q2, k2, v2, bias_of, lo, transposed=False):
    zero = jnp.zeros_like(q2)
    res = []
    for i, qh in enumerate((jnp.where(lo, q2, zero), jnp.where(lo, zero, q2))):
        s = (_dot(qh, k2) if transposed else _dot_nt(qh, k2)) + bias_of(i)
        mx = jnp.max(s, axis=-1, keepdims=True)
        p = jnp.exp(s - mx)
        pb = p.astype(BF16)
        o = _dot_nt(pb, v2) if transposed else _dot(pb, v2)
        res.append((o, mx, jnp.sum(p, axis=-1, keepdims=True)))
    (oa, ma, la), (ob, mb, lb) = res
    return jnp.where(lo, oa, ob), jnp.where(lo, ma, mb), jnp.where(lo, la, lb)


def _merge_softmax(acc, new):
    ao, am, al = acc
    o, m, l = new
    mn = jnp.maximum(am, m)
    a1 = jnp.exp(am - mn)
    a2 = jnp.exp(m - mn)
    return ao * a1 + o * a2, mn, al * a1 + l * a2


def _attn_prompt_kernel(*refs, seq, n_alias):
    ng = len(A_GROUPS)
    qkv = refs[:3 * ng]
    bias_ref = refs[3 * ng]
    o_ref = refs[3 * ng + 1 + n_alias]
    kvt_refs = refs[3 * ng + 2 + n_alias:3 * ng + 2 + n_alias + ng]
    acc_o, acc_m, acc_l = refs[-3:]
    blk = A_BLOCK

    for g in range(ng):
        keep = kvt_refs[g].shape[-1]
        for kv in range(2):
            src = qkv[3 * g + 1 + kv]
            for c in range(keep // blk):
                t = src[pl.ds(seq - keep + c * blk, blk), :].T
                for i in range(2):
                    kvt_refs[g][kv, i, :, c * blk:(c + 1) * blk] = t[i * A_HEAD_DIM:(i + 1) * A_HEAD_DIM]

    hp = pl.program_id(1)
    lo = lax.broadcasted_iota(jnp.int32, (blk, LANES), 1) < A_HEAD_DIM

    def tiles(g, starts, kw):
        dil = A_GROUPS[g][1]
        q_ref, k_ref, v_ref = qkv[3 * g:3 * g + 3]

        def rows(start, n):
            return pl.ds(start, n) if dil == 1 else pl.ds(start, n, stride=dil)

        if kw == blk:
            bias_of = lambda i: bias_ref[g, 2 * hp + i, :, blk:2 * blk]
        else:
            bias_of = lambda i: bias_ref[g, 2 * hp + i]
        news = []
        for qstart, kstart in starts:
            q2 = q_ref[rows(qstart, blk), :].astype(BF16)
            k2 = k_ref[rows(kstart, kw), :].astype(BF16)
            v2 = v_ref[rows(kstart, kw), :].astype(BF16)
            news.append(_pair_softmax(q2, k2, v2, bias_of, lo))
        for (qstart, _), new in zip(starts, news):
            sel = rows(qstart, blk)
            if g > 0:
                new = _merge_softmax((acc_o[sel, :], acc_m[sel, :], acc_l[sel, :]), new)
            acc_o[sel, :] = new[0]
            acc_m[sel, :] = new[1]
            acc_l[sel, :] = new[2]

    for g, (_, dil) in enumerate(A_GROUPS):
        nb = seq // dil // blk
        span = blk * dil
        ur = min(dil, ATTN_UNROLL)
        ub = 1 if dil > 1 else max(u for u in range(1, ATTN_UNROLL + 1) if (nb - 1) % u == 0)

        def first(i, carry, g=g, ur=ur):
            tiles(g, [(i * ur + j, i * ur + j) for j in range(ur)], blk)
            return carry

        def rest(i, carry, g=g, span=span, ur=ur, ub=ub, nr=dil // ur):
            b = 1 + (i // nr) * ub
            r = (i % nr) * ur
            tiles(g, [(r + j + (b + k) * span, r + j + (b + k - 1) * span)
                      for k in range(ub) for j in range(ur)], 2 * blk)
            return carry

        lax.fori_loop(0, dil // ur, first, 0)
        if nb > 1:
            lax.fori_loop(0, (nb - 1) // ub * (dil // ur), rest, 0)
    o_ref[...] = (acc_o[...] / acc_l[...]).astype(o_ref.dtype)


def _attn_prompt(qkv, bias_tab, kvt_prev, layer, depth, batch, seq, o_dtype):
    npair = A_WIDTH // LANES
    ng = len(A_GROUPS)
    col = lambda j: pl.BlockSpec((seq, LANES), lambda b, h: (b, j * npair + h))
    keeps = [min(win, seq) for win, _ in A_GROUPS]
    n_alias = 0 if kvt_prev is None else ng
    alias_specs = [pl.BlockSpec(memory_space=pl.ANY)] * n_alias
    alias_args = [] if kvt_prev is None else list(kvt_prev)
    n_in = 3 * ng + 1
    outs = pl.pallas_call(
        functools.partial(_attn_prompt_kernel, seq=seq, n_alias=n_alias),
        grid=(batch, npair),
        in_specs=[col(j) for j in range(3 * ng)]
        + [pl.BlockSpec(bias_tab.shape, lambda b, h: (0, 0, 0, 0))] + alias_specs,
        out_specs=[pl.BlockSpec((seq, LANES), lambda b, h: (b, h))]
        + [pl.BlockSpec((None, None, 2, 2, A_HEAD_DIM, keep), lambda b, h: (layer, b, 0, h, 0, 0))
           for keep in keeps],
        out_shape=[jax.ShapeDtypeStruct((batch * seq, A_WIDTH), o_dtype)]
        + [jax.ShapeDtypeStruct((depth, batch, 2, A_HEADS, A_HEAD_DIM, keep), F32) for keep in keeps],
        input_output_aliases={n_in + g: 1 + g for g in range(n_alias)},
        scratch_shapes=[pltpu.VMEM((seq, LANES), F32)] * 3,
        compiler_params=_cparams("parallel", "parallel"),
        name="attn_prompt",
    )(*([qkv] * (3 * ng)), bias_tab, *alias_args)
    return outs[0], list(outs[1:])


def _attn_sample_kernel(qkv_ref, buf1_ref, buf2_ref, buf3_ref, bb1, bn1, bb2, bn2, bb3, bn3,
                        o_ref, knew_scr, vnew_scr):
    t = qkv_ref.shape[0]
    aw = A_WIDTH
    bufs = (buf1_ref, buf2_ref, buf3_ref)
    bias = ((bb1, bn1), (bb2, bn2), (bb3, bn3))
    lo = lax.broadcasted_iota(jnp.int32, (t, LANES), 1) < A_HEAD_DIM
    knew_scr[...] = jnp.zeros_like(knew_scr)
    vnew_scr[...] = jnp.zeros_like(vnew_scr)
    for g in range(len(A_GROUPS)):
        knew_scr[g, 0:t, :] = qkv_ref[:, (3 * g + 1) * aw:(3 * g + 2) * aw]
        vnew_scr[g, 0:t, :] = qkv_ref[:, (3 * g + 2) * aw:(3 * g + 3) * aw]
    for hp in range(A_HEADS // 2):
        cols = slice(hp * LANES, (hp + 1) * LANES)
        acc = None
        for g in range(len(A_GROUPS)):
            q2 = qkv_ref[:, 3 * g * aw + hp * LANES:3 * g * aw + (hp + 1) * LANES].astype(BF16)
            wb = bufs[g].shape[-1]
            kb = bufs[g][0, 2 * hp:2 * hp + 2].reshape(LANES, wb).astype(BF16)
            vb = bufs[g][1, 2 * hp:2 * hp + 2].reshape(LANES, wb).astype(BF16)
            kn = knew_scr[g, :, cols].astype(BF16)
            vn = vnew_scr[g, :, cols].astype(BF16)
            bbuf, bnew = bias[g]
            new = _pair_softmax(q2, kb, vb, lambda i: bbuf[2 * hp + i], lo, transposed=True)
            new = _merge_softmax(new, _pair_softmax(q2, kn, vn, lambda i: bnew[2 * hp + i], lo))
            acc = new if acc is None else _merge_softmax(acc, new)
        o_ref[:, cols] = acc[0] / acc[2]


def _attn_sample(qkv, caches, layer, bias_s, batch, t):
    aw = A_WIDTH
    ng = len(A_GROUPS)
    bufs = [jnp.transpose(c, (0, 1, 3, 4, 5, 2)) for c in caches]
    buf_specs = [pl.BlockSpec((None, None) + bv.shape[2:], lambda b: (layer, b, 0, 0, 0, 0)) for bv in bufs]
    tabs = [tab for pair in bias_s for tab in pair]
    return pl.pallas_call(
        _attn_sample_kernel,
        grid=(batch,),
        in_specs=[pl.BlockSpec((t, 3 * ng * aw), lambda b: (b, 0))] + buf_specs
        + [pl.BlockSpec(tab.shape, lambda b: (0, 0, 0)) for tab in tabs],
        out_specs=pl.BlockSpec((t, aw), lambda b: (b, 0)),
        out_shape=jax.ShapeDtypeStruct((batch * t, aw), F32),
        scratch_shapes=[pltpu.VMEM((ng, LANES, aw), F32)] * 2,
        compiler_params=_cparams("parallel"),
        name="attn_sample",
    )(qkv, *bufs, *tabs)


def _ssd_kernel(xbc_ref, z_ref, dt_ref, conv0_ref, st0_ref, cw_ref, cb_ref, dtb_ref, aneg_ref,
                dsk_ref, gn_ref, e_ref, y_ref, convo_ref, sto_ref, xpad, st_t, y_scr, *, lv, nch):
    q = SSD_CHUNK
    c = pl.program_id(1)
    width = y_ref.shape[1]
    nst = B_STATE
    hpg = width // B_HEAD_DIM // B_GROUPS // 2

    @pl.when(c == 0)
    def _():
        xpad[...] = jnp.zeros_like(xpad)
        xpad[0:SUBLANES, :] = conv0_ref[0]
        for j in range(width // LANES):
            st_t[:, j * LANES:(j + 1) * LANES] = st0_ref[0, j * LANES:(j + 1) * LANES, :].T

    @pl.when(c > 0)
    def _():
        xpad[0:SUBLANES, :] = xpad[q:q + SUBLANES, :]

    xpad[SUBLANES:SUBLANES + lv, :] = xbc_ref[...].astype(F32)

    w = cw_ref[...]
    conv = cb_ref[...]
    for k in range(B_CONV):
        off = SUBLANES - (B_CONV - 1) + k
        conv = conv + xpad[off:off + q, :] * w[k:k + 1, :]
    act = conv * jax.nn.sigmoid(conv)
    xs = act[:, :width]
    bm = [act[:, width + gi * nst:width + (gi + 1) * nst] for gi in range(B_GROUPS)]
    cm = [act[:, width + (B_GROUPS + gi) * nst:width + (B_GROUPS + gi + 1) * nst] for gi in range(B_GROUPS)]

    dt_raw = dt_ref[...]
    if lv < q:
        dt_raw = jnp.concatenate([dt_raw, jnp.zeros((q - lv, LANES), F32)], axis=0)
    dt = _softplus(dt_raw + dtb_ref[...])
    if lv < q:
        dt = jnp.where(lax.broadcasted_iota(jnp.int32, (q, LANES), 0) < lv, dt, 0.0)
    dta = dt * aneg_ref[...]

    row = lax.broadcasted_iota(jnp.int32, (q, q), 0)
    col = lax.broadcasted_iota(jnp.int32, (q, q), 1)
    causal = row >= col
    tri = jnp.where(causal, 1.0, 0.0).astype(BF16)
    acum = sum(_dot(tri, p) for p in _split_bf16(dta, 3))
    acum_t = acum.T
    e = e_ref[...]
    dt_x = sum(_dot(p, e) for p in _split_bf16(dt, 2))
    acum_x = sum(_dot(p, e) for p in _split_bf16(acum, 2))
    last_x = acum_x[q - 1:q, :]
    ea_x = jnp.exp(acum_x)
    xdt = xs * dt_x
    xdt_te = (xdt * jnp.exp(last_x - acum_x)).astype(BF16)
    xdt_b = xdt.astype(BF16)
    chunk_decay = jnp.exp(last_x)
    lo = lax.broadcasted_iota(jnp.int32, (q, LANES), 1) < B_HEAD_DIM

    for gi in range(B_GROUPS):
        cmb = cm[gi].astype(BF16)
        cb = _dot_nt(cmb, bm[gi].astype(BF16))
        bm_t = bm[gi].T.astype(BF16)
        for hp in range(gi * hpg, (gi + 1) * hpg):
            cols = slice(hp * LANES, (hp + 1) * LANES)
            ax = acum_x[:, cols]
            ax_r = pltpu.roll(ax, B_HEAD_DIM, axis=1)
            ys = []
            for col_v, h in ((jnp.where(lo, ax, ax_r), 2 * hp), (jnp.where(lo, ax_r, ax), 2 * hp + 1)):
                seg = col_v - acum_t[h:h + 1, :]
                dec = jnp.exp(jnp.where(causal, seg, -jnp.inf))
                ys.append(_dot((cb * dec).astype(BF16), xdt_b[:, cols]))
            st_old = st_t[:, cols]
            y_off = _dot(cmb, st_old.astype(BF16)) * ea_x[:, cols]
            st_t[:, cols] = st_old * chunk_decay[:, cols] + _dot(bm_t, xdt_te[:, cols])
            y_scr[:, cols] = jnp.where(lo, ys[0], ys[1]) + y_off + dsk_ref[:, cols] * xs[:, cols]

    z = z_ref[...].astype(F32)
    if lv < q:
        z = jnp.concatenate([z, jnp.zeros((q - lv, width), F32)], axis=0)
    y = y_scr[...] * (z * jax.nn.sigmoid(z))
    gw = width // B_GROUPS
    for gi in range(B_GROUPS):
        yg = y[:, gi * gw:(gi + 1) * gw]
        ms = jnp.mean(yg * yg, axis=-1, keepdims=True)
        yn = yg * lax.rsqrt(ms + NORM_EPS) * gn_ref[:, gi * gw:(gi + 1) * gw]
        y_ref[:, gi * gw:(gi + 1) * gw] = yn[0:lv].astype(y_ref.dtype)

    @pl.when(c == nch - 1)
    def _():
        convo_ref[0] = xpad[lv:lv + SUBLANES, :]
        for j in range(width // LANES):
            sto_ref[0, j * LANES:(j + 1) * LANES, :] = st_t[:, j * LANES:(j + 1) * LANES].T


def _ssd(big, qkv, dt_col, conv0, st0, lw, batch, seq, o_dtype):
    width = lw["w_b_proj"].shape[0]
    cch = lw["conv_b_w"].shape[1]
    nheads = width // B_HEAD_DIM
    lv = min(seq, SSD_CHUNK)
    nch = seq // lv
    assert lv == SSD_CHUNK or nch == 1
    pad = lambda v: jnp.pad(v.astype(F32), (0, LANES - nheads)).reshape(1, LANES)
    e = (np.arange(LANES)[:, None] == (np.arange(width)[None, :] // B_HEAD_DIM)).astype(np.float32)
    conv0p = jnp.pad(conv0, ((0, 0), (SUBLANES - (B_CONV - 1), 0), (0, 0)))
    y, convo, sto = pl.pallas_call(
        functools.partial(_ssd_kernel, lv=lv, nch=nch),
        grid=(batch, nch),
        in_specs=[pl.BlockSpec((lv, cch), lambda b, c: (b * nch + c, 4)),
                  pl.BlockSpec((lv, width), lambda b, c: (b * nch + c, 0)),
                  pl.BlockSpec((lv, LANES), lambda b, c: (b * nch + c, dt_col)),
                  pl.BlockSpec((1, SUBLANES, cch), lambda b, c: (b, 0, 0)),
                  pl.BlockSpec((1, width, B_STATE), lambda b, c: (b, 0, 0)),
                  pl.BlockSpec((B_CONV, cch), lambda b, c: (0, 0)),
                  pl.BlockSpec((1, cch), lambda b, c: (0, 0)),
                  pl.BlockSpec((1, LANES), lambda b, c: (0, 0)),
                  pl.BlockSpec((1, LANES), lambda b, c: (0, 0)),
                  pl.BlockSpec((1, width), lambda b, c: (0, 0)),
                  pl.BlockSpec((1, width), lambda b, c: (0, 0)),
                  pl.BlockSpec((LANES, width), lambda b, c: (0, 0))],
        out_specs=[pl.BlockSpec((lv, width), lambda b, c: (b * nch + c, 0)),
                   pl.BlockSpec((1, SUBLANES, cch), lambda b, c: (b, 0, 0)),
                   pl.BlockSpec((1, width, B_STATE), lambda b, c: (b, 0, 0))],
        out_shape=[jax.ShapeDtypeStruct((batch * seq, width), o_dtype),
                   jax.ShapeDtypeStruct((batch, SUBLANES, cch), F32),
                   jax.ShapeDtypeStruct((batch, width, B_STATE), F32)],
        scratch_shapes=[pltpu.VMEM((SUBLANES + SSD_CHUNK, cch), F32),
                        pltpu.VMEM((B_STATE, width), F32),
                        pltpu.VMEM((SSD_CHUNK, width), F32)],
        compiler_params=_cparams("parallel", "arbitrary"),
        name="ssd",
    )(big, big, qkv, conv0p, st0.reshape(batch, width, B_STATE),
      lw["conv_b_w"], lw["conv_b_b"].reshape(1, cch), pad(lw["dt_bias"]),
      pad(-jnp.exp(lw["a_log"].astype(F32))),
      jnp.repeat(lw["d_skip"].astype(F32), B_HEAD_DIM).reshape(1, width),
      lw["g_ssm_norm"].reshape(1, width), jnp.asarray(e, BF16))
    return (y, convo[:, SUBLANES - (B_CONV - 1):, :],
            sto.reshape(batch, nheads, B_HEAD_DIM, B_STATE))


def _lru_kernel(xc_ref, gc_ref, conv0_ref, h0_ref, cw_ref, cb_ref, wr_ref, wi_ref, br_ref, bi_ref,
                lam_ref, y_ref, convo_ref, ho_ref, xpad, hprev, hs_scr, *, q, nch):
    c = pl.program_id(1)
    width = y_ref.shape[1]

    @pl.when(c == 0)
    def _():
        xpad[0:SUBLANES, :] = conv0_ref[0]
        hprev[...] = h0_ref[0]

    @pl.when(c > 0)
    def _():
        xpad[0:SUBLANES, :] = xpad[q:q + SUBLANES, :]

    xpad[SUBLANES:SUBLANES + q, :] = xc_ref[...].astype(F32)
    w = cw_ref[...]
    x = cb_ref[...]
    for k in range(C_CONV):
        off = SUBLANES - (C_CONV - 1) + k
        x = x + xpad[off:off + q, :] * w[k:k + 1, :]

    bd = width // C_BLOCKS
    rs, gs = [], []
    for j in range(C_BLOCKS):
        xb = x[:, j * bd:(j + 1) * bd].astype(BF16)
        rs.append(_dot(xb, wr_ref[j]))
        gs.append(_dot(xb, wi_ref[j]))
    rg = jax.nn.sigmoid(jnp.concatenate(rs, axis=1) + br_ref[...])
    ig = jax.nn.sigmoid(jnp.concatenate(gs, axis=1) + bi_ref[...])
    log_a = -C_POW * rg * _softplus(-lam_ref[...])
    a = jnp.exp(log_a)
    u = x * ig * jnp.sqrt(-jnp.tanh(log_a) * (a * a + 1.0))

    rowi = lax.broadcasted_iota(jnp.int32, (SUBLANES, width), 0)
    h = hprev[...]
    for g in range(q // SUBLANES):
        ag = a[g * SUBLANES:(g + 1) * SUBLANES]
        bg = u[g * SUBLANES:(g + 1) * SUBLANES]
        s = 1
        while s < SUBLANES:
            keep = rowi >= s
            a_sh = jnp.where(keep, pltpu.roll(ag, s, axis=0), 1.0)
            b_sh = jnp.where(keep, pltpu.roll(bg, s, axis=0), 0.0)
            bg = ag * b_sh + bg
            ag = ag * a_sh
            s *= 2
        hg = bg + ag * h
        hs_scr[g * SUBLANES:(g + 1) * SUBLANES, :] = hg
        h = hg[SUBLANES - 1:SUBLANES]
    hprev[...] = h
    y_ref[...] = (hs_scr[...] * jax.nn.gelu(gc_ref[...].astype(F32))).astype(y_ref.dtype)

    @pl.when(c == nch - 1)
    def _():
        convo_ref[0] = xpad[q:q + SUBLANES, :]
        ho_ref[0] = h


def _lru(big, conv0, h0, lw, batch, seq, o_dtype):
    width = lw["w_c_proj"].shape[0]
    q = min(seq, 128)
    nch = seq // q
    bd = width // C_BLOCKS
    conv0p = jnp.pad(conv0, ((0, 0), (SUBLANES - (C_CONV - 1), 0), (0, 0)))
    vec = lambda v: v.astype(F32).reshape(1, width)
    y, convo, ho = pl.pallas_call(
        functools.partial(_lru_kernel, q=q, nch=nch),
        grid=(batch, nch),
        in_specs=[pl.BlockSpec((q, width), lambda b, c: (b * nch + c, 1)),
                  pl.BlockSpec((q, width), lambda b, c: (b * nch + c, 2)),
                  pl.BlockSpec((1, SUBLANES, width), lambda b, c: (b, 0, 0)),
                  pl.BlockSpec((1, 1, width), lambda b, c: (b, 0, 0)),
                  pl.BlockSpec((C_CONV, width), lambda b, c: (0, 0)),
                  pl.BlockSpec((1, width), lambda b, c: (0, 0)),
                  pl.BlockSpec((C_BLOCKS, bd, bd), lambda b, c: (0, 0, 0)),
                  pl.BlockSpec((C_BLOCKS, bd, bd), lambda b, c: (0, 0, 0)),
                  pl.BlockSpec((1, width), lambda b, c: (0, 0)),
                  pl.BlockSpec((1, width), lambda b, c: (0, 0)),
                  pl.BlockSpec((1, width), lambda b, c: (0, 0))],
        out_specs=[pl.BlockSpec((q, width), lambda b, c: (b * nch + c, 0)),
                   pl.BlockSpec((1, SUBLANES, width), lambda b, c: (b, 0, 0)),
                   pl.BlockSpec((1, 1, width), lambda b, c: (b, 0, 0))],
        out_shape=[jax.ShapeDtypeStruct((batch * seq, width), o_dtype),
                   jax.ShapeDtypeStruct((batch, SUBLANES, width), F32),
                   jax.ShapeDtypeStruct((batch, 1, width), F32)],
        scratch_shapes=[pltpu.VMEM((SUBLANES + q, width), F32),
                        pltpu.VMEM((1, width), F32),
                        pltpu.VMEM((q, width), F32)],
        compiler_params=_cparams("parallel", "arbitrary"),
        name="lru",
    )(big, big, conv0p, h0.reshape(batch, 1, width), lw["conv_c_w"], vec(lw["conv_c_b"]),
      lw["w_rgate"].astype(BF16), lw["w_igate"].astype(BF16), vec(lw["b_rgate"]), vec(lw["b_igate"]),
      vec(lw["lru_lambda"]))
    return y, convo[:, SUBLANES - (C_CONV - 1):, :], ho.reshape(batch, width)


def _merge_kernel(x_ref, gt_ref, oa_ref, yb_ref, yc_ref, gates_ref, wa_ref, wb_ref, wc_ref, wo_ref,
                  out_ref):
    d = x_ref.shape[1]
    ya = _dot(oa_ref[...].astype(BF16), wa_ref[...])
    yb = _dot(yb_ref[...].astype(BF16), wb_ref[...])
    yc = _dot(yc_ref[...].astype(BF16), wc_ref[...])
    sg = jax.nn.sigmoid(gates_ref[...].astype(F32))
    mixed = sg[:, :d] * ya + sg[:, d:2 * d] * yb + sg[:, 2 * d:] * yc
    out_ref[...] = x_ref[...] + _ld(gt_ref) * _dot(mixed.astype(BF16), wo_ref[...])


def _merge(x, mod, oa, yb, yc, big, lw):
    m, d = x.shape
    aw = A_WIDTH
    tm = _row_tile(m, mod.seq, mod.per_row, 512)
    row = lambda wd, j=0: pl.BlockSpec((tm, wd), lambda i: (i, j))
    return pl.pallas_call(
        _merge_kernel,
        grid=(m // tm,),
        in_specs=[row(d), mod.spec(5, tm, 1), row(aw), row(yb.shape[1]), row(yc.shape[1]), row(3 * d, 1)]
        + [_resident(lw[k].shape) for k in ("w_a_proj", "w_b_proj", "w_c_proj", "w_out")],
        out_specs=row(d),
        out_shape=jax.ShapeDtypeStruct((m, d), F32),
        compiler_params=_cparams("parallel"),
        name="merge",
    )(x, mod.arr, oa, yb, yc, big,
      lw["w_a_proj"], lw["w_b_proj"], lw["w_c_proj"], lw["w_out"])


def _t5_bucket(dist):
    dist = np.asarray(dist)
    large = REL_MAX_EXACT + (np.log(np.maximum(dist, 1) / REL_MAX_EXACT)
                             / math.log(REL_MAX_DISTANCE / REL_MAX_EXACT)
                             * (REL_BUCKETS - REL_MAX_EXACT)).astype(np.int64)
    large = np.minimum(large, REL_BUCKETS - 1)
    return np.where(dist < REL_MAX_EXACT, dist, large).astype(np.int32)


def _group_bias(rel_bias, g):
    dil = A_GROUPS[g][1]
    buckets = _t5_bucket(np.arange(A_KEYS + 1) * dil)
    return rel_bias[buckets][:, g * A_HEADS:(g + 1) * A_HEADS].T.astype(F32)


def _prompt_bias_table(bias):
    assert A_KEYS == A_BLOCK
    h = bias.shape[0]
    n = 2 * A_BLOCK
    v = jnp.concatenate([bias[:, ::-1], jnp.full((h, n - A_KEYS), -jnp.inf, F32)], axis=1)
    return jnp.tile(v, (1, A_BLOCK))[:, :A_BLOCK * n].reshape(h, A_BLOCK, n)


def _sample_bias_tables(bias, wb, dil, t):
    h = bias.shape[0]
    front = t - 1 + wb - A_KEYS * dil
    assert front >= 0
    sparse = jnp.concatenate([bias[:, ::-1, None], jnp.full((h, A_KEYS + 1, dil - 1), -jnp.inf, F32)], axis=2)
    base = jnp.concatenate([jnp.full((h, front), -jnp.inf, F32), sparse.reshape(h, (A_KEYS + 1) * dil)], axis=1)
    tab_buf = jnp.stack([base[:, t - 1 - tq:t - 1 - tq + wb] for tq in range(t)], axis=1)
    tq = np.arange(t)[:, None]
    dist = tq - np.arange(LANES)[None, :]
    valid = (dist >= 0) & (dist % dil == 0) & (np.arange(LANES)[None, :] < t)
    tab_new = jnp.where(valid[None], bias[:, np.clip(dist // dil, 0, A_KEYS)], -jnp.inf)
    return tab_buf, tab_new


def _mixer_weights(w_in):
    na = len(A_GROUPS) * A_WIDTH
    d = w_in.shape[0]
    offs = np.cumsum([0, na, na, na, 1024, 1536, 16, 1024, 1024, 3 * d])
    seg = lambda i: w_in[:, offs[i]:offs[i + 1]]
    qa, ka, va, zb, xbc, dtb, xc, gc, gates = [seg(i) for i in range(9)]
    cols = []
    for g in range(len(A_GROUPS)):
        sl = slice(g * A_WIDTH, (g + 1) * A_WIDTH)
        cols += [qa[:, sl] * (A_HEAD_DIM ** -0.5), ka[:, sl], va[:, sl]]
    w_qkv = jnp.concatenate(cols + [dtb, jnp.zeros((d, PROJ_TN - dtb.shape[1]), w_in.dtype)], axis=1)
    w_big = jnp.concatenate([zb, xc, gc, gates, xbc], axis=1)
    return w_qkv.astype(BF16), w_big.astype(BF16)


def _mixer(x, mod, lw, st, bias, prompt, batch, seq):
    act_dtype = BF16 if prompt else F32
    qkv = _proj(x, mod, lw["g_mix"], lw["w_qkv"], F32)
    big = _proj(x, mod, lw["g_mix"], lw["w_big"], act_dtype)
    ng = len(A_GROUPS)
    if prompt:
        oa, new_kv = _attn_prompt(qkv, bias, st["kvt"], st["layer"], st["depth"], batch, seq, act_dtype)
    else:
        oa = _attn_sample(qkv, st["kv"], st["layer"], bias, batch, seq)
        new_kv = [qkv[:, (3 * g + 1) * A_WIDTH:(3 * g + 3) * A_WIDTH].reshape(batch, seq, 2, A_HEADS, A_HEAD_DIM)
                  for g in range(ng)]
    dt_col = 3 * ng * A_WIDTH // LANES
    yb, conv_b_new, ssm_new = _ssd(big, qkv, dt_col, st["conv_b"], st["ssm"], lw, batch, seq, act_dtype)
    yc, conv_c_new, lru_new = _lru(big, st["conv_c"], st["lru"], lw, batch, seq, act_dtype)
    x = _merge(x, mod, oa, yb, yc, big, lw)
    return x, (new_kv[0], new_kv[1], new_kv[2], conv_b_new, ssm_new, conv_c_new, lru_new)


def _block(x, mod, lw, st, bias, prompt, batch, seq, g_final):
    x = _ffn(x, mod, 0, lw["g_ff1"], lw["w_ff1_in"], lw["w_ff1_out"])
    x, new_st = _mixer(x, mod, lw, st, bias, prompt, batch, seq)
    x = _ffn(x, mod, 6, lw["g_ff2"], lw["w_ff2_in"], lw["w_ff2_out"], g_final)
    return x, new_st


@jax.jit
def _forward(x_prompt, x_sample, c_prompt, c_sample, cache_win1_kv, cache_win2_kv, cache_win3_kv,
             state_conv_b, state_ssm, state_conv_c, state_lru, rel_bias, w_ada, b_ada, g_ff1,
             w_ff1_in, w_ff1_out, g_mix, w_in, w_a_proj, conv_b_w, conv_b_b, dt_bias, a_log, d_skip,
             g_ssm_norm, w_b_proj, conv_c_w, conv_c_b, w_rgate, b_rgate, w_igate, b_igate, lru_lambda,
             w_c_proj, w_out, g_ff2, w_ff2_in, w_ff2_out, g_final):
    bp, lp, d = x_prompt.shape
    bs, ls, _ = x_sample.shape
    depth = w_ada.shape[0]
    caches = (cache_win1_kv, cache_win2_kv, cache_win3_kv)
    biases = [_group_bias(rel_bias, g) for g in range(len(A_GROUPS))]
    bias_p = jnp.stack([_prompt_bias_table(b) for b in biases], 0)
    bias_s = [_sample_bias_tables(b, caches[g].shape[2], A_GROUPS[g][1], ls) for g, b in enumerate(biases)]

    yp = x_prompt.reshape(bp * lp, d)
    ys = x_sample.reshape(bs * ls, d)
    c_all = jnp.concatenate([c_prompt, c_sample], axis=0)
    new_p = [[] for _ in range(7)]
    new_s = [[] for _ in range(7)]
    kvt_p = None
    for l in range(depth):
        w_qkv, w_big = _mixer_weights(w_in[l])
        lw = dict(g_ff1=g_ff1[l], w_ff1_in=w_ff1_in[l].astype(BF16), w_ff1_out=w_ff1_out[l].astype(BF16),
                  g_mix=g_mix[l], w_qkv=w_qkv, w_big=w_big, w_a_proj=w_a_proj[l].astype(BF16),
                  conv_b_w=conv_b_w[l], conv_b_b=conv_b_b[l], dt_bias=dt_bias[l], a_log=a_log[l],
                  d_skip=d_skip[l], g_ssm_norm=g_ssm_norm[l], w_b_proj=w_b_proj[l].astype(BF16),
                  conv_c_w=conv_c_w[l], conv_c_b=conv_c_b[l], w_rgate=w_rgate[l], b_rgate=b_rgate[l],
                  w_igate=w_igate[l], b_igate=b_igate[l], lru_lambda=lru_lambda[l],
                  w_c_proj=w_c_proj[l].astype(BF16), w_out=w_out[l].astype(BF16),
                  g_ff2=g_ff2[l], w_ff2_in=w_ff2_in[l].astype(BF16), w_ff2_out=w_ff2_out[l].astype(BF16))
        mod_all = _ada(c_all, w_ada, b_ada, l)
        mod_p = _Mod(mod_all[:bp], bp, lp, per_row=False)
        mod_s = _Mod(mod_all[bp:], bs, ls, per_row=True)
        gf = g_final if l == depth - 1 else None
        st_p = dict(kvt=kvt_p, layer=l, depth=depth,
                    conv_b=jnp.zeros((bp, B_CONV - 1, conv_b_w.shape[2]), F32),
                    ssm=jnp.zeros((bp,) + state_ssm.shape[2:], F32),
                    conv_c=jnp.zeros((bp, C_CONV - 1, conv_c_w.shape[2]), F32),
                    lru=jnp.zeros((bp, state_lru.shape[2]), F32))
        st_s = dict(kv=caches, layer=l, conv_b=state_conv_b[l], ssm=state_ssm[l],
                    conv_c=state_conv_c[l], lru=state_lru[l])
        yp, stp = _block(yp, mod_p, lw, st_p, bias_p, True, bp, lp, gf)
        ys, sts = _block(ys, mod_s, lw, st_s, bias_s, False, bs, ls, gf)
        kvt_p = list(stp[:3])
        for i in range(7):
            new_p[i].append(stp[i])
            new_s[i].append(sts[i])
    outs_p = [jnp.transpose(v, (0, 1, 5, 2, 3, 4)) for v in kvt_p] + [jnp.stack(v, 0) for v in new_p[3:]]
    outs_s = [jnp.stack(v, 0) for v in new_s]
    return (yp.reshape(bp, lp, d), ys.reshape(bs, ls, d), *outs_p, *outs_s)


def kernel(x_prompt, x_sample, c_prompt, c_sample, cache_win1_kv, cache_win2_kv, cache_win3_kv,
           state_conv_b, state_ssm, state_conv_c, state_lru, rel_bias, w_ada, b_ada, g_ff1,
           w_ff1_in, w_ff1_out, g_mix, w_in, w_a_proj, conv_b_w, conv_b_b, dt_bias, a_log, d_skip,
           g_ssm_norm, w_b_proj, conv_c_w, conv_c_b, w_rgate, b_rgate, w_igate, b_igate, lru_lambda,
           w_c_proj, w_out, g_ff2, w_ff2_in, w_ff2_out, g_final):
    return _forward(x_prompt, x_sample, c_prompt, c_sample, cache_win1_kv, cache_win2_kv,
                    cache_win3_kv, state_conv_b, state_ssm, state_conv_c, state_lru, rel_bias, w_ada,
                    b_ada, g_ff1, w_ff1_in, w_ff1_out, g_mix, w_in, w_a_proj, conv_b_w, conv_b_b,
                    dt_bias, a_log, d_skip, g_ssm_norm, w_b_proj, conv_c_w, conv_c_b, w_rgate,
                    b_rgate, w_igate, b_igate, lru_lambda, w_c_proj, w_out, g_ff2, w_ff2_in,
                    w_ff2_out, g_final)
```

```python
import functools
import math

import jax
import jax.numpy as jnp
import numpy as np
from jax import lax
from jax.experimental import pallas as pl
from jax.experimental.pallas import tpu as pltpu

F32 = jnp.float32
BF16 = jnp.bfloat16

NORM_EPS = 1e-6
N_MOD = 9
A_GROUPS = ((128, 1), (512, 4), (2048, 16))
A_HEADS = 8
A_HEAD_DIM = 64
A_WIDTH = A_HEADS * A_HEAD_DIM
A_KEYS = 128
A_BLOCK = 128
REL_BUCKETS = 32
REL_MAX_EXACT = 16
REL_MAX_DISTANCE = 2048
B_HEAD_DIM = 64
B_GROUPS = 2
B_STATE = 128
B_CONV = 4
C_BLOCKS = 8
C_CONV = 4
C_POW = 8.0

LANES = 128
SUBLANES = 8
SSD_CHUNK = 128
FFN_TM = 1024
PROJ_TM = 2048
MERGE_TM = 512
PROJ_TN = 512
ATTN_UNROLL = 4
VMEM_LIMIT = 56 * 1024 * 1024


def _cparams(*sem):
    return pltpu.CompilerParams(dimension_semantics=sem, vmem_limit_bytes=VMEM_LIMIT)


def _resident(shape):
    nd = len(shape)
    return pl.BlockSpec(shape, lambda *_: (0,) * nd, pipeline_mode=pl.Buffered(1))


def _ld(ref):
    return ref[0] if len(ref.shape) == 3 else ref[...]


def _norm_mod(x, g, sc, sh):
    ms = jnp.mean(x * x, axis=-1, keepdims=True)
    return (x * lax.rsqrt(ms + NORM_EPS) * g) * (1.0 + sc) + sh


def _softplus(x):
    return jnp.maximum(x, 0.0) + jnp.log1p(jnp.exp(-jnp.abs(x)))


def _split_bf16(v, n):
    parts = []
    r = v
    for _ in range(n):
        p = r.astype(BF16)
        parts.append(p)
        r = r - p.astype(F32)
    return parts


def _dot(a, b):
    return jnp.dot(a, b, preferred_element_type=F32)


def _dot_nt(a, b):
    return lax.dot_general(a, b, (((1,), (1,)), ((), ())), preferred_element_type=F32)


def _ada_kernel(c_ref, w_ref, b_ref, o_ref):
    c = c_ref[...]
    a = (c * jax.nn.sigmoid(c)).astype(BF16)
    o_ref[...] = _dot(a, w_ref[...].astype(BF16)) + b_ref[...]


def _ada(c, w, b, layer):
    rows, d = c.shape
    depth, _, n = w.shape
    tn = 1024
    return pl.pallas_call(
        _ada_kernel,
        grid=(n // tn,),
        in_specs=[pl.BlockSpec((rows, d), lambda j: (0, 0)),
                  pl.BlockSpec((None, d, tn), lambda j: (layer, 0, j)),
                  pl.BlockSpec((None, 1, tn), lambda j: (layer, 0, j))],
        out_specs=pl.BlockSpec((rows, tn), lambda j: (0, j)),
        out_shape=jax.ShapeDtypeStruct((rows, n), F32),
        compiler_params=_cparams("parallel"),
        name="ada",
    )(c, w, b.reshape(depth, 1, n))


class _Mod:
    def __init__(self, mod, batch, seq, per_row):
        d = mod.shape[1] // N_MOD
        self.d = d
        self.seq = seq
        self.per_row = per_row
        if per_row:
            self.arr = jnp.repeat(mod, seq, axis=0)
        else:
            self.arr = mod.reshape(batch * N_MOD, 1, d)

    def spec(self, k, tm, grid_rank):
        d = self.d
        if self.per_row:
            if grid_rank == 1:
                return pl.BlockSpec((tm, d), lambda i: (i, k))
            return pl.BlockSpec((tm, d), lambda i, j: (i, k))
        per = self.seq // tm
        if grid_rank == 1:
            return pl.BlockSpec((1, 1, d), lambda i: ((i // per) * N_MOD + k, 0, 0))
        return pl.BlockSpec((1, 1, d), lambda i, j: ((i // per) * N_MOD + k, 0, 0))


def _row_tile(m, seq, per_row, want):
    tm = min(want, m if per_row else seq)
    assert m % tm == 0 and (per_row or seq % tm == 0)
    return tm


def _ffn_kernel(x_ref, sh_ref, sc_ref, gt_ref, gn_ref, win_ref, wout_ref, *rest, d_ff, fc, final):
    if final:
        gf_ref, o_ref, acc_ref = rest
    else:
        o_ref, acc_ref = rest
    x = x_ref[...]
    h = _norm_mod(x, gn_ref[...], _ld(sc_ref), _ld(sh_ref)).astype(BF16)
    for c in range(d_ff // fc):
        u = _dot(h, win_ref[:, c * fc:(c + 1) * fc])
        v = _dot(h, win_ref[:, d_ff + c * fc:d_ff + (c + 1) * fc])
        a = (u * jax.nn.sigmoid(u) * v).astype(BF16)
        part = _dot(a, wout_ref[c * fc:(c + 1) * fc, :])
        if c == 0:
            acc_ref[...] = part
        else:
            acc_ref[...] += part
    y = x + 0.5 * _ld(gt_ref) * acc_ref[...]
    if final:
        ms = jnp.mean(y * y, axis=-1, keepdims=True)
        y = y * lax.rsqrt(ms + NORM_EPS) * gf_ref[...]
    o_ref[...] = y


def _ffn(x, mod, k0, gn, w_in, w_out, g_final=None):
    m, d = x.shape
    d_ff = w_out.shape[0]
    tm = _row_tile(m, mod.seq, mod.per_row, FFN_TM)
    final = g_final is not None
    in_specs = [pl.BlockSpec((tm, d), lambda i: (i, 0)),
                mod.spec(k0, tm, 1), mod.spec(k0 + 1, tm, 1), mod.spec(k0 + 2, tm, 1),
                _resident((1, d)), _resident(w_in.shape), _resident(w_out.shape)]
    args = [x, mod.arr, mod.arr, mod.arr, gn.reshape(1, d), w_in, w_out]
    if final:
        in_specs.append(_resident((1, d)))
        args.append(g_final.reshape(1, d))
    return pl.pallas_call(
        functools.partial(_ffn_kernel, d_ff=d_ff, fc=256, final=final),
        grid=(m // tm,),
        in_specs=in_specs,
        out_specs=pl.BlockSpec((tm, d), lambda i: (i, 0)),
        out_shape=jax.ShapeDtypeStruct((m, d), F32),
        scratch_shapes=[pltpu.VMEM((tm, d), F32)],
        compiler_params=_cparams("parallel"),
        name="ffn",
    )(*args)


def _proj_kernel(x_ref, sh_ref, sc_ref, gn_ref, w_ref, o_ref, h_ref):
    @pl.when(pl.program_id(1) == 0)
    def _():
        h_ref[...] = _norm_mod(x_ref[...], gn_ref[...], _ld(sc_ref), _ld(sh_ref)).astype(BF16)

    o_ref[...] = _dot(h_ref[...], w_ref[...]).astype(o_ref.dtype)


def _proj(x, mod, gn, w, out_dtype):
    m, d = x.shape
    n = w.shape[1]
    tm = _row_tile(m, mod.seq, mod.per_row, PROJ_TM)
    tn = PROJ_TN
    return pl.pallas_call(
        _proj_kernel,
        grid=(m // tm, n // tn),
        in_specs=[pl.BlockSpec((tm, d), lambda i, j: (i, 0)),
                  mod.spec(3, tm, 2), mod.spec(4, tm, 2),
                  pl.BlockSpec((1, d), lambda i, j: (0, 0)),
                  pl.BlockSpec((d, tn), lambda i, j: (0, j))],
        out_specs=pl.BlockSpec((tm, tn), lambda i, j: (i, j)),
        out_shape=jax.ShapeDtypeStruct((m, n), out_dtype),
        scratch_shapes=[pltpu.VMEM((tm, d), BF16)],
        compiler_params=_cparams("parallel", "arbitrary"),
        name="proj",
    )(x, mod.arr, mod.arr, gn.reshape(1, d), w)


def _pair_softmax(q2, k2, v2, bias_of, lo, transposed=False):
    zero = jnp.zeros_like(q2)
    res = []
    for i, qh in enumerate((jnp.where(lo, q2, zero), jnp.where(lo, zero, q2))):
        s = (_dot(qh, k2) if transposed else _dot_nt(qh, k2)) + bias_of(i)
        mx = jnp.max(s, axis=-1, keepdims=True)
        p = jnp.exp(s - mx)
        pb = p.astype(BF16)
        o = _dot_nt(pb, v2) if transposed else _dot(pb, v2)
        res.append((o, mx, jnp.sum(p, axis=-1, keepdims=True)))
    (oa, ma, la), (ob, mb, lb) = res
    return jnp.where(lo, oa, ob), jnp.where(lo, ma, mb), jnp.where(lo, la, lb)


def _merge_softmax(acc, new):
    ao, am, al = acc
    o, m, l = new
    mn = jnp.maximum(am, m)
    a1 = jnp.exp(am - mn)
    a2 = jnp.exp(m - mn)
    return ao * a1 + o * a2, mn, al * a1 + l * a2


def _attn_prompt_kernel(*refs, seq, n_alias):
    ng = len(A_GROUPS)
    qkv = refs[:3 * ng]
    bias_ref = refs[3 * ng]
    o_ref = refs[3 * ng + 1 + n_alias]
    kvt_refs = refs[3 * ng + 2 + n_alias:3 * ng + 2 + n_alias + ng]
    acc_o, acc_m, acc_l = refs[-3:]
    blk = A_BLOCK

    for g in range(ng):
        keep = kvt_refs[g].shape[-1]
        for kv in range(2):
            src = qkv[3 * g + 1 + kv]
            for c in range(keep // blk):
                t = src[pl.ds(seq - keep + c * blk, blk), :].T
                for i in range(2):
                    kvt_refs[g][kv, i, :, c * blk:(c + 1) * blk] = t[i * A_HEAD_DIM:(i + 1) * A_HEAD_DIM]

    hp = pl.program_id(1)
    lo = lax.broadcasted_iota(jnp.int32, (blk, LANES), 1) < A_HEAD_DIM

    def tiles(g, starts, kw):
        dil = A_GROUPS[g][1]
        q_ref, k_ref, v_ref = qkv[3 * g:3 * g + 3]

        def rows(start, n):
            return pl.ds(start, n) if dil == 1 else pl.ds(start, n, stride=dil)

        if kw == blk:
            bias_of = lambda i: bias_ref[g, 2 * hp + i, :, blk:2 * blk]
        else:
            bias_of = lambda i: bias_ref[g, 2 * hp + i]
        news = []
        for qstart, kstart in starts:
            q2 = q_ref[rows(qstart, blk), :].astype(BF16)
            k2 = k_ref[rows(kstart, kw), :].astype(BF16)
            v2 = v_ref[rows(kstart, kw), :].astype(BF16)
            news.append(_pair_softmax(q2, k2, v2, bias_of, lo))
        for (qstart, _), new in zip(starts, news):
            sel = rows(qstart, blk)
            if g > 0:
                new = _merge_softmax((acc_o[sel, :], acc_m[sel, :], acc_l[sel, :]), new)
            acc_o[sel, :] = new[0]
            acc_m[sel, :] = new[1]
            acc_l[sel, :] = new[2]

    for g, (_, dil) in enumerate(A_GROUPS):
        nb = seq // dil // blk
        span = blk * dil
        ur = min(dil, ATTN_UNROLL)
        ub = 1 if dil > 1 else max(u for u in range(1, ATTN_UNROLL + 1) if (nb - 1) % u == 0)

        def first(i, carry, g=g, ur=ur):
            tiles(g, [(i * ur + j, i * ur + j) for j in range(ur)], blk)
            return carry

        def rest(i, carry, g=g, span=span, ur=ur, ub=ub, nr=dil // ur):
            b = 1 + (i // nr) * ub
            r = (i % nr) * ur
            tiles(g, [(r + j + (b + k) * span, r + j + (b + k - 1) * span)
                      for k in range(ub) for j in range(ur)], 2 * blk)
            return carry

        lax.fori_loop(0, dil // ur, first, 0)
        if nb > 1:
            lax.fori_loop(0, (nb - 1) // ub * (dil // ur), rest, 0)
    o_ref[...] = (acc_o[...] / acc_l[...]).astype(o_ref.dtype)


def _attn_prompt(qkv, bias_tab, kvt_prev, layer, depth, batch, seq, o_dtype):
    npair = A_WIDTH // LANES
    ng = len(A_GROUPS)
    col = lambda j: pl.BlockSpec((seq, LANES), lambda b, h: (b, j * npair + h))
    keeps = [min(win, seq) for win, _ in A_GROUPS]
    n_alias = 0 if kvt_prev is None else ng
    alias_specs = [pl.BlockSpec(memory_space=pl.ANY)] * n_alias
    alias_args = [] if kvt_prev is None else list(kvt_prev)
    n_in = 3 * ng + 1
    outs = pl.pallas_call(
        functools.partial(_attn_prompt_kernel, seq=seq, n_alias=n_alias),
        grid=(batch, npair),
        in_specs=[col(j) for j in range(3 * ng)]
        + [pl.BlockSpec(bias_tab.shape, lambda b, h: (0, 0, 0, 0))] + alias_specs,
        out_specs=[pl.BlockSpec((seq, LANES), lambda b, h: (b, h))]
        + [pl.BlockSpec((None, None, 2, 2, A_HEAD_DIM, keep), lambda b, h: (layer, b, 0, h, 0, 0))
           for keep in keeps],
        out_shape=[jax.ShapeDtypeStruct((batch * seq, A_WIDTH), o_dtype)]
        + [jax.ShapeDtypeStruct((depth, batch, 2, A_HEADS, A_HEAD_DIM, keep), F32) for keep in keeps],
        input_output_aliases={n_in + g: 1 + g for g in range(n_alias)},
        scratch_shapes=[pltpu.VMEM((seq, LANES), F32)] * 3,
        compiler_params=_cparams("parallel", "parallel"),
        name="attn_prompt",
    )(*([qkv] * (3 * ng)), bias_tab, *alias_args)
    return outs[0], list(outs[1:])


def _attn_sample_kernel(qkv_ref, buf1_ref, buf2_ref, buf3_ref, bb1, bn1, bb2, bn2, bb3, bn3,
                        o_ref, knew_scr, vnew_scr):
    t = qkv_ref.shape[0]
    aw = A_WIDTH
    bufs = (buf1_ref, buf2_ref, buf3_ref)
    bias = ((bb1, bn1), (bb2, bn2), (bb3, bn3))
    lo = lax.broadcasted_iota(jnp.int32, (t, LANES), 1) < A_HEAD_DIM
    knew_scr[...] = jnp.zeros_like(knew_scr)
    vnew_scr[...] = jnp.zeros_like(vnew_scr)
    for g in range(len(A_GROUPS)):
        knew_scr[g, 0:t, :] = qkv_ref[:, (3 * g + 1) * aw:(3 * g + 2) * aw]
        vnew_scr[g, 0:t, :] = qkv_ref[:, (3 * g + 2) * aw:(3 * g + 3) * aw]
    for hp in range(A_HEADS // 2):
        cols = slice(hp * LANES, (hp + 1) * LANES)
        acc = None
        for g in range(len(A_GROUPS)):
            q2 = qkv_ref[:, 3 * g * aw + hp * LANES:3 * g * aw + (hp + 1) * LANES].astype(BF16)
            wb = bufs[g].shape[-1]
            kb = bufs[g][0, 2 * hp:2 * hp + 2].reshape(LANES, wb).astype(BF16)
            vb = bufs[g][1, 2 * hp:2 * hp + 2].reshape(LANES, wb).astype(BF16)
            kn = knew_scr[g, :, cols].astype(BF16)
            vn = vnew_scr[g, :, cols].astype(BF16)
            bbuf, bnew = bias[g]
            new = _pair_softmax(q2, kb, vb, lambda i: bbuf[2 * hp + i], lo, transposed=True)
            new = _merge_softmax(new, _pair_softmax(q2, kn, vn, lambda i: bnew[2 * hp + i], lo))
            acc = new if acc is None else _merge_softmax(acc, new)
        o_ref[:, cols] = acc[0] / acc[2]


def _attn_sample(qkv, caches, layer, bias_s, batch, t):
    aw = A_WIDTH
    ng = len(A_GROUPS)
    bufs = [jnp.transpose(c, (0, 1, 3, 4, 5, 2)) for c in caches]
    buf_specs = [pl.BlockSpec((None, None) + bv.shape[2:], lambda b: (layer, b, 0, 0, 0, 0)) for bv in bufs]
    tabs = [tab for pair in bias_s for tab in pair]
    return pl.pallas_call(
        _attn_sample_kernel,
        grid=(batch,),
        in_specs=[pl.BlockSpec((t, 3 * ng * aw), lambda b: (b, 0))] + buf_specs
        + [pl.BlockSpec(tab.shape, lambda b: (0, 0, 0)) for tab in tabs],
        out_specs=pl.BlockSpec((t, aw), lambda b: (b, 0)),
        out_shape=jax.ShapeDtypeStruct((batch * t, aw), F32),
        scratch_shapes=[pltpu.VMEM((ng, LANES, aw), F32)] * 2,
        compiler_params=_cparams("parallel"),
        name="attn_sample",
    )(qkv, *bufs, *tabs)


def _ssd_kernel(xbc_ref, z_ref, dt_ref, conv0_ref, st0_ref, cw_ref, cb_ref, dtb_ref, aneg_ref,
                dsk_ref, gn_ref, e_ref, y_ref, convo_ref, sto_ref, xpad, st_t, y_scr, *, lv, nch):
    q = SSD_CHUNK
    c = pl.program_id(1)
    width = y_ref.shape[1]
    nst = B_STATE
    hpg = width // B_HEAD_DIM // B_GROUPS // 2

    @pl.when(c == 0)
    def _():
        xpad[...] = jnp.zeros_like(xpad)
        xpad[0:SUBLANES, :] = conv0_ref[0]
        for j in range(width // LANES):
            st_t[:, j * LANES:(j + 1) * LANES] = st0_ref[0, j * LANES:(j + 1) * LANES, :].T

    @pl.when(c > 0)
    def _():
        xpad[0:SUBLANES, :] = xpad[q:q + SUBLANES, :]

    xpad[SUBLANES:SUBLANES + lv, :] = xbc_ref[...].astype(F32)

    w = cw_ref[...]
    conv = cb_ref[...]
    for k in range(B_CONV):
        off = SUBLANES - (B_CONV - 1) + k
        conv = conv + xpad[off:off + q, :] * w[k:k + 1, :]
    act = conv * jax.nn.sigmoid(conv)
    xs = act[:, :width]
    bm = [act[:, width + gi * nst:width + (gi + 1) * nst] for gi in range(B_GROUPS)]
    cm = [act[:, width + (B_GROUPS + gi) * nst:width + (B_GROUPS + gi + 1) * nst] for gi in range(B_GROUPS)]

    dt_raw = dt_ref[...]
    if lv < q:
        dt_raw = jnp.concatenate([dt_raw, jnp.zeros((q - lv, LANES), F32)], axis=0)
    dt = _softplus(dt_raw + dtb_ref[...])
    if lv < q:
        dt = jnp.where(lax.broadcasted_iota(jnp.int32, (q, LANES), 0) < lv, dt, 0.0)
    dta = dt * aneg_ref[...]

    row = lax.broadcasted_iota(jnp.int32, (q, q), 0)
    col = lax.broadcasted_iota(jnp.int32, (q, q), 1)
    causal = row >= col
    tri = jnp.where(causal, 1.0, 0.0).astype(BF16)
    acum = sum(_dot(tri, p) for p in _split_bf16(dta, 3))
    acum_t = acum.T
    e = e_ref[...]
    dt_x = sum(_dot(p, e) for p in _split_bf16(dt, 2))
    acum_x = sum(_dot(p, e) for p in _split_bf16(acum, 2))
    last_x = acum_x[q - 1:q, :]
    ea_x = jnp.exp(acum_x)
    xdt = xs * dt_x
    xdt_te = (xdt * jnp.exp(last_x - acum_x)).astype(BF16)
    xdt_b = xdt.astype(BF16)
    chunk_decay = jnp.exp(last_x)
    lo = lax.broadcasted_iota(jnp.int32, (q, LANES), 1) < B_HEAD_DIM

    for gi in range(B_GROUPS):
        cmb = cm[gi].astype(BF16)
        cb = _dot_nt(cmb, bm[gi].astype(BF16))
        bm_t = bm[gi].T.astype(BF16)
        for hp in range(gi * hpg, (gi + 1) * hpg):
            cols = slice(hp * LANES, (hp + 1) * LANES)
            ax = acum_x[:, cols]
            ax_r = pltpu.roll(ax, B_HEAD_DIM, axis=1)
            ys = []
            for col_v, h in ((jnp.where(lo, ax, ax_r), 2 * hp), (jnp.where(lo, ax_r, ax), 2 * hp + 1)):
                seg = col_v - acum_t[h:h + 1, :]
                dec = jnp.exp(jnp.where(causal, seg, -jnp.inf))
                ys.append(_dot((cb * dec).astype(BF16), xdt_b[:, cols]))
            st_old = st_t[:, cols]
            y_off = _dot(cmb, st_old.astype(BF16)) * ea_x[:, cols]
            st_t[:, cols] = st_old * chunk_decay[:, cols] + _dot(bm_t, xdt_te[:, cols])
            y_scr[:, cols] = jnp.where(lo, ys[0], ys[1]) + y_off + dsk_ref[:, cols] * xs[:, cols]

    z = z_ref[...].astype(F32)
    if lv < q:
        z = jnp.concatenate([z, jnp.zeros((q - lv, width), F32)], axis=0)
    y = y_scr[...] * (z * jax.nn.sigmoid(z))
    gw = width // B_GROUPS
    for gi in range(B_GROUPS):
        yg = y[:, gi * gw:(gi + 1) * gw]
        ms = jnp.mean(yg * yg, axis=-1, keepdims=True)
        yn = yg * lax.rsqrt(ms + NORM_EPS) * gn_ref[:, gi * gw:(gi + 1) * gw]
        y_ref[:, gi * gw:(gi + 1) * gw] = yn[0:lv].astype(y_ref.dtype)

    @pl.when(c == nch - 1)
    def _():
        convo_ref[0] = xpad[lv:lv + SUBLANES, :]
        for j in range(width // LANES):
            sto_ref[0, j * LANES:(j + 1) * LANES, :] = st_t[:, j * LANES:(j + 1) * LANES].T


def _ssd(big, qkv, dt_col, conv0, st0, lw, batch, seq, o_dtype):
    width = lw["w_b_proj"].shape[0]
    cch = lw["conv_b_w"].shape[1]
    nheads = width // B_HEAD_DIM
    lv = min(seq, SSD_CHUNK)
    nch = seq // lv
    assert lv == SSD_CHUNK or nch == 1
    pad = lambda v: jnp.pad(v.astype(F32), (0, LANES - nheads)).reshape(1, LANES)
    e = (np.arange(LANES)[:, None] == (np.arange(width)[None, :] // B_HEAD_DIM)).astype(np.float32)
    conv0p = jnp.pad(conv0, ((0, 0), (SUBLANES - (B_CONV - 1), 0), (0, 0)))
    y, convo, sto = pl.pallas_call(
        functools.partial(_ssd_kernel, lv=lv, nch=nch),
        grid=(batch, nch),
        in_specs=[pl.BlockSpec((lv, cch), lambda b, c: (b * nch + c, 4)),
                  pl.BlockSpec((lv, width), lambda b, c: (b * nch + c, 0)),
                  pl.BlockSpec((lv, LANES), lambda b, c: (b * nch + c, dt_col)),
                  pl.BlockSpec((1, SUBLANES, cch), lambda b, c: (b, 0, 0)),
                  pl.BlockSpec((1, width, B_STATE), lambda b, c: (b, 0, 0)),
                  pl.BlockSpec((B_CONV, cch), lambda b, c: (0, 0)),
                  pl.BlockSpec((1, cch), lambda b, c: (0, 0)),
                  pl.BlockSpec((1, LANES), lambda b, c: (0, 0)),
                  pl.BlockSpec((1, LANES), lambda b, c: (0, 0)),
                  pl.BlockSpec((1, width), lambda b, c: (0, 0)),
                  pl.BlockSpec((1, width), lambda b, c: (0, 0)),
                  pl.BlockSpec((LANES, width), lambda b, c: (0, 0))],
        out_specs=[pl.BlockSpec((lv, width), lambda b, c: (b * nch + c, 0)),
                   pl.BlockSpec((1, SUBLANES, cch), lambda b, c: (b, 0, 0)),
                   pl.BlockSpec((1, width, B_STATE), lambda b, c: (b, 0, 0))],
        out_shape=[jax.ShapeDtypeStruct((batch * seq, width), o_dtype),
                   jax.ShapeDtypeStruct((batch, SUBLANES, cch), F32),
                   jax.ShapeDtypeStruct((batch, width, B_STATE), F32)],
        scratch_shapes=[pltpu.VMEM((SUBLANES + SSD_CHUNK, cch), F32),
                        pltpu.VMEM((B_STATE, width), F32),
                        pltpu.VMEM((SSD_CHUNK, width), F32)],
        compiler_params=_cparams("parallel", "arbitrary"),
        name="ssd",
    )(big, big, qkv, conv0p, st0.reshape(batch, width, B_STATE),
      lw["conv_b_w"], lw["conv_b_b"].reshape(1, cch), pad(lw["dt_bias"]),
      pad(-jnp.exp(lw["a_log"].astype(F32))),
      jnp.repeat(lw["d_skip"].astype(F32), B_HEAD_DIM).reshape(1, width),
      lw["g_ssm_norm"].reshape(1, width), jnp.asarray(e, BF16))
    return (y, convo[:, SUBLANES - (B_CONV - 1):, :],
            sto.reshape(batch, nheads, B_HEAD_DIM, B_STATE))


def _lru_kernel(xc_ref, gc_ref, conv0_ref, h0_ref, cw_ref, cb_ref, wr_ref, wi_ref, br_ref, bi_ref,
                lam_ref, y_ref, convo_ref, ho_ref, xpad, hprev, hs_scr, *, q, nch):
    c = pl.program_id(1)
    width = y_ref.shape[1]

    @pl.when(c == 0)
    def _():
        xpad[0:SUBLANES, :] = conv0_ref[0]
        hprev[...] = h0_ref[0]

    @pl.when(c > 0)
    def _():
        xpad[0:SUBLANES, :] = xpad[q:q + SUBLANES, :]

    xpad[SUBLANES:SUBLANES + q, :] = xc_ref[...].astype(F32)
    w = cw_ref[...]
    x = cb_ref[...]
    for k in range(C_CONV):
        off = SUBLANES - (C_CONV - 1) + k
        x = x + xpad[off:off + q, :] * w[k:k + 1, :]

    bd = width // C_BLOCKS
    rs, gs = [], []
    for j in range(C_BLOCKS):
        xb = x[:, j * bd:(j + 1) * bd].astype(BF16)
        rs.append(_dot(xb, wr_ref[j]))
        gs.append(_dot(xb, wi_ref[j]))
    rg = jax.nn.sigmoid(jnp.concatenate(rs, axis=1) + br_ref[...])
    ig = jax.nn.sigmoid(jnp.concatenate(gs, axis=1) + bi_ref[...])
    log_a = -C_POW * rg * _softplus(-lam_ref[...])
    a = jnp.exp(log_a)
    u = x * ig * jnp.sqrt(-jnp.tanh(log_a) * (a * a + 1.0))

    rowi = lax.broadcasted_iota(jnp.int32, (SUBLANES, width), 0)
    h = hprev[...]
    for g in range(q // SUBLANES):
        ag = a[g * SUBLANES:(g + 1) * SUBLANES]
        bg = u[g * SUBLANES:(g + 1) * SUBLANES]
        s = 1
        while s < SUBLANES:
            keep = rowi >= s
            a_sh = jnp.where(keep, pltpu.roll(ag, s, axis=0), 1.0)
            b_sh = jnp.where(keep, pltpu.roll(bg, s, axis=0), 0.0)
            bg = ag * b_sh + bg
            ag = ag * a_sh
            s *= 2
        hg = bg + ag * h
        hs_scr[g * SUBLANES:(g + 1) * SUBLANES, :] = hg
        h = hg[SUBLANES - 1:SUBLANES]
    hprev[...] = h
    y_ref[...] = (hs_scr[...] * jax.nn.gelu(gc_ref[...].astype(F32))).astype(y_ref.dtype)

    @pl.when(c == nch - 1)
    def _():
        convo_ref[0] = xpad[q:q + SUBLANES, :]
        ho_ref[0] = h


def _lru(big, conv0, h0, lw, batch, seq, o_dtype):
    width = lw["w_c_proj"].shape[0]
    q = min(seq, 128)
    nch = seq // q
    bd = width // C_BLOCKS
    conv0p = jnp.pad(conv0, ((0, 0), (SUBLANES - (C_CONV - 1), 0), (0, 0)))
    vec = lambda v: v.astype(F32).reshape(1, width)
    y, convo, ho = pl.pallas_call(
        functools.partial(_lru_kernel, q=q, nch=nch),
        grid=(batch, nch),
        in_specs=[pl.BlockSpec((q, width), lambda b, c: (b * nch + c, 1)),
                  pl.BlockSpec((q, width), lambda b, c: (b * nch + c, 2)),
                  pl.BlockSpec((1, SUBLANES, width), lambda b, c: (b, 0, 0)),
                  pl.BlockSpec((1, 1, width), lambda b, c: (b, 0, 0)),
                  pl.BlockSpec((C_CONV, width), lambda b, c: (0, 0)),
                  pl.BlockSpec((1, width), lambda b, c: (0, 0)),
                  pl.BlockSpec((C_BLOCKS, bd, bd), lambda b, c: (0, 0, 0)),
                  pl.BlockSpec((C_BLOCKS, bd, bd), lambda b, c: (0, 0, 0)),
                  pl.BlockSpec((1, width), lambda b, c: (0, 0)),
                  pl.BlockSpec((1, width), lambda b, c: (0, 0)),
                  pl.BlockSpec((1, width), lambda b, c: (0, 0))],
        out_specs=[pl.BlockSpec((q, width), lambda b, c: (b * nch + c, 0)),
                   pl.BlockSpec((1, SUBLANES, width), lambda b, c: (b, 0, 0)),
                   pl.BlockSpec((1, 1, width), lambda b, c: (b, 0, 0))],
        out_shape=[jax.ShapeDtypeStruct((batch * seq, width), o_dtype),
                   jax.ShapeDtypeStruct((batch, SUBLANES, width), F32),
                   jax.ShapeDtypeStruct((batch, 1, width), F32)],
        scratch_shapes=[pltpu.VMEM((SUBLANES + q, width), F32),
                        pltpu.VMEM((1, width), F32),
                        pltpu.VMEM((q, width), F32)],
        compiler_params=_cparams("parallel", "arbitrary"),
        name="lru",
    )(big, big, conv0p, h0.reshape(batch, 1, width), lw["conv_c_w"], vec(lw["conv_c_b"]),
      lw["w_rgate"].astype(BF16), lw["w_igate"].astype(BF16), vec(lw["b_rgate"]), vec(lw["b_igate"]),
      vec(lw["lru_lambda"]))
    return y, convo[:, SUBLANES - (C_CONV - 1):, :], ho.reshape(batch, width)


def _merge_kernel(x_ref, gt_ref, oa_ref, yb_ref, yc_ref, gates_ref, wa_ref, wb_ref, wc_ref, wo_ref,
                  out_ref):
    d = x_ref.shape[1]
    ya = _dot(oa_ref[...].astype(BF16), wa_ref[...])
    yb = _dot(yb_ref[...].astype(BF16), wb_ref[...])
    yc = _dot(yc_ref[...].astype(BF16), wc_ref[...])
    sg = jax.nn.sigmoid(gates_ref[...].astype(F32))
    mixed = sg[:, :d] * ya + sg[:, d:2 * d] * yb + sg[:, 2 * d:] * yc
    out_ref[...] = x_ref[...] + _ld(gt_ref) * _dot(mixed.astype(BF16), wo_ref[...])


def _merge(x, mod, oa, yb, yc, big, lw):
    m, d = x.shape
    aw = A_WIDTH
    tm = _row_tile(m, mod.seq, mod.per_row, MERGE_TM)
    row = lambda wd, j=0: pl.BlockSpec((tm, wd), lambda i: (i, j))
    return pl.pallas_call(
        _merge_kernel,
        grid=(m // tm,),
        in_specs=[row(d), mod.spec(5, tm, 1), row(aw), row(yb.shape[1]), row(yc.shape[1]), row(3 * d, 1)]
        + [_resident(lw[k].shape) for k in ("w_a_proj", "w_b_proj", "w_c_proj", "w_out")],
        out_specs=row(d),
        out_shape=jax.ShapeDtypeStruct((m, d), F32),
        compiler_params=_cparams("parallel"),
        name="merge",
    )(x, mod.arr, oa, yb, yc, big,
      lw["w_a_proj"], lw["w_b_proj"], lw["w_c_proj"], lw["w_out"])


def _t5_bucket(dist):
    dist = np.asarray(dist)
    large = REL_MAX_EXACT + (np.log(np.maximum(dist, 1) / REL_MAX_EXACT)
                             / math.log(REL_MAX_DISTANCE / REL_MAX_EXACT)
                             * (REL_BUCKETS - REL_MAX_EXACT)).astype(np.int64)
    large = np.minimum(large, REL_BUCKETS - 1)
    return np.where(dist < REL_MAX_EXACT, dist, large).astype(np.int32)


def _group_bias(rel_bias, g):
    dil = A_GROUPS[g][1]
    buckets = _t5_bucket(np.arange(A_KEYS + 1) * dil)
    return rel_bias[buckets][:, g * A_HEADS:(g + 1) * A_HEADS].T.astype(F32)


def _prompt_bias_table(bias):
    assert A_KEYS == A_BLOCK
    h = bias.shape[0]
    n = 2 * A_BLOCK
    v = jnp.concatenate([bias[:, ::-1], jnp.full((h, n - A_KEYS), -jnp.inf, F32)], axis=1)
    return jnp.tile(v, (1, A_BLOCK))[:, :A_BLOCK * n].reshape(h, A_BLOCK, n)


def _sample_bias_tables(bias, wb, dil, t):
    h = bias.shape[0]
    front = t - 1 + wb - A_KEYS * dil
    assert front >= 0
    sparse = jnp.concatenate([bias[:, ::-1, None], jnp.full((h, A_KEYS + 1, dil - 1), -jnp.inf, F32)], axis=2)
    base = jnp.concatenate([jnp.full((h, front), -jnp.inf, F32), sparse.reshape(h, (A_KEYS + 1) * dil)], axis=1)
    tab_buf = jnp.stack([base[:, t - 1 - tq:t - 1 - tq + wb] for tq in range(t)], axis=1)
    tq = np.arange(t)[:, None]
    dist = tq - np.arange(LANES)[None, :]
    valid = (dist >= 0) & (dist % dil == 0) & (np.arange(LANES)[None, :] < t)
    tab_new = jnp.where(valid[None], bias[:, np.clip(dist // dil, 0, A_KEYS)], -jnp.inf)
    return tab_buf, tab_new


def _mixer_weights(w_in):
    na = len(A_GROUPS) * A_WIDTH
    d = w_in.shape[0]
    offs = np.cumsum([0, na, na, na, 1024, 1536, 16, 1024, 1024, 3 * d])
    seg = lambda i: w_in[:, offs[i]:offs[i + 1]]
    qa, ka, va, zb, xbc, dtb, xc, gc, gates = [seg(i) for i in range(9)]
    cols = []
    for g in range(len(A_GROUPS)):
        sl = slice(g * A_WIDTH, (g + 1) * A_WIDTH)
        cols += [qa[:, sl] * (A_HEAD_DIM ** -0.5), ka[:, sl], va[:, sl]]
    w_qkv = jnp.concatenate(cols + [dtb, jnp.zeros((d, PROJ_TN - dtb.shape[1]), w_in.dtype)], axis=1)
    w_big = jnp.concatenate([zb, xc, gc, gates, xbc], axis=1)
    return w_qkv.astype(BF16), w_big.astype(BF16)


def _mixer(x, mod, lw, st, bias, prompt, batch, seq):
    act_dtype = BF16 if prompt else F32
    qkv = _proj(x, mod, lw["g_mix"], lw["w_qkv"], F32)
    big = _proj(x, mod, lw["g_mix"], lw["w_big"], act_dtype)
    ng = len(A_GROUPS)
    if prompt:
        oa, new_kv = _attn_prompt(qkv, bias, st["kvt"], st["layer"], st["depth"], batch, seq, act_dtype)
    else:
        oa = _attn_sample(qkv, st["kv"], st["layer"], bias, batch, seq)
        new_kv = [qkv[:, (3 * g + 1) * A_WIDTH:(3 * g + 3) * A_WIDTH].reshape(batch, seq, 2, A_HEADS, A_HEAD_DIM)
                  for g in range(ng)]
    dt_col = 3 * ng * A_WIDTH // LANES
    yb, conv_b_new, ssm_new = _ssd(big, qkv, dt_col, st["conv_b"], st["ssm"], lw, batch, seq, act_dtype)
    yc, conv_c_new, lru_new = _lru(big, st["conv_c"], st["lru"], lw, batch, seq, act_dtype)
    x = _merge(x, mod, oa, yb, yc, big, lw)
    return x, (new_kv[0], new_kv[1], new_kv[2], conv_b_new, ssm_new, conv_c_new, lru_new)


def _block(x, mod, lw, st, bias, prompt, batch, seq, g_final):
    x = _ffn(x, mod, 0, lw["g_ff1"], lw["w_ff1_in"], lw["w_ff1_out"])
    x, new_st = _mixer(x, mod, lw, st, bias, prompt, batch, seq)
    x = _ffn(x, mod, 6, lw["g_ff2"], lw["w_ff2_in"], lw["w_ff2_out"], g_final)
    return x, new_st


@jax.jit
def _forward(x_prompt, x_sample, c_prompt, c_sample, cache_win1_kv, cache_win2_kv, cache_win3_kv,
             state_conv_b, state_ssm, state_conv_c, state_lru, rel_bias, w_ada, b_ada, g_ff1,
             w_ff1_in, w_ff1_out, g_mix, w_in, w_a_proj, conv_b_w, conv_b_b, dt_bias, a_log, d_skip,
             g_ssm_norm, w_b_proj, conv_c_w, conv_c_b, w_rgate, b_rgate, w_igate, b_igate, lru_lambda,
             w_c_proj, w_out, g_ff2, w_ff2_in, w_ff2_out, g_final):
    bp, lp, d = x_prompt.shape
    bs, ls, _ = x_sample.shape
    depth = w_ada.shape[0]
    caches = (cache_win1_kv, cache_win2_kv, cache_win3_kv)
    biases = [_group_bias(rel_bias, g) for g in range(len(A_GROUPS))]
    bias_p = jnp.stack([_prompt_bias_table(b) for b in biases], 0)
    bias_s = [_sample_bias_tables(b, caches[g].shape[2], A_GROUPS[g][1], ls) for g, b in enumerate(biases)]

    yp = x_prompt.reshape(bp * lp, d)
    ys = x_sample.reshape(bs * ls, d)
    c_all = jnp.concatenate([c_prompt, c_sample], axis=0)
    new_p = [[] for _ in range(7)]
    new_s = [[] for _ in range(7)]
    kvt_p = None
    for l in range(depth):
        w_qkv, w_big = _mixer_weights(w_in[l])
        lw = dict(g_ff1=g_ff1[l], w_ff1_in=w_ff1_in[l].astype(BF16), w_ff1_out=w_ff1_out[l].astype(BF16),
                  g_mix=g_mix[l], w_qkv=w_qkv, w_big=w_big, w_a_proj=w_a_proj[l].astype(BF16),
                  conv_b_w=conv_b_w[l], conv_b_b=conv_b_b[l], dt_bias=dt_bias[l], a_log=a_log[l],
                  d_skip=d_skip[l], g_ssm_norm=g_ssm_norm[l], w_b_proj=w_b_proj[l].astype(BF16),
                  conv_c_w=conv_c_w[l], conv_c_b=conv_c_b[l], w_rgate=w_rgate[l], b_rgate=b_rgate[l],
                  w_igate=w_igate[l], b_igate=b_igate[l], lru_lambda=lru_lambda[l],
                  w_c_proj=w_c_proj[l].astype(BF16), w_out=w_out[l].astype(BF16),
                  g_ff2=g_ff2[l], w_ff2_in=w_ff2_in[l].astype(BF16), w_ff2_out=w_ff2_out[l].astype(BF16))
        mod_all = _ada(c_all, w_ada, b_ada, l)
        mod_p = _Mod(mod_all[:bp], bp, lp, per_row=False)
        mod_s = _Mod(mod_all[bp:], bs, ls, per_row=True)
        gf = g_final if l == depth - 1 else None
        st_p = dict(kvt=kvt_p, layer=l, depth=depth,
                    conv_b=jnp.zeros((bp, B_CONV - 1, conv_b_w.shape[2]), F32),
                    ssm=jnp.zeros((bp,) + state_ssm.shape[2:], F32),
                    conv_c=jnp.zeros((bp, C_CONV - 1, conv_c_w.shape[2]), F32),
                    lru=jnp.zeros((bp, state_lru.shape[2]), F32))
        st_s = dict(kv=caches, layer=l, conv_b=state_conv_b[l], ssm=state_ssm[l],
                    conv_c=state_conv_c[l], lru=state_lru[l])
        yp, stp = _block(yp, mod_p, lw, st_p, bias_p, True, bp, lp, gf)
        ys, sts = _block(ys, mod_s, lw, st_s, bias_s, False, bs, ls, gf)
        kvt_p = list(stp[:3])
        for i in range(7):
            new_p[i].append(stp[i])
            new_s[i].append(sts[i])
    outs_p = [jnp.transpose(v, (0, 1, 5, 2, 3, 4)) for v in kvt_p] + [jnp.stack(v, 0) for v in new_p[3:]]
    outs_s = [jnp.stack(v, 0) for v in new_s]
    return (yp.reshape(bp, lp, d), ys.reshape(bs, ls, d), *outs_p, *outs_s)


def kernel(x_prompt, x_sample, c_prompt, c_sample, cache_win1_kv, cache_win2_kv, cache_win3_kv,
           state_conv_b, state_ssm, state_conv_c, state_lru, rel_bias, w_ada, b_ada, g_ff1,
           w_ff1_in, w_ff1_out, g_mix, w_in, w_a_proj, conv_b_w, conv_b_b, dt_bias, a_log, d_skip,
           g_ssm_norm, w_b_proj, conv_c_w, conv_c_b, w_rgate, b_rgate, w_igate, b_igate, lru_lambda,
           w_c_proj, w_out, g_ff2, w_ff2_in, w_ff2_out, g_final):
    return _forward(x_prompt, x_sample, c_prompt, c_sample, cache_win1_kv, cache_win2_kv,
                    cache_win3_kv, state_conv_b, state_ssm, state_conv_c, state_lru, rel_bias, w_ada,
                    b_ada, g_ff1, w_ff1_in, w_ff1_out, g_mix, w_in, w_a_proj, conv_b_w, conv_b_b,
                    dt_bias, a_log, d_skip, g_ssm_norm, w_b_proj, conv_c_w, conv_c_b, w_rgate,
                    b_rgate, w_igate, b_igate, lru_lambda, w_c_proj, w_out, g_ff2, w_ff2_in,
                    w_ff2_out, g_final)
```

```python
import functools
import math

import jax
import jax.numpy as jnp
import numpy as np
from jax import lax
from jax.experimental import pallas as pl
from jax.experimental.pallas import tpu as pltpu

F32 = jnp.float32
BF16 = jnp.bfloat16

NORM_EPS = 1e-6
N_MOD = 9
A_GROUPS = ((128, 1), (512, 4), (2048, 16))
A_HEADS = 8
A_HEAD_DIM = 64
A_WIDTH = A_HEADS * A_HEAD_DIM
A_KEYS = 128
A_BLOCK = 128
REL_BUCKETS = 32
REL_MAX_EXACT = 16
REL_MAX_DISTANCE = 2048
B_HEAD_DIM = 64
B_GROUPS = 2
B_STATE = 128
B_CONV = 4
C_BLOCKS = 8
C_CONV = 4
C_POW = 8.0

LANES = 128
SUBLANES = 8
SSD_CHUNK = 128
FFN_TM = 1024
PROJ_TM = 2048
MERGE_TM = 512
QKV_TN = 1024
BIG_TN = 1536
DT_PAD = 512
ATTN_UNROLL = 4
VMEM_LIMIT = 56 * 1024 * 1024


def _cparams(*sem):
    return pltpu.CompilerParams(dimension_semantics=sem, vmem_limit_bytes=VMEM_LIMIT)


def _resident(shape):
    nd = len(shape)
    return pl.BlockSpec(shape, lambda *_: (0,) * nd, pipeline_mode=pl.Buffered(1))


def _ld(ref):
    return ref[0] if len(ref.shape) == 3 else ref[...]


def _norm_mod(x, g, sc, sh):
    ms = jnp.mean(x * x, axis=-1, keepdims=True)
    return (x * lax.rsqrt(ms + NORM_EPS) * g) * (1.0 + sc) + sh


def _softplus(x):
    return jnp.maximum(x, 0.0) + jnp.log1p(jnp.exp(-jnp.abs(x)))


def _split_bf16(v, n):
    parts = []
    r = v
    for _ in range(n):
        p = r.astype(BF16)
        parts.append(p)
        r = r - p.astype(F32)
    return parts


def _dot(a, b):
    return jnp.dot(a, b, preferred_element_type=F32)


def _dot_nt(a, b):
    return lax.dot_general(a, b, (((1,), (1,)), ((), ())), preferred_element_type=F32)


def _ada_kernel(c_ref, w_ref, b_ref, o_ref):
    c = c_ref[...]
    a = (c * jax.nn.sigmoid(c)).astype(BF16)
    o_ref[...] = _dot(a, w_ref[...].astype(BF16)) + b_ref[...]


def _ada(c, w, b, layer):
    rows, d = c.shape
    depth, _, n = w.shape
    tn = 1024
    return pl.pallas_call(
        _ada_kernel,
        grid=(n // tn,),
        in_specs=[pl.BlockSpec((rows, d), lambda j: (0, 0)),
                  pl.BlockSpec((None, d, tn), lambda j: (layer, 0, j)),
                  pl.BlockSpec((None, 1, tn), lambda j: (layer, 0, j))],
        out_specs=pl.BlockSpec((rows, tn), lambda j: (0, j)),
        out_shape=jax.ShapeDtypeStruct((rows, n), F32),
        compiler_params=_cparams("parallel"),
        name="ada",
    )(c, w, b.reshape(depth, 1, n))


class _Mod:
    def __init__(self, mod, batch, seq, per_row):
        d = mod.shape[1] // N_MOD
        self.d = d
        self.seq = seq
        self.per_row = per_row
        if per_row:
            self.arr = jnp.repeat(mod, seq, axis=0)
        else:
            self.arr = mod.reshape(batch * N_MOD, 1, d)

    def spec(self, k, tm, grid_rank):
        d = self.d
        if self.per_row:
            if grid_rank == 1:
                return pl.BlockSpec((tm, d), lambda i: (i, k))
            return pl.BlockSpec((tm, d), lambda i, j: (i, k))
        per = self.seq // tm
        if grid_rank == 1:
            return pl.BlockSpec((1, 1, d), lambda i: ((i // per) * N_MOD + k, 0, 0))
        return pl.BlockSpec((1, 1, d), lambda i, j: ((i // per) * N_MOD + k, 0, 0))


def _row_tile(m, seq, per_row, want):
    tm = min(want, m if per_row else seq)
    assert m % tm == 0 and (per_row or seq % tm == 0)
    return tm


def _ffn_kernel(x_ref, sh_ref, sc_ref, gt_ref, gn_ref, win_ref, wout_ref, *rest, d_ff, fc, final):
    if final:
        gf_ref, o_ref, acc_ref = rest
    else:
        o_ref, acc_ref = rest
    x = x_ref[...]
    h = _norm_mod(x, gn_ref[...], _ld(sc_ref), _ld(sh_ref)).astype(BF16)
    for c in range(d_ff // fc):
        u = _dot(h, win_ref[:, c * fc:(c + 1) * fc])
        v = _dot(h, win_ref[:, d_ff + c * fc:d_ff + (c + 1) * fc])
        a = (u * jax.nn.sigmoid(u) * v).astype(BF16)
        part = _dot(a, wout_ref[c * fc:(c + 1) * fc, :])
        if c == 0:
            acc_ref[...] = part
        else:
            acc_ref[...] += part
    y = x + 0.5 * _ld(gt_ref) * acc_ref[...]
    if final:
        ms = jnp.mean(y * y, axis=-1, keepdims=True)
        y = y * lax.rsqrt(ms + NORM_EPS) * gf_ref[...]
    o_ref[...] = y


def _ffn(x, mod, k0, gn, w_in, w_out, g_final=None):
    m, d = x.shape
    d_ff = w_out.shape[0]
    tm = _row_tile(m, mod.seq, mod.per_row, FFN_TM)
    final = g_final is not None
    in_specs = [pl.BlockSpec((tm, d), lambda i: (i, 0)),
                mod.spec(k0, tm, 1), mod.spec(k0 + 1, tm, 1), mod.spec(k0 + 2, tm, 1),
                _resident((1, d)), _resident(w_in.shape), _resident(w_out.shape)]
    args = [x, mod.arr, mod.arr, mod.arr, gn.reshape(1, d), w_in, w_out]
    if final:
        in_specs.append(_resident((1, d)))
        args.append(g_final.reshape(1, d))
    return pl.pallas_call(
        functools.partial(_ffn_kernel, d_ff=d_ff, fc=256, final=final),
        grid=(m // tm,),
        in_specs=in_specs,
        out_specs=pl.BlockSpec((tm, d), lambda i: (i, 0)),
        out_shape=jax.ShapeDtypeStruct((m, d), F32),
        scratch_shapes=[pltpu.VMEM((tm, d), F32)],
        compiler_params=_cparams("parallel"),
        name="ffn",
    )(*args)


def _proj_kernel(x_ref, sh_ref, sc_ref, gn_ref, w_ref, o_ref, h_ref):
    @pl.when(pl.program_id(1) == 0)
    def _():
        h_ref[...] = _norm_mod(x_ref[...], gn_ref[...], _ld(sc_ref), _ld(sh_ref)).astype(BF16)

    res = _dot(h_ref[...], w_ref[...]).astype(o_ref.dtype)
    if len(o_ref.shape) == 2:
        o_ref[...] = res
    else:
        for s in range(o_ref.shape[0]):
            o_ref[s] = res[:, s * LANES:(s + 1) * LANES]


def _proj(x, mod, gn, w, out_dtype, tn, slabs=False):
    m, d = x.shape
    n = w.shape[1]
    tm = _row_tile(m, mod.seq, mod.per_row, PROJ_TM)
    if slabs:
        out_spec = pl.BlockSpec((tn // LANES, tm, LANES), lambda i, j: (j, i, 0))
        out_shape = jax.ShapeDtypeStruct((n // LANES, m, LANES), out_dtype)
    else:
        out_spec = pl.BlockSpec((tm, tn), lambda i, j: (i, j))
        out_shape = jax.ShapeDtypeStruct((m, n), out_dtype)
    return pl.pallas_call(
        _proj_kernel,
        grid=(m // tm, n // tn),
        in_specs=[pl.BlockSpec((tm, d), lambda i, j: (i, 0)),
                  mod.spec(3, tm, 2), mod.spec(4, tm, 2),
                  pl.BlockSpec((1, d), lambda i, j: (0, 0)),
                  pl.BlockSpec((d, tn), lambda i, j: (0, j))],
        out_specs=out_spec,
        out_shape=out_shape,
        scratch_shapes=[pltpu.VMEM((tm, d), BF16)],
        compiler_params=_cparams("parallel", "arbitrary"),
        name="proj",
    )(x, mod.arr, mod.arr, gn.reshape(1, d), w)


def _pair_softmax(q2, k2, v2, bias_of, lo, transposed=False):
    zero = jnp.zeros_like(q2)
    res = []
    for i, qh in enumerate((jnp.where(lo, q2, zero), jnp.where(lo, zero, q2))):
        s = (_dot(qh, k2) if transposed else _dot_nt(qh, k2)) + bias_of(i)
        mx = jnp.max(s, axis=-1, keepdims=True)
        p = jnp.exp(s - mx)
        pb = p.astype(BF16)
        o = _dot_nt(pb, v2) if transposed else _dot(pb, v2)
        res.append((o, mx, jnp.sum(p, axis=-1, keepdims=True)))
    (oa, ma, la), (ob, mb, lb) = res
    return jnp.where(lo, oa, ob), jnp.where(lo, ma, mb), jnp.where(lo, la, lb)


def _merge_softmax(acc, new):
    ao, am, al = acc
    o, m, l = new
    mn = jnp.maximum(am, m)
    a1 = jnp.exp(am - mn)
    a2 = jnp.exp(m - mn)
    return ao * a1 + o * a2, mn, al * a1 + l * a2


def _attn_prompt_kernel(*refs, seq, n_alias):
    ng = len(A_GROUPS)
    qkv = refs[:3 * ng]
    bias_ref = refs[3 * ng]
    o_ref = refs[3 * ng + 1 + n_alias]
    kvt_refs = refs[3 * ng + 2 + n_alias:3 * ng + 2 + n_alias + ng]
    acc_o, acc_m, acc_l = refs[-3:]
    blk = A_BLOCK

    for g in range(ng):
        keep = kvt_refs[g].shape[-1]
        for kv in range(2):
            src = qkv[3 * g + 1 + kv]
            for c in range(keep // blk):
                t = src[pl.ds(seq - keep + c * blk, blk), :].T
                for i in range(2):
                    kvt_refs[g][kv, i, :, c * blk:(c + 1) * blk] = t[i * A_HEAD_DIM:(i + 1) * A_HEAD_DIM]

    hp = pl.program_id(1)
    lo = lax.broadcasted_iota(jnp.int32, (blk, LANES), 1) < A_HEAD_DIM

    def tiles(g, starts, kw):
        dil = A_GROUPS[g][1]
        q_ref, k_ref, v_ref = qkv[3 * g:3 * g + 3]

        def rows(start, n):
            return pl.ds(start, n) if dil == 1 else pl.ds(start, n, stride=dil)

        if kw == blk:
            bias_of = lambda i: bias_ref[g, 2 * hp + i, :, blk:2 * blk]
        else:
            bias_of = lambda i: bias_ref[g, 2 * hp + i]
        news = []
        for qstart, kstart in starts:
            q2 = q_ref[rows(qstart, blk), :].astype(BF16)
            k2 = k_ref[rows(kstart, kw), :].astype(BF16)
            v2 = v_ref[rows(kstart, kw), :].astype(BF16)
            news.append(_pair_softmax(q2, k2, v2, bias_of, lo))
        for (qstart, _), new in zip(starts, news):
            sel = rows(qstart, blk)
            if g > 0:
                new = _merge_softmax((acc_o[sel, :], acc_m[sel, :], acc_l[sel, :]), new)
            acc_o[sel, :] = new[0]
            acc_m[sel, :] = new[1]
            acc_l[sel, :] = new[2]

    for g, (_, dil) in enumerate(A_GROUPS):
        nb = seq // dil // blk
        span = blk * dil
        ur = min(dil, ATTN_UNROLL)
        ub = 1 if dil > 1 else max(u for u in range(1, ATTN_UNROLL + 1) if (nb - 1) % u == 0)

        def first(i, carry, g=g, ur=ur):
            tiles(g, [(i * ur + j, i * ur + j) for j in range(ur)], blk)
            return carry

        def rest(i, carry, g=g, span=span, ur=ur, ub=ub, nr=dil // ur):
            b = 1 + (i // nr) * ub
            r = (i % nr) * ur
            tiles(g, [(r + j + (b + k) * span, r + j + (b + k - 1) * span)
                      for k in range(ub) for j in range(ur)], 2 * blk)
            return carry

        lax.fori_loop(0, dil // ur, first, 0)
        if nb > 1:
            lax.fori_loop(0, (nb - 1) // ub * (dil // ur), rest, 0)
    o_ref[...] = (acc_o[...] / acc_l[...]).astype(o_ref.dtype)


def _attn_prompt(qkv, bias_tab, kvt_prev, layer, depth, batch, seq, o_dtype):
    npair = A_WIDTH // LANES
    ng = len(A_GROUPS)
    col = lambda j: pl.BlockSpec((None, seq, LANES), lambda b, h: (j * npair + h, b, 0))
    keeps = [min(win, seq) for win, _ in A_GROUPS]
    n_alias = 0 if kvt_prev is None else ng
    alias_specs = [pl.BlockSpec(memory_space=pl.ANY)] * n_alias
    alias_args = [] if kvt_prev is None else list(kvt_prev)
    n_in = 3 * ng + 1
    outs = pl.pallas_call(
        functools.partial(_attn_prompt_kernel, seq=seq, n_alias=n_alias),
        grid=(batch, npair),
        in_specs=[col(j) for j in range(3 * ng)]
        + [pl.BlockSpec(bias_tab.shape, lambda b, h: (0, 0, 0, 0))] + alias_specs,
        out_specs=[pl.BlockSpec((seq, LANES), lambda b, h: (b, h))]
        + [pl.BlockSpec((None, None, 2, 2, A_HEAD_DIM, keep), lambda b, h: (layer, b, 0, h, 0, 0))
           for keep in keeps],
        out_shape=[jax.ShapeDtypeStruct((batch * seq, A_WIDTH), o_dtype)]
        + [jax.ShapeDtypeStruct((depth, batch, 2, A_HEADS, A_HEAD_DIM, keep), F32) for keep in keeps],
        input_output_aliases={n_in + g: 1 + g for g in range(n_alias)},
        scratch_shapes=[pltpu.VMEM((seq, LANES), F32)] * 3,
        compiler_params=_cparams("parallel", "parallel"),
        name="attn_prompt",
    )(*([qkv] * (3 * ng)), bias_tab, *alias_args)
    return outs[0], list(outs[1:])


def _attn_sample_kernel(qkv_ref, buf1_ref, buf2_ref, buf3_ref, bb1, bn1, bb2, bn2, bb3, bn3,
                        o_ref, knew_scr, vnew_scr):
    t = qkv_ref.shape[1]
    npair = A_HEADS // 2
    bufs = (buf1_ref, buf2_ref, buf3_ref)
    bias = ((bb1, bn1), (bb2, bn2), (bb3, bn3))
    lo = lax.broadcasted_iota(jnp.int32, (t, LANES), 1) < A_HEAD_DIM
    knew_scr[...] = jnp.zeros_like(knew_scr)
    vnew_scr[...] = jnp.zeros_like(vnew_scr)
    for hp in range(npair):
        cols = slice(hp * LANES, (hp + 1) * LANES)
        acc = None
        for g in range(len(A_GROUPS)):
            q2 = qkv_ref[3 * g * npair + hp].astype(BF16)
            wb = bufs[g].shape[-1]
            kb = bufs[g][0, 2 * hp:2 * hp + 2].reshape(LANES, wb).astype(BF16)
            vb = bufs[g][1, 2 * hp:2 * hp + 2].reshape(LANES, wb).astype(BF16)
            knew_scr[g, hp, 0:t, :] = qkv_ref[(3 * g + 1) * npair + hp]
            vnew_scr[g, hp, 0:t, :] = qkv_ref[(3 * g + 2) * npair + hp]
            kn = knew_scr[g, hp].astype(BF16)
            vn = vnew_scr[g, hp].astype(BF16)
            bbuf, bnew = bias[g]
            new = _pair_softmax(q2, kb, vb, lambda i: bbuf[2 * hp + i], lo, transposed=True)
            new = _merge_softmax(new, _pair_softmax(q2, kn, vn, lambda i: bnew[2 * hp + i], lo))
            acc = new if acc is None else _merge_softmax(acc, new)
        o_ref[:, cols] = acc[0] / acc[2]


def _attn_sample(qkv, caches, layer, bias_s, batch, t):
    aw = A_WIDTH
    ng = len(A_GROUPS)
    bufs = [jnp.transpose(c, (0, 1, 3, 4, 5, 2)) for c in caches]
    buf_specs = [pl.BlockSpec((None, None) + bv.shape[2:], lambda b: (layer, b, 0, 0, 0, 0)) for bv in bufs]
    tabs = [tab for pair in bias_s for tab in pair]
    return pl.pallas_call(
        _attn_sample_kernel,
        grid=(batch,),
        in_specs=[pl.BlockSpec((3 * ng * aw // LANES, t, LANES), lambda b: (0, b, 0))] + buf_specs
        + [pl.BlockSpec(tab.shape, lambda b: (0, 0, 0)) for tab in tabs],
        out_specs=pl.BlockSpec((t, aw), lambda b: (b, 0)),
        out_shape=jax.ShapeDtypeStruct((batch * t, aw), F32),
        scratch_shapes=[pltpu.VMEM((ng, aw // LANES, LANES, LANES), F32)] * 2,
        compiler_params=_cparams("parallel"),
        name="attn_sample",
    )(qkv, *bufs, *tabs)


def _ssd_kernel(xbc_ref, z_ref, dt_ref, conv0_ref, st0_ref, cw_ref, cb_ref, dtb_ref, aneg_ref,
                dsk_ref, gn_ref, e_ref, y_ref, convo_ref, sto_ref, xpad, st_t, y_scr, *, lv, nch):
    q = SSD_CHUNK
    c = pl.program_id(1)
    width = y_ref.shape[1]
    nst = B_STATE
    hpg = width // B_HEAD_DIM // B_GROUPS // 2

    @pl.when(c == 0)
    def _():
        xpad[...] = jnp.zeros_like(xpad)
        xpad[0:SUBLANES, :] = conv0_ref[0]
        for j in range(width // LANES):
            st_t[:, j * LANES:(j + 1) * LANES] = st0_ref[0, j * LANES:(j + 1) * LANES, :].T

    @pl.when(c > 0)
    def _():
        xpad[0:SUBLANES, :] = xpad[q:q + SUBLANES, :]

    xpad[SUBLANES:SUBLANES + lv, :] = xbc_ref[...].astype(F32)

    w = cw_ref[...]
    conv = cb_ref[...]
    for k in range(B_CONV):
        off = SUBLANES - (B_CONV - 1) + k
        conv = conv + xpad[off:off + q, :] * w[k:k + 1, :]
    act = conv * jax.nn.sigmoid(conv)
    xs = act[:, :width]
    bm = [act[:, width + gi * nst:width + (gi + 1) * nst] for gi in range(B_GROUPS)]
    cm = [act[:, width + (B_GROUPS + gi) * nst:width + (B_GROUPS + gi + 1) * nst] for gi in range(B_GROUPS)]

    dt_raw = dt_ref[...]
    if lv < q:
        dt_raw = jnp.concatenate([dt_raw, jnp.zeros((q - lv, LANES), F32)], axis=0)
    dt = _softplus(dt_raw + dtb_ref[...])
    if lv < q:
        dt = jnp.where(lax.broadcasted_iota(jnp.int32, (q, LANES), 0) < lv, dt, 0.0)
    dta = dt * aneg_ref[...]

    row = lax.broadcasted_iota(jnp.int32, (q, q), 0)
    col = lax.broadcasted_iota(jnp.int32, (q, q), 1)
    causal = row >= col
    tri = jnp.where(causal, 1.0, 0.0).astype(BF16)
    acum = sum(_dot(tri, p) for p in _split_bf16(dta, 3))
    acum_t = acum.T
    e = e_ref[...]
    dt_x = sum(_dot(p, e) for p in _split_bf16(dt, 2))
    acum_x = sum(_dot(p, e) for p in _split_bf16(acum, 2))
    last_x = acum_x[q - 1:q, :]
    ea_x = jnp.exp(acum_x)
    xdt = xs * dt_x
    xdt_te = (xdt * jnp.exp(last_x - acum_x)).astype(BF16)
    xdt_b = xdt.astype(BF16)
    chunk_decay = jnp.exp(last_x)
    lo = lax.broadcasted_iota(jnp.int32, (q, LANES), 1) < B_HEAD_DIM

    for gi in range(B_GROUPS):
        cmb = cm[gi].astype(BF16)
        cb = _dot_nt(cmb, bm[gi].astype(BF16))
        bm_t = bm[gi].T.astype(BF16)
        for hp in range(gi * hpg, (gi + 1) * hpg):
            cols = slice(hp * LANES, (hp + 1) * LANES)
            ax = acum_x[:, cols]
            ax_r = pltpu.roll(ax, B_HEAD_DIM, axis=1)
            ys = []
            for col_v, h in ((jnp.where(lo, ax, ax_r), 2 * hp), (jnp.where(lo, ax_r, ax), 2 * hp + 1)):
                seg = col_v - acum_t[h:h + 1, :]
                dec = jnp.exp(jnp.where(causal, seg, -jnp.inf))
                ys.append(_dot((cb * dec).astype(BF16), xdt_b[:, cols]))
            st_old = st_t[:, cols]
            y_off = _dot(cmb, st_old.astype(BF16)) * ea_x[:, cols]
            st_t[:, cols] = st_old * chunk_decay[:, cols] + _dot(bm_t, xdt_te[:, cols])
            y_scr[:, cols] = jnp.where(lo, ys[0], ys[1]) + y_off + dsk_ref[:, cols] * xs[:, cols]

    z = z_ref[...].astype(F32)
    if lv < q:
        z = jnp.concatenate([z, jnp.zeros((q - lv, width), F32)], axis=0)
    y = y_scr[...] * (z * jax.nn.sigmoid(z))
    gw = width // B_GROUPS
    for gi in range(B_GROUPS):
        yg = y[:, gi * gw:(gi + 1) * gw]
        ms = jnp.mean(yg * yg, axis=-1, keepdims=True)
        yn = yg * lax.rsqrt(ms + NORM_EPS) * gn_ref[:, gi * gw:(gi + 1) * gw]
        y_ref[:, gi * gw:(gi + 1) * gw] = yn[0:lv].astype(y_ref.dtype)

    @pl.when(c == nch - 1)
    def _():
        convo_ref[0] = xpad[lv:lv + SUBLANES, :]
        for j in range(width // LANES):
            sto_ref[0, j * LANES:(j + 1) * LANES, :] = st_t[:, j * LANES:(j + 1) * LANES].T


def _ssd(big, qkv, dt_col, conv0, st0, lw, batch, seq, o_dtype):
    width = lw["w_b_proj"].shape[0]
    cch = lw["conv_b_w"].shape[1]
    nheads = width // B_HEAD_DIM
    lv = min(seq, SSD_CHUNK)
    nch = seq // lv
    assert lv == SSD_CHUNK or nch == 1
    pad = lambda v: jnp.pad(v.astype(F32), (0, LANES - nheads)).reshape(1, LANES)
    e = (np.arange(LANES)[:, None] == (np.arange(width)[None, :] // B_HEAD_DIM)).astype(np.float32)
    conv0p = jnp.pad(conv0, ((0, 0), (SUBLANES - (B_CONV - 1), 0), (0, 0)))
    y, convo, sto = pl.pallas_call(
        functools.partial(_ssd_kernel, lv=lv, nch=nch),
        grid=(batch, nch),
        in_specs=[pl.BlockSpec((lv, cch), lambda b, c: (b * nch + c, 4)),
                  pl.BlockSpec((lv, width), lambda b, c: (b * nch + c, 0)),
                  pl.BlockSpec((None, lv, LANES), lambda b, c: (dt_col, b * nch + c, 0)),
                  pl.BlockSpec((1, SUBLANES, cch), lambda b, c: (b, 0, 0)),
                  pl.BlockSpec((1, width, B_STATE), lambda b, c: (b, 0, 0)),
                  pl.BlockSpec((B_CONV, cch), lambda b, c: (0, 0)),
                  pl.BlockSpec((1, cch), lambda b, c: (0, 0)),
                  pl.BlockSpec((1, LANES), lambda b, c: (0, 0)),
                  pl.BlockSpec((1, LANES), lambda b, c: (0, 0)),
                  pl.BlockSpec((1, width), lambda b, c: (0, 0)),
                  pl.BlockSpec((1, width), lambda b, c: (0, 0)),
                  pl.BlockSpec((LANES, width), lambda b, c: (0, 0))],
        out_specs=[pl.BlockSpec((lv, width), lambda b, c: (b * nch + c, 0)),
                   pl.BlockSpec((1, SUBLANES, cch), lambda b, c: (b, 0, 0)),
                   pl.BlockSpec((1, width, B_STATE), lambda b, c: (b, 0, 0))],
        out_shape=[jax.ShapeDtypeStruct((batch * seq, width), o_dtype),
                   jax.ShapeDtypeStruct((batch, SUBLANES, cch), F32),
                   jax.ShapeDtypeStruct((batch, width, B_STATE), F32)],
        scratch_shapes=[pltpu.VMEM((SUBLANES + SSD_CHUNK, cch), F32),
                        pltpu.VMEM((B_STATE, width), F32),
                        pltpu.VMEM((SSD_CHUNK, width), F32)],
        compiler_params=_cparams("parallel", "arbitrary"),
        name="ssd",
    )(big, big, qkv, conv0p, st0.reshape(batch, width, B_STATE),
      lw["conv_b_w"], lw["conv_b_b"].reshape(1, cch), pad(lw["dt_bias"]),
      pad(-jnp.exp(lw["a_log"].astype(F32))),
      jnp.repeat(lw["d_skip"].astype(F32), B_HEAD_DIM).reshape(1, width),
      lw["g_ssm_norm"].reshape(1, width), jnp.asarray(e, BF16))
    return (y, convo[:, SUBLANES - (B_CONV - 1):, :],
            sto.reshape(batch, nheads, B_HEAD_DIM, B_STATE))


def _lru_kernel(xc_ref, gc_ref, conv0_ref, h0_ref, cw_ref, cb_ref, wr_ref, wi_ref, br_ref, bi_ref,
                lam_ref, y_ref, convo_ref, ho_ref, xpad, hprev, hs_scr, *, q, nch):
    c = pl.program_id(1)
    width = y_ref.shape[1]

    @pl.when(c == 0)
    def _():
        xpad[0:SUBLANES, :] = conv0_ref[0]
        hprev[...] = h0_ref[0]

    @pl.when(c > 0)
    def _():
        xpad[0:SUBLANES, :] = xpad[q:q + SUBLANES, :]

    xpad[SUBLANES:SUBLANES + q, :] = xc_ref[...].astype(F32)
    w = cw_ref[...]
    x = cb_ref[...]
    for k in range(C_CONV):
        off = SUBLANES - (C_CONV - 1) + k
        x = x + xpad[off:off + q, :] * w[k:k + 1, :]

    bd = width // C_BLOCKS
    rs, gs = [], []
    for j in range(C_BLOCKS):
        xb = x[:, j * bd:(j + 1) * bd].astype(BF16)
        rs.append(_dot(xb, wr_ref[j]))
        gs.append(_dot(xb, wi_ref[j]))
    rg = jax.nn.sigmoid(jnp.concatenate(rs, axis=1) + br_ref[...])
    ig = jax.nn.sigmoid(jnp.concatenate(gs, axis=1) + bi_ref[...])
    log_a = -C_POW * rg * _softplus(-lam_ref[...])
    a = jnp.exp(log_a)
    u = x * ig * jnp.sqrt(-jnp.tanh(log_a) * (a * a + 1.0))

    rowi = lax.broadcasted_iota(jnp.int32, (SUBLANES, width), 0)
    h = hprev[...]
    for g in range(q // SUBLANES):
        ag = a[g * SUBLANES:(g + 1) * SUBLANES]
        bg = u[g * SUBLANES:(g + 1) * SUBLANES]
        s = 1
        while s < SUBLANES:
            keep = rowi >= s
            a_sh = jnp.where(keep, pltpu.roll(ag, s, axis=0), 1.0)
            b_sh = jnp.where(keep, pltpu.roll(bg, s, axis=0), 0.0)
            bg = ag * b_sh + bg
            ag = ag * a_sh
            s *= 2
        hg = bg + ag * h
        hs_scr[g * SUBLANES:(g + 1) * SUBLANES, :] = hg
        h = hg[SUBLANES - 1:SUBLANES]
    hprev[...] = h
    y_ref[...] = (hs_scr[...] * jax.nn.gelu(gc_ref[...].astype(F32))).astype(y_ref.dtype)

    @pl.when(c == nch - 1)
    def _():
        convo_ref[0] = xpad[q:q + SUBLANES, :]
        ho_ref[0] = h


def _lru(big, conv0, h0, lw, batch, seq, o_dtype):
    width = lw["w_c_proj"].shape[0]
    q = min(seq, 128)
    nch = seq // q
    bd = width // C_BLOCKS
    conv0p = jnp.pad(conv0, ((0, 0), (SUBLANES - (C_CONV - 1), 0), (0, 0)))
    vec = lambda v: v.astype(F32).reshape(1, width)
    y, convo, ho = pl.pallas_call(
        functools.partial(_lru_kernel, q=q, nch=nch),
        grid=(batch, nch),
        in_specs=[pl.BlockSpec((q, width), lambda b, c: (b * nch + c, 1)),
                  pl.BlockSpec((q, width), lambda b, c: (b * nch + c, 2)),
                  pl.BlockSpec((1, SUBLANES, width), lambda b, c: (b, 0, 0)),
                  pl.BlockSpec((1, 1, width), lambda b, c: (b, 0, 0)),
                  pl.BlockSpec((C_CONV, width), lambda b, c: (0, 0)),
                  pl.BlockSpec((1, width), lambda b, c: (0, 0)),
                  pl.BlockSpec((C_BLOCKS, bd, bd), lambda b, c: (0, 0, 0)),
                  pl.BlockSpec((C_BLOCKS, bd, bd), lambda b, c: (0, 0, 0)),
                  pl.BlockSpec((1, width), lambda b, c: (0, 0)),
                  pl.BlockSpec((1, width), lambda b, c: (0, 0)),
                  pl.BlockSpec((1, width), lambda b, c: (0, 0))],
        out_specs=[pl.BlockSpec((q, width), lambda b, c: (b * nch + c, 0)),
                   pl.BlockSpec((1, SUBLANES, width), lambda b, c: (b, 0, 0)),
                   pl.BlockSpec((1, 1, width), lambda b, c: (b, 0, 0))],
        out_shape=[jax.ShapeDtypeStruct((batch * seq, width), o_dtype),
                   jax.ShapeDtypeStruct((batch, SUBLANES, width), F32),
                   jax.ShapeDtypeStruct((batch, 1, width), F32)],
        scratch_shapes=[pltpu.VMEM((SUBLANES + q, width), F32),
                        pltpu.VMEM((1, width), F32),
                        pltpu.VMEM((q, width), F32)],
        compiler_params=_cparams("parallel", "arbitrary"),
        name="lru",
    )(big, big, conv0p, h0.reshape(batch, 1, width), lw["conv_c_w"], vec(lw["conv_c_b"]),
      lw["w_rgate"].astype(BF16), lw["w_igate"].astype(BF16), vec(lw["b_rgate"]), vec(lw["b_igate"]),
      vec(lw["lru_lambda"]))
    return y, convo[:, SUBLANES - (C_CONV - 1):, :], ho.reshape(batch, width)


def _merge_kernel(x_ref, gt_ref, oa_ref, yb_ref, yc_ref, gates_ref, wa_ref, wb_ref, wc_ref, wo_ref,
                  out_ref):
    d = x_ref.shape[1]
    ya = _dot(oa_ref[...].astype(BF16), wa_ref[...])
    yb = _dot(yb_ref[...].astype(BF16), wb_ref[...])
    yc = _dot(yc_ref[...].astype(BF16), wc_ref[...])
    sg = jax.nn.sigmoid(gates_ref[...].astype(F32))
    mixed = sg[:, :d] * ya + sg[:, d:2 * d] * yb + sg[:, 2 * d:] * yc
    out_ref[...] = x_ref[...] + _ld(gt_ref) * _dot(mixed.astype(BF16), wo_ref[...])


def _merge(x, mod, oa, yb, yc, big, lw):
    m, d = x.shape
    aw = A_WIDTH
    tm = _row_tile(m, mod.seq, mod.per_row, MERGE_TM)
    row = lambda wd, j=0: pl.BlockSpec((tm, wd), lambda i: (i, j))
    return pl.pallas_call(
        _merge_kernel,
        grid=(m // tm,),
        in_specs=[row(d), mod.spec(5, tm, 1), row(aw), row(yb.shape[1]), row(yc.shape[1]), row(3 * d, 1)]
        + [_resident(lw[k].shape) for k in ("w_a_proj", "w_b_proj", "w_c_proj", "w_out")],
        out_specs=row(d),
        out_shape=jax.ShapeDtypeStruct((m, d), F32),
        compiler_params=_cparams("parallel"),
        name="merge",
    )(x, mod.arr, oa, yb, yc, big,
      lw["w_a_proj"], lw["w_b_proj"], lw["w_c_proj"], lw["w_out"])


def _t5_bucket(dist):
    dist = np.asarray(dist)
    large = REL_MAX_EXACT + (np.log(np.maximum(dist, 1) / REL_MAX_EXACT)
                             / math.log(REL_MAX_DISTANCE / REL_MAX_EXACT)
                             * (REL_BUCKETS - REL_MAX_EXACT)).astype(np.int64)
    large = np.minimum(large, REL_BUCKETS - 1)
    return np.where(dist < REL_MAX_EXACT, dist, large).astype(np.int32)


def _group_bias(rel_bias, g):
    dil = A_GROUPS[g][1]
    buckets = _t5_bucket(np.arange(A_KEYS + 1) * dil)
    return rel_bias[buckets][:, g * A_HEADS:(g + 1) * A_HEADS].T.astype(F32)


def _prompt_bias_table(bias):
    assert A_KEYS == A_BLOCK
    h = bias.shape[0]
    n = 2 * A_BLOCK
    v = jnp.concatenate([bias[:, ::-1], jnp.full((h, n - A_KEYS), -jnp.inf, F32)], axis=1)
    return jnp.tile(v, (1, A_BLOCK))[:, :A_BLOCK * n].reshape(h, A_BLOCK, n)


def _sample_bias_tables(bias, wb, dil, t):
    h = bias.shape[0]
    front = t - 1 + wb - A_KEYS * dil
    assert front >= 0
    sparse = jnp.concatenate([bias[:, ::-1, None], jnp.full((h, A_KEYS + 1, dil - 1), -jnp.inf, F32)], axis=2)
    base = jnp.concatenate([jnp.full((h, front), -jnp.inf, F32), sparse.reshape(h, (A_KEYS + 1) * dil)], axis=1)
    tab_buf = jnp.stack([base[:, t - 1 - tq:t - 1 - tq + wb] for tq in range(t)], axis=1)
    tq = np.arange(t)[:, None]
    dist = tq - np.arange(LANES)[None, :]
    valid = (dist >= 0) & (dist % dil == 0) & (np.arange(LANES)[None, :] < t)
    tab_new = jnp.where(valid[None], bias[:, np.clip(dist // dil, 0, A_KEYS)], -jnp.inf)
    return tab_buf, tab_new


def _mixer_weights(w_in):
    na = len(A_GROUPS) * A_WIDTH
    d = w_in.shape[0]
    offs = np.cumsum([0, na, na, na, 1024, 1536, 16, 1024, 1024, 3 * d])
    seg = lambda i: w_in[:, offs[i]:offs[i + 1]]
    qa, ka, va, zb, xbc, dtb, xc, gc, gates = [seg(i) for i in range(9)]
    cols = []
    for g in range(len(A_GROUPS)):
        sl = slice(g * A_WIDTH, (g + 1) * A_WIDTH)
        cols += [qa[:, sl] * (A_HEAD_DIM ** -0.5), ka[:, sl], va[:, sl]]
    w_qkv = jnp.concatenate(cols + [dtb, jnp.zeros((d, DT_PAD - dtb.shape[1]), w_in.dtype)], axis=1)
    w_big = jnp.concatenate([zb, xc, gc, gates, xbc], axis=1)
    return w_qkv.astype(BF16), w_big.astype(BF16)


def _mixer(x, mod, lw, st, bias, prompt, batch, seq):
    act_dtype = BF16 if prompt else F32
    qkv = _proj(x, mod, lw["g_mix"], lw["w_qkv"], F32, QKV_TN, slabs=True)
    big = _proj(x, mod, lw["g_mix"], lw["w_big"], act_dtype, BIG_TN)
    ng = len(A_GROUPS)
    npair = A_WIDTH // LANES
    if prompt:
        oa, new_kv = _attn_prompt(qkv, bias, st["kvt"], st["layer"], st["depth"], batch, seq, act_dtype)
    else:
        oa = _attn_sample(qkv, st["kv"], st["layer"], bias, batch, seq)
        new_kv = [jnp.transpose(qkv[(3 * g + 1) * npair:(3 * g + 3) * npair], (1, 0, 2))
                  .reshape(batch, seq, 2, A_HEADS, A_HEAD_DIM) for g in range(ng)]
    dt_col = 3 * ng * npair
    yb, conv_b_new, ssm_new = _ssd(big, qkv, dt_col, st["conv_b"], st["ssm"], lw, batch, seq, act_dtype)
    yc, conv_c_new, lru_new = _lru(big, st["conv_c"], st["lru"], lw, batch, seq, act_dtype)
    x = _merge(x, mod, oa, yb, yc, big, lw)
    return x, (new_kv[0], new_kv[1], new_kv[2], conv_b_new, ssm_new, conv_c_new, lru_new)


def _block(x, mod, lw, st, bias, prompt, batch, seq, g_final):
    x = _ffn(x, mod, 0, lw["g_ff1"], lw["w_ff1_in"], lw["w_ff1_out"])
    x, new_st = _mixer(x, mod, lw, st, bias, prompt, batch, seq)
    x = _ffn(x, mod, 6, lw["g_ff2"], lw["w_ff2_in"], lw["w_ff2_out"], g_final)
    return x, new_st


@jax.jit
def _forward(x_prompt, x_sample, c_prompt, c_sample, cache_win1_kv, cache_win2_kv, cache_win3_kv,
             state_conv_b, state_ssm, state_conv_c, state_lru, rel_bias, w_ada, b_ada, g_ff1,
             w_ff1_in, w_ff1_out, g_mix, w_in, w_a_proj, conv_b_w, conv_b_b, dt_bias, a_log, d_skip,
             g_ssm_norm, w_b_proj, conv_c_w, conv_c_b, w_rgate, b_rgate, w_igate, b_igate, lru_lambda,
             w_c_proj, w_out, g_ff2, w_ff2_in, w_ff2_out, g_final):
    bp, lp, d = x_prompt.shape
    bs, ls, _ = x_sample.shape
    depth = w_ada.shape[0]
    caches = (cache_win1_kv, cache_win2_kv, cache_win3_kv)
    biases = [_group_bias(rel_bias, g) for g in range(len(A_GROUPS))]
    bias_p = jnp.stack([_prompt_bias_table(b) for b in biases], 0)
    bias_s = [_sample_bias_tables(b, caches[g].shape[2], A_GROUPS[g][1], ls) for g, b in enumerate(biases)]

    yp = x_prompt.reshape(bp * lp, d)
    ys = x_sample.reshape(bs * ls, d)
    c_all = jnp.concatenate([c_prompt, c_sample], axis=0)
    new_p = [[] for _ in range(7)]
    new_s = [[] for _ in range(7)]
    kvt_p = None
    for l in range(depth):
        w_qkv, w_big = _mixer_weights(w_in[l])
        lw = dict(g_ff1=g_ff1[l], w_ff1_in=w_ff1_in[l].astype(BF16), w_ff1_out=w_ff1_out[l].astype(BF16),
                  g_mix=g_mix[l], w_qkv=w_qkv, w_big=w_big, w_a_proj=w_a_proj[l].astype(BF16),
                  conv_b_w=conv_b_w[l], conv_b_b=conv_b_b[l], dt_bias=dt_bias[l], a_log=a_log[l],
                  d_skip=d_skip[l], g_ssm_norm=g_ssm_norm[l], w_b_proj=w_b_proj[l].astype(BF16),
                  conv_c_w=conv_c_w[l], conv_c_b=conv_c_b[l], w_rgate=w_rgate[l], b_rgate=b_rgate[l],
                  w_igate=w_igate[l], b_igate=b_igate[l], lru_lambda=lru_lambda[l],
                  w_c_proj=w_c_proj[l].astype(BF16), w_out=w_out[l].astype(BF16),
                  g_ff2=g_ff2[l], w_ff2_in=w_ff2_in[l].astype(BF16), w_ff2_out=w_ff2_out[l].astype(BF16))
        mod_all = _ada(c_all, w_ada, b_ada, l)
        mod_p = _Mod(mod_all[:bp], bp, lp, per_row=False)
        mod_s = _Mod(mod_all[bp:], bs, ls, per_row=True)
        gf = g_final if l == depth - 1 else None
        st_p = dict(kvt=kvt_p, layer=l, depth=depth,
                    conv_b=jnp.zeros((bp, B_CONV - 1, conv_b_w.shape[2]), F32),
                    ssm=jnp.zeros((bp,) + state_ssm.shape[2:], F32),
                    conv_c=jnp.zeros((bp, C_CONV - 1, conv_c_w.shape[2]), F32),
                    lru=jnp.zeros((bp, state_lru.shape[2]), F32))
        st_s = dict(kv=caches, layer=l, conv_b=state_conv_b[l], ssm=state_ssm[l],
                    conv_c=state_conv_c[l], lru=state_lru[l])
        yp, stp = _block(yp, mod_p, lw, st_p, bias_p, True, bp, lp, gf)
        ys, sts = _block(ys, mod_s, lw, st_s, bias_s, False, bs, ls, gf)
        kvt_p = list(stp[:3])
        for i in range(7):
            new_p[i].append(stp[i])
            new_s[i].append(sts[i])
    outs_p = [jnp.transpose(v, (0, 1, 5, 2, 3, 4)) for v in kvt_p] + [jnp.stack(v, 0) for v in new_p[3:]]
    outs_s = [jnp.stack(v, 0) for v in new_s]
    return (yp.reshape(bp, lp, d), ys.reshape(bs, ls, d), *outs_p, *outs_s)


def kernel(x_prompt, x_sample, c_prompt, c_sample, cache_win1_kv, cache_win2_kv, cache_win3_kv,
           state_conv_b, state_ssm, state_conv_c, state_lru, rel_bias, w_ada, b_ada, g_ff1,
           w_ff1_in, w_ff1_out, g_mix, w_in, w_a_proj, conv_b_w, conv_b_b, dt_bias, a_log, d_skip,
           g_ssm_norm, w_b_proj, conv_c_w, conv_c_b, w_rgate, b_rgate, w_igate, b_igate, lru_lambda,
           w_c_proj, w_out, g_ff2, w_ff2_in, w_ff2_out, g_final):
    return _forward(x_prompt, x_sample, c_prompt, c_sample, cache_win1_kv, cache_win2_kv,
                    cache_win3_kv, state_conv_b, state_ssm, state_conv_c, state_lru, rel_bias, w_ada,
                    b_ada, g_ff1, w_ff1_in, w_ff1_out, g_mix, w_in, w_a_proj, conv_b_w, conv_b_b,
                    dt_bias, a_log, d_skip, g_ssm_norm, w_b_proj, conv_c_w, conv_c_b, w_rgate,
                    b_rgate, w_igate, b_igate, lru_lambda, w_c_proj, w_out, g_ff2, w_ff2_in,
                    w_ff2_out, g_final)
```

```python
import functools
import math

import jax
import jax.numpy as jnp
import numpy as np
from jax import lax
from jax.experimental import pallas as pl
from jax.experimental.pallas import tpu as pltpu

F32 = jnp.float32
BF16 = jnp.bfloat16

NORM_EPS = 1e-6
N_MOD = 9
A_GROUPS = ((128, 1), (512, 4), (2048, 16))
A_HEADS = 8
A_HEAD_DIM = 64
A_WIDTH = A_HEADS * A_HEAD_DIM
A_KEYS = 128
A_BLOCK = 128
REL_BUCKETS = 32
REL_MAX_EXACT = 16
REL_MAX_DISTANCE = 2048
B_HEAD_DIM = 64
B_GROUPS = 2
B_STATE = 128
B_CONV = 4
C_BLOCKS = 8
C_CONV = 4
C_POW = 8.0

LANES = 128
SUBLANES = 8
SSD_CHUNK = 128
FFN_TM = 1024
PROJ_TM = 2048
MERGE_TM = 512
QKV_TN = 1024
BIG_TN = 1536
DT_PAD = 512
ATTN_UNROLL = 8
ATTN_UNROLL_REST = 12
ATTN_MAX_STRIDE = 4
VMEM_LIMIT = 56 * 1024 * 1024


def _cparams(*sem):
    return pltpu.CompilerParams(dimension_semantics=sem, vmem_limit_bytes=VMEM_LIMIT)


def _resident(shape):
    nd = len(shape)
    return pl.BlockSpec(shape, lambda *_: (0,) * nd, pipeline_mode=pl.Buffered(1))


def _ld(ref):
    return ref[0] if len(ref.shape) == 3 else ref[...]


def _norm_mod(x, g, sc, sh):
    ms = jnp.mean(x * x, axis=-1, keepdims=True)
    return (x * lax.rsqrt(ms + NORM_EPS) * g) * (1.0 + sc) + sh


def _softplus(x):
    return jnp.maximum(x, 0.0) + jnp.log1p(jnp.exp(-jnp.abs(x)))


def _split_bf16(v, n):
    parts = []
    r = v
    for _ in range(n):
        p = r.astype(BF16)
        parts.append(p)
        r = r - p.astype(F32)
    return parts


def _dot(a, b):
    return jnp.dot(a, b, preferred_element_type=F32)


def _dot_nt(a, b):
    return lax.dot_general(a, b, (((1,), (1,)), ((), ())), preferred_element_type=F32)


def _ada_kernel(c_ref, w_ref, b_ref, o_ref):
    c = c_ref[...]
    a = (c * jax.nn.sigmoid(c)).astype(BF16)
    o_ref[...] = _dot(a, w_ref[...].astype(BF16)) + b_ref[...]


def _ada(c, w, b, layer):
    rows, d = c.shape
    depth, _, n = w.shape
    tn = 1024
    return pl.pallas_call(
        _ada_kernel,
        grid=(n // tn,),
        in_specs=[pl.BlockSpec((rows, d), lambda j: (0, 0)),
                  pl.BlockSpec((None, d, tn), lambda j: (layer, 0, j)),
                  pl.BlockSpec((None, 1, tn), lambda j: (layer, 0, j))],
        out_specs=pl.BlockSpec((rows, tn), lambda j: (0, j)),
        out_shape=jax.ShapeDtypeStruct((rows, n), F32),
        compiler_params=_cparams("parallel"),
        name="ada",
    )(c, w, b.reshape(depth, 1, n))


class _Mod:
    def __init__(self, mod, batch, seq, per_row):
        d = mod.shape[1] // N_MOD
        self.d = d
        self.seq = seq
        self.per_row = per_row
        if per_row:
            self.arr = jnp.repeat(mod, seq, axis=0)
        else:
            self.arr = mod.reshape(batch * N_MOD, 1, d)

    def spec(self, k, tm, grid_rank):
        d = self.d
        if self.per_row:
            if grid_rank == 1:
                return pl.BlockSpec((tm, d), lambda i: (i, k))
            return pl.BlockSpec((tm, d), lambda i, j: (i, k))
        per = self.seq // tm
        if grid_rank == 1:
            return pl.BlockSpec((1, 1, d), lambda i: ((i // per) * N_MOD + k, 0, 0))
        return pl.BlockSpec((1, 1, d), lambda i, j: ((i // per) * N_MOD + k, 0, 0))


def _row_tile(m, seq, per_row, want):
    tm = min(want, m if per_row else seq)
    assert m % tm == 0 and (per_row or seq % tm == 0)
    return tm


def _ffn_kernel(x_ref, sh_ref, sc_ref, gt_ref, gn_ref, win_ref, wout_ref, *rest, d_ff, fc, final):
    if final:
        gf_ref, o_ref, acc_ref = rest
    else:
        o_ref, acc_ref = rest
    x = x_ref[...]
    h = _norm_mod(x, gn_ref[...], _ld(sc_ref), _ld(sh_ref)).astype(BF16)
    for c in range(d_ff // fc):
        u = _dot(h, win_ref[:, c * fc:(c + 1) * fc])
        v = _dot(h, win_ref[:, d_ff + c * fc:d_ff + (c + 1) * fc])
        a = (u * jax.nn.sigmoid(u) * v).astype(BF16)
        part = _dot(a, wout_ref[c * fc:(c + 1) * fc, :])
        if c == 0:
            acc_ref[...] = part
        else:
            acc_ref[...] += part
    y = x + 0.5 * _ld(gt_ref) * acc_ref[...]
    if final:
        ms = jnp.mean(y * y, axis=-1, keepdims=True)
        y = y * lax.rsqrt(ms + NORM_EPS) * gf_ref[...]
    o_ref[...] = y


def _ffn(x, mod, k0, gn, w_in, w_out, g_final=None):
    m, d = x.shape
    d_ff = w_out.shape[0]
    tm = _row_tile(m, mod.seq, mod.per_row, FFN_TM)
    final = g_final is not None
    in_specs = [pl.BlockSpec((tm, d), lambda i: (i, 0)),
                mod.spec(k0, tm, 1), mod.spec(k0 + 1, tm, 1), mod.spec(k0 + 2, tm, 1),
                _resident((1, d)), _resident(w_in.shape), _resident(w_out.shape)]
    args = [x, mod.arr, mod.arr, mod.arr, gn.reshape(1, d), w_in, w_out]
    if final:
        in_specs.append(_resident((1, d)))
        args.append(g_final.reshape(1, d))
    return pl.pallas_call(
        functools.partial(_ffn_kernel, d_ff=d_ff, fc=256, final=final),
        grid=(m // tm,),
        in_specs=in_specs,
        out_specs=pl.BlockSpec((tm, d), lambda i: (i, 0)),
        out_shape=jax.ShapeDtypeStruct((m, d), F32),
        scratch_shapes=[pltpu.VMEM((tm, d), F32)],
        compiler_params=_cparams("parallel"),
        name="ffn",
    )(*args)


def _proj_kernel(x_ref, sh_ref, sc_ref, gn_ref, w_ref, o_ref, h_ref):
    @pl.when(pl.program_id(1) == 0)
    def _():
        h_ref[...] = _norm_mod(x_ref[...], gn_ref[...], _ld(sc_ref), _ld(sh_ref)).astype(BF16)

    res = _dot(h_ref[...], w_ref[...]).astype(o_ref.dtype)
    if len(o_ref.shape) == 2:
        o_ref[...] = res
    else:
        for s in range(o_ref.shape[0]):
            o_ref[s] = res[:, s * LANES:(s + 1) * LANES]


def _proj(x, mod, gn, w, out_dtype, tn, slabs=False):
    m, d = x.shape
    n = w.shape[1]
    tm = _row_tile(m, mod.seq, mod.per_row, PROJ_TM)
    if slabs:
        out_spec = pl.BlockSpec((tn // LANES, tm, LANES), lambda i, j: (j, i, 0))
        out_shape = jax.ShapeDtypeStruct((n // LANES, m, LANES), out_dtype)
    else:
        out_spec = pl.BlockSpec((tm, tn), lambda i, j: (i, j))
        out_shape = jax.ShapeDtypeStruct((m, n), out_dtype)
    return pl.pallas_call(
        _proj_kernel,
        grid=(m // tm, n // tn),
        in_specs=[pl.BlockSpec((tm, d), lambda i, j: (i, 0)),
                  mod.spec(3, tm, 2), mod.spec(4, tm, 2),
                  pl.BlockSpec((1, d), lambda i, j: (0, 0)),
                  pl.BlockSpec((d, tn), lambda i, j: (0, j))],
        out_specs=out_spec,
        out_shape=out_shape,
        scratch_shapes=[pltpu.VMEM((tm, d), BF16)],
        compiler_params=_cparams("parallel", "arbitrary"),
        name="proj",
    )(x, mod.arr, mod.arr, gn.reshape(1, d), w)


def _pair_softmax(q2, k2, v2, bias_of, lo, transposed=False):
    zero = jnp.zeros_like(q2)
    res = []
    for i, qh in enumerate((jnp.where(lo, q2, zero), jnp.where(lo, zero, q2))):
        s = (_dot(qh, k2) if transposed else _dot_nt(qh, k2)) + bias_of(i)
        mx = jnp.max(s, axis=-1, keepdims=True)
        p = jnp.exp(s - mx)
        pb = p.astype(BF16)
        o = _dot_nt(pb, v2) if transposed else _dot(pb, v2)
        res.append((o, mx, jnp.sum(p, axis=-1, keepdims=True)))
    (oa, ma, la), (ob, mb, lb) = res
    return jnp.where(lo, oa, ob), jnp.where(lo, ma, mb), jnp.where(lo, la, lb)


def _merge_softmax(acc, new):
    ao, am, al = acc
    o, m, l = new
    mn = jnp.maximum(am, m)
    a1 = jnp.exp(am - mn)
    a2 = jnp.exp(m - mn)
    return ao * a1 + o * a2, mn, al * a1 + l * a2


def _attn_plan(g):
    dil = A_GROUPS[g][1]
    if dil <= ATTN_MAX_STRIDE:
        return dil, 1, A_KEYS // A_BLOCK
    assert dil % ATTN_MAX_STRIDE == 0
    step = dil // ATTN_MAX_STRIDE
    return ATTN_MAX_STRIDE, step, step * A_KEYS // A_BLOCK


def _attn_prompt_kernel(*refs, seq, n_alias):
    ng = len(A_GROUPS)
    qkv = refs[:3 * ng]
    bias_refs = refs[3 * ng:4 * ng]
    o_ref = refs[4 * ng + n_alias]
    kvt_refs = refs[4 * ng + 1 + n_alias:5 * ng + 1 + n_alias]
    acc_o, acc_m, acc_l = refs[-3:]
    blk = A_BLOCK

    for g in range(ng):
        keep = kvt_refs[g].shape[-1]
        for kv in range(2):
            src = qkv[3 * g + 1 + kv]
            for c in range(keep // blk):
                t = src[pl.ds(seq - keep + c * blk, blk), :].T
                for i in range(2):
                    kvt_refs[g][kv, i, :, c * blk:(c + 1) * blk] = t[i * A_HEAD_DIM:(i + 1) * A_HEAD_DIM]

    hp = pl.program_id(1)
    lo = lax.broadcasted_iota(jnp.int32, (blk, LANES), 1) < A_HEAD_DIM

    def tiles(g, starts, kw):
        stride = _attn_plan(g)[0]
        q_ref, k_ref, v_ref = qkv[3 * g:3 * g + 3]
        bias_ref = bias_refs[g]
        wcols = bias_ref.shape[-1]

        def rows(start, n):
            return pl.ds(start, n) if stride == 1 else pl.ds(start, n, stride=stride)

        bias_of = lambda i: bias_ref[2 * hp + i, :, wcols - kw:wcols]
        news = []
        for qstart, kstart in starts:
            q2 = q_ref[rows(qstart, blk), :].astype(BF16)
            k2 = k_ref[rows(kstart, kw), :].astype(BF16)
            v2 = v_ref[rows(kstart, kw), :].astype(BF16)
            news.append(_pair_softmax(q2, k2, v2, bias_of, lo))
        for (qstart, _), new in zip(starts, news):
            sel = rows(qstart, blk)
            if g > 0:
                new = _merge_softmax((acc_o[sel, :], acc_m[sel, :], acc_l[sel, :]), new)
            acc_o[sel, :] = new[0]
            acc_m[sel, :] = new[1]
            acc_l[sel, :] = new[2]

    for g in range(ng):
        stride, _, wblk = _attn_plan(g)
        assert stride <= ATTN_UNROLL
        nb = seq // stride // blk
        span = blk * stride
        for b in range(min(wblk, nb)):
            tiles(g, [(r + b * span, r) for r in range(stride)], (b + 1) * blk)
        nfull = nb - wblk
        if nfull > 0:
            ub = max(u for u in range(1, nfull + 1) if nfull % u == 0 and u * stride <= ATTN_UNROLL_REST)

            def full(i, carry, g=g, span=span, stride=stride, ub=ub, wblk=wblk):
                b = wblk + i * ub
                tiles(g, [(r + (b + k) * span, r + (b + k - wblk) * span)
                          for k in range(ub) for r in range(stride)], (wblk + 1) * blk)
                return carry

            lax.fori_loop(0, nfull // ub, full, 0)
    o_ref[...] = (acc_o[...] / acc_l[...]).astype(o_ref.dtype)


def _attn_prompt(qkv, bias_tab, kvt_prev, layer, depth, batch, seq, o_dtype):
    npair = A_WIDTH // LANES
    ng = len(A_GROUPS)
    col = lambda j: pl.BlockSpec((None, seq, LANES), lambda b, h: (j * npair + h, b, 0))
    keeps = [min(win, seq) for win, _ in A_GROUPS]
    n_alias = 0 if kvt_prev is None else ng
    alias_specs = [pl.BlockSpec(memory_space=pl.ANY)] * n_alias
    alias_args = [] if kvt_prev is None else list(kvt_prev)
    n_in = 4 * ng
    outs = pl.pallas_call(
        functools.partial(_attn_prompt_kernel, seq=seq, n_alias=n_alias),
        grid=(batch, npair),
        in_specs=[col(j) for j in range(3 * ng)]
        + [pl.BlockSpec(tab.shape, lambda b, h: (0, 0, 0)) for tab in bias_tab] + alias_specs,
        out_specs=[pl.BlockSpec((seq, LANES), lambda b, h: (b, h))]
        + [pl.BlockSpec((None, None, 2, 2, A_HEAD_DIM, keep), lambda b, h: (layer, b, 0, h, 0, 0))
           for keep in keeps],
        out_shape=[jax.ShapeDtypeStruct((batch * seq, A_WIDTH), o_dtype)]
        + [jax.ShapeDtypeStruct((depth, batch, 2, A_HEADS, A_HEAD_DIM, keep), F32) for keep in keeps],
        input_output_aliases={n_in + g: 1 + g for g in range(n_alias)},
        scratch_shapes=[pltpu.VMEM((seq, LANES), F32)] * 3,
        compiler_params=_cparams("parallel", "parallel"),
        name="attn_prompt",
    )(*([qkv] * (3 * ng)), *bias_tab, *alias_args)
    return outs[0], list(outs[1:])


def _attn_sample_kernel(qkv_ref, buf1_ref, buf2_ref, buf3_ref, bb1, bn1, bb2, bn2, bb3, bn3,
                        o_ref, knew_scr, vnew_scr):
    t = qkv_ref.shape[1]
    npair = A_HEADS // 2
    bufs = (buf1_ref, buf2_ref, buf3_ref)
    bias = ((bb1, bn1), (bb2, bn2), (bb3, bn3))
    lo = lax.broadcasted_iota(jnp.int32, (t, LANES), 1) < A_HEAD_DIM
    knew_scr[...] = jnp.zeros_like(knew_scr)
    vnew_scr[...] = jnp.zeros_like(vnew_scr)
    for hp in range(npair):
        cols = slice(hp * LANES, (hp + 1) * LANES)
        acc = None
        for g in range(len(A_GROUPS)):
            q2 = qkv_ref[3 * g * npair + hp].astype(BF16)
            wb = bufs[g].shape[-1]
            kb = bufs[g][0, 2 * hp:2 * hp + 2].reshape(LANES, wb).astype(BF16)
            vb = bufs[g][1, 2 * hp:2 * hp + 2].reshape(LANES, wb).astype(BF16)
            knew_scr[g, hp, 0:t, :] = qkv_ref[(3 * g + 1) * npair + hp]
            vnew_scr[g, hp, 0:t, :] = qkv_ref[(3 * g + 2) * npair + hp]
            kn = knew_scr[g, hp].astype(BF16)
            vn = vnew_scr[g, hp].astype(BF16)
            bbuf, bnew = bias[g]
            new = _pair_softmax(q2, kb, vb, lambda i: bbuf[2 * hp + i], lo, transposed=True)
            new = _merge_softmax(new, _pair_softmax(q2, kn, vn, lambda i: bnew[2 * hp + i], lo))
            acc = new if acc is None else _merge_softmax(acc, new)
        o_ref[:, cols] = acc[0] / acc[2]


def _attn_sample(qkv, caches, layer, bias_s, batch, t):
    aw = A_WIDTH
    ng = len(A_GROUPS)
    bufs = [jnp.transpose(c, (0, 1, 3, 4, 5, 2)) for c in caches]
    buf_specs = [pl.BlockSpec((None, None) + bv.shape[2:], lambda b: (layer, b, 0, 0, 0, 0)) for bv in bufs]
    tabs = [tab for pair in bias_s for tab in pair]
    return pl.pallas_call(
        _attn_sample_kernel,
        grid=(batch,),
        in_specs=[pl.BlockSpec((3 * ng * aw // LANES, t, LANES), lambda b: (0, b, 0))] + buf_specs
        + [pl.BlockSpec(tab.shape, lambda b: (0, 0, 0)) for tab in tabs],
        out_specs=pl.BlockSpec((t, aw), lambda b: (b, 0)),
        out_shape=jax.ShapeDtypeStruct((batch * t, aw), F32),
        scratch_shapes=[pltpu.VMEM((ng, aw // LANES, LANES, LANES), F32)] * 2,
        compiler_params=_cparams("parallel"),
        name="attn_sample",
    )(qkv, *bufs, *tabs)


def _ssd_kernel(xbc_ref, z_ref, dt_ref, conv0_ref, st0_ref, cw_ref, cb_ref, dtb_ref, aneg_ref,
                dsk_ref, gn_ref, e_ref, y_ref, convo_ref, sto_ref, xpad, st_t, y_scr, *, lv, nch):
    q = SSD_CHUNK
    c = pl.program_id(1)
    width = y_ref.shape[1]
    nst = B_STATE
    hpg = width // B_HEAD_DIM // B_GROUPS // 2

    @pl.when(c == 0)
    def _():
        xpad[...] = jnp.zeros_like(xpad)
        xpad[0:SUBLANES, :] = conv0_ref[0]
        for j in range(width // LANES):
            st_t[:, j * LANES:(j + 1) * LANES] = st0_ref[0, j * LANES:(j + 1) * LANES, :].T

    @pl.when(c > 0)
    def _():
        xpad[0:SUBLANES, :] = xpad[q:q + SUBLANES, :]

    xpad[SUBLANES:SUBLANES + lv, :] = xbc_ref[...].astype(F32)

    w = cw_ref[...]
    conv = cb_ref[...]
    for k in range(B_CONV):
        off = SUBLANES - (B_CONV - 1) + k
        conv = conv + xpad[off:off + q, :] * w[k:k + 1, :]
    act = conv * jax.nn.sigmoid(conv)
    xs = act[:, :width]
    bm = [act[:, width + gi * nst:width + (gi + 1) * nst] for gi in range(B_GROUPS)]
    cm = [act[:, width + (B_GROUPS + gi) * nst:width + (B_GROUPS + gi + 1) * nst] for gi in range(B_GROUPS)]

    dt_raw = dt_ref[...]
    if lv < q:
        dt_raw = jnp.concatenate([dt_raw, jnp.zeros((q - lv, LANES), F32)], axis=0)
    dt = _softplus(dt_raw + dtb_ref[...])
    if lv < q:
        dt = jnp.where(lax.broadcasted_iota(jnp.int32, (q, LANES), 0) < lv, dt, 0.0)
    dta = dt * aneg_ref[...]

    row = lax.broadcasted_iota(jnp.int32, (q, q), 0)
    col = lax.broadcasted_iota(jnp.int32, (q, q), 1)
    causal = row >= col
    tri = jnp.where(causal, 1.0, 0.0).astype(BF16)
    acum = sum(_dot(tri, p) for p in _split_bf16(dta, 3))
    acum_t = acum.T
    e = e_ref[...]
    dt_x = sum(_dot(p, e) for p in _split_bf16(dt, 2))
    acum_x = sum(_dot(p, e) for p in _split_bf16(acum, 2))
    last_x = acum_x[q - 1:q, :]
    ea_x = jnp.exp(acum_x)
    xdt = xs * dt_x
    xdt_te = (xdt * jnp.exp(last_x - acum_x)).astype(BF16)
    xdt_b = xdt.astype(BF16)
    chunk_decay = jnp.exp(last_x)
    lo = lax.broadcasted_iota(jnp.int32, (q, LANES), 1) < B_HEAD_DIM

    for gi in range(B_GROUPS):
        cmb = cm[gi].astype(BF16)
        cb = _dot_nt(cmb, bm[gi].astype(BF16))
        bm_t = bm[gi].T.astype(BF16)
        for hp in range(gi * hpg, (gi + 1) * hpg):
            cols = slice(hp * LANES, (hp + 1) * LANES)
            ax = acum_x[:, cols]
            ax_r = pltpu.roll(ax, B_HEAD_DIM, axis=1)
            ys = []
            for col_v, h in ((jnp.where(lo, ax, ax_r), 2 * hp), (jnp.where(lo, ax_r, ax), 2 * hp + 1)):
                seg = col_v - acum_t[h:h + 1, :]
                dec = jnp.exp(jnp.where(causal, seg, -jnp.inf))
                ys.append(_dot((cb * dec).astype(BF16), xdt_b[:, cols]))
            st_old = st_t[:, cols]
            y_off = _dot(cmb, st_old.astype(BF16)) * ea_x[:, cols]
            st_t[:, cols] = st_old * chunk_decay[:, cols] + _dot(bm_t, xdt_te[:, cols])
            y_scr[:, cols] = jnp.where(lo, ys[0], ys[1]) + y_off + dsk_ref[:, cols] * xs[:, cols]

    z = z_ref[...].astype(F32)
    if lv < q:
        z = jnp.concatenate([z, jnp.zeros((q - lv, width), F32)], axis=0)
    y = y_scr[...] * (z * jax.nn.sigmoid(z))
    gw = width // B_GROUPS
    for gi in range(B_GROUPS):
        yg = y[:, gi * gw:(gi + 1) * gw]
        ms = jnp.mean(yg * yg, axis=-1, keepdims=True)
        yn = yg * lax.rsqrt(ms + NORM_EPS) * gn_ref[:, gi * gw:(gi + 1) * gw]
        y_ref[:, gi * gw:(gi + 1) * gw] = yn[0:lv].astype(y_ref.dtype)

    @pl.when(c == nch - 1)
    def _():
        convo_ref[0] = xpad[lv:lv + SUBLANES, :]
        for j in range(width // LANES):
            sto_ref[0, j * LANES:(j + 1) * LANES, :] = st_t[:, j * LANES:(j + 1) * LANES].T


def _ssd(big, qkv, dt_col, conv0, st0, lw, batch, seq, o_dtype):
    width = lw["w_b_proj"].shape[0]
    cch = lw["conv_b_w"].shape[1]
    nheads = width // B_HEAD_DIM
    lv = min(seq, SSD_CHUNK)
    nch = seq // lv
    assert lv == SSD_CHUNK or nch == 1
    pad = lambda v: jnp.pad(v.astype(F32), (0, LANES - nheads)).reshape(1, LANES)
    e = (np.arange(LANES)[:, None] == (np.arange(width)[None, :] // B_HEAD_DIM)).astype(np.float32)
    conv0p = jnp.pad(conv0, ((0, 0), (SUBLANES - (B_CONV - 1), 0), (0, 0)))
    y, convo, sto = pl.pallas_call(
        functools.partial(_ssd_kernel, lv=lv, nch=nch),
        grid=(batch, nch),
        in_specs=[pl.BlockSpec((lv, cch), lambda b, c: (b * nch + c, 4)),
                  pl.BlockSpec((lv, width), lambda b, c: (b * nch + c, 0)),
                  pl.BlockSpec((None, lv, LANES), lambda b, c: (dt_col, b * nch + c, 0)),
                  pl.BlockSpec((1, SUBLANES, cch), lambda b, c: (b, 0, 0)),
                  pl.BlockSpec((1, width, B_STATE), lambda b, c: (b, 0, 0)),
                  pl.BlockSpec((B_CONV, cch), lambda b, c: (0, 0)),
                  pl.BlockSpec((1, cch), lambda b, c: (0, 0)),
                  pl.BlockSpec((1, LANES), lambda b, c: (0, 0)),
                  pl.BlockSpec((1, LANES), lambda b, c: (0, 0)),
                  pl.BlockSpec((1, width), lambda b, c: (0, 0)),
                  pl.BlockSpec((1, width), lambda b, c: (0, 0)),
                  pl.BlockSpec((LANES, width), lambda b, c: (0, 0))],
        out_specs=[pl.BlockSpec((lv, width), lambda b, c: (b * nch + c, 0)),
                   pl.BlockSpec((1, SUBLANES, cch), lambda b, c: (b, 0, 0)),
                   pl.BlockSpec((1, width, B_STATE), lambda b, c: (b, 0, 0))],
        out_shape=[jax.ShapeDtypeStruct((batch * seq, width), o_dtype),
                   jax.ShapeDtypeStruct((batch, SUBLANES, cch), F32),
                   jax.ShapeDtypeStruct((batch, width, B_STATE), F32)],
        scratch_shapes=[pltpu.VMEM((SUBLANES + SSD_CHUNK, cch), F32),
                        pltpu.VMEM((B_STATE, width), F32),
                        pltpu.VMEM((SSD_CHUNK, width), F32)],
        compiler_params=_cparams("parallel", "arbitrary"),
        name="ssd",
    )(big, big, qkv, conv0p, st0.reshape(batch, width, B_STATE),
      lw["conv_b_w"], lw["conv_b_b"].reshape(1, cch), pad(lw["dt_bias"]),
      pad(-jnp.exp(lw["a_log"].astype(F32))),
      jnp.repeat(lw["d_skip"].astype(F32), B_HEAD_DIM).reshape(1, width),
      lw["g_ssm_norm"].reshape(1, width), jnp.asarray(e, BF16))
    return (y, convo[:, SUBLANES - (B_CONV - 1):, :],
            sto.reshape(batch, nheads, B_HEAD_DIM, B_STATE))


def _lru_kernel(xc_ref, gc_ref, conv0_ref, h0_ref, cw_ref, cb_ref, wr_ref, wi_ref, br_ref, bi_ref,
                lam_ref, y_ref, convo_ref, ho_ref, xpad, hprev, hs_scr, *, q, nch):
    c = pl.program_id(1)
    width = y_ref.shape[1]

    @pl.when(c == 0)
    def _():
        xpad[0:SUBLANES, :] = conv0_ref[0]
        hprev[...] = h0_ref[0]

    @pl.when(c > 0)
    def _():
        xpad[0:SUBLANES, :] = xpad[q:q + SUBLANES, :]

    xpad[SUBLANES:SUBLANES + q, :] = xc_ref[...].astype(F32)
    w = cw_ref[...]
    x = cb_ref[...]
    for k in range(C_CONV):
        off = SUBLANES - (C_CONV - 1) + k
        x = x + xpad[off:off + q, :] * w[k:k + 1, :]

    bd = width // C_BLOCKS
    rs, gs = [], []
    for j in range(C_BLOCKS):
        xb = x[:, j * bd:(j + 1) * bd].astype(BF16)
        rs.append(_dot(xb, wr_ref[j]))
        gs.append(_dot(xb, wi_ref[j]))
    rg = jax.nn.sigmoid(jnp.concatenate(rs, axis=1) + br_ref[...])
    ig = jax.nn.sigmoid(jnp.concatenate(gs, axis=1) + bi_ref[...])
    log_a = -C_POW * rg * _softplus(-lam_ref[...])
    a = jnp.exp(log_a)
    u = x * ig * jnp.sqrt(-jnp.tanh(log_a) * (a * a + 1.0))

    rowi = lax.broadcasted_iota(jnp.int32, (SUBLANES, width), 0)
    h = hprev[...]
    for g in range(q // SUBLANES):
        ag = a[g * SUBLANES:(g + 1) * SUBLANES]
        bg = u[g * SUBLANES:(g + 1) * SUBLANES]
        s = 1
        while s < SUBLANES:
            keep = rowi >= s
            a_sh = jnp.where(keep, pltpu.roll(ag, s, axis=0), 1.0)
            b_sh = jnp.where(keep, pltpu.roll(bg, s, axis=0), 0.0)
            bg = ag * b_sh + bg
            ag = ag * a_sh
            s *= 2
        hg = bg + ag * h
        hs_scr[g * SUBLANES:(g + 1) * SUBLANES, :] = hg
        h = hg[SUBLANES - 1:SUBLANES]
    hprev[...] = h
    y_ref[...] = (hs_scr[...] * jax.nn.gelu(gc_ref[...].astype(F32))).astype(y_ref.dtype)

    @pl.when(c == nch - 1)
    def _():
        convo_ref[0] = xpad[q:q + SUBLANES, :]
        ho_ref[0] = h


def _lru(big, conv0, h0, lw, batch, seq, o_dtype):
    width = lw["w_c_proj"].shape[0]
    q = min(seq, 128)
    nch = seq // q
    bd = width // C_BLOCKS
    conv0p = jnp.pad(conv0, ((0, 0), (SUBLANES - (C_CONV - 1), 0), (0, 0)))
    vec = lambda v: v.astype(F32).reshape(1, width)
    y, convo, ho = pl.pallas_call(
        functools.partial(_lru_kernel, q=q, nch=nch),
        grid=(batch, nch),
        in_specs=[pl.BlockSpec((q, width), lambda b, c: (b * nch + c, 1)),
                  pl.BlockSpec((q, width), lambda b, c: (b * nch + c, 2)),
                  pl.BlockSpec((1, SUBLANES, width), lambda b, c: (b, 0, 0)),
                  pl.BlockSpec((1, 1, width), lambda b, c: (b, 0, 0)),
                  pl.BlockSpec((C_CONV, width), lambda b, c: (0, 0)),
                  pl.BlockSpec((1, width), lambda b, c: (0, 0)),
                  pl.BlockSpec((C_BLOCKS, bd, bd), lambda b, c: (0, 0, 0)),
                  pl.BlockSpec((C_BLOCKS, bd, bd), lambda b, c: (0, 0, 0)),
                  pl.BlockSpec((1, width), lambda b, c: (0, 0)),
                  pl.BlockSpec((1, width), lambda b, c: (0, 0)),
                  pl.BlockSpec((1, width), lambda b, c: (0, 0))],
        out_specs=[pl.BlockSpec((q, width), lambda b, c: (b * nch + c, 0)),
                   pl.BlockSpec((1, SUBLANES, width), lambda b, c: (b, 0, 0)),
                   pl.BlockSpec((1, 1, width), lambda b, c: (b, 0, 0))],
        out_shape=[jax.ShapeDtypeStruct((batch * seq, width), o_dtype),
                   jax.ShapeDtypeStruct((batch, SUBLANES, width), F32),
                   jax.ShapeDtypeStruct((batch, 1, width), F32)],
        scratch_shapes=[pltpu.VMEM((SUBLANES + q, width), F32),
                        pltpu.VMEM((1, width), F32),
                        pltpu.VMEM((q, width), F32)],
        compiler_params=_cparams("parallel", "arbitrary"),
        name="lru",
    )(big, big, conv0p, h0.reshape(batch, 1, width), lw["conv_c_w"], vec(lw["conv_c_b"]),
      lw["w_rgate"].astype(BF16), lw["w_igate"].astype(BF16), vec(lw["b_rgate"]), vec(lw["b_igate"]),
      vec(lw["lru_lambda"]))
    return y, convo[:, SUBLANES - (C_CONV - 1):, :], ho.reshape(batch, width)


def _merge_kernel(x_ref, gt_ref, oa_ref, yb_ref, yc_ref, gates_ref, wa_ref, wb_ref, wc_ref, wo_ref,
                  out_ref):
    d = x_ref.shape[1]
    ya = _dot(oa_ref[...].astype(BF16), wa_ref[...])
    yb = _dot(yb_ref[...].astype(BF16), wb_ref[...])
    yc = _dot(yc_ref[...].astype(BF16), wc_ref[...])
    sg = jax.nn.sigmoid(gates_ref[...].astype(F32))
    mixed = sg[:, :d] * ya + sg[:, d:2 * d] * yb + sg[:, 2 * d:] * yc
    out_ref[...] = x_ref[...] + _ld(gt_ref) * _dot(mixed.astype(BF16), wo_ref[...])


def _merge(x, mod, oa, yb, yc, big, lw):
    m, d = x.shape
    aw = A_WIDTH
    tm = _row_tile(m, mod.seq, mod.per_row, MERGE_TM)
    row = lambda wd, j=0: pl.BlockSpec((tm, wd), lambda i: (i, j))
    return pl.pallas_call(
        _merge_kernel,
        grid=(m // tm,),
        in_specs=[row(d), mod.spec(5, tm, 1), row(aw), row(yb.shape[1]), row(yc.shape[1]), row(3 * d, 1)]
        + [_resident(lw[k].shape) for k in ("w_a_proj", "w_b_proj", "w_c_proj", "w_out")],
        out_specs=row(d),
        out_shape=jax.ShapeDtypeStruct((m, d), F32),
        compiler_params=_cparams("parallel"),
        name="merge",
    )(x, mod.arr, oa, yb, yc, big,
      lw["w_a_proj"], lw["w_b_proj"], lw["w_c_proj"], lw["w_out"])


def _t5_bucket(dist):
    dist = np.asarray(dist)
    large = REL_MAX_EXACT + (np.log(np.maximum(dist, 1) / REL_MAX_EXACT)
                             / math.log(REL_MAX_DISTANCE / REL_MAX_EXACT)
                             * (REL_BUCKETS - REL_MAX_EXACT)).astype(np.int64)
    large = np.minimum(large, REL_BUCKETS - 1)
    return np.where(dist < REL_MAX_EXACT, dist, large).astype(np.int32)


def _group_bias(rel_bias, g):
    dil = A_GROUPS[g][1]
    buckets = _t5_bucket(np.arange(A_KEYS + 1) * dil)
    return rel_bias[buckets][:, g * A_HEADS:(g + 1) * A_HEADS].T.astype(F32)


def _prompt_bias_table(bias, g):
    _, step, wblk = _attn_plan(g)
    h = bias.shape[0]
    cols = (wblk + 1) * A_BLOCK
    n = cols + A_BLOCK - 1
    x = np.arange(n + 1)
    x = np.where(x < cols, x, x - (n + 1))
    dist = wblk * A_BLOCK - x
    valid = (dist >= 0) & (dist % step == 0) & (dist // step <= A_KEYS)
    v = jnp.where(valid[None], bias[:, np.clip(dist // step, 0, A_KEYS)], -jnp.inf)
    return jnp.tile(v, (1, A_BLOCK))[:, :A_BLOCK * n].reshape(h, A_BLOCK, n)[:, :, :cols]


def _sample_bias_tables(bias, wb, dil, t):
    h = bias.shape[0]
    front = t - 1 + wb - A_KEYS * dil
    assert front >= 0
    sparse = jnp.concatenate([bias[:, ::-1, None], jnp.full((h, A_KEYS + 1, dil - 1), -jnp.inf, F32)], axis=2)
    base = jnp.concatenate([jnp.full((h, front), -jnp.inf, F32), sparse.reshape(h, (A_KEYS + 1) * dil)], axis=1)
    tab_buf = jnp.stack([base[:, t - 1 - tq:t - 1 - tq + wb] for tq in range(t)], axis=1)
    tq = np.arange(t)[:, None]
    dist = tq - np.arange(LANES)[None, :]
    valid = (dist >= 0) & (dist % dil == 0) & (np.arange(LANES)[None, :] < t)
    tab_new = jnp.where(valid[None], bias[:, np.clip(dist // dil, 0, A_KEYS)], -jnp.inf)
    return tab_buf, tab_new


def _mixer_weights(w_in):
    na = len(A_GROUPS) * A_WIDTH
    d = w_in.shape[0]
    offs = np.cumsum([0, na, na, na, 1024, 1536, 16, 1024, 1024, 3 * d])
    seg = lambda i: w_in[:, offs[i]:offs[i + 1]]
    qa, ka, va, zb, xbc, dtb, xc, gc, gates = [seg(i) for i in range(9)]
    cols = []
    for g in range(len(A_GROUPS)):
        sl = slice(g * A_WIDTH, (g + 1) * A_WIDTH)
        cols += [qa[:, sl] * (A_HEAD_DIM ** -0.5), ka[:, sl], va[:, sl]]
    w_qkv = jnp.concatenate(cols + [dtb, jnp.zeros((d, DT_PAD - dtb.shape[1]), w_in.dtype)], axis=1)
    w_big = jnp.concatenate([zb, xc, gc, gates, xbc], axis=1)
    return w_qkv.astype(BF16), w_big.astype(BF16)


def _mixer(x, mod, lw, st, bias, prompt, batch, seq):
    act_dtype = BF16 if prompt else F32
    qkv = _proj(x, mod, lw["g_mix"], lw["w_qkv"], F32, QKV_TN, slabs=True)
    big = _proj(x, mod, lw["g_mix"], lw["w_big"], act_dtype, BIG_TN)
    ng = len(A_GROUPS)
    npair = A_WIDTH // LANES
    if prompt:
        oa, new_kv = _attn_prompt(qkv, bias, st["kvt"], st["layer"], st["depth"], batch, seq, act_dtype)
    else:
        oa = _attn_sample(qkv, st["kv"], st["layer"], bias, batch, seq)
        new_kv = [jnp.transpose(qkv[(3 * g + 1) * npair:(3 * g + 3) * npair], (1, 0, 2))
                  .reshape(batch, seq, 2, A_HEADS, A_HEAD_DIM) for g in range(ng)]
    dt_col = 3 * ng * npair
    yb, conv_b_new, ssm_new = _ssd(big, qkv, dt_col, st["conv_b"], st["ssm"], lw, batch, seq, act_dtype)
    yc, conv_c_new, lru_new = _lru(big, st["conv_c"], st["lru"], lw, batch, seq, act_dtype)
    x = _merge(x, mod, oa, yb, yc, big, lw)
    return x, (new_kv[0], new_kv[1], new_kv[2], conv_b_new, ssm_new, conv_c_new, lru_new)


def _block(x, mod, lw, st, bias, prompt, batch, seq, g_final):
    x = _ffn(x, mod, 0, lw["g_ff1"], lw["w_ff1_in"], lw["w_ff1_out"])
    x, new_st = _mixer(x, mod, lw, st, bias, prompt, batch, seq)
    x = _ffn(x, mod, 6, lw["g_ff2"], lw["w_ff2_in"], lw["w_ff2_out"], g_final)
    return x, new_st


@jax.jit
def _forward(x_prompt, x_sample, c_prompt, c_sample, cache_win1_kv, cache_win2_kv, cache_win3_kv,
             state_conv_b, state_ssm, state_conv_c, state_lru, rel_bias, w_ada, b_ada, g_ff1,
             w_ff1_in, w_ff1_out, g_mix, w_in, w_a_proj, conv_b_w, conv_b_b, dt_bias, a_log, d_skip,
             g_ssm_norm, w_b_proj, conv_c_w, conv_c_b, w_rgate, b_rgate, w_igate, b_igate, lru_lambda,
             w_c_proj, w_out, g_ff2, w_ff2_in, w_ff2_out, g_final):
    bp, lp, d = x_prompt.shape
    bs, ls, _ = x_sample.shape
    depth = w_ada.shape[0]
    caches = (cache_win1_kv, cache_win2_kv, cache_win3_kv)
    biases = [_group_bias(rel_bias, g) for g in range(len(A_GROUPS))]
    bias_p = [_prompt_bias_table(b, g) for g, b in enumerate(biases)]
    bias_s = [_sample_bias_tables(b, caches[g].shape[2], A_GROUPS[g][1], ls) for g, b in enumerate(biases)]

    yp = x_prompt.reshape(bp * lp, d)
    ys = x_sample.reshape(bs * ls, d)
    c_all = jnp.concatenate([c_prompt, c_sample], axis=0)
    new_p = [[] for _ in range(7)]
    new_s = [[] for _ in range(7)]
    kvt_p = None
    for l in range(depth):
        w_qkv, w_big = _mixer_weights(w_in[l])
        lw = dict(g_ff1=g_ff1[l], w_ff1_in=w_ff1_in[l].astype(BF16), w_ff1_out=w_ff1_out[l].astype(BF16),
                  g_mix=g_mix[l], w_qkv=w_qkv, w_big=w_big, w_a_proj=w_a_proj[l].astype(BF16),
                  conv_b_w=conv_b_w[l], conv_b_b=conv_b_b[l], dt_bias=dt_bias[l], a_log=a_log[l],
                  d_skip=d_skip[l], g_ssm_norm=g_ssm_norm[l], w_b_proj=w_b_proj[l].astype(BF16),
                  conv_c_w=conv_c_w[l], conv_c_b=conv_c_b[l], w_rgate=w_rgate[l], b_rgate=b_rgate[l],
                  w_igate=w_igate[l], b_igate=b_igate[l], lru_lambda=lru_lambda[l],
                  w_c_proj=w_c_proj[l].astype(BF16), w_out=w_out[l].astype(BF16),
                  g_ff2=g_ff2[l], w_ff2_in=w_ff2_in[l].astype(BF16), w_ff2_out=w_ff2_out[l].astype(BF16))
        mod_all = _ada(c_all, w_ada, b_ada, l)
        mod_p = _Mod(mod_all[:bp], bp, lp, per_row=False)
        mod_s = _Mod(mod_all[bp:], bs, ls, per_row=True)
        gf = g_final if l == depth - 1 else None
        st_p = dict(kvt=kvt_p, layer=l, depth=depth,
                    conv_b=jnp.zeros((bp, B_CONV - 1, conv_b_w.shape[2]), F32),
                    ssm=jnp.zeros((bp,) + state_ssm.shape[2:], F32),
                    conv_c=jnp.zeros((bp, C_CONV - 1, conv_c_w.shape[2]), F32),
                    lru=jnp.zeros((bp, state_lru.shape[2]), F32))
        st_s = dict(kv=caches, layer=l, conv_b=state_conv_b[l], ssm=state_ssm[l],
                    conv_c=state_conv_c[l], lru=state_lru[l])
        yp, stp = _block(yp, mod_p, lw, st_p, bias_p, True, bp, lp, gf)
        ys, sts = _block(ys, mod_s, lw, st_s, bias_s, False, bs, ls, gf)
        kvt_p = list(stp[:3])
        for i in range(7):
            new_p[i].append(stp[i])
            new_s[i].append(sts[i])
    outs_p = [jnp.transpose(v, (0, 1, 5, 2, 3, 4)) for v in kvt_p] + [jnp.stack(v, 0) for v in new_p[3:]]
    outs_s = [jnp.stack(v, 0) for v in new_s]
    return (yp.reshape(bp, lp, d), ys.reshape(bs, ls, d), *outs_p, *outs_s)


def kernel(x_prompt, x_sample, c_prompt, c_sample, cache_win1_kv, cache_win2_kv, cache_win3_kv,
           state_conv_b, state_ssm, state_conv_c, state_lru, rel_bias, w_ada, b_ada, g_ff1,
           w_ff1_in, w_ff1_out, g_mix, w_in, w_a_proj, conv_b_w, conv_b_b, dt_bias, a_log, d_skip,
           g_ssm_norm, w_b_proj, conv_c_w, conv_c_b, w_rgate, b_rgate, w_igate, b_igate, lru_lambda,
           w_c_proj, w_out, g_ff2, w_ff2_in, w_ff2_out, g_final):
    return _forward(x_prompt, x_sample, c_prompt, c_sample, cache_win1_kv, cache_win2_kv,
                    cache_win3_kv, state_conv_b, state_ssm, state_conv_c, state_lru, rel_bias, w_ada,
                    b_ada, g_ff1, w_ff1_in, w_ff1_out, g_mix, w_in, w_a_proj, conv_b_w, conv_b_b,
                    dt_bias, a_log, d_skip, g_ssm_norm, w_b_proj, conv_c_w, conv_c_b, w_rgate,
                    b_rgate, w_igate, b_igate, lru_lambda, w_c_proj, w_out, g_ff2, w_ff2_in,
                    w_ff2_out, g_final)
```

```python
import functools
import math

import jax
import jax.numpy as jnp
import numpy as np
from jax import lax
from jax.experimental import pallas as pl
from jax.experimental.pallas import tpu as pltpu

F32 = jnp.float32
BF16 = jnp.bfloat16

NORM_EPS = 1e-6
N_MOD = 9
A_GROUPS = ((128, 1), (512, 4), (2048, 16))
A_HEADS = 8
A_HEAD_DIM = 64
A_WIDTH = A_HEADS * A_HEAD_DIM
A_KEYS = 128
A_BLOCK = 128
REL_BUCKETS = 32
REL_MAX_EXACT = 16
REL_MAX_DISTANCE = 2048
B_HEAD_DIM = 64
B_GROUPS = 2
B_STATE = 128
B_CONV = 4
C_BLOCKS = 8
C_CONV = 4
C_POW = 8.0

LANES = 128
SUBLANES = 8
SSD_CHUNK = 128
FFN_TM = 1024
PROJ_TM = 2048
MERGE_TM = 512
QKV_TN = 1024
BIG_TN = 1536
DT_PAD = 512
ATTN_UNROLL = 8
ATTN_UNROLL_REST = 12
ATTN_MAX_STRIDE = 4
VMEM_LIMIT = 56 * 1024 * 1024


def _cparams(*sem):
    return pltpu.CompilerParams(dimension_semantics=sem, vmem_limit_bytes=VMEM_LIMIT)


def _resident(shape):
    nd = len(shape)
    return pl.BlockSpec(shape, lambda *_: (0,) * nd, pipeline_mode=pl.Buffered(1))


def _ld(ref):
    return ref[0] if len(ref.shape) == 3 else ref[...]


def _norm_mod(x, g, sc, sh):
    ms = jnp.mean(x * x, axis=-1, keepdims=True)
    return (x * lax.rsqrt(ms + NORM_EPS) * g) * (1.0 + sc) + sh


def _softplus(x):
    return jnp.maximum(x, 0.0) + jnp.log1p(jnp.exp(-jnp.abs(x)))


def _split_bf16(v, n):
    parts = []
    r = v
    for _ in range(n):
        p = r.astype(BF16)
        parts.append(p)
        r = r - p.astype(F32)
    return parts


def _dot(a, b):
    return jnp.dot(a, b, preferred_element_type=F32)


def _dot_nt(a, b):
    return lax.dot_general(a, b, (((1,), (1,)), ((), ())), preferred_element_type=F32)


def _ada_kernel(c_ref, w_ref, b_ref, o_ref):
    c = c_ref[...]
    a = (c * jax.nn.sigmoid(c)).astype(BF16)
    o_ref[...] = _dot(a, w_ref[...].astype(BF16)) + b_ref[...]


def _ada(c, w, b, layer):
    rows, d = c.shape
    depth, _, n = w.shape
    tn = 1024
    return pl.pallas_call(
        _ada_kernel,
        grid=(n // tn,),
        in_specs=[pl.BlockSpec((rows, d), lambda j: (0, 0)),
                  pl.BlockSpec((None, d, tn), lambda j: (layer, 0, j)),
                  pl.BlockSpec((None, 1, tn), lambda j: (layer, 0, j))],
        out_specs=pl.BlockSpec((rows, tn), lambda j: (0, j)),
        out_shape=jax.ShapeDtypeStruct((rows, n), F32),
        compiler_params=_cparams("parallel"),
        name="ada",
    )(c, w, b.reshape(depth, 1, n))


class _Mod:
    def __init__(self, mod, batch, seq, per_row):
        d = mod.shape[1] // N_MOD
        self.d = d
        self.seq = seq
        self.per_row = per_row
        if per_row:
            self.arr = jnp.repeat(mod, seq, axis=0)
        else:
            self.arr = mod.reshape(batch * N_MOD, 1, d)

    def spec(self, k, tm, grid_rank):
        d = self.d
        if self.per_row:
            if grid_rank == 1:
                return pl.BlockSpec((tm, d), lambda i: (i, k))
            return pl.BlockSpec((tm, d), lambda i, j: (i, k))
        per = self.seq // tm
        if grid_rank == 1:
            return pl.BlockSpec((1, 1, d), lambda i: ((i // per) * N_MOD + k, 0, 0))
        return pl.BlockSpec((1, 1, d), lambda i, j: ((i // per) * N_MOD + k, 0, 0))


def _row_tile(m, seq, per_row, want):
    tm = min(want, m if per_row else seq)
    assert m % tm == 0 and (per_row or seq % tm == 0)
    return tm


def _ffn_kernel(x_ref, sh_ref, sc_ref, gt_ref, gn_ref, win_ref, wout_ref, *rest, d_ff, fc, final):
    if final:
        gf_ref, o_ref, acc_ref = rest
    else:
        o_ref, acc_ref = rest
    x = x_ref[...]
    h = _norm_mod(x, gn_ref[...], _ld(sc_ref), _ld(sh_ref)).astype(BF16)
    for c in range(d_ff // fc):
        u = _dot(h, win_ref[:, c * fc:(c + 1) * fc])
        v = _dot(h, win_ref[:, d_ff + c * fc:d_ff + (c + 1) * fc])
        a = (u * jax.nn.sigmoid(u) * v).astype(BF16)
        part = _dot(a, wout_ref[c * fc:(c + 1) * fc, :])
        if c == 0:
            acc_ref[...] = part
        else:
            acc_ref[...] += part
    y = x + 0.5 * _ld(gt_ref) * acc_ref[...]
    if final:
        ms = jnp.mean(y * y, axis=-1, keepdims=True)
        y = y * lax.rsqrt(ms + NORM_EPS) * gf_ref[...]
    o_ref[...] = y


def _ffn(x, mod, k0, gn, w_in, w_out, g_final=None):
    m, d = x.shape
    d_ff = w_out.shape[0]
    tm = _row_tile(m, mod.seq, mod.per_row, FFN_TM)
    final = g_final is not None
    in_specs = [pl.BlockSpec((tm, d), lambda i: (i, 0)),
                mod.spec(k0, tm, 1), mod.spec(k0 + 1, tm, 1), mod.spec(k0 + 2, tm, 1),
                _resident((1, d)), _resident(w_in.shape), _resident(w_out.shape)]
    args = [x, mod.arr, mod.arr, mod.arr, gn.reshape(1, d), w_in, w_out]
    if final:
        in_specs.append(_resident((1, d)))
        args.append(g_final.reshape(1, d))
    return pl.pallas_call(
        functools.partial(_ffn_kernel, d_ff=d_ff, fc=256, final=final),
        grid=(m // tm,),
        in_specs=in_specs,
        out_specs=pl.BlockSpec((tm, d), lambda i: (i, 0)),
        out_shape=jax.ShapeDtypeStruct((m, d), F32),
        scratch_shapes=[pltpu.VMEM((tm, d), F32)],
        compiler_params=_cparams("parallel"),
        name="ffn",
    )(*args)


def _proj_kernel(x_ref, sh_ref, sc_ref, gn_ref, w_ref, o_ref, h_ref):
    @pl.when(pl.program_id(1) == 0)
    def _():
        h_ref[...] = _norm_mod(x_ref[...], gn_ref[...], _ld(sc_ref), _ld(sh_ref)).astype(BF16)

    res = _dot(h_ref[...], w_ref[...]).astype(o_ref.dtype)
    if len(o_ref.shape) == 2:
        o_ref[...] = res
    else:
        for s in range(o_ref.shape[0]):
            o_ref[s] = res[:, s * LANES:(s + 1) * LANES]


def _proj(x, mod, gn, w, out_dtype, tn, slabs=False):
    m, d = x.shape
    n = w.shape[1]
    tm = _row_tile(m, mod.seq, mod.per_row, PROJ_TM)
    if slabs:
        out_spec = pl.BlockSpec((tn // LANES, tm, LANES), lambda i, j: (j, i, 0))
        out_shape = jax.ShapeDtypeStruct((n // LANES, m, LANES), out_dtype)
    else:
        out_spec = pl.BlockSpec((tm, tn), lambda i, j: (i, j))
        out_shape = jax.ShapeDtypeStruct((m, n), out_dtype)
    return pl.pallas_call(
        _proj_kernel,
        grid=(m // tm, n // tn),
        in_specs=[pl.BlockSpec((tm, d), lambda i, j: (i, 0)),
                  mod.spec(3, tm, 2), mod.spec(4, tm, 2),
                  pl.BlockSpec((1, d), lambda i, j: (0, 0)),
                  pl.BlockSpec((d, tn), lambda i, j: (0, j))],
        out_specs=out_spec,
        out_shape=out_shape,
        scratch_shapes=[pltpu.VMEM((tm, d), BF16)],
        compiler_params=_cparams("parallel", "arbitrary"),
        name="proj",
    )(x, mod.arr, mod.arr, gn.reshape(1, d), w)


def _pair_softmax(q2, k2, v2, bias2, lo, transposed=False):
    rows = q2.shape[0]
    zero = jnp.zeros_like(q2)
    qs = jnp.concatenate([jnp.where(lo, q2, zero), jnp.where(lo, zero, q2)], axis=0).astype(BF16)
    s = (_dot(qs, k2) if transposed else _dot_nt(qs, k2)) + bias2
    mx = jnp.max(s, axis=-1, keepdims=True)
    p = jnp.exp(s - mx)
    l = jnp.sum(p, axis=-1, keepdims=True)
    pb = p.astype(BF16)
    o = _dot_nt(pb, v2) if transposed else _dot(pb, v2)
    return (jnp.where(lo, o[:rows], o[rows:]), jnp.where(lo, mx[:rows], mx[rows:]),
            jnp.where(lo, l[:rows], l[rows:]))


def _merge_softmax(acc, new):
    ao, am, al = acc
    o, m, l = new
    mn = jnp.maximum(am, m)
    a1 = jnp.exp(am - mn)
    a2 = jnp.exp(m - mn)
    return ao * a1 + o * a2, mn, al * a1 + l * a2


def _attn_plan(g):
    dil = A_GROUPS[g][1]
    if dil <= ATTN_MAX_STRIDE:
        return dil, 1, A_KEYS // A_BLOCK
    assert dil % ATTN_MAX_STRIDE == 0
    step = dil // ATTN_MAX_STRIDE
    return ATTN_MAX_STRIDE, step, step * A_KEYS // A_BLOCK


def _attn_prompt_kernel(*refs, seq, n_alias):
    ng = len(A_GROUPS)
    qkv = refs[:3 * ng]
    bias_refs = refs[3 * ng:4 * ng]
    o_ref = refs[4 * ng + n_alias]
    kvt_refs = refs[4 * ng + 1 + n_alias:5 * ng + 1 + n_alias]
    acc_o, acc_m, acc_l = refs[-3:]
    blk = A_BLOCK

    for g in range(ng):
        keep = kvt_refs[g].shape[-1]
        for kv in range(2):
            src = qkv[3 * g + 1 + kv]
            for c in range(keep // blk):
                t = src[pl.ds(seq - keep + c * blk, blk), :].T
                for i in range(2):
                    kvt_refs[g][kv, i, :, c * blk:(c + 1) * blk] = t[i * A_HEAD_DIM:(i + 1) * A_HEAD_DIM]

    hp = pl.program_id(1)
    lo = lax.broadcasted_iota(jnp.int32, (blk, LANES), 1) < A_HEAD_DIM

    def tiles(g, starts, kw):
        stride = _attn_plan(g)[0]
        q_ref, k_ref, v_ref = qkv[3 * g:3 * g + 3]
        bias_ref = bias_refs[g]
        wcols = bias_ref.shape[-1]

        def rows(start, n):
            return pl.ds(start, n) if stride == 1 else pl.ds(start, n, stride=stride)

        news = []
        for qstart, kstart in starts:
            q2 = q_ref[rows(qstart, blk), :]
            k2 = k_ref[rows(kstart, kw), :].astype(BF16)
            v2 = v_ref[rows(kstart, kw), :].astype(BF16)
            news.append(_pair_softmax(q2, k2, v2, bias_ref[hp, :, wcols - kw:wcols], lo))
        for (qstart, _), new in zip(starts, news):
            sel = rows(qstart, blk)
            if g > 0:
                new = _merge_softmax((acc_o[sel, :], acc_m[sel, :], acc_l[sel, :]), new)
            acc_o[sel, :] = new[0]
            acc_m[sel, :] = new[1]
            acc_l[sel, :] = new[2]

    for g in range(ng):
        stride, _, wblk = _attn_plan(g)
        assert stride <= ATTN_UNROLL
        nb = seq // stride // blk
        span = blk * stride
        for b in range(min(wblk, nb)):
            tiles(g, [(r + b * span, r) for r in range(stride)], (b + 1) * blk)
        nfull = nb - wblk
        if nfull > 0:
            ub = max(u for u in range(1, nfull + 1) if nfull % u == 0 and u * stride <= ATTN_UNROLL_REST)

            def full(i, carry, g=g, span=span, stride=stride, ub=ub, wblk=wblk):
                b = wblk + i * ub
                tiles(g, [(r + (b + k) * span, r + (b + k - wblk) * span)
                          for k in range(ub) for r in range(stride)], (wblk + 1) * blk)
                return carry

            lax.fori_loop(0, nfull // ub, full, 0)
    o_ref[...] = (acc_o[...] / acc_l[...]).astype(o_ref.dtype)


def _attn_prompt(qkv, bias_tab, kvt_prev, layer, depth, batch, seq, o_dtype):
    npair = A_WIDTH // LANES
    ng = len(A_GROUPS)
    col = lambda j: pl.BlockSpec((None, seq, LANES), lambda b, h: (j * npair + h, b, 0))
    keeps = [min(win, seq) for win, _ in A_GROUPS]
    n_alias = 0 if kvt_prev is None else ng
    alias_specs = [pl.BlockSpec(memory_space=pl.ANY)] * n_alias
    alias_args = [] if kvt_prev is None else list(kvt_prev)
    n_in = 4 * ng
    outs = pl.pallas_call(
        functools.partial(_attn_prompt_kernel, seq=seq, n_alias=n_alias),
        grid=(batch, npair),
        in_specs=[col(j) for j in range(3 * ng)]
        + [pl.BlockSpec(tab.shape, lambda b, h: (0, 0, 0)) for tab in bias_tab] + alias_specs,
        out_specs=[pl.BlockSpec((seq, LANES), lambda b, h: (b, h))]
        + [pl.BlockSpec((None, None, 2, 2, A_HEAD_DIM, keep), lambda b, h: (layer, b, 0, h, 0, 0))
           for keep in keeps],
        out_shape=[jax.ShapeDtypeStruct((batch * seq, A_WIDTH), o_dtype)]
        + [jax.ShapeDtypeStruct((depth, batch, 2, A_HEADS, A_HEAD_DIM, keep), F32) for keep in keeps],
        input_output_aliases={n_in + g: 1 + g for g in range(n_alias)},
        scratch_shapes=[pltpu.VMEM((seq, LANES), F32)] * 3,
        compiler_params=_cparams("parallel", "parallel"),
        name="attn_prompt",
    )(*([qkv] * (3 * ng)), *bias_tab, *alias_args)
    return outs[0], list(outs[1:])


def _attn_sample_kernel(qkv_ref, buf1_ref, buf2_ref, buf3_ref, bb1, bn1, bb2, bn2, bb3, bn3,
                        o_ref, knew_scr, vnew_scr):
    t = qkv_ref.shape[1]
    npair = A_HEADS // 2
    bufs = (buf1_ref, buf2_ref, buf3_ref)
    bias = ((bb1, bn1), (bb2, bn2), (bb3, bn3))
    lo = lax.broadcasted_iota(jnp.int32, (t, LANES), 1) < A_HEAD_DIM
    knew_scr[...] = jnp.zeros_like(knew_scr)
    vnew_scr[...] = jnp.zeros_like(vnew_scr)
    for hp in range(npair):
        cols = slice(hp * LANES, (hp + 1) * LANES)
        acc = None
        for g in range(len(A_GROUPS)):
            q2 = qkv_ref[3 * g * npair + hp]
            wb = bufs[g].shape[-1]
            kb = bufs[g][0, 2 * hp:2 * hp + 2].reshape(LANES, wb).astype(BF16)
            vb = bufs[g][1, 2 * hp:2 * hp + 2].reshape(LANES, wb).astype(BF16)
            knew_scr[g, hp, 0:t, :] = qkv_ref[(3 * g + 1) * npair + hp]
            vnew_scr[g, hp, 0:t, :] = qkv_ref[(3 * g + 2) * npair + hp]
            kn = knew_scr[g, hp].astype(BF16)
            vn = vnew_scr[g, hp].astype(BF16)
            bbuf, bnew = bias[g]
            new = _pair_softmax(q2, kb, vb, bbuf[hp], lo, transposed=True)
            new = _merge_softmax(new, _pair_softmax(q2, kn, vn, bnew[hp], lo))
            acc = new if acc is None else _merge_softmax(acc, new)
        o_ref[:, cols] = acc[0] / acc[2]


def _attn_sample(qkv, caches, layer, bias_s, batch, t):
    aw = A_WIDTH
    ng = len(A_GROUPS)
    bufs = [jnp.transpose(c, (0, 1, 3, 4, 5, 2)) for c in caches]
    buf_specs = [pl.BlockSpec((None, None) + bv.shape[2:], lambda b: (layer, b, 0, 0, 0, 0)) for bv in bufs]
    tabs = [tab for pair in bias_s for tab in pair]
    return pl.pallas_call(
        _attn_sample_kernel,
        grid=(batch,),
        in_specs=[pl.BlockSpec((3 * ng * aw // LANES, t, LANES), lambda b: (0, b, 0))] + buf_specs
        + [pl.BlockSpec(tab.shape, lambda b: (0, 0, 0)) for tab in tabs],
        out_specs=pl.BlockSpec((t, aw), lambda b: (b, 0)),
        out_shape=jax.ShapeDtypeStruct((batch * t, aw), F32),
        scratch_shapes=[pltpu.VMEM((ng, aw // LANES, LANES, LANES), F32)] * 2,
        compiler_params=_cparams("parallel"),
        name="attn_sample",
    )(qkv, *bufs, *tabs)


def _ssd_kernel(xbc_ref, z_ref, dt_ref, conv0_ref, st0_ref, cw_ref, cb_ref, dtb_ref, aneg_ref,
                dsk_ref, gn_ref, e_ref, y_ref, convo_ref, sto_ref, xpad, st_t, y_scr, *, lv, nch):
    q = SSD_CHUNK
    c = pl.program_id(1)
    width = y_ref.shape[1]
    nst = B_STATE
    hpg = width // B_HEAD_DIM // B_GROUPS // 2

    @pl.when(c == 0)
    def _():
        xpad[...] = jnp.zeros_like(xpad)
        xpad[0:SUBLANES, :] = conv0_ref[0]
        for j in range(width // LANES):
            st_t[:, j * LANES:(j + 1) * LANES] = st0_ref[0, j * LANES:(j + 1) * LANES, :].T

    @pl.when(c > 0)
    def _():
        xpad[0:SUBLANES, :] = xpad[q:q + SUBLANES, :]

    xpad[SUBLANES:SUBLANES + lv, :] = xbc_ref[...].astype(F32)

    w = cw_ref[...]
    conv = cb_ref[...]
    for k in range(B_CONV):
        off = SUBLANES - (B_CONV - 1) + k
        conv = conv + xpad[off:off + q, :] * w[k:k + 1, :]
    act = conv * jax.nn.sigmoid(conv)
    xs = act[:, :width]
    bm = [act[:, width + gi * nst:width + (gi + 1) * nst] for gi in range(B_GROUPS)]
    cm = [act[:, width + (B_GROUPS + gi) * nst:width + (B_GROUPS + gi + 1) * nst] for gi in range(B_GROUPS)]

    dt_raw = dt_ref[...]
    if lv < q:
        dt_raw = jnp.concatenate([dt_raw, jnp.zeros((q - lv, LANES), F32)], axis=0)
    dt = _softplus(dt_raw + dtb_ref[...])
    if lv < q:
        dt = jnp.where(lax.broadcasted_iota(jnp.int32, (q, LANES), 0) < lv, dt, 0.0)
    dta = dt * aneg_ref[...]

    row = lax.broadcasted_iota(jnp.int32, (q, q), 0)
    col = lax.broadcasted_iota(jnp.int32, (q, q), 1)
    causal = row >= col
    tri = jnp.where(causal, 1.0, 0.0).astype(BF16)
    acum = sum(_dot(tri, p) for p in _split_bf16(dta, 3))
    acum_t = acum.T
    e = e_ref[...]
    dt_x = sum(_dot(p, e) for p in _split_bf16(dt, 2))
    acum_x = sum(_dot(p, e) for p in _split_bf16(acum, 2))
    last_x = acum_x[q - 1:q, :]
    ea_x = jnp.exp(acum_x)
    xdt = xs * dt_x
    xdt_te = (xdt * jnp.exp(last_x - acum_x)).astype(BF16)
    xdt_b = xdt.astype(BF16)
    chunk_decay = jnp.exp(last_x)
    lo = lax.broadcasted_iota(jnp.int32, (q, LANES), 1) < B_HEAD_DIM

    for gi in range(B_GROUPS):
        cmb = cm[gi].astype(BF16)
        cb = _dot_nt(cmb, bm[gi].astype(BF16))
        bm_t = bm[gi].T.astype(BF16)
        for hp in range(gi * hpg, (gi + 1) * hpg):
            cols = slice(hp * LANES, (hp + 1) * LANES)
            ax = acum_x[:, cols]
            ax_r = pltpu.roll(ax, B_HEAD_DIM, axis=1)
            ys = []
            for col_v, h in ((jnp.where(lo, ax, ax_r), 2 * hp), (jnp.where(lo, ax_r, ax), 2 * hp + 1)):
                seg = col_v - acum_t[h:h + 1, :]
                dec = jnp.exp(jnp.where(causal, seg, -jnp.inf))
                ys.append(_dot((cb * dec).astype(BF16), xdt_b[:, cols]))
            st_old = st_t[:, cols]
            y_off = _dot(cmb, st_old.astype(BF16)) * ea_x[:, cols]
            st_t[:, cols] = st_old * chunk_decay[:, cols] + _dot(bm_t, xdt_te[:, cols])
            y_scr[:, cols] = jnp.where(lo, ys[0], ys[1]) + y_off + dsk_ref[:, cols] * xs[:, cols]

    z = z_ref[...].astype(F32)
    if lv < q:
        z = jnp.concatenate([z, jnp.zeros((q - lv, width), F32)], axis=0)
    y = y_scr[...] * (z * jax.nn.sigmoid(z))
    gw = width // B_GROUPS
    for gi in range(B_GROUPS):
        yg = y[:, gi * gw:(gi + 1) * gw]
        ms = jnp.mean(yg * yg, axis=-1, keepdims=True)
        yn = yg * lax.rsqrt(ms + NORM_EPS) * gn_ref[:, gi * gw:(gi + 1) * gw]
        y_ref[:, gi * gw:(gi + 1) * gw] = yn[0:lv].astype(y_ref.dtype)

    @pl.when(c == nch - 1)
    def _():
        convo_ref[0] = xpad[lv:lv + SUBLANES, :]
        for j in range(width // LANES):
            sto_ref[0, j * LANES:(j + 1) * LANES, :] = st_t[:, j * LANES:(j + 1) * LANES].T


def _ssd(big, qkv, dt_col, conv0, st0, lw, batch, seq, o_dtype):
    width = lw["w_b_proj"].shape[0]
    cch = lw["conv_b_w"].shape[1]
    nheads = width // B_HEAD_DIM
    lv = min(seq, SSD_CHUNK)
    nch = seq // lv
    assert lv == SSD_CHUNK or nch == 1
    pad = lambda v: jnp.pad(v.astype(F32), (0, LANES - nheads)).reshape(1, LANES)
    e = (np.arange(LANES)[:, None] == (np.arange(width)[None, :] // B_HEAD_DIM)).astype(np.float32)
    conv0p = jnp.pad(conv0, ((0, 0), (SUBLANES - (B_CONV - 1), 0), (0, 0)))
    y, convo, sto = pl.pallas_call(
        functools.partial(_ssd_kernel, lv=lv, nch=nch),
        grid=(batch, nch),
        in_specs=[pl.BlockSpec((lv, cch), lambda b, c: (b * nch + c, 4)),
                  pl.BlockSpec((lv, width), lambda b, c: (b * nch + c, 0)),
                  pl.BlockSpec((None, lv, LANES), lambda b, c: (dt_col, b * nch + c, 0)),
                  pl.BlockSpec((1, SUBLANES, cch), lambda b, c: (b, 0, 0)),
                  pl.BlockSpec((1, width, B_STATE), lambda b, c: (b, 0, 0)),
                  pl.BlockSpec((B_CONV, cch), lambda b, c: (0, 0)),
                  pl.BlockSpec((1, cch), lambda b, c: (0, 0)),
                  pl.BlockSpec((1, LANES), lambda b, c: (0, 0)),
                  pl.BlockSpec((1, LANES), lambda b, c: (0, 0)),
                  pl.BlockSpec((1, width), lambda b, c: (0, 0)),
                  pl.BlockSpec((1, width), lambda b, c: (0, 0)),
                  pl.BlockSpec((LANES, width), lambda b, c: (0, 0))],
        out_specs=[pl.BlockSpec((lv, width), lambda b, c: (b * nch + c, 0)),
                   pl.BlockSpec((1, SUBLANES, cch), lambda b, c: (b, 0, 0)),
                   pl.BlockSpec((1, width, B_STATE), lambda b, c: (b, 0, 0))],
        out_shape=[jax.ShapeDtypeStruct((batch * seq, width), o_dtype),
                   jax.ShapeDtypeStruct((batch, SUBLANES, cch), F32),
                   jax.ShapeDtypeStruct((batch, width, B_STATE), F32)],
        scratch_shapes=[pltpu.VMEM((SUBLANES + SSD_CHUNK, cch), F32),
                        pltpu.VMEM((B_STATE, width), F32),
                        pltpu.VMEM((SSD_CHUNK, width), F32)],
        compiler_params=_cparams("parallel", "arbitrary"),
        name="ssd",
    )(big, big, qkv, conv0p, st0.reshape(batch, width, B_STATE),
      lw["conv_b_w"], lw["conv_b_b"].reshape(1, cch), pad(lw["dt_bias"]),
      pad(-jnp.exp(lw["a_log"].astype(F32))),
      jnp.repeat(lw["d_skip"].astype(F32), B_HEAD_DIM).reshape(1, width),
      lw["g_ssm_norm"].reshape(1, width), jnp.asarray(e, BF16))
    return (y, convo[:, SUBLANES - (B_CONV - 1):, :],
            sto.reshape(batch, nheads, B_HEAD_DIM, B_STATE))


def _lru_kernel(xc_ref, gc_ref, conv0_ref, h0_ref, cw_ref, cb_ref, wr_ref, wi_ref, br_ref, bi_ref,
                lam_ref, y_ref, convo_ref, ho_ref, xpad, hprev, hs_scr, *, q, nch):
    c = pl.program_id(1)
    width = y_ref.shape[1]

    @pl.when(c == 0)
    def _():
        xpad[0:SUBLANES, :] = conv0_ref[0]
        hprev[...] = h0_ref[0]

    @pl.when(c > 0)
    def _():
        xpad[0:SUBLANES, :] = xpad[q:q + SUBLANES, :]

    xpad[SUBLANES:SUBLANES + q, :] = xc_ref[...].astype(F32)
    w = cw_ref[...]
    x = cb_ref[...]
    for k in range(C_CONV):
        off = SUBLANES - (C_CONV - 1) + k
        x = x + xpad[off:off + q, :] * w[k:k + 1, :]

    bd = width // C_BLOCKS
    rs, gs = [], []
    for j in range(C_BLOCKS):
        xb = x[:, j * bd:(j + 1) * bd].astype(BF16)
        rs.append(_dot(xb, wr_ref[j]))
        gs.append(_dot(xb, wi_ref[j]))
    rg = jax.nn.sigmoid(jnp.concatenate(rs, axis=1) + br_ref[...])
    ig = jax.nn.sigmoid(jnp.concatenate(gs, axis=1) + bi_ref[...])
    log_a = -C_POW * rg * _softplus(-lam_ref[...])
    a = jnp.exp(log_a)
    u = x * ig * jnp.sqrt(-jnp.tanh(log_a) * (a * a + 1.0))

    rowi = lax.broadcasted_iota(jnp.int32, (SUBLANES, width), 0)
    h = hprev[...]
    for g in range(q // SUBLANES):
        ag = a[g * SUBLANES:(g + 1) * SUBLANES]
        bg = u[g * SUBLANES:(g + 1) * SUBLANES]
        s = 1
        while s < SUBLANES:
            keep = rowi >= s
            a_sh = jnp.where(keep, pltpu.roll(ag, s, axis=0), 1.0)
            b_sh = jnp.where(keep, pltpu.roll(bg, s, axis=0), 0.0)
            bg = ag * b_sh + bg
            ag = ag * a_sh
            s *= 2
        hg = bg + ag * h
        hs_scr[g * SUBLANES:(g + 1) * SUBLANES, :] = hg
        h = hg[SUBLANES - 1:SUBLANES]
    hprev[...] = h
    y_ref[...] = (hs_scr[...] * jax.nn.gelu(gc_ref[...].astype(F32))).astype(y_ref.dtype)

    @pl.when(c == nch - 1)
    def _():
        convo_ref[0] = xpad[q:q + SUBLANES, :]
        ho_ref[0] = h


def _lru(big, conv0, h0, lw, batch, seq, o_dtype):
    width = lw["w_c_proj"].shape[0]
    q = min(seq, 128)
    nch = seq // q
    bd = width // C_BLOCKS
    conv0p = jnp.pad(conv0, ((0, 0), (SUBLANES - (C_CONV - 1), 0), (0, 0)))
    vec = lambda v: v.astype(F32).reshape(1, width)
    y, convo, ho = pl.pallas_call(
        functools.partial(_lru_kernel, q=q, nch=nch),
        grid=(batch, nch),
        in_specs=[pl.BlockSpec((q, width), lambda b, c: (b * nch + c, 1)),
                  pl.BlockSpec((q, width), lambda b, c: (b * nch + c, 2)),
                  pl.BlockSpec((1, SUBLANES, width), lambda b, c: (b, 0, 0)),
                  pl.BlockSpec((1, 1, width), lambda b, c: (b, 0, 0)),
                  pl.BlockSpec((C_CONV, width), lambda b, c: (0, 0)),
                  pl.BlockSpec((1, width), lambda b, c: (0, 0)),
                  pl.BlockSpec((C_BLOCKS, bd, bd), lambda b, c: (0, 0, 0)),
                  pl.BlockSpec((C_BLOCKS, bd, bd), lambda b, c: (0, 0, 0)),
                  pl.BlockSpec((1, width), lambda b, c: (0, 0)),
                  pl.BlockSpec((1, width), lambda b, c: (0, 0)),
                  pl.BlockSpec((1, width), lambda b, c: (0, 0))],
        out_specs=[pl.BlockSpec((q, width), lambda b, c: (b * nch + c, 0)),
                   pl.BlockSpec((1, SUBLANES, width), lambda b, c: (b, 0, 0)),
                   pl.BlockSpec((1, 1, width), lambda b, c: (b, 0, 0))],
        out_shape=[jax.ShapeDtypeStruct((batch * seq, width), o_dtype),
                   jax.ShapeDtypeStruct((batch, SUBLANES, width), F32),
                   jax.ShapeDtypeStruct((batch, 1, width), F32)],
        scratch_shapes=[pltpu.VMEM((SUBLANES + q, width), F32),
                        pltpu.VMEM((1, width), F32),
                        pltpu.VMEM((q, width), F32)],
        compiler_params=_cparams("parallel", "arbitrary"),
        name="lru",
    )(big, big, conv0p, h0.reshape(batch, 1, width), lw["conv_c_w"], vec(lw["conv_c_b"]),
      lw["w_rgate"].astype(BF16), lw["w_igate"].astype(BF16), vec(lw["b_rgate"]), vec(lw["b_igate"]),
      vec(lw["lru_lambda"]))
    return y, convo[:, SUBLANES - (C_CONV - 1):, :], ho.reshape(batch, width)


def _merge_kernel(x_ref, gt_ref, oa_ref, yb_ref, yc_ref, gates_ref, wa_ref, wb_ref, wc_ref, wo_ref,
                  out_ref):
    d = x_ref.shape[1]
    ya = _dot(oa_ref[...].astype(BF16), wa_ref[...])
    yb = _dot(yb_ref[...].astype(BF16), wb_ref[...])
    yc = _dot(yc_ref[...].astype(BF16), wc_ref[...])
    sg = jax.nn.sigmoid(gates_ref[...].astype(F32))
    mixed = sg[:, :d] * ya + sg[:, d:2 * d] * yb + sg[:, 2 * d:] * yc
    out_ref[...] = x_ref[...] + _ld(gt_ref) * _dot(mixed.astype(BF16), wo_ref[...])


def _merge(x, mod, oa, yb, yc, big, lw):
    m, d = x.shape
    aw = A_WIDTH
    tm = _row_tile(m, mod.seq, mod.per_row, MERGE_TM)
    row = lambda wd, j=0: pl.BlockSpec((tm, wd), lambda i: (i, j))
    return pl.pallas_call(
        _merge_kernel,
        grid=(m // tm,),
        in_specs=[row(d), mod.spec(5, tm, 1), row(aw), row(yb.shape[1]), row(yc.shape[1]), row(3 * d, 1)]
        + [_resident(lw[k].shape) for k in ("w_a_proj", "w_b_proj", "w_c_proj", "w_out")],
        out_specs=row(d),
        out_shape=jax.ShapeDtypeStruct((m, d), F32),
        compiler_params=_cparams("parallel"),
        name="merge",
    )(x, mod.arr, oa, yb, yc, big,
      lw["w_a_proj"], lw["w_b_proj"], lw["w_c_proj"], lw["w_out"])


def _t5_bucket(dist):
    dist = np.asarray(dist)
    large = REL_MAX_EXACT + (np.log(np.maximum(dist, 1) / REL_MAX_EXACT)
                             / math.log(REL_MAX_DISTANCE / REL_MAX_EXACT)
                             * (REL_BUCKETS - REL_MAX_EXACT)).astype(np.int64)
    large = np.minimum(large, REL_BUCKETS - 1)
    return np.where(dist < REL_MAX_EXACT, dist, large).astype(np.int32)


def _group_bias(rel_bias, g):
    dil = A_GROUPS[g][1]
    buckets = _t5_bucket(np.arange(A_KEYS + 1) * dil)
    return rel_bias[buckets][:, g * A_HEADS:(g + 1) * A_HEADS].T.astype(F32)


def _prompt_bias_table(bias, g):
    _, step, wblk = _attn_plan(g)
    h = bias.shape[0]
    cols = (wblk + 1) * A_BLOCK
    n = cols + A_BLOCK - 1
    x = np.arange(n + 1)
    x = np.where(x < cols, x, x - (n + 1))
    dist = wblk * A_BLOCK - x
    valid = (dist >= 0) & (dist % step == 0) & (dist // step <= A_KEYS)
    v = jnp.where(valid[None], bias[:, np.clip(dist // step, 0, A_KEYS)], -jnp.inf)
    tab = jnp.tile(v, (1, A_BLOCK))[:, :A_BLOCK * n].reshape(h, A_BLOCK, n)[:, :, :cols]
    return tab.reshape(h // 2, 2 * A_BLOCK, cols)


def _sample_bias_tables(bias, wb, dil, t):
    h = bias.shape[0]
    front = t - 1 + wb - A_KEYS * dil
    assert front >= 0
    sparse = jnp.concatenate([bias[:, ::-1, None], jnp.full((h, A_KEYS + 1, dil - 1), -jnp.inf, F32)], axis=2)
    base = jnp.concatenate([jnp.full((h, front), -jnp.inf, F32), sparse.reshape(h, (A_KEYS + 1) * dil)], axis=1)
    tab_buf = jnp.stack([base[:, t - 1 - tq:t - 1 - tq + wb] for tq in range(t)], axis=1)
    tq = np.arange(t)[:, None]
    dist = tq - np.arange(LANES)[None, :]
    valid = (dist >= 0) & (dist % dil == 0) & (np.arange(LANES)[None, :] < t)
    tab_new = jnp.where(valid[None], bias[:, np.clip(dist // dil, 0, A_KEYS)], -jnp.inf)
    return tab_buf.reshape(h // 2, 2 * t, wb), tab_new.reshape(h // 2, 2 * t, LANES)


def _mixer_weights(w_in):
    na = len(A_GROUPS) * A_WIDTH
    d = w_in.shape[0]
    offs = np.cumsum([0, na, na, na, 1024, 1536, 16, 1024, 1024, 3 * d])
    seg = lambda i: w_in[:, offs[i]:offs[i + 1]]
    qa, ka, va, zb, xbc, dtb, xc, gc, gates = [seg(i) for i in range(9)]
    cols = []
    for g in range(len(A_GROUPS)):
        sl = slice(g * A_WIDTH, (g + 1) * A_WIDTH)
        cols += [qa[:, sl] * (A_HEAD_DIM ** -0.5), ka[:, sl], va[:, sl]]
    w_qkv = jnp.concatenate(cols + [dtb, jnp.zeros((d, DT_PAD - dtb.shape[1]), w_in.dtype)], axis=1)
    w_big = jnp.concatenate([zb, xc, gc, gates, xbc], axis=1)
    return w_qkv.astype(BF16), w_big.astype(BF16)


def _mixer(x, mod, lw, st, bias, prompt, batch, seq):
    act_dtype = BF16 if prompt else F32
    qkv = _proj(x, mod, lw["g_mix"], lw["w_qkv"], F32, QKV_TN, slabs=True)
    big = _proj(x, mod, lw["g_mix"], lw["w_big"], act_dtype, BIG_TN)
    ng = len(A_GROUPS)
    npair = A_WIDTH // LANES
    if prompt:
        oa, new_kv = _attn_prompt(qkv, bias, st["kvt"], st["layer"], st["depth"], batch, seq, act_dtype)
    else:
        oa = _attn_sample(qkv, st["kv"], st["layer"], bias, batch, seq)
        new_kv = [jnp.transpose(qkv[(3 * g + 1) * npair:(3 * g + 3) * npair], (1, 0, 2))
                  .reshape(batch, seq, 2, A_HEADS, A_HEAD_DIM) for g in range(ng)]
    dt_col = 3 * ng * npair
    yb, conv_b_new, ssm_new = _ssd(big, qkv, dt_col, st["conv_b"], st["ssm"], lw, batch, seq, act_dtype)
    yc, conv_c_new, lru_new = _lru(big, st["conv_c"], st["lru"], lw, batch, seq, act_dtype)
    x = _merge(x, mod, oa, yb, yc, big, lw)
    return x, (new_kv[0], new_kv[1], new_kv[2], conv_b_new, ssm_new, conv_c_new, lru_new)


def _block(x, mod, lw, st, bias, prompt, batch, seq, g_final):
    x = _ffn(x, mod, 0, lw["g_ff1"], lw["w_ff1_in"], lw["w_ff1_out"])
    x, new_st = _mixer(x, mod, lw, st, bias, prompt, batch, seq)
    x = _ffn(x, mod, 6, lw["g_ff2"], lw["w_ff2_in"], lw["w_ff2_out"], g_final)
    return x, new_st


@jax.jit
def _forward(x_prompt, x_sample, c_prompt, c_sample, cache_win1_kv, cache_win2_kv, cache_win3_kv,
             state_conv_b, state_ssm, state_conv_c, state_lru, rel_bias, w_ada, b_ada, g_ff1,
             w_ff1_in, w_ff1_out, g_mix, w_in, w_a_proj, conv_b_w, conv_b_b, dt_bias, a_log, d_skip,
             g_ssm_norm, w_b_proj, conv_c_w, conv_c_b, w_rgate, b_rgate, w_igate, b_igate, lru_lambda,
             w_c_proj, w_out, g_ff2, w_ff2_in, w_ff2_out, g_final):
    bp, lp, d = x_prompt.shape
    bs, ls, _ = x_sample.shape
    depth = w_ada.shape[0]
    caches = (cache_win1_kv, cache_win2_kv, cache_win3_kv)
    biases = [_group_bias(rel_bias, g) for g in range(len(A_GROUPS))]
    bias_p = [_prompt_bias_table(b, g) for g, b in enumerate(biases)]
    bias_s = [_sample_bias_tables(b, caches[g].shape[2], A_GROUPS[g][1], ls) for g, b in enumerate(biases)]

    yp = x_prompt.reshape(bp * lp, d)
    ys = x_sample.reshape(bs * ls, d)
    c_all = jnp.concatenate([c_prompt, c_sample], axis=0)
    new_p = [[] for _ in range(7)]
    new_s = [[] for _ in range(7)]
    kvt_p = None
    for l in range(depth):
        w_qkv, w_big = _mixer_weights(w_in[l])
        lw = dict(g_ff1=g_ff1[l], w_ff1_in=w_ff1_in[l].astype(BF16), w_ff1_out=w_ff1_out[l].astype(BF16),
                  g_mix=g_mix[l], w_qkv=w_qkv, w_big=w_big, w_a_proj=w_a_proj[l].astype(BF16),
                  conv_b_w=conv_b_w[l], conv_b_b=conv_b_b[l], dt_bias=dt_bias[l], a_log=a_log[l],
                  d_skip=d_skip[l], g_ssm_norm=g_ssm_norm[l], w_b_proj=w_b_proj[l].astype(BF16),
                  conv_c_w=conv_c_w[l], conv_c_b=conv_c_b[l], w_rgate=w_rgate[l], b_rgate=b_rgate[l],
                  w_igate=w_igate[l], b_igate=b_igate[l], lru_lambda=lru_lambda[l],
                  w_c_proj=w_c_proj[l].astype(BF16), w_out=w_out[l].astype(BF16),
                  g_ff2=g_ff2[l], w_ff2_in=w_ff2_in[l].astype(BF16), w_ff2_out=w_ff2_out[l].astype(BF16))
        mod_all = _ada(c_all, w_ada, b_ada, l)
        mod_p = _Mod(mod_all[:bp], bp, lp, per_row=False)
        mod_s = _Mod(mod_all[bp:], bs, ls, per_row=True)
        gf = g_final if l == depth - 1 else None
        st_p = dict(kvt=kvt_p, layer=l, depth=depth,
                    conv_b=jnp.zeros((bp, B_CONV - 1, conv_b_w.shape[2]), F32),
                    ssm=jnp.zeros((bp,) + state_ssm.shape[2:], F32),
                    conv_c=jnp.zeros((bp, C_CONV - 1, conv_c_w.shape[2]), F32),
                    lru=jnp.zeros((bp, state_lru.shape[2]), F32))
        st_s = dict(kv=caches, layer=l, conv_b=state_conv_b[l], ssm=state_ssm[l],
                    conv_c=state_conv_c[l], lru=state_lru[l])
        yp, stp = _block(yp, mod_p, lw, st_p, bias_p, True, bp, lp, gf)
        ys, sts = _block(ys, mod_s, lw, st_s, bias_s, False, bs, ls, gf)
        kvt_p = list(stp[:3])
        for i in range(7):
            new_p[i].append(stp[i])
            new_s[i].append(sts[i])
    outs_p = [jnp.transpose(v, (0, 1, 5, 2, 3, 4)) for v in kvt_p] + [jnp.stack(v, 0) for v in new_p[3:]]
    outs_s = [jnp.stack(v, 0) for v in new_s]
    return (yp.reshape(bp, lp, d), ys.reshape(bs, ls, d), *outs_p, *outs_s)


def kernel(x_prompt, x_sample, c_prompt, c_sample, cache_win1_kv, cache_win2_kv, cache_win3_kv,
           state_conv_b, state_ssm, state_conv_c, state_lru, rel_bias, w_ada, b_ada, g_ff1,
           w_ff1_in, w_ff1_out, g_mix, w_in, w_a_proj, conv_b_w, conv_b_b, dt_bias, a_log, d_skip,
           g_ssm_norm, w_b_proj, conv_c_w, conv_c_b, w_rgate, b_rgate, w_igate, b_igate, lru_lambda,
           w_c_proj, w_out, g_ff2, w_ff2_in, w_ff2_out, g_final):
    return _forward(x_prompt, x_sample, c_prompt, c_sample, cache_win1_kv, cache_win2_kv,
                    cache_win3_kv, state_conv_b, state_ssm, state_conv_c, state_lru, rel_bias, w_ada,
                    b_ada, g_ff1, w_ff1_in, w_ff1_out, g_mix, w_in, w_a_proj, conv_b_w, conv_b_b,
                    dt_bias, a_log, d_skip, g_ssm_norm, w_b_proj, conv_c_w, conv_c_b, w_rgate,
                    b_rgate, w_igate, b_igate, lru_lambda, w_c_proj, w_out, g_ff2, w_ff2_in,
                    w_ff2_out, g_final)
```

```python
import functools
import math

import jax
import jax.numpy as jnp
import numpy as np
from jax import lax
from jax.experimental import pallas as pl
from jax.experimental.pallas import tpu as pltpu

F32 = jnp.float32
BF16 = jnp.bfloat16

NORM_EPS = 1e-6
N_MOD = 9
A_GROUPS = ((128, 1), (512, 4), (2048, 16))
A_HEADS = 8
A_HEAD_DIM = 64
A_WIDTH = A_HEADS * A_HEAD_DIM
A_KEYS = 128
A_BLOCK = 128
REL_BUCKETS = 32
REL_MAX_EXACT = 16
REL_MAX_DISTANCE = 2048
B_HEAD_DIM = 64
B_GROUPS = 2
B_STATE = 128
B_CONV = 4
C_BLOCKS = 8
C_CONV = 4
C_POW = 8.0

LANES = 128
SUBLANES = 8
SSD_CHUNK = 128
FFN_TM = 1024
PROJ_TM = 2048
MERGE_TM = 512
QKV_TN = 1024
BIG_TN = 1536
DT_PAD = 512
ATTN_UNROLL = 8
ATTN_UNROLL_REST = 12
ATTN_MAX_STRIDE = 4
VMEM_LIMIT = 56 * 1024 * 1024


def _cparams(*sem):
    return pltpu.CompilerParams(dimension_semantics=sem, vmem_limit_bytes=VMEM_LIMIT)


def _resident(shape):
    nd = len(shape)
    return pl.BlockSpec(shape, lambda *_: (0,) * nd, pipeline_mode=pl.Buffered(1))


def _ld(ref):
    return ref[0] if len(ref.shape) == 3 else ref[...]


def _norm_mod(x, g, sc, sh):
    ms = jnp.mean(x * x, axis=-1, keepdims=True)
    return (x * lax.rsqrt(ms + NORM_EPS) * g) * (1.0 + sc) + sh


def _softplus(x):
    return jnp.maximum(x, 0.0) + jnp.log1p(jnp.exp(-jnp.abs(x)))


def _causal_conv(xpad, w_ref, b_ref, q):
    taps = w_ref.shape[0]
    assert taps - 1 <= SUBLANES
    xe = xpad[...]
    w = w_ref[...]
    acc = xe * w[0:1, :]
    for k in range(1, taps):
        acc = xe * w[k:k + 1, :] + pltpu.roll(acc, 1, axis=0)
    return acc[SUBLANES:SUBLANES + q] + b_ref[...]


def _split_bf16(v, n):
    parts = []
    r = v
    for _ in range(n):
        p = r.astype(BF16)
        parts.append(p)
        r = r - p.astype(F32)
    return parts


def _dot(a, b):
    return jnp.dot(a, b, preferred_element_type=F32)


def _dot_nt(a, b):
    return lax.dot_general(a, b, (((1,), (1,)), ((), ())), preferred_element_type=F32)


def _ada_kernel(c_ref, w_ref, b_ref, o_ref):
    c = c_ref[...]
    a = (c * jax.nn.sigmoid(c)).astype(BF16)
    o_ref[...] = _dot(a, w_ref[...].astype(BF16)) + b_ref[...]


def _ada(c, w, b, layer):
    rows, d = c.shape
    depth, _, n = w.shape
    tn = 1024
    return pl.pallas_call(
        _ada_kernel,
        grid=(n // tn,),
        in_specs=[pl.BlockSpec((rows, d), lambda j: (0, 0)),
                  pl.BlockSpec((None, d, tn), lambda j: (layer, 0, j)),
                  pl.BlockSpec((None, 1, tn), lambda j: (layer, 0, j))],
        out_specs=pl.BlockSpec((rows, tn), lambda j: (0, j)),
        out_shape=jax.ShapeDtypeStruct((rows, n), F32),
        compiler_params=_cparams("parallel"),
        name="ada",
    )(c, w, b.reshape(depth, 1, n))


class _Mod:
    def __init__(self, mod, batch, seq, per_row):
        d = mod.shape[1] // N_MOD
        self.d = d
        self.seq = seq
        self.per_row = per_row
        if per_row:
            self.arr = jnp.repeat(mod, seq, axis=0)
        else:
            self.arr = mod.reshape(batch * N_MOD, 1, d)

    def spec(self, k, tm, grid_rank):
        d = self.d
        if self.per_row:
            if grid_rank == 1:
                return pl.BlockSpec((tm, d), lambda i: (i, k))
            return pl.BlockSpec((tm, d), lambda i, j: (i, k))
        per = self.seq // tm
        if grid_rank == 1:
            return pl.BlockSpec((1, 1, d), lambda i: ((i // per) * N_MOD + k, 0, 0))
        return pl.BlockSpec((1, 1, d), lambda i, j: ((i // per) * N_MOD + k, 0, 0))


def _row_tile(m, seq, per_row, want):
    tm = min(want, m if per_row else seq)
    assert m % tm == 0 and (per_row or seq % tm == 0)
    return tm


def _ffn_kernel(x_ref, sh_ref, sc_ref, gt_ref, gn_ref, win_ref, wout_ref, *rest, d_ff, fc, final):
    if final:
        gf_ref, o_ref, acc_ref = rest
    else:
        o_ref, acc_ref = rest
    x = x_ref[...]
    h = _norm_mod(x, gn_ref[...], _ld(sc_ref), _ld(sh_ref)).astype(BF16)
    for c in range(d_ff // fc):
        u = _dot(h, win_ref[:, c * fc:(c + 1) * fc])
        v = _dot(h, win_ref[:, d_ff + c * fc:d_ff + (c + 1) * fc])
        a = (u * jax.nn.sigmoid(u) * v).astype(BF16)
        part = _dot(a, wout_ref[c * fc:(c + 1) * fc, :])
        if c == 0:
            acc_ref[...] = part
        else:
            acc_ref[...] += part
    y = x + 0.5 * _ld(gt_ref) * acc_ref[...]
    if final:
        ms = jnp.mean(y * y, axis=-1, keepdims=True)
        y = y * lax.rsqrt(ms + NORM_EPS) * gf_ref[...]
    o_ref[...] = y


def _ffn(x, mod, k0, gn, w_in, w_out, g_final=None):
    m, d = x.shape
    d_ff = w_out.shape[0]
    tm = _row_tile(m, mod.seq, mod.per_row, FFN_TM)
    final = g_final is not None
    in_specs = [pl.BlockSpec((tm, d), lambda i: (i, 0)),
                mod.spec(k0, tm, 1), mod.spec(k0 + 1, tm, 1), mod.spec(k0 + 2, tm, 1),
                _resident((1, d)), _resident(w_in.shape), _resident(w_out.shape)]
    args = [x, mod.arr, mod.arr, mod.arr, gn.reshape(1, d), w_in, w_out]
    if final:
        in_specs.append(_resident((1, d)))
        args.append(g_final.reshape(1, d))
    return pl.pallas_call(
        functools.partial(_ffn_kernel, d_ff=d_ff, fc=256, final=final),
        grid=(m // tm,),
        in_specs=in_specs,
        out_specs=pl.BlockSpec((tm, d), lambda i: (i, 0)),
        out_shape=jax.ShapeDtypeStruct((m, d), F32),
        scratch_shapes=[pltpu.VMEM((tm, d), F32)],
        compiler_params=_cparams("parallel"),
        name="ffn",
    )(*args)


_ACTS = {"sigmoid": jax.nn.sigmoid, "silu": lambda v: v * jax.nn.sigmoid(v), "gelu": jax.nn.gelu}


def _proj_kernel(x_ref, sh_ref, sc_ref, gn_ref, w_ref, o_ref, h_ref, *, tile_acts):
    j = pl.program_id(1)

    @pl.when(j == 0)
    def _():
        h_ref[...] = _norm_mod(x_ref[...], gn_ref[...], _ld(sc_ref), _ld(sh_ref)).astype(BF16)

    res = _dot(h_ref[...], w_ref[...])
    if len(o_ref.shape) == 3:
        for s in range(o_ref.shape[0]):
            o_ref[s] = res[:, s * LANES:(s + 1) * LANES].astype(o_ref.dtype)
    elif tile_acts is None:
        o_ref[...] = res.astype(o_ref.dtype)
    else:
        for jj, segs in enumerate(tile_acts):
            @pl.when(j == jj)
            def _(segs=segs):
                for lo, hi, act in segs:
                    v = res[:, lo:hi]
                    o_ref[:, lo:hi] = (v if act is None else _ACTS[act](v)).astype(o_ref.dtype)


def _proj(x, mod, gn, w, out_dtype, tn, slabs=False, acts=None):
    m, d = x.shape
    n = w.shape[1]
    tm = _row_tile(m, mod.seq, mod.per_row, PROJ_TM)
    tile_acts = None
    if acts is not None:
        cuts = sorted({0, n} | {c for a in acts for c in a[:2]} | set(range(0, n, tn)))
        tile_acts = [[] for _ in range(n // tn)]
        for lo, hi in zip(cuts[:-1], cuts[1:]):
            name = next((a[2] for a in acts if a[0] <= lo and hi <= a[1]), None)
            tile_acts[lo // tn].append((lo % tn, hi - lo // tn * tn, name))
        tile_acts = tuple(tuple(t) for t in tile_acts)
    if slabs:
        out_spec = pl.BlockSpec((tn // LANES, tm, LANES), lambda i, j: (j, i, 0))
        out_shape = jax.ShapeDtypeStruct((n // LANES, m, LANES), out_dtype)
    else:
        out_spec = pl.BlockSpec((tm, tn), lambda i, j: (i, j))
        out_shape = jax.ShapeDtypeStruct((m, n), out_dtype)
    return pl.pallas_call(
        functools.partial(_proj_kernel, tile_acts=tile_acts),
        grid=(m // tm, n // tn),
        in_specs=[pl.BlockSpec((tm, d), lambda i, j: (i, 0)),
                  mod.spec(3, tm, 2), mod.spec(4, tm, 2),
                  pl.BlockSpec((1, d), lambda i, j: (0, 0)),
                  pl.BlockSpec((d, tn), lambda i, j: (0, j))],
        out_specs=out_spec,
        out_shape=out_shape,
        scratch_shapes=[pltpu.VMEM((tm, d), BF16)],
        compiler_params=_cparams("parallel", "arbitrary"),
        name="proj",
    )(x, mod.arr, mod.arr, gn.reshape(1, d), w)


def _pair_softmax(q2, k2, v2, bias2, lo, transposed=False):
    rows = q2.shape[0]
    zero = jnp.zeros_like(q2)
    qs = jnp.concatenate([jnp.where(lo, q2, zero), jnp.where(lo, zero, q2)], axis=0).astype(BF16)
    s = (_dot(qs, k2) if transposed else _dot_nt(qs, k2)) + bias2
    mx = jnp.max(s, axis=-1, keepdims=True)
    p = jnp.exp(s - mx)
    l = jnp.sum(p, axis=-1, keepdims=True)
    pb = p.astype(BF16)
    o = _dot_nt(pb, v2) if transposed else _dot(pb, v2)
    return (jnp.where(lo, o[:rows], o[rows:]), jnp.where(lo, mx[:rows], mx[rows:]),
            jnp.where(lo, l[:rows], l[rows:]))


def _merge_softmax(acc, new):
    ao, am, al = acc
    o, m, l = new
    mn = jnp.maximum(am, m)
    a1 = jnp.exp(am - mn)
    a2 = jnp.exp(m - mn)
    return ao * a1 + o * a2, mn, al * a1 + l * a2


def _attn_plan(g):
    dil = A_GROUPS[g][1]
    if dil <= ATTN_MAX_STRIDE:
        return dil, 1, A_KEYS // A_BLOCK
    assert dil % ATTN_MAX_STRIDE == 0
    step = dil // ATTN_MAX_STRIDE
    return ATTN_MAX_STRIDE, step, step * A_KEYS // A_BLOCK


def _attn_prompt_kernel(*refs, seq, n_alias):
    ng = len(A_GROUPS)
    qkv = refs[:3 * ng]
    bias_refs = refs[3 * ng:4 * ng]
    o_ref = refs[4 * ng + n_alias]
    kvt_refs = refs[4 * ng + 1 + n_alias:5 * ng + 1 + n_alias]
    acc_o, acc_m, acc_l = refs[-3:]
    blk = A_BLOCK

    for g in range(ng):
        keep = kvt_refs[g].shape[-1]
        for kv in range(2):
            src = qkv[3 * g + 1 + kv]
            for c in range(keep // blk):
                t = src[pl.ds(seq - keep + c * blk, blk), :].T
                for i in range(2):
                    kvt_refs[g][kv, i, :, c * blk:(c + 1) * blk] = t[i * A_HEAD_DIM:(i + 1) * A_HEAD_DIM]

    hp = pl.program_id(1)
    lo = lax.broadcasted_iota(jnp.int32, (blk, LANES), 1) < A_HEAD_DIM

    def tiles(g, starts, kw):
        stride = _attn_plan(g)[0]
        q_ref, k_ref, v_ref = qkv[3 * g:3 * g + 3]
        bias_ref = bias_refs[g]
        wcols = bias_ref.shape[-1]

        def rows(start, n):
            return pl.ds(start, n) if stride == 1 else pl.ds(start, n, stride=stride)

        news = []
        for qstart, kstart in starts:
            q2 = q_ref[rows(qstart, blk), :]
            k2 = k_ref[rows(kstart, kw), :].astype(BF16)
            v2 = v_ref[rows(kstart, kw), :].astype(BF16)
            news.append(_pair_softmax(q2, k2, v2, bias_ref[hp, :, wcols - kw:wcols], lo))
        for (qstart, _), new in zip(starts, news):
            sel = rows(qstart, blk)
            if g > 0:
                new = _merge_softmax((acc_o[sel, :], acc_m[sel, :], acc_l[sel, :]), new)
            acc_o[sel, :] = new[0]
            acc_m[sel, :] = new[1]
            acc_l[sel, :] = new[2]

    for g in range(ng):
        stride, _, wblk = _attn_plan(g)
        assert stride <= ATTN_UNROLL
        nb = seq // stride // blk
        span = blk * stride
        for b in range(min(wblk, nb)):
            tiles(g, [(r + b * span, r) for r in range(stride)], (b + 1) * blk)
        nfull = nb - wblk
        if nfull > 0:
            ub = max(u for u in range(1, nfull + 1) if nfull % u == 0 and u * stride <= ATTN_UNROLL_REST)

            def full(i, carry, g=g, span=span, stride=stride, ub=ub, wblk=wblk):
                b = wblk + i * ub
                tiles(g, [(r + (b + k) * span, r + (b + k - wblk) * span)
                          for k in range(ub) for r in range(stride)], (wblk + 1) * blk)
                return carry

            lax.fori_loop(0, nfull // ub, full, 0)
    o_ref[...] = (acc_o[...] / acc_l[...]).astype(o_ref.dtype)


def _attn_prompt(qkv, bias_tab, kvt_prev, layer, depth, batch, seq, o_dtype):
    npair = A_WIDTH // LANES
    ng = len(A_GROUPS)
    col = lambda j: pl.BlockSpec((None, seq, LANES), lambda b, h: (j * npair + h, b, 0))
    keeps = [min(win, seq) for win, _ in A_GROUPS]
    n_alias = 0 if kvt_prev is None else ng
    alias_specs = [pl.BlockSpec(memory_space=pl.ANY)] * n_alias
    alias_args = [] if kvt_prev is None else list(kvt_prev)
    n_in = 4 * ng
    outs = pl.pallas_call(
        functools.partial(_attn_prompt_kernel, seq=seq, n_alias=n_alias),
        grid=(batch, npair),
        in_specs=[col(j) for j in range(3 * ng)]
        + [pl.BlockSpec(tab.shape, lambda b, h: (0, 0, 0)) for tab in bias_tab] + alias_specs,
        out_specs=[pl.BlockSpec((seq, LANES), lambda b, h: (b, h))]
        + [pl.BlockSpec((None, None, 2, 2, A_HEAD_DIM, keep), lambda b, h: (layer, b, 0, h, 0, 0))
           for keep in keeps],
        out_shape=[jax.ShapeDtypeStruct((batch * seq, A_WIDTH), o_dtype)]
        + [jax.ShapeDtypeStruct((depth, batch, 2, A_HEADS, A_HEAD_DIM, keep), F32) for keep in keeps],
        input_output_aliases={n_in + g: 1 + g for g in range(n_alias)},
        scratch_shapes=[pltpu.VMEM((seq, LANES), F32)] * 3,
        compiler_params=_cparams("parallel", "parallel"),
        name="attn_prompt",
    )(*([qkv] * (3 * ng)), *bias_tab, *alias_args)
    return outs[0], list(outs[1:])


def _attn_sample_kernel(qkv_ref, buf1_ref, buf2_ref, buf3_ref, bb1, bn1, bb2, bn2, bb3, bn3,
                        o_ref, knew_scr, vnew_scr):
    t = qkv_ref.shape[1]
    npair = A_HEADS // 2
    bufs = (buf1_ref, buf2_ref, buf3_ref)
    bias = ((bb1, bn1), (bb2, bn2), (bb3, bn3))
    lo = lax.broadcasted_iota(jnp.int32, (t, LANES), 1) < A_HEAD_DIM
    knew_scr[...] = jnp.zeros_like(knew_scr)
    vnew_scr[...] = jnp.zeros_like(vnew_scr)
    for hp in range(npair):
        cols = slice(hp * LANES, (hp + 1) * LANES)
        acc = None
        for g in range(len(A_GROUPS)):
            q2 = qkv_ref[3 * g * npair + hp]
            wb = bufs[g].shape[-1]
            kb = bufs[g][0, 2 * hp:2 * hp + 2].reshape(LANES, wb).astype(BF16)
            vb = bufs[g][1, 2 * hp:2 * hp + 2].reshape(LANES, wb).astype(BF16)
            knew_scr[g, hp, 0:t, :] = qkv_ref[(3 * g + 1) * npair + hp]
            vnew_scr[g, hp, 0:t, :] = qkv_ref[(3 * g + 2) * npair + hp]
            kn = knew_scr[g, hp].astype(BF16)
            vn = vnew_scr[g, hp].astype(BF16)
            bbuf, bnew = bias[g]
            new = _pair_softmax(q2, kb, vb, bbuf[hp], lo, transposed=True)
            new = _merge_softmax(new, _pair_softmax(q2, kn, vn, bnew[hp], lo))
            acc = new if acc is None else _merge_softmax(acc, new)
        o_ref[:, cols] = acc[0] / acc[2]


def _attn_sample(qkv, caches, layer, bias_s, batch, t):
    aw = A_WIDTH
    ng = len(A_GROUPS)
    bufs = [jnp.transpose(c, (0, 1, 3, 4, 5, 2)) for c in caches]
    buf_specs = [pl.BlockSpec((None, None) + bv.shape[2:], lambda b: (layer, b, 0, 0, 0, 0)) for bv in bufs]
    tabs = [tab for pair in bias_s for tab in pair]
    return pl.pallas_call(
        _attn_sample_kernel,
        grid=(batch,),
        in_specs=[pl.BlockSpec((3 * ng * aw // LANES, t, LANES), lambda b: (0, b, 0))] + buf_specs
        + [pl.BlockSpec(tab.shape, lambda b: (0, 0, 0)) for tab in tabs],
        out_specs=pl.BlockSpec((t, aw), lambda b: (b, 0)),
        out_shape=jax.ShapeDtypeStruct((batch * t, aw), F32),
        scratch_shapes=[pltpu.VMEM((ng, aw // LANES, LANES, LANES), F32)] * 2,
        compiler_params=_cparams("parallel"),
        name="attn_sample",
    )(qkv, *bufs, *tabs)


def _ssd_kernel(xbc_ref, z_ref, dt_ref, conv0_ref, st0_ref, cw_ref, cb_ref, dtb_ref, aneg_ref,
                dsk_ref, gn_ref, e_ref, y_ref, convo_ref, sto_ref, xpad, st_t, y_scr, *, lv, nch):
    q = SSD_CHUNK
    c = pl.program_id(1)
    width = y_ref.shape[1]
    nst = B_STATE
    hpg = width // B_HEAD_DIM // B_GROUPS // 2

    @pl.when(c == 0)
    def _():
        xpad[...] = jnp.zeros_like(xpad)
        xpad[0:SUBLANES, :] = conv0_ref[0]
        for j in range(width // LANES):
            st_t[:, j * LANES:(j + 1) * LANES] = st0_ref[0, j * LANES:(j + 1) * LANES, :].T

    @pl.when(c > 0)
    def _():
        xpad[0:SUBLANES, :] = xpad[q:q + SUBLANES, :]

    xpad[SUBLANES:SUBLANES + lv, :] = xbc_ref[...].astype(F32)

    conv = _causal_conv(xpad, cw_ref, cb_ref, q)
    act = conv * jax.nn.sigmoid(conv)
    xs = act[:, :width]
    bm = [act[:, width + gi * nst:width + (gi + 1) * nst] for gi in range(B_GROUPS)]
    cm = [act[:, width + (B_GROUPS + gi) * nst:width + (B_GROUPS + gi + 1) * nst] for gi in range(B_GROUPS)]

    dt_raw = dt_ref[...]
    if lv < q:
        dt_raw = jnp.concatenate([dt_raw, jnp.zeros((q - lv, LANES), F32)], axis=0)
    dt = _softplus(dt_raw + dtb_ref[...])
    if lv < q:
        dt = jnp.where(lax.broadcasted_iota(jnp.int32, (q, LANES), 0) < lv, dt, 0.0)
    dta = dt * aneg_ref[...]

    row = lax.broadcasted_iota(jnp.int32, (q, q), 0)
    col = lax.broadcasted_iota(jnp.int32, (q, q), 1)
    causal = row >= col
    tri = jnp.where(causal, 1.0, 0.0).astype(BF16)
    acum = sum(_dot(tri, p) for p in _split_bf16(dta, 3))
    acum_t = acum.T
    e = e_ref[...]
    dt_x = sum(_dot(p, e) for p in _split_bf16(dt, 2))
    acum_x = sum(_dot(p, e) for p in _split_bf16(acum, 2))
    last_x = acum_x[q - 1:q, :]
    ea_x = jnp.exp(acum_x)
    xdt = xs * dt_x
    xdt_te = (xdt * jnp.exp(last_x - acum_x)).astype(BF16)
    xdt_b = xdt.astype(BF16)
    chunk_decay = jnp.exp(last_x)
    lo = lax.broadcasted_iota(jnp.int32, (q, LANES), 1) < B_HEAD_DIM

    for gi in range(B_GROUPS):
        cmb = cm[gi].astype(BF16)
        cb = _dot_nt(cmb, bm[gi].astype(BF16))
        bm_t = bm[gi].T.astype(BF16)
        for hp in range(gi * hpg, (gi + 1) * hpg):
            cols = slice(hp * LANES, (hp + 1) * LANES)
            ax = acum_x[:, cols]
            ax_r = pltpu.roll(ax, B_HEAD_DIM, axis=1)
            ys = []
            for col_v, h in ((jnp.where(lo, ax, ax_r), 2 * hp), (jnp.where(lo, ax_r, ax), 2 * hp + 1)):
                seg = col_v - acum_t[h:h + 1, :]
                dec = jnp.exp(jnp.where(causal, seg, -jnp.inf))
                ys.append(_dot((cb * dec).astype(BF16), xdt_b[:, cols]))
            st_old = st_t[:, cols]
            y_off = _dot(cmb, st_old.astype(BF16)) * ea_x[:, cols]
            st_t[:, cols] = st_old * chunk_decay[:, cols] + _dot(bm_t, xdt_te[:, cols])
            y_scr[:, cols] = jnp.where(lo, ys[0], ys[1]) + y_off + dsk_ref[:, cols] * xs[:, cols]

    z = z_ref[...].astype(F32)
    if lv < q:
        z = jnp.concatenate([z, jnp.zeros((q - lv, width), F32)], axis=0)
    y = y_scr[...] * z
    gw = width // B_GROUPS
    for gi in range(B_GROUPS):
        yg = y[:, gi * gw:(gi + 1) * gw]
        ms = jnp.mean(yg * yg, axis=-1, keepdims=True)
        yn = yg * lax.rsqrt(ms + NORM_EPS) * gn_ref[:, gi * gw:(gi + 1) * gw]
        y_ref[:, gi * gw:(gi + 1) * gw] = yn[0:lv].astype(y_ref.dtype)

    @pl.when(c == nch - 1)
    def _():
        convo_ref[0] = xpad[lv:lv + SUBLANES, :]
        for j in range(width // LANES):
            sto_ref[0, j * LANES:(j + 1) * LANES, :] = st_t[:, j * LANES:(j + 1) * LANES].T


def _ssd(big, qkv, dt_col, conv0, st0, lw, batch, seq, o_dtype):
    width = lw["w_b_proj"].shape[0]
    cch = lw["conv_b_w"].shape[1]
    nheads = width // B_HEAD_DIM
    lv = min(seq, SSD_CHUNK)
    nch = seq // lv
    assert lv == SSD_CHUNK or nch == 1
    pad = lambda v: jnp.pad(v.astype(F32), (0, LANES - nheads)).reshape(1, LANES)
    e = (np.arange(LANES)[:, None] == (np.arange(width)[None, :] // B_HEAD_DIM)).astype(np.float32)
    conv0p = jnp.pad(conv0, ((0, 0), (SUBLANES - (B_CONV - 1), 0), (0, 0)))
    y, convo, sto = pl.pallas_call(
        functools.partial(_ssd_kernel, lv=lv, nch=nch),
        grid=(batch, nch),
        in_specs=[pl.BlockSpec((lv, cch), lambda b, c: (b * nch + c, 4)),
                  pl.BlockSpec((lv, width), lambda b, c: (b * nch + c, 3)),
                  pl.BlockSpec((None, lv, LANES), lambda b, c: (dt_col, b * nch + c, 0)),
                  pl.BlockSpec((1, SUBLANES, cch), lambda b, c: (b, 0, 0)),
                  pl.BlockSpec((1, width, B_STATE), lambda b, c: (b, 0, 0)),
                  pl.BlockSpec((B_CONV, cch), lambda b, c: (0, 0)),
                  pl.BlockSpec((1, cch), lambda b, c: (0, 0)),
                  pl.BlockSpec((1, LANES), lambda b, c: (0, 0)),
                  pl.BlockSpec((1, LANES), lambda b, c: (0, 0)),
                  pl.BlockSpec((1, width), lambda b, c: (0, 0)),
                  pl.BlockSpec((1, width), lambda b, c: (0, 0)),
                  pl.BlockSpec((LANES, width), lambda b, c: (0, 0))],
        out_specs=[pl.BlockSpec((lv, width), lambda b, c: (b * nch + c, 0)),
                   pl.BlockSpec((1, SUBLANES, cch), lambda b, c: (b, 0, 0)),
                   pl.BlockSpec((1, width, B_STATE), lambda b, c: (b, 0, 0))],
        out_shape=[jax.ShapeDtypeStruct((batch * seq, width), o_dtype),
                   jax.ShapeDtypeStruct((batch, SUBLANES, cch), F32),
                   jax.ShapeDtypeStruct((batch, width, B_STATE), F32)],
        scratch_shapes=[pltpu.VMEM((SUBLANES + SSD_CHUNK, cch), F32),
                        pltpu.VMEM((B_STATE, width), F32),
                        pltpu.VMEM((SSD_CHUNK, width), F32)],
        compiler_params=_cparams("parallel", "arbitrary"),
        name="ssd",
    )(big, big, qkv, conv0p, st0.reshape(batch, width, B_STATE),
      lw["conv_b_w"], lw["conv_b_b"].reshape(1, cch), pad(lw["dt_bias"]),
      pad(-jnp.exp(lw["a_log"].astype(F32))),
      jnp.repeat(lw["d_skip"].astype(F32), B_HEAD_DIM).reshape(1, width),
      lw["g_ssm_norm"].reshape(1, width), jnp.asarray(e, BF16))
    return (y, convo[:, SUBLANES - (B_CONV - 1):, :],
            sto.reshape(batch, nheads, B_HEAD_DIM, B_STATE))


def _lru_kernel(xc_ref, gc_ref, conv0_ref, h0_ref, cw_ref, cb_ref, wr_ref, wi_ref, br_ref, bi_ref,
                lam_ref, y_ref, convo_ref, ho_ref, xpad, hprev, hs_scr, *, q, nch):
    c = pl.program_id(1)
    width = y_ref.shape[1]

    @pl.when(c == 0)
    def _():
        xpad[0:SUBLANES, :] = conv0_ref[0]
        hprev[...] = h0_ref[0]

    @pl.when(c > 0)
    def _():
        xpad[0:SUBLANES, :] = xpad[q:q + SUBLANES, :]

    xpad[SUBLANES:SUBLANES + q, :] = xc_ref[...].astype(F32)
    x = _causal_conv(xpad, cw_ref, cb_ref, q)

    bd = width // C_BLOCKS
    rs, gs = [], []
    for j in range(C_BLOCKS):
        xb = x[:, j * bd:(j + 1) * bd].astype(BF16)
        rs.append(_dot(xb, wr_ref[j]))
        gs.append(_dot(xb, wi_ref[j]))
    rg = jax.nn.sigmoid(jnp.concatenate(rs, axis=1) + br_ref[...])
    ig = jax.nn.sigmoid(jnp.concatenate(gs, axis=1) + bi_ref[...])
    log_a = -C_POW * rg * _softplus(-lam_ref[...])
    a = jnp.exp(log_a)
    u = x * ig * jnp.sqrt(-jnp.tanh(log_a) * (a * a + 1.0))

    rowi = lax.broadcasted_iota(jnp.int32, (SUBLANES, width), 0)
    h = hprev[...]
    for g in range(q // SUBLANES):
        ag = a[g * SUBLANES:(g + 1) * SUBLANES]
        bg = u[g * SUBLANES:(g + 1) * SUBLANES]
        s = 1
        while s < SUBLANES:
            keep = rowi >= s
            a_sh = jnp.where(keep, pltpu.roll(ag, s, axis=0), 1.0)
            b_sh = jnp.where(keep, pltpu.roll(bg, s, axis=0), 0.0)
            bg = ag * b_sh + bg
            ag = ag * a_sh
            s *= 2
        hg = bg + ag * h
        hs_scr[g * SUBLANES:(g + 1) * SUBLANES, :] = hg
        h = hg[SUBLANES - 1:SUBLANES]
    hprev[...] = h
    y_ref[...] = (hs_scr[...] * gc_ref[...].astype(F32)).astype(y_ref.dtype)

    @pl.when(c == nch - 1)
    def _():
        convo_ref[0] = xpad[q:q + SUBLANES, :]
        ho_ref[0] = h


def _lru(big, conv0, h0, lw, batch, seq, o_dtype):
    width = lw["w_c_proj"].shape[0]
    q = min(seq, 128)
    nch = seq // q
    bd = width // C_BLOCKS
    conv0p = jnp.pad(conv0, ((0, 0), (SUBLANES - (C_CONV - 1), 0), (0, 0)))
    vec = lambda v: v.astype(F32).reshape(1, width)
    y, convo, ho = pl.pallas_call(
        functools.partial(_lru_kernel, q=q, nch=nch),
        grid=(batch, nch),
        in_specs=[pl.BlockSpec((q, width), lambda b, c: (b * nch + c, 5)),
                  pl.BlockSpec((q, width), lambda b, c: (b * nch + c, 4)),
                  pl.BlockSpec((1, SUBLANES, width), lambda b, c: (b, 0, 0)),
                  pl.BlockSpec((1, 1, width), lambda b, c: (b, 0, 0)),
                  pl.BlockSpec((C_CONV, width), lambda b, c: (0, 0)),
                  pl.BlockSpec((1, width), lambda b, c: (0, 0)),
                  pl.BlockSpec((C_BLOCKS, bd, bd), lambda b, c: (0, 0, 0)),
                  pl.BlockSpec((C_BLOCKS, bd, bd), lambda b, c: (0, 0, 0)),
                  pl.BlockSpec((1, width), lambda b, c: (0, 0)),
                  pl.BlockSpec((1, width), lambda b, c: (0, 0)),
                  pl.BlockSpec((1, width), lambda b, c: (0, 0))],
        out_specs=[pl.BlockSpec((q, width), lambda b, c: (b * nch + c, 0)),
                   pl.BlockSpec((1, SUBLANES, width), lambda b, c: (b, 0, 0)),
                   pl.BlockSpec((1, 1, width), lambda b, c: (b, 0, 0))],
        out_shape=[jax.ShapeDtypeStruct((batch * seq, width), o_dtype),
                   jax.ShapeDtypeStruct((batch, SUBLANES, width), F32),
                   jax.ShapeDtypeStruct((batch, 1, width), F32)],
        scratch_shapes=[pltpu.VMEM((SUBLANES + q, width), F32),
                        pltpu.VMEM((1, width), F32),
                        pltpu.VMEM((q, width), F32)],
        compiler_params=_cparams("parallel", "arbitrary"),
        name="lru",
    )(big, big, conv0p, h0.reshape(batch, 1, width), lw["conv_c_w"], vec(lw["conv_c_b"]),
      lw["w_rgate"].astype(BF16), lw["w_igate"].astype(BF16), vec(lw["b_rgate"]), vec(lw["b_igate"]),
      vec(lw["lru_lambda"]))
    return y, convo[:, SUBLANES - (C_CONV - 1):, :], ho.reshape(batch, width)


def _merge_kernel(x_ref, gt_ref, oa_ref, yb_ref, yc_ref, gates_ref, wa_ref, wb_ref, wc_ref, wo_ref,
                  out_ref):
    d = x_ref.shape[1]
    ya = _dot(oa_ref[...].astype(BF16), wa_ref[...])
    yb = _dot(yb_ref[...].astype(BF16), wb_ref[...])
    yc = _dot(yc_ref[...].astype(BF16), wc_ref[...])
    sg = gates_ref[...].astype(F32)
    mixed = sg[:, :d] * ya + sg[:, d:2 * d] * yb + sg[:, 2 * d:] * yc
    out_ref[...] = x_ref[...] + _ld(gt_ref) * _dot(mixed.astype(BF16), wo_ref[...])


def _merge(x, mod, oa, yb, yc, big, lw):
    m, d = x.shape
    aw = A_WIDTH
    tm = _row_tile(m, mod.seq, mod.per_row, MERGE_TM)
    row = lambda wd, j=0: pl.BlockSpec((tm, wd), lambda i: (i, j))
    return pl.pallas_call(
        _merge_kernel,
        grid=(m // tm,),
        in_specs=[row(d), mod.spec(5, tm, 1), row(aw), row(yb.shape[1]), row(yc.shape[1]), row(3 * d, 0)]
        + [_resident(lw[k].shape) for k in ("w_a_proj", "w_b_proj", "w_c_proj", "w_out")],
        out_specs=row(d),
        out_shape=jax.ShapeDtypeStruct((m, d), F32),
        compiler_params=_cparams("parallel"),
        name="merge",
    )(x, mod.arr, oa, yb, yc, big,
      lw["w_a_proj"], lw["w_b_proj"], lw["w_c_proj"], lw["w_out"])


def _t5_bucket(dist):
    dist = np.asarray(dist)
    large = REL_MAX_EXACT + (np.log(np.maximum(dist, 1) / REL_MAX_EXACT)
                             / math.log(REL_MAX_DISTANCE / REL_MAX_EXACT)
                             * (REL_BUCKETS - REL_MAX_EXACT)).astype(np.int64)
    large = np.minimum(large, REL_BUCKETS - 1)
    return np.where(dist < REL_MAX_EXACT, dist, large).astype(np.int32)


def _group_bias(rel_bias, g):
    dil = A_GROUPS[g][1]
    buckets = _t5_bucket(np.arange(A_KEYS + 1) * dil)
    return rel_bias[buckets][:, g * A_HEADS:(g + 1) * A_HEADS].T.astype(F32)


def _prompt_bias_table(bias, g):
    _, step, wblk = _attn_plan(g)
    h = bias.shape[0]
    cols = (wblk + 1) * A_BLOCK
    n = cols + A_BLOCK - 1
    x = np.arange(n + 1)
    x = np.where(x < cols, x, x - (n + 1))
    dist = wblk * A_BLOCK - x
    valid = (dist >= 0) & (dist % step == 0) & (dist // step <= A_KEYS)
    v = jnp.where(valid[None], bias[:, np.clip(dist // step, 0, A_KEYS)], -jnp.inf)
    tab = jnp.tile(v, (1, A_BLOCK))[:, :A_BLOCK * n].reshape(h, A_BLOCK, n)[:, :, :cols]
    return tab.reshape(h // 2, 2 * A_BLOCK, cols)


def _sample_bias_tables(bias, wb, dil, t):
    h = bias.shape[0]
    front = t - 1 + wb - A_KEYS * dil
    assert front >= 0
    sparse = jnp.concatenate([bias[:, ::-1, None], jnp.full((h, A_KEYS + 1, dil - 1), -jnp.inf, F32)], axis=2)
    base = jnp.concatenate([jnp.full((h, front), -jnp.inf, F32), sparse.reshape(h, (A_KEYS + 1) * dil)], axis=1)
    tab_buf = jnp.stack([base[:, t - 1 - tq:t - 1 - tq + wb] for tq in range(t)], axis=1)
    tq = np.arange(t)[:, None]
    dist = tq - np.arange(LANES)[None, :]
    valid = (dist >= 0) & (dist % dil == 0) & (np.arange(LANES)[None, :] < t)
    tab_new = jnp.where(valid[None], bias[:, np.clip(dist // dil, 0, A_KEYS)], -jnp.inf)
    return tab_buf.reshape(h // 2, 2 * t, wb), tab_new.reshape(h // 2, 2 * t, LANES)


def _mixer_weights(w_in):
    na = len(A_GROUPS) * A_WIDTH
    d = w_in.shape[0]
    offs = np.cumsum([0, na, na, na, 1024, 1536, 16, 1024, 1024, 3 * d])
    seg = lambda i: w_in[:, offs[i]:offs[i + 1]]
    qa, ka, va, zb, xbc, dtb, xc, gc, gates = [seg(i) for i in range(9)]
    cols = []
    for g in range(len(A_GROUPS)):
        sl = slice(g * A_WIDTH, (g + 1) * A_WIDTH)
        cols += [qa[:, sl] * (A_HEAD_DIM ** -0.5), ka[:, sl], va[:, sl]]
    w_qkv = jnp.concatenate(cols + [dtb, jnp.zeros((d, DT_PAD - dtb.shape[1]), w_in.dtype)], axis=1)
    w_big = jnp.concatenate([gates, zb, gc, xc, xbc], axis=1)
    return w_qkv.astype(BF16), w_big.astype(BF16)


def _mixer(x, mod, lw, st, bias, prompt, batch, seq):
    act_dtype = BF16 if prompt else F32
    qkv = _proj(x, mod, lw["g_mix"], lw["w_qkv"], F32, QKV_TN, slabs=True)
    d = x.shape[1]
    big = _proj(x, mod, lw["g_mix"], lw["w_big"], act_dtype, BIG_TN,
                acts=((0, 3 * d, "sigmoid"), (3 * d, 4 * d, "silu"), (4 * d, 5 * d, "gelu")))
    ng = len(A_GROUPS)
    npair = A_WIDTH // LANES
    if prompt:
        oa, new_kv = _attn_prompt(qkv, bias, st["kvt"], st["layer"], st["depth"], batch, seq, act_dtype)
    else:
        oa = _attn_sample(qkv, st["kv"], st["layer"], bias, batch, seq)
        new_kv = [jnp.transpose(qkv[(3 * g + 1) * npair:(3 * g + 3) * npair], (1, 0, 2))
                  .reshape(batch, seq, 2, A_HEADS, A_HEAD_DIM) for g in range(ng)]
    dt_col = 3 * ng * npair
    yb, conv_b_new, ssm_new = _ssd(big, qkv, dt_col, st["conv_b"], st["ssm"], lw, batch, seq, act_dtype)
    yc, conv_c_new, lru_new = _lru(big, st["conv_c"], st["lru"], lw, batch, seq, act_dtype)
    x = _merge(x, mod, oa, yb, yc, big, lw)
    return x, (new_kv[0], new_kv[1], new_kv[2], conv_b_new, ssm_new, conv_c_new, lru_new)


def _block(x, mod, lw, st, bias, prompt, batch, seq, g_final):
    x = _ffn(x, mod, 0, lw["g_ff1"], lw["w_ff1_in"], lw["w_ff1_out"])
    x, new_st = _mixer(x, mod, lw, st, bias, prompt, batch, seq)
    x = _ffn(x, mod, 6, lw["g_ff2"], lw["w_ff2_in"], lw["w_ff2_out"], g_final)
    return x, new_st


@jax.jit
def _forward(x_prompt, x_sample, c_prompt, c_sample, cache_win1_kv, cache_win2_kv, cache_win3_kv,
             state_conv_b, state_ssm, state_conv_c, state_lru, rel_bias, w_ada, b_ada, g_ff1,
             w_ff1_in, w_ff1_out, g_mix, w_in, w_a_proj, conv_b_w, conv_b_b, dt_bias, a_log, d_skip,
             g_ssm_norm, w_b_proj, conv_c_w, conv_c_b, w_rgate, b_rgate, w_igate, b_igate, lru_lambda,
             w_c_proj, w_out, g_ff2, w_ff2_in, w_ff2_out, g_final):
    bp, lp, d = x_prompt.shape
    bs, ls, _ = x_sample.shape
    depth = w_ada.shape[0]
    caches = (cache_win1_kv, cache_win2_kv, cache_win3_kv)
    biases = [_group_bias(rel_bias, g) for g in range(len(A_GROUPS))]
    bias_p = [_prompt_bias_table(b, g) for g, b in enumerate(biases)]
    bias_s = [_sample_bias_tables(b, caches[g].shape[2], A_GROUPS[g][1], ls) for g, b in enumerate(biases)]

    yp = x_prompt.reshape(bp * lp, d)
    ys = x_sample.reshape(bs * ls, d)
    c_all = jnp.concatenate([c_prompt, c_sample], axis=0)
    new_p = [[] for _ in range(7)]
    new_s = [[] for _ in range(7)]
    kvt_p = None
    for l in range(depth):
        w_qkv, w_big = _mixer_weights(w_in[l])
        lw = dict(g_ff1=g_ff1[l], w_ff1_in=w_ff1_in[l].astype(BF16), w_ff1_out=w_ff1_out[l].astype(BF16),
                  g_mix=g_mix[l], w_qkv=w_qkv, w_big=w_big, w_a_proj=w_a_proj[l].astype(BF16),
                  conv_b_w=conv_b_w[l], conv_b_b=conv_b_b[l], dt_bias=dt_bias[l], a_log=a_log[l],
                  d_skip=d_skip[l], g_ssm_norm=g_ssm_norm[l], w_b_proj=w_b_proj[l].astype(BF16),
                  conv_c_w=conv_c_w[l], conv_c_b=conv_c_b[l], w_rgate=w_rgate[l], b_rgate=b_rgate[l],
                  w_igate=w_igate[l], b_igate=b_igate[l], lru_lambda=lru_lambda[l],
                  w_c_proj=w_c_proj[l].astype(BF16), w_out=w_out[l].astype(BF16),
                  g_ff2=g_ff2[l], w_ff2_in=w_ff2_in[l].astype(BF16), w_ff2_out=w_ff2_out[l].astype(BF16))
        mod_all = _ada(c_all, w_ada, b_ada, l)
        mod_p = _Mod(mod_all[:bp], bp, lp, per_row=False)
        mod_s = _Mod(mod_all[bp:], bs, ls, per_row=True)
        gf = g_final if l == depth - 1 else None
        st_p = dict(kvt=kvt_p, layer=l, depth=depth,
                    conv_b=jnp.zeros((bp, B_CONV - 1, conv_b_w.shape[2]), F32),
                    ssm=jnp.zeros((bp,) + state_ssm.shape[2:], F32),
                    conv_c=jnp.zeros((bp, C_CONV - 1, conv_c_w.shape[2]), F32),
                    lru=jnp.zeros((bp, state_lru.shape[2]), F32))
        st_s = dict(kv=caches, layer=l, conv_b=state_conv_b[l], ssm=state_ssm[l],
                    conv_c=state_conv_c[l], lru=state_lru[l])
        yp, stp = _block(yp, mod_p, lw, st_p, bias_p, True, bp, lp, gf)
        ys, sts = _block(ys, mod_s, lw, st_s, bias_s, False, bs, ls, gf)
        kvt_p = list(stp[:3])
        for i in range(7):
            new_p[i].append(stp[i])
            new_s[i].append(sts[i])
    outs_p = [jnp.transpose(v, (0, 1, 5, 2, 3, 4)) for v in kvt_p] + [jnp.stack(v, 0) for v in new_p[3:]]
    outs_s = [jnp.stack(v, 0) for v in new_s]
    return (yp.reshape(bp, lp, d), ys.reshape(bs, ls, d), *outs_p, *outs_s)


def kernel(x_prompt, x_sample, c_prompt, c_sample, cache_win1_kv, cache_win2_kv, cache_win3_kv,
           state_conv_b, state_ssm, state_conv_c, state_lru, rel_bias, w_ada, b_ada, g_ff1,
           w_ff1_in, w_ff1_out, g_mix, w_in, w_a_proj, conv_b_w, conv_b_b, dt_bias, a_log, d_skip,
           g_ssm_norm, w_b_proj, conv_c_w, conv_c_b, w_rgate, b_rgate, w_igate, b_igate, lru_lambda,
           w_c_proj, w_out, g_ff2, w_ff2_in, w_ff2_out, g_final):
    return _forward(x_prompt, x_sample, c_prompt, c_sample, cache_win1_kv, cache_win2_kv,
                    cache_win3_kv, state_conv_b, state_ssm, state_conv_c, state_lru, rel_bias, w_ada,
                    b_ada, g_ff1, w_ff1_in, w_ff1_out, g_mix, w_in, w_a_proj, conv_b_w, conv_b_b,
                    dt_bias, a_log, d_skip, g_ssm_norm, w_b_proj, conv_c_w, conv_c_b, w_rgate,
                    b_rgate, w_igate, b_igate, lru_lambda, w_c_proj, w_out, g_ff2, w_ff2_in,
                    w_ff2_out, g_final)
```

```python
import functools
import math

import jax
import jax.numpy as jnp
import numpy as np
from jax import lax
from jax.experimental import pallas as pl
from jax.experimental.pallas import tpu as pltpu

F32 = jnp.float32
BF16 = jnp.bfloat16

NORM_EPS = 1e-6
N_MOD = 9
A_GROUPS = ((128, 1), (512, 4), (2048, 16))
A_HEADS = 8
A_HEAD_DIM = 64
A_WIDTH = A_HEADS * A_HEAD_DIM
A_KEYS = 128
A_BLOCK = 128
REL_BUCKETS = 32
REL_MAX_EXACT = 16
REL_MAX_DISTANCE = 2048
B_HEAD_DIM = 64
B_GROUPS = 2
B_STATE = 128
B_CONV = 4
C_BLOCKS = 8
C_CONV = 4
C_POW = 8.0

LANES = 128
SUBLANES = 8
SSD_CHUNK = 128
FFN_TM = 1024
PROJ_TM = 2048
PROJ_SUB = 512
MERGE_TM = 512
QKV_TN = 1024
BIG_TN = 1536
DT_PAD = 512
ATTN_UNROLL = 8
ATTN_UNROLL_REST = 12
ATTN_MAX_STRIDE = 4
VMEM_LIMIT = 56 * 1024 * 1024


def _cparams(*sem):
    return pltpu.CompilerParams(dimension_semantics=sem, vmem_limit_bytes=VMEM_LIMIT)


def _resident(shape):
    nd = len(shape)
    return pl.BlockSpec(shape, lambda *_: (0,) * nd, pipeline_mode=pl.Buffered(1))


def _ld(ref):
    return ref[0] if len(ref.shape) == 3 else ref[...]


def _norm_mod(x, g, sc, sh):
    ms = jnp.mean(x * x, axis=-1, keepdims=True)
    return (x * lax.rsqrt(ms + NORM_EPS) * g) * (1.0 + sc) + sh


def _softplus(x):
    return jnp.maximum(x, 0.0) + jnp.log1p(jnp.exp(-jnp.abs(x)))


def _causal_conv(xpad, w_ref, b_ref, q):
    taps = w_ref.shape[0]
    assert taps - 1 <= SUBLANES
    xe = xpad[...]
    w = w_ref[...]
    acc = xe * w[0:1, :]
    for k in range(1, taps):
        acc = xe * w[k:k + 1, :] + pltpu.roll(acc, 1, axis=0)
    return acc[SUBLANES:SUBLANES + q] + b_ref[...]


def _split_bf16(v, n):
    parts = []
    r = v
    for _ in range(n):
        p = r.astype(BF16)
        parts.append(p)
        r = r - p.astype(F32)
    return parts


def _dot(a, b):
    return jnp.dot(a, b, preferred_element_type=F32)


def _dot_nt(a, b):
    return lax.dot_general(a, b, (((1,), (1,)), ((), ())), preferred_element_type=F32)


def _ada_kernel(c_ref, w_ref, b_ref, o_ref):
    c = c_ref[...]
    a = (c * jax.nn.sigmoid(c)).astype(BF16)
    o_ref[...] = _dot(a, w_ref[...].astype(BF16)) + b_ref[...]


def _ada(c, w, b, layer):
    rows, d = c.shape
    depth, _, n = w.shape
    tn = 1024
    return pl.pallas_call(
        _ada_kernel,
        grid=(n // tn,),
        in_specs=[pl.BlockSpec((rows, d), lambda j: (0, 0)),
                  pl.BlockSpec((None, d, tn), lambda j: (layer, 0, j)),
                  pl.BlockSpec((None, 1, tn), lambda j: (layer, 0, j))],
        out_specs=pl.BlockSpec((rows, tn), lambda j: (0, j)),
        out_shape=jax.ShapeDtypeStruct((rows, n), F32),
        compiler_params=_cparams("parallel"),
        name="ada",
    )(c, w, b.reshape(depth, 1, n))


class _Mod:
    def __init__(self, mod, batch, seq, per_row):
        d = mod.shape[1] // N_MOD
        self.d = d
        self.seq = seq
        self.per_row = per_row
        if per_row:
            self.arr = jnp.repeat(mod, seq, axis=0)
        else:
            self.arr = mod.reshape(batch * N_MOD, 1, d)

    def spec(self, k, tm, grid_rank):
        d = self.d
        if self.per_row:
            if grid_rank == 1:
                return pl.BlockSpec((tm, d), lambda i: (i, k))
            return pl.BlockSpec((tm, d), lambda i, j: (i, k))
        per = self.seq // tm
        if grid_rank == 1:
            return pl.BlockSpec((1, 1, d), lambda i: ((i // per) * N_MOD + k, 0, 0))
        return pl.BlockSpec((1, 1, d), lambda i, j: ((i // per) * N_MOD + k, 0, 0))


def _row_tile(m, seq, per_row, want):
    tm = min(want, m if per_row else seq)
    assert m % tm == 0 and (per_row or seq % tm == 0)
    return tm


def _ffn_kernel(x_ref, sh_ref, sc_ref, gt_ref, gn_ref, win_ref, wout_ref, *rest, d_ff, fc, final):
    if final:
        gf_ref, o_ref, acc_ref = rest
    else:
        o_ref, acc_ref = rest
    x = x_ref[...]
    h = _norm_mod(x, gn_ref[...], _ld(sc_ref), _ld(sh_ref)).astype(BF16)
    for c in range(d_ff // fc):
        u = _dot(h, win_ref[:, c * fc:(c + 1) * fc])
        v = _dot(h, win_ref[:, d_ff + c * fc:d_ff + (c + 1) * fc])
        a = (u * jax.nn.sigmoid(u) * v).astype(BF16)
        part = _dot(a, wout_ref[c * fc:(c + 1) * fc, :])
        if c == 0:
            acc_ref[...] = part
        else:
            acc_ref[...] += part
    y = x + 0.5 * _ld(gt_ref) * acc_ref[...]
    if final:
        ms = jnp.mean(y * y, axis=-1, keepdims=True)
        y = y * lax.rsqrt(ms + NORM_EPS) * gf_ref[...]
    o_ref[...] = y


def _ffn(x, mod, k0, gn, w_in, w_out, g_final=None):
    m, d = x.shape
    d_ff = w_out.shape[0]
    tm = _row_tile(m, mod.seq, mod.per_row, FFN_TM)
    final = g_final is not None
    in_specs = [pl.BlockSpec((tm, d), lambda i: (i, 0)),
                mod.spec(k0, tm, 1), mod.spec(k0 + 1, tm, 1), mod.spec(k0 + 2, tm, 1),
                _resident((1, d)), _resident(w_in.shape), _resident(w_out.shape)]
    args = [x, mod.arr, mod.arr, mod.arr, gn.reshape(1, d), w_in, w_out]
    if final:
        in_specs.append(_resident((1, d)))
        args.append(g_final.reshape(1, d))
    return pl.pallas_call(
        functools.partial(_ffn_kernel, d_ff=d_ff, fc=256, final=final),
        grid=(m // tm,),
        in_specs=in_specs,
        out_specs=pl.BlockSpec((tm, d), lambda i: (i, 0)),
        out_shape=jax.ShapeDtypeStruct((m, d), F32),
        scratch_shapes=[pltpu.VMEM((tm, d), F32)],
        compiler_params=_cparams("parallel"),
        name="ffn",
    )(*args)


_ACTS = {"sigmoid": jax.nn.sigmoid, "silu": lambda v: v * jax.nn.sigmoid(v), "gelu": jax.nn.gelu}


def _proj_kernel(x_ref, sh_ref, sc_ref, gn_ref, w_ref, o_ref, h_ref, *, tile_acts):
    j = pl.program_id(1)

    @pl.when(j == 0)
    def _():
        h_ref[...] = _norm_mod(x_ref[...], gn_ref[...], _ld(sc_ref), _ld(sh_ref)).astype(BF16)

    tm = h_ref.shape[0]
    sub = min(tm, PROJ_SUB)

    def emit(segs):
        for r in range(tm // sub):
            rs = slice(r * sub, (r + 1) * sub)
            res = _dot(h_ref[rs, :], w_ref[...])
            if len(o_ref.shape) == 3:
                for s in range(o_ref.shape[0]):
                    o_ref[s, rs, :] = res[:, s * LANES:(s + 1) * LANES].astype(o_ref.dtype)
            else:
                for lo, hi, act in segs:
                    v = res[:, lo:hi]
                    o_ref[rs, lo:hi] = (v if act is None else _ACTS[act](v)).astype(o_ref.dtype)

    if tile_acts is None:
        emit(((0, w_ref.shape[1], None),))
    else:
        for jj, segs in enumerate(tile_acts):
            pl.when(j == jj)(functools.partial(emit, segs))


def _proj(x, mod, gn, w, out_dtype, tn, slabs=False, acts=None):
    m, d = x.shape
    n = w.shape[1]
    tm = _row_tile(m, mod.seq, mod.per_row, PROJ_TM)
    tile_acts = None
    if acts is not None:
        cuts = sorted({0, n} | {c for a in acts for c in a[:2]} | set(range(0, n, tn)))
        tile_acts = [[] for _ in range(n // tn)]
        for lo, hi in zip(cuts[:-1], cuts[1:]):
            name = next((a[2] for a in acts if a[0] <= lo and hi <= a[1]), None)
            tile_acts[lo // tn].append((lo % tn, hi - lo // tn * tn, name))
        tile_acts = tuple(tuple(t) for t in tile_acts)
    if slabs:
        out_spec = pl.BlockSpec((tn // LANES, tm, LANES), lambda i, j: (j, i, 0))
        out_shape = jax.ShapeDtypeStruct((n // LANES, m, LANES), out_dtype)
    else:
        out_spec = pl.BlockSpec((tm, tn), lambda i, j: (i, j))
        out_shape = jax.ShapeDtypeStruct((m, n), out_dtype)
    return pl.pallas_call(
        functools.partial(_proj_kernel, tile_acts=tile_acts),
        grid=(m // tm, n // tn),
        in_specs=[pl.BlockSpec((tm, d), lambda i, j: (i, 0)),
                  mod.spec(3, tm, 2), mod.spec(4, tm, 2),
                  pl.BlockSpec((1, d), lambda i, j: (0, 0)),
                  pl.BlockSpec((d, tn), lambda i, j: (0, j))],
        out_specs=out_spec,
        out_shape=out_shape,
        scratch_shapes=[pltpu.VMEM((tm, d), BF16)],
        compiler_params=_cparams("parallel", "arbitrary"),
        name="proj",
    )(x, mod.arr, mod.arr, gn.reshape(1, d), w)


def _pair_softmax(q2, k2, v2, bias2, lo, transposed=False):
    rows = q2.shape[0]
    zero = jnp.zeros_like(q2)
    qs = jnp.concatenate([jnp.where(lo, q2, zero), jnp.where(lo, zero, q2)], axis=0).astype(BF16)
    s = (_dot(qs, k2) if transposed else _dot_nt(qs, k2)) + bias2
    mx = jnp.max(s, axis=-1, keepdims=True)
    p = jnp.exp(s - mx)
    l = jnp.sum(p, axis=-1, keepdims=True)
    pb = p.astype(BF16)
    o = _dot_nt(pb, v2) if transposed else _dot(pb, v2)
    return (jnp.where(lo, o[:rows], o[rows:]), jnp.where(lo, mx[:rows], mx[rows:]),
            jnp.where(lo, l[:rows], l[rows:]))


def _merge_softmax(acc, new):
    ao, am, al = acc
    o, m, l = new
    mn = jnp.maximum(am, m)
    a1 = jnp.exp(am - mn)
    a2 = jnp.exp(m - mn)
    return ao * a1 + o * a2, mn, al * a1 + l * a2


def _attn_plan(g):
    dil = A_GROUPS[g][1]
    if dil <= ATTN_MAX_STRIDE:
        return dil, 1, A_KEYS // A_BLOCK
    assert dil % ATTN_MAX_STRIDE == 0
    step = dil // ATTN_MAX_STRIDE
    return ATTN_MAX_STRIDE, step, step * A_KEYS // A_BLOCK


def _attn_prompt_kernel(*refs, seq, n_alias):
    ng = len(A_GROUPS)
    qkv = refs[:3 * ng]
    bias_refs = refs[3 * ng:4 * ng]
    o_ref = refs[4 * ng + n_alias]
    kvt_refs = refs[4 * ng + 1 + n_alias:5 * ng + 1 + n_alias]
    acc_o, acc_m, acc_l = refs[-3:]
    blk = A_BLOCK

    for g in range(ng):
        keep = kvt_refs[g].shape[-1]
        for kv in range(2):
            src = qkv[3 * g + 1 + kv]
            for c in range(keep // blk):
                t = src[pl.ds(seq - keep + c * blk, blk), :].T
                for i in range(2):
                    kvt_refs[g][kv, i, :, c * blk:(c + 1) * blk] = t[i * A_HEAD_DIM:(i + 1) * A_HEAD_DIM]

    hp = pl.program_id(1)
    lo = lax.broadcasted_iota(jnp.int32, (blk, LANES), 1) < A_HEAD_DIM

    def tiles(g, starts, kw):
        stride = _attn_plan(g)[0]
        q_ref, k_ref, v_ref = qkv[3 * g:3 * g + 3]
        bias_ref = bias_refs[g]
        wcols = bias_ref.shape[-1]

        def rows(start, n):
            return pl.ds(start, n) if stride == 1 else pl.ds(start, n, stride=stride)

        news = []
        for qstart, kstart in starts:
            q2 = q_ref[rows(qstart, blk), :]
            k2 = k_ref[rows(kstart, kw), :].astype(BF16)
            v2 = v_ref[rows(kstart, kw), :].astype(BF16)
            news.append(_pair_softmax(q2, k2, v2, bias_ref[hp, :, wcols - kw:wcols], lo))
        for (qstart, _), new in zip(starts, news):
            sel = rows(qstart, blk)
            if g > 0:
                new = _merge_softmax((acc_o[sel, :], acc_m[sel, :], acc_l[sel, :]), new)
            acc_o[sel, :] = new[0]
            acc_m[sel, :] = new[1]
            acc_l[sel, :] = new[2]

    for g in range(ng):
        stride, _, wblk = _attn_plan(g)
        assert stride <= ATTN_UNROLL
        nb = seq // stride // blk
        span = blk * stride
        for b in range(min(wblk, nb)):
            tiles(g, [(r + b * span, r) for r in range(stride)], (b + 1) * blk)
        nfull = nb - wblk
        if nfull > 0:
            ub = max(u for u in range(1, nfull + 1) if nfull % u == 0 and u * stride <= ATTN_UNROLL_REST)

            def full(i, carry, g=g, span=span, stride=stride, ub=ub, wblk=wblk):
                b = wblk + i * ub
                tiles(g, [(r + (b + k) * span, r + (b + k - wblk) * span)
                          for k in range(ub) for r in range(stride)], (wblk + 1) * blk)
                return carry

            lax.fori_loop(0, nfull // ub, full, 0)
    o_ref[...] = (acc_o[...] / acc_l[...]).astype(o_ref.dtype)


def _attn_prompt(qkv, bias_tab, kvt_prev, layer, depth, batch, seq, o_dtype):
    npair = A_WIDTH // LANES
    ng = len(A_GROUPS)
    col = lambda j: pl.BlockSpec((None, seq, LANES), lambda b, h: (j * npair + h, b, 0))
    keeps = [min(win, seq) for win, _ in A_GROUPS]
    n_alias = 0 if kvt_prev is None else ng
    alias_specs = [pl.BlockSpec(memory_space=pl.ANY)] * n_alias
    alias_args = [] if kvt_prev is None else list(kvt_prev)
    n_in = 4 * ng
    outs = pl.pallas_call(
        functools.partial(_attn_prompt_kernel, seq=seq, n_alias=n_alias),
        grid=(batch, npair),
        in_specs=[col(j) for j in range(3 * ng)]
        + [pl.BlockSpec(tab.shape, lambda b, h: (0, 0, 0)) for tab in bias_tab] + alias_specs,
        out_specs=[pl.BlockSpec((seq, LANES), lambda b, h: (b, h))]
        + [pl.BlockSpec((None, None, 2, 2, A_HEAD_DIM, keep), lambda b, h: (layer, b, 0, h, 0, 0))
           for keep in keeps],
        out_shape=[jax.ShapeDtypeStruct((batch * seq, A_WIDTH), o_dtype)]
        + [jax.ShapeDtypeStruct((depth, batch, 2, A_HEADS, A_HEAD_DIM, keep), F32) for keep in keeps],
        input_output_aliases={n_in + g: 1 + g for g in range(n_alias)},
        scratch_shapes=[pltpu.VMEM((seq, LANES), F32)] * 3,
        compiler_params=_cparams("parallel", "parallel"),
        name="attn_prompt",
    )(*([qkv] * (3 * ng)), *bias_tab, *alias_args)
    return outs[0], list(outs[1:])


def _attn_sample_kernel(qkv_ref, buf1_ref, buf2_ref, buf3_ref, bb1, bn1, bb2, bn2, bb3, bn3,
                        o_ref, knew_scr, vnew_scr):
    t = qkv_ref.shape[1]
    npair = A_HEADS // 2
    bufs = (buf1_ref, buf2_ref, buf3_ref)
    bias = ((bb1, bn1), (bb2, bn2), (bb3, bn3))
    lo = lax.broadcasted_iota(jnp.int32, (t, LANES), 1) < A_HEAD_DIM
    knew_scr[...] = jnp.zeros_like(knew_scr)
    vnew_scr[...] = jnp.zeros_like(vnew_scr)
    for hp in range(npair):
        cols = slice(hp * LANES, (hp + 1) * LANES)
        acc = None
        for g in range(len(A_GROUPS)):
            q2 = qkv_ref[3 * g * npair + hp]
            wb = bufs[g].shape[-1]
            kb = bufs[g][0, 2 * hp:2 * hp + 2].reshape(LANES, wb).astype(BF16)
            vb = bufs[g][1, 2 * hp:2 * hp + 2].reshape(LANES, wb).astype(BF16)
            knew_scr[g, hp, 0:t, :] = qkv_ref[(3 * g + 1) * npair + hp]
            vnew_scr[g, hp, 0:t, :] = qkv_ref[(3 * g + 2) * npair + hp]
            kn = knew_scr[g, hp].astype(BF16)
            vn = vnew_scr[g, hp].astype(BF16)
            bbuf, bnew = bias[g]
            new = _pair_softmax(q2, kb, vb, bbuf[hp], lo, transposed=True)
            new = _merge_softmax(new, _pair_softmax(q2, kn, vn, bnew[hp], lo))
            acc = new if acc is None else _merge_softmax(acc, new)
        o_ref[:, cols] = acc[0] / acc[2]


def _attn_sample(qkv, caches, layer, bias_s, batch, t):
    aw = A_WIDTH
    ng = len(A_GROUPS)
    bufs = [jnp.transpose(c, (0, 1, 3, 4, 5, 2)) for c in caches]
    buf_specs = [pl.BlockSpec((None, None) + bv.shape[2:], lambda b: (layer, b, 0, 0, 0, 0)) for bv in bufs]
    tabs = [tab for pair in bias_s for tab in pair]
    return pl.pallas_call(
        _attn_sample_kernel,
        grid=(batch,),
        in_specs=[pl.BlockSpec((3 * ng * aw // LANES, t, LANES), lambda b: (0, b, 0))] + buf_specs
        + [pl.BlockSpec(tab.shape, lambda b: (0, 0, 0)) for tab in tabs],
        out_specs=pl.BlockSpec((t, aw), lambda b: (b, 0)),
        out_shape=jax.ShapeDtypeStruct((batch * t, aw), F32),
        scratch_shapes=[pltpu.VMEM((ng, aw // LANES, LANES, LANES), F32)] * 2,
        compiler_params=_cparams("parallel"),
        name="attn_sample",
    )(qkv, *bufs, *tabs)


def _ssd_kernel(xbc_ref, z_ref, dt_ref, conv0_ref, st0_ref, cw_ref, cb_ref, dtb_ref, aneg_ref,
                dsk_ref, gn_ref, e_ref, y_ref, convo_ref, sto_ref, xpad, st_t, y_scr, *, lv, nch):
    q = SSD_CHUNK
    c = pl.program_id(1)
    width = y_ref.shape[1]
    nst = B_STATE
    hpg = width // B_HEAD_DIM // B_GROUPS // 2

    @pl.when(c == 0)
    def _():
        xpad[...] = jnp.zeros_like(xpad)
        xpad[0:SUBLANES, :] = conv0_ref[0]
        for j in range(width // LANES):
            st_t[:, j * LANES:(j + 1) * LANES] = st0_ref[0, j * LANES:(j + 1) * LANES, :].T

    @pl.when(c > 0)
    def _():
        xpad[0:SUBLANES, :] = xpad[q:q + SUBLANES, :]

    xpad[SUBLANES:SUBLANES + lv, :] = xbc_ref[...].astype(F32)

    conv = _causal_conv(xpad, cw_ref, cb_ref, q)
    act = conv * jax.nn.sigmoid(conv)
    xs = act[:, :width]
    bm = [act[:, width + gi * nst:width + (gi + 1) * nst] for gi in range(B_GROUPS)]
    cm = [act[:, width + (B_GROUPS + gi) * nst:width + (B_GROUPS + gi + 1) * nst] for gi in range(B_GROUPS)]

    dt_raw = dt_ref[...]
    if lv < q:
        dt_raw = jnp.concatenate([dt_raw, jnp.zeros((q - lv, LANES), F32)], axis=0)
    dt = _softplus(dt_raw + dtb_ref[...])
    if lv < q:
        dt = jnp.where(lax.broadcasted_iota(jnp.int32, (q, LANES), 0) < lv, dt, 0.0)
    dta = dt * aneg_ref[...]

    row = lax.broadcasted_iota(jnp.int32, (q, q), 0)
    col = lax.broadcasted_iota(jnp.int32, (q, q), 1)
    causal = row >= col
    tri = jnp.where(causal, 1.0, 0.0).astype(BF16)
    acum = sum(_dot(tri, p) for p in _split_bf16(dta, 3))
    acum_t = acum.T
    e = e_ref[...]
    dt_x = sum(_dot(p, e) for p in _split_bf16(dt, 2))
    acum_x = sum(_dot(p, e) for p in _split_bf16(acum, 2))
    last_x = acum_x[q - 1:q, :]
    ea_x = jnp.exp(acum_x)
    xdt = xs * dt_x
    xdt_te = (xdt * jnp.exp(last_x - acum_x)).astype(BF16)
    xdt_b = xdt.astype(BF16)
    chunk_decay = jnp.exp(last_x)
    lo = lax.broadcasted_iota(jnp.int32, (q, LANES), 1) < B_HEAD_DIM

    for gi in range(B_GROUPS):
        cmb = cm[gi].astype(BF16)
        cb = _dot_nt(cmb, bm[gi].astype(BF16))
        bm_t = bm[gi].T.astype(BF16)
        for hp in range(gi * hpg, (gi + 1) * hpg):
            cols = slice(hp * LANES, (hp + 1) * LANES)
            ax = acum_x[:, cols]
            ax_r = pltpu.roll(ax, B_HEAD_DIM, axis=1)
            ys = []
            for col_v, h in ((jnp.where(lo, ax, ax_r), 2 * hp), (jnp.where(lo, ax_r, ax), 2 * hp + 1)):
                seg = col_v - acum_t[h:h + 1, :]
                dec = jnp.exp(jnp.where(causal, seg, -jnp.inf))
                ys.append(_dot((cb * dec).astype(BF16), xdt_b[:, cols]))
            st_old = st_t[:, cols]
            y_off = _dot(cmb, st_old.astype(BF16)) * ea_x[:, cols]
            st_t[:, cols] = st_old * chunk_decay[:, cols] + _dot(bm_t, xdt_te[:, cols])
            y_scr[:, cols] = jnp.where(lo, ys[0], ys[1]) + y_off + dsk_ref[:, cols] * xs[:, cols]

    z = z_ref[...].astype(F32)
    if lv < q:
        z = jnp.concatenate([z, jnp.zeros((q - lv, width), F32)], axis=0)
    y = y_scr[...] * z
    gw = width // B_GROUPS
    for gi in range(B_GROUPS):
        yg = y[:, gi * gw:(gi + 1) * gw]
        ms = jnp.mean(yg * yg, axis=-1, keepdims=True)
        yn = yg * lax.rsqrt(ms + NORM_EPS) * gn_ref[:, gi * gw:(gi + 1) * gw]
        y_ref[:, gi * gw:(gi + 1) * gw] = yn[0:lv].astype(y_ref.dtype)

    @pl.when(c == nch - 1)
    def _():
        convo_ref[0] = xpad[lv:lv + SUBLANES, :]
        for j in range(width // LANES):
            sto_ref[0, j * LANES:(j + 1) * LANES, :] = st_t[:, j * LANES:(j + 1) * LANES].T


def _ssd(big, qkv, dt_col, conv0, st0, lw, batch, seq, o_dtype):
    width = lw["w_b_proj"].shape[0]
    cch = lw["conv_b_w"].shape[1]
    nheads = width // B_HEAD_DIM
    lv = min(seq, SSD_CHUNK)
    nch = seq // lv
    assert lv == SSD_CHUNK or nch == 1
    pad = lambda v: jnp.pad(v.astype(F32), (0, LANES - nheads)).reshape(1, LANES)
    e = (np.arange(LANES)[:, None] == (np.arange(width)[None, :] // B_HEAD_DIM)).astype(np.float32)
    conv0p = jnp.pad(conv0, ((0, 0), (SUBLANES - (B_CONV - 1), 0), (0, 0)))
    y, convo, sto = pl.pallas_call(
        functools.partial(_ssd_kernel, lv=lv, nch=nch),
        grid=(batch, nch),
        in_specs=[pl.BlockSpec((lv, cch), lambda b, c: (b * nch + c, 4)),
                  pl.BlockSpec((lv, width), lambda b, c: (b * nch + c, 3)),
                  pl.BlockSpec((None, lv, LANES), lambda b, c: (dt_col, b * nch + c, 0)),
                  pl.BlockSpec((1, SUBLANES, cch), lambda b, c: (b, 0, 0)),
                  pl.BlockSpec((1, width, B_STATE), lambda b, c: (b, 0, 0)),
                  pl.BlockSpec((B_CONV, cch), lambda b, c: (0, 0)),
                  pl.BlockSpec((1, cch), lambda b, c: (0, 0)),
                  pl.BlockSpec((1, LANES), lambda b, c: (0, 0)),
                  pl.BlockSpec((1, LANES), lambda b, c: (0, 0)),
                  pl.BlockSpec((1, width), lambda b, c: (0, 0)),
                  pl.BlockSpec((1, width), lambda b, c: (0, 0)),
                  pl.BlockSpec((LANES, width), lambda b, c: (0, 0))],
        out_specs=[pl.BlockSpec((lv, width), lambda b, c: (b * nch + c, 0)),
                   pl.BlockSpec((1, SUBLANES, cch), lambda b, c: (b, 0, 0)),
                   pl.BlockSpec((1, width, B_STATE), lambda b, c: (b, 0, 0))],
        out_shape=[jax.ShapeDtypeStruct((batch * seq, width), o_dtype),
                   jax.ShapeDtypeStruct((batch, SUBLANES, cch), F32),
                   jax.ShapeDtypeStruct((batch, width, B_STATE), F32)],
        scratch_shapes=[pltpu.VMEM((SUBLANES + SSD_CHUNK, cch), F32),
                        pltpu.VMEM((B_STATE, width), F32),
                        pltpu.VMEM((SSD_CHUNK, width), F32)],
        compiler_params=_cparams("parallel", "arbitrary"),
        name="ssd",
    )(big, big, qkv, conv0p, st0.reshape(batch, width, B_STATE),
      lw["conv_b_w"], lw["conv_b_b"].reshape(1, cch), pad(lw["dt_bias"]),
      pad(-jnp.exp(lw["a_log"].astype(F32))),
      jnp.repeat(lw["d_skip"].astype(F32), B_HEAD_DIM).reshape(1, width),
      lw["g_ssm_norm"].reshape(1, width), jnp.asarray(e, BF16))
    return (y, convo[:, SUBLANES - (B_CONV - 1):, :],
            sto.reshape(batch, nheads, B_HEAD_DIM, B_STATE))


def _lru_kernel(xc_ref, gc_ref, conv0_ref, h0_ref, cw_ref, cb_ref, wr_ref, wi_ref, br_ref, bi_ref,
                lam_ref, y_ref, convo_ref, ho_ref, xpad, hprev, hs_scr, *, q, nch):
    c = pl.program_id(1)
    width = y_ref.shape[1]

    @pl.when(c == 0)
    def _():
        xpad[0:SUBLANES, :] = conv0_ref[0]
        hprev[...] = h0_ref[0]

    @pl.when(c > 0)
    def _():
        xpad[0:SUBLANES, :] = xpad[q:q + SUBLANES, :]

    xpad[SUBLANES:SUBLANES + q, :] = xc_ref[...].astype(F32)
    x = _causal_conv(xpad, cw_ref, cb_ref, q)

    bd = width // C_BLOCKS
    rs, gs = [], []
    for j in range(C_BLOCKS):
        xb = x[:, j * bd:(j + 1) * bd].astype(BF16)
        rs.append(_dot(xb, wr_ref[j]))
        gs.append(_dot(xb, wi_ref[j]))
    rg = jax.nn.sigmoid(jnp.concatenate(rs, axis=1) + br_ref[...])
    ig = jax.nn.sigmoid(jnp.concatenate(gs, axis=1) + bi_ref[...])
    log_a = -C_POW * rg * _softplus(-lam_ref[...])
    a = jnp.exp(log_a)
    u = x * ig * jnp.sqrt(-jnp.tanh(log_a) * (a * a + 1.0))

    rowi = lax.broadcasted_iota(jnp.int32, (SUBLANES, width), 0)
    h = hprev[...]
    for g in range(q // SUBLANES):
        ag = a[g * SUBLANES:(g + 1) * SUBLANES]
        bg = u[g * SUBLANES:(g + 1) * SUBLANES]
        s = 1
        while s < SUBLANES:
            keep = rowi >= s
            a_sh = jnp.where(keep, pltpu.roll(ag, s, axis=0), 1.0)
            b_sh = jnp.where(keep, pltpu.roll(bg, s, axis=0), 0.0)
            bg = ag * b_sh + bg
            ag = ag * a_sh
            s *= 2
        hg = bg + ag * h
        hs_scr[g * SUBLANES:(g + 1) * SUBLANES, :] = hg
        h = hg[SUBLANES - 1:SUBLANES]
    hprev[...] = h
    y_ref[...] = (hs_scr[...] * gc_ref[...].astype(F32)).astype(y_ref.dtype)

    @pl.when(c == nch - 1)
    def _():
        convo_ref[0] = xpad[q:q + SUBLANES, :]
        ho_ref[0] = h


def _lru(big, conv0, h0, lw, batch, seq, o_dtype):
    width = lw["w_c_proj"].shape[0]
    q = min(seq, 128)
    nch = seq // q
    bd = width // C_BLOCKS
    conv0p = jnp.pad(conv0, ((0, 0), (SUBLANES - (C_CONV - 1), 0), (0, 0)))
    vec = lambda v: v.astype(F32).reshape(1, width)
    y, convo, ho = pl.pallas_call(
        functools.partial(_lru_kernel, q=q, nch=nch),
        grid=(batch, nch),
        in_specs=[pl.BlockSpec((q, width), lambda b, c: (b * nch + c, 5)),
                  pl.BlockSpec((q, width), lambda b, c: (b * nch + c, 4)),
                  pl.BlockSpec((1, SUBLANES, width), lambda b, c: (b, 0, 0)),
                  pl.BlockSpec((1, 1, width), lambda b, c: (b, 0, 0)),
                  pl.BlockSpec((C_CONV, width), lambda b, c: (0, 0)),
                  pl.BlockSpec((1, width), lambda b, c: (0, 0)),
                  pl.BlockSpec((C_BLOCKS, bd, bd), lambda b, c: (0, 0, 0)),
                  pl.BlockSpec((C_BLOCKS, bd, bd), lambda b, c: (0, 0, 0)),
                  pl.BlockSpec((1, width), lambda b, c: (0, 0)),
                  pl.BlockSpec((1, width), lambda b, c: (0, 0)),
                  pl.BlockSpec((1, width), lambda b, c: (0, 0))],
        out_specs=[pl.BlockSpec((q, width), lambda b, c: (b * nch + c, 0)),
                   pl.BlockSpec((1, SUBLANES, width), lambda b, c: (b, 0, 0)),
                   pl.BlockSpec((1, 1, width), lambda b, c: (b, 0, 0))],
        out_shape=[jax.ShapeDtypeStruct((batch * seq, width), o_dtype),
                   jax.ShapeDtypeStruct((batch, SUBLANES, width), F32),
                   jax.ShapeDtypeStruct((batch, 1, width), F32)],
        scratch_shapes=[pltpu.VMEM((SUBLANES + q, width), F32),
                        pltpu.VMEM((1, width), F32),
                        pltpu.VMEM((q, width), F32)],
        compiler_params=_cparams("parallel", "arbitrary"),
        name="lru",
    )(big, big, conv0p, h0.reshape(batch, 1, width), lw["conv_c_w"], vec(lw["conv_c_b"]),
      lw["w_rgate"].astype(BF16), lw["w_igate"].astype(BF16), vec(lw["b_rgate"]), vec(lw["b_igate"]),
      vec(lw["lru_lambda"]))
    return y, convo[:, SUBLANES - (C_CONV - 1):, :], ho.reshape(batch, width)


def _merge_kernel(x_ref, gt_ref, oa_ref, yb_ref, yc_ref, gates_ref, wa_ref, wb_ref, wc_ref, wo_ref,
                  out_ref):
    d = x_ref.shape[1]
    ya = _dot(oa_ref[...].astype(BF16), wa_ref[...])
    yb = _dot(yb_ref[...].astype(BF16), wb_ref[...])
    yc = _dot(yc_ref[...].astype(BF16), wc_ref[...])
    sg = gates_ref[...].astype(F32)
    mixed = sg[:, :d] * ya + sg[:, d:2 * d] * yb + sg[:, 2 * d:] * yc
    out_ref[...] = x_ref[...] + _ld(gt_ref) * _dot(mixed.astype(BF16), wo_ref[...])


def _merge(x, mod, oa, yb, yc, big, lw):
    m, d = x.shape
    aw = A_WIDTH
    tm = _row_tile(m, mod.seq, mod.per_row, MERGE_TM)
    row = lambda wd, j=0: pl.BlockSpec((tm, wd), lambda i: (i, j))
    return pl.pallas_call(
        _merge_kernel,
        grid=(m // tm,),
        in_specs=[row(d), mod.spec(5, tm, 1), row(aw), row(yb.shape[1]), row(yc.shape[1]), row(3 * d, 0)]
        + [_resident(lw[k].shape) for k in ("w_a_proj", "w_b_proj", "w_c_proj", "w_out")],
        out_specs=row(d),
        out_shape=jax.ShapeDtypeStruct((m, d), F32),
        compiler_params=_cparams("parallel"),
        name="merge",
    )(x, mod.arr, oa, yb, yc, big,
      lw["w_a_proj"], lw["w_b_proj"], lw["w_c_proj"], lw["w_out"])


def _t5_bucket(dist):
    dist = np.asarray(dist)
    large = REL_MAX_EXACT + (np.log(np.maximum(dist, 1) / REL_MAX_EXACT)
                             / math.log(REL_MAX_DISTANCE / REL_MAX_EXACT)
                             * (REL_BUCKETS - REL_MAX_EXACT)).astype(np.int64)
    large = np.minimum(large, REL_BUCKETS - 1)
    return np.where(dist < REL_MAX_EXACT, dist, large).astype(np.int32)


def _group_bias(rel_bias, g):
    dil = A_GROUPS[g][1]
    buckets = _t5_bucket(np.arange(A_KEYS + 1) * dil)
    return rel_bias[buckets][:, g * A_HEADS:(g + 1) * A_HEADS].T.astype(F32)


def _prompt_bias_table(bias, g):
    _, step, wblk = _attn_plan(g)
    h = bias.shape[0]
    cols = (wblk + 1) * A_BLOCK
    n = cols + A_BLOCK - 1
    x = np.arange(n + 1)
    x = np.where(x < cols, x, x - (n + 1))
    dist = wblk * A_BLOCK - x
    valid = (dist >= 0) & (dist % step == 0) & (dist // step <= A_KEYS)
    v = jnp.where(valid[None], bias[:, np.clip(dist // step, 0, A_KEYS)], -jnp.inf)
    tab = jnp.tile(v, (1, A_BLOCK))[:, :A_BLOCK * n].reshape(h, A_BLOCK, n)[:, :, :cols]
    return tab.reshape(h // 2, 2 * A_BLOCK, cols)


def _sample_bias_tables(bias, wb, dil, t):
    h = bias.shape[0]
    front = t - 1 + wb - A_KEYS * dil
    assert front >= 0
    sparse = jnp.concatenate([bias[:, ::-1, None], jnp.full((h, A_KEYS + 1, dil - 1), -jnp.inf, F32)], axis=2)
    base = jnp.concatenate([jnp.full((h, front), -jnp.inf, F32), sparse.reshape(h, (A_KEYS + 1) * dil)], axis=1)
    tab_buf = jnp.stack([base[:, t - 1 - tq:t - 1 - tq + wb] for tq in range(t)], axis=1)
    tq = np.arange(t)[:, None]
    dist = tq - np.arange(LANES)[None, :]
    valid = (dist >= 0) & (dist % dil == 0) & (np.arange(LANES)[None, :] < t)
    tab_new = jnp.where(valid[None], bias[:, np.clip(dist // dil, 0, A_KEYS)], -jnp.inf)
    return tab_buf.reshape(h // 2, 2 * t, wb), tab_new.reshape(h // 2, 2 * t, LANES)


def _mixer_weights(w_in):
    na = len(A_GROUPS) * A_WIDTH
    d = w_in.shape[0]
    offs = np.cumsum([0, na, na, na, 1024, 1536, 16, 1024, 1024, 3 * d])
    seg = lambda i: w_in[:, offs[i]:offs[i + 1]]
    qa, ka, va, zb, xbc, dtb, xc, gc, gates = [seg(i) for i in range(9)]
    cols = []
    for g in range(len(A_GROUPS)):
        sl = slice(g * A_WIDTH, (g + 1) * A_WIDTH)
        cols += [qa[:, sl] * (A_HEAD_DIM ** -0.5), ka[:, sl], va[:, sl]]
    w_qkv = jnp.concatenate(cols + [dtb, jnp.zeros((d, DT_PAD - dtb.shape[1]), w_in.dtype)], axis=1)
    w_big = jnp.concatenate([gates, zb, gc, xc, xbc], axis=1)
    return w_qkv.astype(BF16), w_big.astype(BF16)


def _mixer(x, mod, lw, st, bias, prompt, batch, seq):
    act_dtype = BF16 if prompt else F32
    qkv = _proj(x, mod, lw["g_mix"], lw["w_qkv"], F32, QKV_TN, slabs=True)
    d = x.shape[1]
    big = _proj(x, mod, lw["g_mix"], lw["w_big"], act_dtype, BIG_TN,
                acts=((0, 3 * d, "sigmoid"), (3 * d, 4 * d, "silu"), (4 * d, 5 * d, "gelu")))
    ng = len(A_GROUPS)
    npair = A_WIDTH // LANES
    if prompt:
        oa, new_kv = _attn_prompt(qkv, bias, st["kvt"], st["layer"], st["depth"], batch, seq, act_dtype)
    else:
        oa = _attn_sample(qkv, st["kv"], st["layer"], bias, batch, seq)
        new_kv = [jnp.transpose(qkv[(3 * g + 1) * npair:(3 * g + 3) * npair], (1, 0, 2))
                  .reshape(batch, seq, 2, A_HEADS, A_HEAD_DIM) for g in range(ng)]
    dt_col = 3 * ng * npair
    yb, conv_b_new, ssm_new = _ssd(big, qkv, dt_col, st["conv_b"], st["ssm"], lw, batch, seq, act_dtype)
    yc, conv_c_new, lru_new = _lru(big, st["conv_c"], st["lru"], lw, batch, seq, act_dtype)
    x = _merge(x, mod, oa, yb, yc, big, lw)
    return x, (new_kv[0], new_kv[1], new_kv[2], conv_b_new, ssm_new, conv_c_new, lru_new)


def _block(x, mod, lw, st, bias, prompt, batch, seq, g_final):
    x = _ffn(x, mod, 0, lw["g_ff1"], lw["w_ff1_in"], lw["w_ff1_out"])
    x, new_st = _mixer(x, mod, lw, st, bias, prompt, batch, seq)
    x = _ffn(x, mod, 6, lw["g_ff2"], lw["w_ff2_in"], lw["w_ff2_out"], g_final)
    return x, new_st


@jax.jit
def _forward(x_prompt, x_sample, c_prompt, c_sample, cache_win1_kv, cache_win2_kv, cache_win3_kv,
             state_conv_b, state_ssm, state_conv_c, state_lru, rel_bias, w_ada, b_ada, g_ff1,
             w_ff1_in, w_ff1_out, g_mix, w_in, w_a_proj, conv_b_w, conv_b_b, dt_bias, a_log, d_skip,
             g_ssm_norm, w_b_proj, conv_c_w, conv_c_b, w_rgate, b_rgate, w_igate, b_igate, lru_lambda,
             w_c_proj, w_out, g_ff2, w_ff2_in, w_ff2_out, g_final):
    bp, lp, d = x_prompt.shape
    bs, ls, _ = x_sample.shape
    depth = w_ada.shape[0]
    caches = (cache_win1_kv, cache_win2_kv, cache_win3_kv)
    biases = [_group_bias(rel_bias, g) for g in range(len(A_GROUPS))]
    bias_p = [_prompt_bias_table(b, g) for g, b in enumerate(biases)]
    bias_s = [_sample_bias_tables(b, caches[g].shape[2], A_GROUPS[g][1], ls) for g, b in enumerate(biases)]

    yp = x_prompt.reshape(bp * lp, d)
    ys = x_sample.reshape(bs * ls, d)
    c_all = jnp.concatenate([c_prompt, c_sample], axis=0)
    new_p = [[] for _ in range(7)]
    new_s = [[] for _ in range(7)]
    kvt_p = None
    for l in range(depth):
        w_qkv, w_big = _mixer_weights(w_in[l])
        lw = dict(g_ff1=g_ff1[l], w_ff1_in=w_ff1_in[l].astype(BF16), w_ff1_out=w_ff1_out[l].astype(BF16),
                  g_mix=g_mix[l], w_qkv=w_qkv, w_big=w_big, w_a_proj=w_a_proj[l].astype(BF16),
                  conv_b_w=conv_b_w[l], conv_b_b=conv_b_b[l], dt_bias=dt_bias[l], a_log=a_log[l],
                  d_skip=d_skip[l], g_ssm_norm=g_ssm_norm[l], w_b_proj=w_b_proj[l].astype(BF16),
                  conv_c_w=conv_c_w[l], conv_c_b=conv_c_b[l], w_rgate=w_rgate[l], b_rgate=b_rgate[l],
                  w_igate=w_igate[l], b_igate=b_igate[l], lru_lambda=lru_lambda[l],
                  w_c_proj=w_c_proj[l].astype(BF16), w_out=w_out[l].astype(BF16),
                  g_ff2=g_ff2[l], w_ff2_in=w_ff2_in[l].astype(BF16), w_ff2_out=w_ff2_out[l].astype(BF16))
        mod_all = _ada(c_all, w_ada, b_ada, l)
        mod_p = _Mod(mod_all[:bp], bp, lp, per_row=False)
        mod_s = _Mod(mod_all[bp:], bs, ls, per_row=True)
        gf = g_final if l == depth - 1 else None
        st_p = dict(kvt=kvt_p, layer=l, depth=depth,
                    conv_b=jnp.zeros((bp, B_CONV - 1, conv_b_w.shape[2]), F32),
                    ssm=jnp.zeros((bp,) + state_ssm.shape[2:], F32),
                    conv_c=jnp.zeros((bp, C_CONV - 1, conv_c_w.shape[2]), F32),
                    lru=jnp.zeros((bp, state_lru.shape[2]), F32))
        st_s = dict(kv=caches, layer=l, conv_b=state_conv_b[l], ssm=state_ssm[l],
                    conv_c=state_conv_c[l], lru=state_lru[l])
        yp, stp = _block(yp, mod_p, lw, st_p, bias_p, True, bp, lp, gf)
        ys, sts = _block(ys, mod_s, lw, st_s, bias_s, False, bs, ls, gf)
        kvt_p = list(stp[:3])
        for i in range(7):
            new_p[i].append(stp[i])
            new_s[i].append(sts[i])
    outs_p = [jnp.transpose(v, (0, 1, 5, 2, 3, 4)) for v in kvt_p] + [jnp.stack(v, 0) for v in new_p[3:]]
    outs_s = [jnp.stack(v, 0) for v in new_s]
    return (yp.reshape(bp, lp, d), ys.reshape(bs, ls, d), *outs_p, *outs_s)


def kernel(x_prompt, x_sample, c_prompt, c_sample, cache_win1_kv, cache_win2_kv, cache_win3_kv,
           state_conv_b, state_ssm, state_conv_c, state_lru, rel_bias, w_ada, b_ada, g_ff1,
           w_ff1_in, w_ff1_out, g_mix, w_in, w_a_proj, conv_b_w, conv_b_b, dt_bias, a_log, d_skip,
           g_ssm_norm, w_b_proj, conv_c_w, conv_c_b, w_rgate, b_rgate, w_igate, b_igate, lru_lambda,
           w_c_proj, w_out, g_ff2, w_ff2_in, w_ff2_out, g_final):
    return _forward(x_prompt, x_sample, c_prompt, c_sample, cache_win1_kv, cache_win2_kv,
                    cache_win3_kv, state_conv_b, state_ssm, state_conv_c, state_lru, rel_bias, w_ada,
                    b_ada, g_ff1, w_ff1_in, w_ff1_out, g_mix, w_in, w_a_proj, conv_b_w, conv_b_b,
                    dt_bias, a_log, d_skip, g_ssm_norm, w_b_proj, conv_c_w, conv_c_b, w_rgate,
                    b_rgate, w_igate, b_igate, lru_lambda, w_c_proj, w_out, g_ff2, w_ff2_in,
                    w_ff2_out, g_final)
```

```python
import functools
import math

import jax
import jax.numpy as jnp
import numpy as np
from jax import lax
from jax.experimental import pallas as pl
from jax.experimental.pallas import tpu as pltpu

F32 = jnp.float32
BF16 = jnp.bfloat16

NORM_EPS = 1e-6
N_MOD = 9
A_GROUPS = ((128, 1), (512, 4), (2048, 16))
A_HEADS = 8
A_HEAD_DIM = 64
A_WIDTH = A_HEADS * A_HEAD_DIM
A_KEYS = 128
A_BLOCK = 128
REL_BUCKETS = 32
REL_MAX_EXACT = 16
REL_MAX_DISTANCE = 2048
B_HEAD_DIM = 64
B_GROUPS = 2
B_STATE = 128
B_CONV = 4
C_BLOCKS = 8
C_CONV = 4
C_POW = 8.0

LANES = 128
SUBLANES = 8
SSD_CHUNK = 128
FFN_TM = 1024
PROJ_TM = 2048
PROJ_SUB = 512
MERGE_TM = 1024
MERGE_SUB = 256
QKV_TN = 1024
BIG_TN = 1536
DT_PAD = 512
ATTN_UNROLL = 8
ATTN_UNROLL_REST = 12
ATTN_MAX_STRIDE = 4
VMEM_LIMIT = 56 * 1024 * 1024


def _cparams(*sem):
    return pltpu.CompilerParams(dimension_semantics=sem, vmem_limit_bytes=VMEM_LIMIT)


def _resident(shape):
    nd = len(shape)
    return pl.BlockSpec(shape, lambda *_: (0,) * nd, pipeline_mode=pl.Buffered(1))


def _ld(ref):
    return ref[0] if len(ref.shape) == 3 else ref[...]


def _norm_mod(x, g, sc, sh):
    ms = jnp.mean(x * x, axis=-1, keepdims=True)
    return (x * lax.rsqrt(ms + NORM_EPS) * g) * (1.0 + sc) + sh


def _softplus(x):
    return jnp.maximum(x, 0.0) + jnp.log1p(jnp.exp(-jnp.abs(x)))


def _causal_conv(xpad, w_ref, b_ref, q):
    taps = w_ref.shape[0]
    assert taps - 1 <= SUBLANES
    xe = xpad[...]
    w = w_ref[...]
    acc = xe * w[0:1, :]
    for k in range(1, taps):
        acc = xe * w[k:k + 1, :] + pltpu.roll(acc, 1, axis=0)
    return acc[SUBLANES:SUBLANES + q] + b_ref[...]


def _split_bf16(v, n):
    parts = []
    r = v
    for _ in range(n):
        p = r.astype(BF16)
        parts.append(p)
        r = r - p.astype(F32)
    return parts


def _dot(a, b):
    return jnp.dot(a, b, preferred_element_type=F32)


def _dot_nt(a, b):
    return lax.dot_general(a, b, (((1,), (1,)), ((), ())), preferred_element_type=F32)


def _ada_kernel(c_ref, w_ref, b_ref, o_ref):
    c = c_ref[...]
    a = (c * jax.nn.sigmoid(c)).astype(BF16)
    o_ref[...] = _dot(a, w_ref[...].astype(BF16)) + b_ref[...]


def _ada(c, w, b, layer):
    rows, d = c.shape
    depth, _, n = w.shape
    tn = 1024
    return pl.pallas_call(
        _ada_kernel,
        grid=(n // tn,),
        in_specs=[pl.BlockSpec((rows, d), lambda j: (0, 0)),
                  pl.BlockSpec((None, d, tn), lambda j: (layer, 0, j)),
                  pl.BlockSpec((None, 1, tn), lambda j: (layer, 0, j))],
        out_specs=pl.BlockSpec((rows, tn), lambda j: (0, j)),
        out_shape=jax.ShapeDtypeStruct((rows, n), F32),
        compiler_params=_cparams("parallel"),
        name="ada",
    )(c, w, b.reshape(depth, 1, n))


class _Mod:
    def __init__(self, mod, batch, seq, per_row):
        d = mod.shape[1] // N_MOD
        self.d = d
        self.seq = seq
        self.per_row = per_row
        if per_row:
            self.arr = jnp.repeat(mod, seq, axis=0)
        else:
            self.arr = mod.reshape(batch * N_MOD, 1, d)

    def spec(self, k, tm, grid_rank):
        d = self.d
        if self.per_row:
            if grid_rank == 1:
                return pl.BlockSpec((tm, d), lambda i: (i, k))
            return pl.BlockSpec((tm, d), lambda i, j: (i, k))
        per = self.seq // tm
        if grid_rank == 1:
            return pl.BlockSpec((1, 1, d), lambda i: ((i // per) * N_MOD + k, 0, 0))
        return pl.BlockSpec((1, 1, d), lambda i, j: ((i // per) * N_MOD + k, 0, 0))


def _row_tile(m, seq, per_row, want):
    tm = min(want, m if per_row else seq)
    assert m % tm == 0 and (per_row or seq % tm == 0)
    return tm


def _ffn_kernel(x_ref, sh_ref, sc_ref, gt_ref, gn_ref, win_ref, wout_ref, *rest, d_ff, fc, final):
    if final:
        gf_ref, o_ref, acc_ref = rest
    else:
        o_ref, acc_ref = rest
    x = x_ref[...]
    h = _norm_mod(x, gn_ref[...], _ld(sc_ref), _ld(sh_ref)).astype(BF16)
    for c in range(d_ff // fc):
        u = _dot(h, win_ref[:, c * fc:(c + 1) * fc])
        v = _dot(h, win_ref[:, d_ff + c * fc:d_ff + (c + 1) * fc])
        a = (u * jax.nn.sigmoid(u) * v).astype(BF16)
        part = _dot(a, wout_ref[c * fc:(c + 1) * fc, :])
        if c == 0:
            acc_ref[...] = part
        else:
            acc_ref[...] += part
    y = x + 0.5 * _ld(gt_ref) * acc_ref[...]
    if final:
        ms = jnp.mean(y * y, axis=-1, keepdims=True)
        y = y * lax.rsqrt(ms + NORM_EPS) * gf_ref[...]
    o_ref[...] = y


def _ffn(x, mod, k0, gn, w_in, w_out, g_final=None):
    m, d = x.shape
    d_ff = w_out.shape[0]
    tm = _row_tile(m, mod.seq, mod.per_row, FFN_TM)
    final = g_final is not None
    in_specs = [pl.BlockSpec((tm, d), lambda i: (i, 0)),
                mod.spec(k0, tm, 1), mod.spec(k0 + 1, tm, 1), mod.spec(k0 + 2, tm, 1),
                _resident((1, d)), _resident(w_in.shape), _resident(w_out.shape)]
    args = [x, mod.arr, mod.arr, mod.arr, gn.reshape(1, d), w_in, w_out]
    if final:
        in_specs.append(_resident((1, d)))
        args.append(g_final.reshape(1, d))
    return pl.pallas_call(
        functools.partial(_ffn_kernel, d_ff=d_ff, fc=256, final=final),
        grid=(m // tm,),
        in_specs=in_specs,
        out_specs=pl.BlockSpec((tm, d), lambda i: (i, 0)),
        out_shape=jax.ShapeDtypeStruct((m, d), F32),
        scratch_shapes=[pltpu.VMEM((tm, d), F32)],
        compiler_params=_cparams("parallel"),
        name="ffn",
    )(*args)


def _proj_kernel(x_ref, sh_ref, sc_ref, gn_ref, w_ref, o_ref, h_ref):
    @pl.when(pl.program_id(1) == 0)
    def _():
        h_ref[...] = _norm_mod(x_ref[...], gn_ref[...], _ld(sc_ref), _ld(sh_ref)).astype(BF16)

    tm = h_ref.shape[0]
    sub = min(tm, PROJ_SUB)
    for r in range(tm // sub):
        rs = slice(r * sub, (r + 1) * sub)
        res = _dot(h_ref[rs, :], w_ref[...]).astype(o_ref.dtype)
        if len(o_ref.shape) == 2:
            o_ref[rs, :] = res
        else:
            for s in range(o_ref.shape[0]):
                o_ref[s, rs, :] = res[:, s * LANES:(s + 1) * LANES]


def _proj(x, mod, gn, w, out_dtype, tn, slabs=False):
    m, d = x.shape
    n = w.shape[1]
    tm = _row_tile(m, mod.seq, mod.per_row, PROJ_TM)
    if slabs:
        out_spec = pl.BlockSpec((tn // LANES, tm, LANES), lambda i, j: (j, i, 0))
        out_shape = jax.ShapeDtypeStruct((n // LANES, m, LANES), out_dtype)
    else:
        out_spec = pl.BlockSpec((tm, tn), lambda i, j: (i, j))
        out_shape = jax.ShapeDtypeStruct((m, n), out_dtype)
    return pl.pallas_call(
        _proj_kernel,
        grid=(m // tm, n // tn),
        in_specs=[pl.BlockSpec((tm, d), lambda i, j: (i, 0)),
                  mod.spec(3, tm, 2), mod.spec(4, tm, 2),
                  pl.BlockSpec((1, d), lambda i, j: (0, 0)),
                  pl.BlockSpec((d, tn), lambda i, j: (0, j))],
        out_specs=out_spec,
        out_shape=out_shape,
        scratch_shapes=[pltpu.VMEM((tm, d), BF16)],
        compiler_params=_cparams("parallel", "arbitrary"),
        name="proj",
    )(x, mod.arr, mod.arr, gn.reshape(1, d), w)


def _pair_softmax(q2, k2, v2, bias2, lo, transposed=False):
    rows = q2.shape[0]
    zero = jnp.zeros_like(q2)
    qs = jnp.concatenate([jnp.where(lo, q2, zero), jnp.where(lo, zero, q2)], axis=0).astype(BF16)
    s = (_dot(qs, k2) if transposed else _dot_nt(qs, k2)) + bias2
    mx = jnp.max(s, axis=-1, keepdims=True)
    p = jnp.exp(s - mx)
    l = jnp.sum(p, axis=-1, keepdims=True)
    pb = p.astype(BF16)
    o = _dot_nt(pb, v2) if transposed else _dot(pb, v2)
    return (jnp.where(lo, o[:rows], o[rows:]), jnp.where(lo, mx[:rows], mx[rows:]),
            jnp.where(lo, l[:rows], l[rows:]))


def _merge_softmax(acc, new):
    ao, am, al = acc
    o, m, l = new
    mn = jnp.maximum(am, m)
    a1 = jnp.exp(am - mn)
    a2 = jnp.exp(m - mn)
    return ao * a1 + o * a2, mn, al * a1 + l * a2


def _attn_plan(g):
    dil = A_GROUPS[g][1]
    if dil <= ATTN_MAX_STRIDE:
        return dil, 1, A_KEYS // A_BLOCK
    assert dil % ATTN_MAX_STRIDE == 0
    step = dil // ATTN_MAX_STRIDE
    return ATTN_MAX_STRIDE, step, step * A_KEYS // A_BLOCK


def _attn_prompt_kernel(*refs, seq, n_alias):
    ng = len(A_GROUPS)
    qkv = refs[:3 * ng]
    bias_refs = refs[3 * ng:4 * ng]
    o_ref = refs[4 * ng + n_alias]
    kvt_refs = refs[4 * ng + 1 + n_alias:5 * ng + 1 + n_alias]
    acc_o, acc_m, acc_l = refs[-3:]
    blk = A_BLOCK

    for g in range(ng):
        keep = kvt_refs[g].shape[-1]
        for kv in range(2):
            src = qkv[3 * g + 1 + kv]
            for c in range(keep // blk):
                t = src[pl.ds(seq - keep + c * blk, blk), :].T
                for i in range(2):
                    kvt_refs[g][kv, i, :, c * blk:(c + 1) * blk] = t[i * A_HEAD_DIM:(i + 1) * A_HEAD_DIM]

    hp = pl.program_id(1)
    lo = lax.broadcasted_iota(jnp.int32, (blk, LANES), 1) < A_HEAD_DIM

    def tiles(g, starts, kw):
        stride = _attn_plan(g)[0]
        q_ref, k_ref, v_ref = qkv[3 * g:3 * g + 3]
        bias_ref = bias_refs[g]
        wcols = bias_ref.shape[-1]

        def rows(start, n):
            return pl.ds(start, n) if stride == 1 else pl.ds(start, n, stride=stride)

        news = []
        for qstart, kstart in starts:
            q2 = q_ref[rows(qstart, blk), :]
            k2 = k_ref[rows(kstart, kw), :].astype(BF16)
            v2 = v_ref[rows(kstart, kw), :].astype(BF16)
            news.append(_pair_softmax(q2, k2, v2, bias_ref[hp, :, wcols - kw:wcols], lo))
        for (qstart, _), new in zip(starts, news):
            sel = rows(qstart, blk)
            if g > 0:
                new = _merge_softmax((acc_o[sel, :], acc_m[sel, :], acc_l[sel, :]), new)
            acc_o[sel, :] = new[0]
            acc_m[sel, :] = new[1]
            acc_l[sel, :] = new[2]

    for g in range(ng):
        stride, _, wblk = _attn_plan(g)
        assert stride <= ATTN_UNROLL
        nb = seq // stride // blk
        span = blk * stride
        for b in range(min(wblk, nb)):
            tiles(g, [(r + b * span, r) for r in range(stride)], (b + 1) * blk)
        nfull = nb - wblk
        if nfull > 0:
            ub = max(u for u in range(1, nfull + 1) if nfull % u == 0 and u * stride <= ATTN_UNROLL_REST)

            def full(i, carry, g=g, span=span, stride=stride, ub=ub, wblk=wblk):
                b = wblk + i * ub
                tiles(g, [(r + (b + k) * span, r + (b + k - wblk) * span)
                          for k in range(ub) for r in range(stride)], (wblk + 1) * blk)
                return carry

            lax.fori_loop(0, nfull // ub, full, 0)
    o_ref[...] = (acc_o[...] / acc_l[...]).astype(o_ref.dtype)


def _attn_prompt(qkv, bias_tab, kvt_prev, layer, depth, batch, seq, o_dtype):
    npair = A_WIDTH // LANES
    ng = len(A_GROUPS)
    col = lambda j: pl.BlockSpec((None, seq, LANES), lambda b, h: (j * npair + h, b, 0))
    keeps = [min(win, seq) for win, _ in A_GROUPS]
    n_alias = 0 if kvt_prev is None else ng
    alias_specs = [pl.BlockSpec(memory_space=pl.ANY)] * n_alias
    alias_args = [] if kvt_prev is None else list(kvt_prev)
    n_in = 4 * ng
    outs = pl.pallas_call(
        functools.partial(_attn_prompt_kernel, seq=seq, n_alias=n_alias),
        grid=(batch, npair),
        in_specs=[col(j) for j in range(3 * ng)]
        + [pl.BlockSpec(tab.shape, lambda b, h: (0, 0, 0)) for tab in bias_tab] + alias_specs,
        out_specs=[pl.BlockSpec((seq, LANES), lambda b, h: (b, h))]
        + [pl.BlockSpec((None, None, 2, 2, A_HEAD_DIM, keep), lambda b, h: (layer, b, 0, h, 0, 0))
           for keep in keeps],
        out_shape=[jax.ShapeDtypeStruct((batch * seq, A_WIDTH), o_dtype)]
        + [jax.ShapeDtypeStruct((depth, batch, 2, A_HEADS, A_HEAD_DIM, keep), F32) for keep in keeps],
        input_output_aliases={n_in + g: 1 + g for g in range(n_alias)},
        scratch_shapes=[pltpu.VMEM((seq, LANES), F32)] * 3,
        compiler_params=_cparams("parallel", "parallel"),
        name="attn_prompt",
    )(*([qkv] * (3 * ng)), *bias_tab, *alias_args)
    return outs[0], list(outs[1:])


def _attn_sample_kernel(qkv_ref, buf1_ref, buf2_ref, buf3_ref, bb1, bn1, bb2, bn2, bb3, bn3,
                        o_ref, knew_scr, vnew_scr):
    t = qkv_ref.shape[1]
    npair = A_HEADS // 2
    bufs = (buf1_ref, buf2_ref, buf3_ref)
    bias = ((bb1, bn1), (bb2, bn2), (bb3, bn3))
    lo = lax.broadcasted_iota(jnp.int32, (t, LANES), 1) < A_HEAD_DIM
    knew_scr[...] = jnp.zeros_like(knew_scr)
    vnew_scr[...] = jnp.zeros_like(vnew_scr)
    for hp in range(npair):
        cols = slice(hp * LANES, (hp + 1) * LANES)
        acc = None
        for g in range(len(A_GROUPS)):
            q2 = qkv_ref[3 * g * npair + hp]
            wb = bufs[g].shape[-1]
            kb = bufs[g][0, 2 * hp:2 * hp + 2].reshape(LANES, wb).astype(BF16)
            vb = bufs[g][1, 2 * hp:2 * hp + 2].reshape(LANES, wb).astype(BF16)
            knew_scr[g, hp, 0:t, :] = qkv_ref[(3 * g + 1) * npair + hp]
            vnew_scr[g, hp, 0:t, :] = qkv_ref[(3 * g + 2) * npair + hp]
            kn = knew_scr[g, hp].astype(BF16)
            vn = vnew_scr[g, hp].astype(BF16)
            bbuf, bnew = bias[g]
            new = _pair_softmax(q2, kb, vb, bbuf[hp], lo, transposed=True)
            new = _merge_softmax(new, _pair_softmax(q2, kn, vn, bnew[hp], lo))
            acc = new if acc is None else _merge_softmax(acc, new)
        o_ref[:, cols] = acc[0] / acc[2]


def _attn_sample(qkv, caches, layer, bias_s, batch, t):
    aw = A_WIDTH
    ng = len(A_GROUPS)
    bufs = [jnp.transpose(c, (0, 1, 3, 4, 5, 2)) for c in caches]
    buf_specs = [pl.BlockSpec((None, None) + bv.shape[2:], lambda b: (layer, b, 0, 0, 0, 0)) for bv in bufs]
    tabs = [tab for pair in bias_s for tab in pair]
    return pl.pallas_call(
        _attn_sample_kernel,
        grid=(batch,),
        in_specs=[pl.BlockSpec((3 * ng * aw // LANES, t, LANES), lambda b: (0, b, 0))] + buf_specs
        + [pl.BlockSpec(tab.shape, lambda b: (0, 0, 0)) for tab in tabs],
        out_specs=pl.BlockSpec((t, aw), lambda b: (b, 0)),
        out_shape=jax.ShapeDtypeStruct((batch * t, aw), F32),
        scratch_shapes=[pltpu.VMEM((ng, aw // LANES, LANES, LANES), F32)] * 2,
        compiler_params=_cparams("parallel"),
        name="attn_sample",
    )(qkv, *bufs, *tabs)


def _ssd_kernel(xbc_ref, z_ref, dt_ref, conv0_ref, st0_ref, cw_ref, cb_ref, dtb_ref, aneg_ref,
                dsk_ref, gn_ref, e_ref, y_ref, convo_ref, sto_ref, xpad, st_t, y_scr, *, lv, nch):
    q = SSD_CHUNK
    c = pl.program_id(1)
    width = y_ref.shape[1]
    nst = B_STATE
    hpg = width // B_HEAD_DIM // B_GROUPS // 2

    @pl.when(c == 0)
    def _():
        xpad[...] = jnp.zeros_like(xpad)
        xpad[0:SUBLANES, :] = conv0_ref[0]
        for j in range(width // LANES):
            st_t[:, j * LANES:(j + 1) * LANES] = st0_ref[0, j * LANES:(j + 1) * LANES, :].T

    @pl.when(c > 0)
    def _():
        xpad[0:SUBLANES, :] = xpad[q:q + SUBLANES, :]

    xpad[SUBLANES:SUBLANES + lv, :] = xbc_ref[...].astype(F32)

    conv = _causal_conv(xpad, cw_ref, cb_ref, q)
    act = conv * jax.nn.sigmoid(conv)
    xs = act[:, :width]
    bm = [act[:, width + gi * nst:width + (gi + 1) * nst] for gi in range(B_GROUPS)]
    cm = [act[:, width + (B_GROUPS + gi) * nst:width + (B_GROUPS + gi + 1) * nst] for gi in range(B_GROUPS)]

    dt_raw = dt_ref[...]
    if lv < q:
        dt_raw = jnp.concatenate([dt_raw, jnp.zeros((q - lv, LANES), F32)], axis=0)
    dt = _softplus(dt_raw + dtb_ref[...])
    if lv < q:
        dt = jnp.where(lax.broadcasted_iota(jnp.int32, (q, LANES), 0) < lv, dt, 0.0)
    dta = dt * aneg_ref[...]

    row = lax.broadcasted_iota(jnp.int32, (q, q), 0)
    col = lax.broadcasted_iota(jnp.int32, (q, q), 1)
    causal = row >= col
    tri = jnp.where(causal, 1.0, 0.0).astype(BF16)
    acum = sum(_dot(tri, p) for p in _split_bf16(dta, 3))
    acum_t = acum.T
    e = e_ref[...]
    dt_x = sum(_dot(p, e) for p in _split_bf16(dt, 2))
    acum_x = sum(_dot(p, e) for p in _split_bf16(acum, 2))
    last_x = acum_x[q - 1:q, :]
    ea_x = jnp.exp(acum_x)
    xdt = xs * dt_x
    xdt_te = (xdt * jnp.exp(last_x - acum_x)).astype(BF16)
    xdt_b = xdt.astype(BF16)
    chunk_decay = jnp.exp(last_x)
    lo = lax.broadcasted_iota(jnp.int32, (q, LANES), 1) < B_HEAD_DIM

    for gi in range(B_GROUPS):
        cmb = cm[gi].astype(BF16)
        cb = _dot_nt(cmb, bm[gi].astype(BF16))
        bm_t = bm[gi].T.astype(BF16)
        for hp in range(gi * hpg, (gi + 1) * hpg):
            cols = slice(hp * LANES, (hp + 1) * LANES)
            ax = acum_x[:, cols]
            ax_r = pltpu.roll(ax, B_HEAD_DIM, axis=1)
            ys = []
            for col_v, h in ((jnp.where(lo, ax, ax_r), 2 * hp), (jnp.where(lo, ax_r, ax), 2 * hp + 1)):
                seg = col_v - acum_t[h:h + 1, :]
                dec = jnp.exp(jnp.where(causal, seg, -jnp.inf))
                ys.append(_dot((cb * dec).astype(BF16), xdt_b[:, cols]))
            st_old = st_t[:, cols]
            y_off = _dot(cmb, st_old.astype(BF16)) * ea_x[:, cols]
            st_t[:, cols] = st_old * chunk_decay[:, cols] + _dot(bm_t, xdt_te[:, cols])
            y_scr[:, cols] = jnp.where(lo, ys[0], ys[1]) + y_off + dsk_ref[:, cols] * xs[:, cols]

    z = z_ref[...].astype(F32)
    if lv < q:
        z = jnp.concatenate([z, jnp.zeros((q - lv, width), F32)], axis=0)
    y = y_scr[...] * (z * jax.nn.sigmoid(z))
    gw = width // B_GROUPS
    for gi in range(B_GROUPS):
        yg = y[:, gi * gw:(gi + 1) * gw]
        ms = jnp.mean(yg * yg, axis=-1, keepdims=True)
        yn = yg * lax.rsqrt(ms + NORM_EPS) * gn_ref[:, gi * gw:(gi + 1) * gw]
        y_ref[:, gi * gw:(gi + 1) * gw] = yn[0:lv].astype(y_ref.dtype)

    @pl.when(c == nch - 1)
    def _():
        convo_ref[0] = xpad[lv:lv + SUBLANES, :]
        for j in range(width // LANES):
            sto_ref[0, j * LANES:(j + 1) * LANES, :] = st_t[:, j * LANES:(j + 1) * LANES].T


def _ssd(big, qkv, dt_col, conv0, st0, lw, batch, seq, o_dtype):
    width = lw["w_b_proj"].shape[0]
    cch = lw["conv_b_w"].shape[1]
    nheads = width // B_HEAD_DIM
    lv = min(seq, SSD_CHUNK)
    nch = seq // lv
    assert lv == SSD_CHUNK or nch == 1
    pad = lambda v: jnp.pad(v.astype(F32), (0, LANES - nheads)).reshape(1, LANES)
    e = (np.arange(LANES)[:, None] == (np.arange(width)[None, :] // B_HEAD_DIM)).astype(np.float32)
    conv0p = jnp.pad(conv0, ((0, 0), (SUBLANES - (B_CONV - 1), 0), (0, 0)))
    y, convo, sto = pl.pallas_call(
        functools.partial(_ssd_kernel, lv=lv, nch=nch),
        grid=(batch, nch),
        in_specs=[pl.BlockSpec((lv, cch), lambda b, c: (b * nch + c, 4)),
                  pl.BlockSpec((lv, width), lambda b, c: (b * nch + c, 3)),
                  pl.BlockSpec((None, lv, LANES), lambda b, c: (dt_col, b * nch + c, 0)),
                  pl.BlockSpec((1, SUBLANES, cch), lambda b, c: (b, 0, 0)),
                  pl.BlockSpec((1, width, B_STATE), lambda b, c: (b, 0, 0)),
                  pl.BlockSpec((B_CONV, cch), lambda b, c: (0, 0)),
                  pl.BlockSpec((1, cch), lambda b, c: (0, 0)),
                  pl.BlockSpec((1, LANES), lambda b, c: (0, 0)),
                  pl.BlockSpec((1, LANES), lambda b, c: (0, 0)),
                  pl.BlockSpec((1, width), lambda b, c: (0, 0)),
                  pl.BlockSpec((1, width), lambda b, c: (0, 0)),
                  pl.BlockSpec((LANES, width), lambda b, c: (0, 0))],
        out_specs=[pl.BlockSpec((lv, width), lambda b, c: (b * nch + c, 0)),
                   pl.BlockSpec((1, SUBLANES, cch), lambda b, c: (b, 0, 0)),
                   pl.BlockSpec((1, width, B_STATE), lambda b, c: (b, 0, 0))],
        out_shape=[jax.ShapeDtypeStruct((batch * seq, width), o_dtype),
                   jax.ShapeDtypeStruct((batch, SUBLANES, cch), F32),
                   jax.ShapeDtypeStruct((batch, width, B_STATE), F32)],
        scratch_shapes=[pltpu.VMEM((SUBLANES + SSD_CHUNK, cch), F32),
                        pltpu.VMEM((B_STATE, width), F32),
                        pltpu.VMEM((SSD_CHUNK, width), F32)],
        compiler_params=_cparams("parallel", "arbitrary"),
        name="ssd",
    )(big, big, qkv, conv0p, st0.reshape(batch, width, B_STATE),
      lw["conv_b_w"], lw["conv_b_b"].reshape(1, cch), pad(lw["dt_bias"]),
      pad(-jnp.exp(lw["a_log"].astype(F32))),
      jnp.repeat(lw["d_skip"].astype(F32), B_HEAD_DIM).reshape(1, width),
      lw["g_ssm_norm"].reshape(1, width), jnp.asarray(e, BF16))
    return (y, convo[:, SUBLANES - (B_CONV - 1):, :],
            sto.reshape(batch, nheads, B_HEAD_DIM, B_STATE))


def _lru_kernel(xc_ref, gc_ref, conv0_ref, h0_ref, cw_ref, cb_ref, wr_ref, wi_ref, br_ref, bi_ref,
                lam_ref, y_ref, convo_ref, ho_ref, xpad, hprev, hs_scr, *, q, nch):
    c = pl.program_id(1)
    width = y_ref.shape[1]

    @pl.when(c == 0)
    def _():
        xpad[0:SUBLANES, :] = conv0_ref[0]
        hprev[...] = h0_ref[0]

    @pl.when(c > 0)
    def _():
        xpad[0:SUBLANES, :] = xpad[q:q + SUBLANES, :]

    xpad[SUBLANES:SUBLANES + q, :] = xc_ref[...].astype(F32)
    x = _causal_conv(xpad, cw_ref, cb_ref, q)

    bd = width // C_BLOCKS
    rs, gs = [], []
    for j in range(C_BLOCKS):
        xb = x[:, j * bd:(j + 1) * bd].astype(BF16)
        rs.append(_dot(xb, wr_ref[j]))
        gs.append(_dot(xb, wi_ref[j]))
    rg = jax.nn.sigmoid(jnp.concatenate(rs, axis=1) + br_ref[...])
    ig = jax.nn.sigmoid(jnp.concatenate(gs, axis=1) + bi_ref[...])
    log_a = -C_POW * rg * _softplus(-lam_ref[...])
    a = jnp.exp(log_a)
    u = x * ig * jnp.sqrt(-jnp.tanh(log_a) * (a * a + 1.0))

    rowi = lax.broadcasted_iota(jnp.int32, (SUBLANES, width), 0)
    h = hprev[...]
    for g in range(q // SUBLANES):
        ag = a[g * SUBLANES:(g + 1) * SUBLANES]
        bg = u[g * SUBLANES:(g + 1) * SUBLANES]
        s = 1
        while s < SUBLANES:
            keep = rowi >= s
            a_sh = jnp.where(keep, pltpu.roll(ag, s, axis=0), 1.0)
            b_sh = jnp.where(keep, pltpu.roll(bg, s, axis=0), 0.0)
            bg = ag * b_sh + bg
            ag = ag * a_sh
            s *= 2
        hg = bg + ag * h
        hs_scr[g * SUBLANES:(g + 1) * SUBLANES, :] = hg
        h = hg[SUBLANES - 1:SUBLANES]
    hprev[...] = h
    y_ref[...] = (hs_scr[...] * jax.nn.gelu(gc_ref[...].astype(F32))).astype(y_ref.dtype)

    @pl.when(c == nch - 1)
    def _():
        convo_ref[0] = xpad[q:q + SUBLANES, :]
        ho_ref[0] = h


def _lru(big, conv0, h0, lw, batch, seq, o_dtype):
    width = lw["w_c_proj"].shape[0]
    q = min(seq, 128)
    nch = seq // q
    bd = width // C_BLOCKS
    conv0p = jnp.pad(conv0, ((0, 0), (SUBLANES - (C_CONV - 1), 0), (0, 0)))
    vec = lambda v: v.astype(F32).reshape(1, width)
    y, convo, ho = pl.pallas_call(
        functools.partial(_lru_kernel, q=q, nch=nch),
        grid=(batch, nch),
        in_specs=[pl.BlockSpec((q, width), lambda b, c: (b * nch + c, 5)),
                  pl.BlockSpec((q, width), lambda b, c: (b * nch + c, 4)),
                  pl.BlockSpec((1, SUBLANES, width), lambda b, c: (b, 0, 0)),
                  pl.BlockSpec((1, 1, width), lambda b, c: (b, 0, 0)),
                  pl.BlockSpec((C_CONV, width), lambda b, c: (0, 0)),
                  pl.BlockSpec((1, width), lambda b, c: (0, 0)),
                  pl.BlockSpec((C_BLOCKS, bd, bd), lambda b, c: (0, 0, 0)),
                  pl.BlockSpec((C_BLOCKS, bd, bd), lambda b, c: (0, 0, 0)),
                  pl.BlockSpec((1, width), lambda b, c: (0, 0)),
                  pl.BlockSpec((1, width), lambda b, c: (0, 0)),
                  pl.BlockSpec((1, width), lambda b, c: (0, 0))],
        out_specs=[pl.BlockSpec((q, width), lambda b, c: (b * nch + c, 0)),
                   pl.BlockSpec((1, SUBLANES, width), lambda b, c: (b, 0, 0)),
                   pl.BlockSpec((1, 1, width), lambda b, c: (b, 0, 0))],
        out_shape=[jax.ShapeDtypeStruct((batch * seq, width), o_dtype),
                   jax.ShapeDtypeStruct((batch, SUBLANES, width), F32),
                   jax.ShapeDtypeStruct((batch, 1, width), F32)],
        scratch_shapes=[pltpu.VMEM((SUBLANES + q, width), F32),
                        pltpu.VMEM((1, width), F32),
                        pltpu.VMEM((q, width), F32)],
        compiler_params=_cparams("parallel", "arbitrary"),
        name="lru",
    )(big, big, conv0p, h0.reshape(batch, 1, width), lw["conv_c_w"], vec(lw["conv_c_b"]),
      lw["w_rgate"].astype(BF16), lw["w_igate"].astype(BF16), vec(lw["b_rgate"]), vec(lw["b_igate"]),
      vec(lw["lru_lambda"]))
    return y, convo[:, SUBLANES - (C_CONV - 1):, :], ho.reshape(batch, width)


def _merge_kernel(x_ref, gt_ref, oa_ref, yb_ref, yc_ref, gates_ref, wa_ref, wb_ref, wc_ref, wo_ref,
                  out_ref):
    tm, d = x_ref.shape
    gt = _ld(gt_ref)
    sub = min(tm, MERGE_SUB)
    for r in range(tm // sub):
        rs = slice(r * sub, (r + 1) * sub)
        ya = _dot(oa_ref[rs, :].astype(BF16), wa_ref[...])
        yb = _dot(yb_ref[rs, :].astype(BF16), wb_ref[...])
        yc = _dot(yc_ref[rs, :].astype(BF16), wc_ref[...])
        sg = jax.nn.sigmoid(gates_ref[rs, :].astype(F32))
        mixed = sg[:, :d] * ya + sg[:, d:2 * d] * yb + sg[:, 2 * d:] * yc
        g = gt if gt.shape[0] == 1 else gt[rs, :]
        out_ref[rs, :] = x_ref[rs, :] + g * _dot(mixed.astype(BF16), wo_ref[...])


def _merge(x, mod, oa, yb, yc, big, lw):
    m, d = x.shape
    aw = A_WIDTH
    tm = _row_tile(m, mod.seq, mod.per_row, MERGE_TM)
    row = lambda wd, j=0: pl.BlockSpec((tm, wd), lambda i: (i, j))
    return pl.pallas_call(
        _merge_kernel,
        grid=(m // tm,),
        in_specs=[row(d), mod.spec(5, tm, 1), row(aw), row(yb.shape[1]), row(yc.shape[1]), row(3 * d, 0)]
        + [_resident(lw[k].shape) for k in ("w_a_proj", "w_b_proj", "w_c_proj", "w_out")],
        out_specs=row(d),
        out_shape=jax.ShapeDtypeStruct((m, d), F32),
        compiler_params=_cparams("parallel"),
        name="merge",
    )(x, mod.arr, oa, yb, yc, big,
      lw["w_a_proj"], lw["w_b_proj"], lw["w_c_proj"], lw["w_out"])


def _t5_bucket(dist):
    dist = np.asarray(dist)
    large = REL_MAX_EXACT + (np.log(np.maximum(dist, 1) / REL_MAX_EXACT)
                             / math.log(REL_MAX_DISTANCE / REL_MAX_EXACT)
                             * (REL_BUCKETS - REL_MAX_EXACT)).astype(np.int64)
    large = np.minimum(large, REL_BUCKETS - 1)
    return np.where(dist < REL_MAX_EXACT, dist, large).astype(np.int32)


def _group_bias(rel_bias, g):
    dil = A_GROUPS[g][1]
    buckets = _t5_bucket(np.arange(A_KEYS + 1) * dil)
    return rel_bias[buckets][:, g * A_HEADS:(g + 1) * A_HEADS].T.astype(F32)


def _prompt_bias_table(bias, g):
    _, step, wblk = _attn_plan(g)
    h = bias.shape[0]
    cols = (wblk + 1) * A_BLOCK
    n = cols + A_BLOCK - 1
    x = np.arange(n + 1)
    x = np.where(x < cols, x, x - (n + 1))
    dist = wblk * A_BLOCK - x
    valid = (dist >= 0) & (dist % step == 0) & (dist // step <= A_KEYS)
    v = jnp.where(valid[None], bias[:, np.clip(dist // step, 0, A_KEYS)], -jnp.inf)
    tab = jnp.tile(v, (1, A_BLOCK))[:, :A_BLOCK * n].reshape(h, A_BLOCK, n)[:, :, :cols]
    return tab.reshape(h // 2, 2 * A_BLOCK, cols)


def _sample_bias_tables(bias, wb, dil, t):
    h = bias.shape[0]
    front = t - 1 + wb - A_KEYS * dil
    assert front >= 0
    sparse = jnp.concatenate([bias[:, ::-1, None], jnp.full((h, A_KEYS + 1, dil - 1), -jnp.inf, F32)], axis=2)
    base = jnp.concatenate([jnp.full((h, front), -jnp.inf, F32), sparse.reshape(h, (A_KEYS + 1) * dil)], axis=1)
    tab_buf = jnp.stack([base[:, t - 1 - tq:t - 1 - tq + wb] for tq in range(t)], axis=1)
    tq = np.arange(t)[:, None]
    dist = tq - np.arange(LANES)[None, :]
    valid = (dist >= 0) & (dist % dil == 0) & (np.arange(LANES)[None, :] < t)
    tab_new = jnp.where(valid[None], bias[:, np.clip(dist // dil, 0, A_KEYS)], -jnp.inf)
    return tab_buf.reshape(h // 2, 2 * t, wb), tab_new.reshape(h // 2, 2 * t, LANES)


def _mixer_weights(w_in):
    na = len(A_GROUPS) * A_WIDTH
    d = w_in.shape[0]
    offs = np.cumsum([0, na, na, na, 1024, 1536, 16, 1024, 1024, 3 * d])
    seg = lambda i: w_in[:, offs[i]:offs[i + 1]]
    qa, ka, va, zb, xbc, dtb, xc, gc, gates = [seg(i) for i in range(9)]
    cols = []
    for g in range(len(A_GROUPS)):
        sl = slice(g * A_WIDTH, (g + 1) * A_WIDTH)
        cols += [qa[:, sl] * (A_HEAD_DIM ** -0.5), ka[:, sl], va[:, sl]]
    w_qkv = jnp.concatenate(cols + [dtb, jnp.zeros((d, DT_PAD - dtb.shape[1]), w_in.dtype)], axis=1)
    w_big = jnp.concatenate([gates, zb, gc, xc, xbc], axis=1)
    return w_qkv.astype(BF16), w_big.astype(BF16)


def _mixer(x, mod, lw, st, bias, prompt, batch, seq):
    act_dtype = BF16 if prompt else F32
    qkv = _proj(x, mod, lw["g_mix"], lw["w_qkv"], F32, QKV_TN, slabs=True)
    big = _proj(x, mod, lw["g_mix"], lw["w_big"], act_dtype, BIG_TN)
    ng = len(A_GROUPS)
    npair = A_WIDTH // LANES
    if prompt:
        oa, new_kv = _attn_prompt(qkv, bias, st["kvt"], st["layer"], st["depth"], batch, seq, act_dtype)
    else:
        oa = _attn_sample(qkv, st["kv"], st["layer"], bias, batch, seq)
        new_kv = [jnp.transpose(qkv[(3 * g + 1) * npair:(3 * g + 3) * npair], (1, 0, 2))
                  .reshape(batch, seq, 2, A_HEADS, A_HEAD_DIM) for g in range(ng)]
    dt_col = 3 * ng * npair
    yb, conv_b_new, ssm_new = _ssd(big, qkv, dt_col, st["conv_b"], st["ssm"], lw, batch, seq, act_dtype)
    yc, conv_c_new, lru_new = _lru(big, st["conv_c"], st["lru"], lw, batch, seq, act_dtype)
    x = _merge(x, mod, oa, yb, yc, big, lw)
    return x, (new_kv[0], new_kv[1], new_kv[2], conv_b_new, ssm_new, conv_c_new, lru_new)


def _block(x, mod, lw, st, bias, prompt, batch, seq, g_final):
    x = _ffn(x, mod, 0, lw["g_ff1"], lw["w_ff1_in"], lw["w_ff1_out"])
    x, new_st = _mixer(x, mod, lw, st, bias, prompt, batch, seq)
    x = _ffn(x, mod, 6, lw["g_ff2"], lw["w_ff2_in"], lw["w_ff2_out"], g_final)
    return x, new_st


@jax.jit
def _forward(x_prompt, x_sample, c_prompt, c_sample, cache_win1_kv, cache_win2_kv, cache_win3_kv,
             state_conv_b, state_ssm, state_conv_c, state_lru, rel_bias, w_ada, b_ada, g_ff1,
             w_ff1_in, w_ff1_out, g_mix, w_in, w_a_proj, conv_b_w, conv_b_b, dt_bias, a_log, d_skip,
             g_ssm_norm, w_b_proj, conv_c_w, conv_c_b, w_rgate, b_rgate, w_igate, b_igate, lru_lambda,
             w_c_proj, w_out, g_ff2, w_ff2_in, w_ff2_out, g_final):
    bp, lp, d = x_prompt.shape
    bs, ls, _ = x_sample.shape
    depth = w_ada.shape[0]
    caches = (cache_win1_kv, cache_win2_kv, cache_win3_kv)
    biases = [_group_bias(rel_bias, g) for g in range(len(A_GROUPS))]
    bias_p = [_prompt_bias_table(b, g) for g, b in enumerate(biases)]
    bias_s = [_sample_bias_tables(b, caches[g].shape[2], A_GROUPS[g][1], ls) for g, b in enumerate(biases)]

    yp = x_prompt.reshape(bp * lp, d)
    ys = x_sample.reshape(bs * ls, d)
    c_all = jnp.concatenate([c_prompt, c_sample], axis=0)
    new_p = [[] for _ in range(7)]
    new_s = [[] for _ in range(7)]
    kvt_p = None
    for l in range(depth):
        w_qkv, w_big = _mixer_weights(w_in[l])
        lw = dict(g_ff1=g_ff1[l], w_ff1_in=w_ff1_in[l].astype(BF16), w_ff1_out=w_ff1_out[l].astype(BF16),
                  g_mix=g_mix[l], w_qkv=w_qkv, w_big=w_big, w_a_proj=w_a_proj[l].astype(BF16),
                  conv_b_w=conv_b_w[l], conv_b_b=conv_b_b[l], dt_bias=dt_bias[l], a_log=a_log[l],
                  d_skip=d_skip[l], g_ssm_norm=g_ssm_norm[l], w_b_proj=w_b_proj[l].astype(BF16),
                  conv_c_w=conv_c_w[l], conv_c_b=conv_c_b[l], w_rgate=w_rgate[l], b_rgate=b_rgate[l],
                  w_igate=w_igate[l], b_igate=b_igate[l], lru_lambda=lru_lambda[l],
                  w_c_proj=w_c_proj[l].astype(BF16), w_out=w_out[l].astype(BF16),
                  g_ff2=g_ff2[l], w_ff2_in=w_ff2_in[l].astype(BF16), w_ff2_out=w_ff2_out[l].astype(BF16))
        mod_all = _ada(c_all, w_ada, b_ada, l)
        mod_p = _Mod(mod_all[:bp], bp, lp, per_row=False)
        mod_s = _Mod(mod_all[bp:], bs, ls, per_row=True)
        gf = g_final if l == depth - 1 else None
        st_p = dict(kvt=kvt_p, layer=l, depth=depth,
                    conv_b=jnp.zeros((bp, B_CONV - 1, conv_b_w.shape[2]), F32),
                    ssm=jnp.zeros((bp,) + state_ssm.shape[2:], F32),
                    conv_c=jnp.zeros((bp, C_CONV - 1, conv_c_w.shape[2]), F32),
                    lru=jnp.zeros((bp, state_lru.shape[2]), F32))
        st_s = dict(kv=caches, layer=l, conv_b=state_conv_b[l], ssm=state_ssm[l],
                    conv_c=state_conv_c[l], lru=state_lru[l])
        yp, stp = _block(yp, mod_p, lw, st_p, bias_p, True, bp, lp, gf)
        ys, sts = _block(ys, mod_s, lw, st_s, bias_s, False, bs, ls, gf)
        kvt_p = list(stp[:3])
        for i in range(7):
            new_p[i].append(stp[i])
            new_s[i].append(sts[i])
    outs_p = [jnp.transpose(v, (0, 1, 5, 2, 3, 4)) for v in kvt_p] + [jnp.stack(v, 0) for v in new_p[3:]]
    outs_s = [jnp.stack(v, 0) for v in new_s]
    return (yp.reshape(bp, lp, d), ys.reshape(bs, ls, d), *outs_p, *outs_s)


def kernel(x_prompt, x_sample, c_prompt, c_sample, cache_win1_kv, cache_win2_kv, cache_win3_kv,
           state_conv_b, state_ssm, state_conv_c, state_lru, rel_bias, w_ada, b_ada, g_ff1,
           w_ff1_in, w_ff1_out, g_mix, w_in, w_a_proj, conv_b_w, conv_b_b, dt_bias, a_log, d_skip,
           g_ssm_norm, w_b_proj, conv_c_w, conv_c_b, w_rgate, b_rgate, w_igate, b_igate, lru_lambda,
           w_c_proj, w_out, g_ff2, w_ff2_in, w_ff2_out, g_final):
    return _forward(x_prompt, x_sample, c_prompt, c_sample, cache_win1_kv, cache_win2_kv,
                    cache_win3_kv, state_conv_b, state_ssm, state_conv_c, state_lru, rel_bias, w_ada,
                    b_ada, g_ff1, w_ff1_in, w_ff1_out, g_mix, w_in, w_a_proj, conv_b_w, conv_b_b,
                    dt_bias, a_log, d_skip, g_ssm_norm, w_b_proj, conv_c_w, conv_c_b, w_rgate,
                    b_rgate, w_igate, b_igate, lru_lambda, w_c_proj, w_out, g_ff2, w_ff2_in,
                    w_ff2_out, g_final)
```

```python
import functools
import math

import jax
import jax.numpy as jnp
import numpy as np
from jax import lax
from jax.experimental import pallas as pl
from jax.experimental.pallas import tpu as pltpu

F32 = jnp.float32
BF16 = jnp.bfloat16

NORM_EPS = 1e-6
N_MOD = 9
A_GROUPS = ((128, 1), (512, 4), (2048, 16))
A_HEADS = 8
A_HEAD_DIM = 64
A_WIDTH = A_HEADS * A_HEAD_DIM
A_KEYS = 128
A_BLOCK = 128
REL_BUCKETS = 32
REL_MAX_EXACT = 16
REL_MAX_DISTANCE = 2048
B_HEAD_DIM = 64
B_GROUPS = 2
B_STATE = 128
B_CONV = 4
C_BLOCKS = 8
C_CONV = 4
C_POW = 8.0

LANES = 128
SUBLANES = 8
SSD_CHUNK = 128
FFN_TM = 1024
PROJ_TM = 2048
PROJ_SUB = 512
MERGE_TM = 1024
MERGE_SUB = 256
FUSE_TM = 512
LRU_CHUNK = 128
QKV_TN = 1024
BIG_TN = 1536
DT_PAD = 512
ATTN_UNROLL = 8
ATTN_UNROLL_REST = 12
ATTN_MAX_STRIDE = 4
VMEM_LIMIT = 56 * 1024 * 1024


def _cparams(*sem):
    return pltpu.CompilerParams(dimension_semantics=sem, vmem_limit_bytes=VMEM_LIMIT)


def _resident(shape):
    nd = len(shape)
    return pl.BlockSpec(shape, lambda *_: (0,) * nd, pipeline_mode=pl.Buffered(1))


def _ld(ref):
    return ref[0] if len(ref.shape) == 3 else ref[...]


def _norm_mod(x, g, sc, sh):
    ms = jnp.mean(x * x, axis=-1, keepdims=True)
    return (x * lax.rsqrt(ms + NORM_EPS) * g) * (1.0 + sc) + sh


def _softplus(x):
    return jnp.maximum(x, 0.0) + jnp.log1p(jnp.exp(-jnp.abs(x)))


def _causal_conv(xpad, w_ref, b_ref, q, start=0):
    taps = w_ref.shape[0]
    assert taps - 1 <= SUBLANES and start % SUBLANES == 0
    xe = xpad[start:start + SUBLANES + q, :]
    w = w_ref[...]
    acc = xe * w[0:1, :]
    for k in range(1, taps):
        acc = xe * w[k:k + 1, :] + pltpu.roll(acc, 1, axis=0)
    return acc[SUBLANES:SUBLANES + q] + b_ref[...]


def _split_bf16(v, n):
    parts = []
    r = v
    for _ in range(n):
        p = r.astype(BF16)
        parts.append(p)
        r = r - p.astype(F32)
    return parts


def _dot(a, b):
    return jnp.dot(a, b, preferred_element_type=F32)


def _dot_nt(a, b):
    return lax.dot_general(a, b, (((1,), (1,)), ((), ())), preferred_element_type=F32)


def _ada_kernel(c_ref, w_ref, b_ref, o_ref):
    c = c_ref[...]
    a = (c * jax.nn.sigmoid(c)).astype(BF16)
    o_ref[...] = _dot(a, w_ref[...].astype(BF16)) + b_ref[...]


def _ada(c, w, b, layer):
    rows, d = c.shape
    depth, _, n = w.shape
    tn = 1024
    return pl.pallas_call(
        _ada_kernel,
        grid=(n // tn,),
        in_specs=[pl.BlockSpec((rows, d), lambda j: (0, 0)),
                  pl.BlockSpec((None, d, tn), lambda j: (layer, 0, j)),
                  pl.BlockSpec((None, 1, tn), lambda j: (layer, 0, j))],
        out_specs=pl.BlockSpec((rows, tn), lambda j: (0, j)),
        out_shape=jax.ShapeDtypeStruct((rows, n), F32),
        compiler_params=_cparams("parallel"),
        name="ada",
    )(c, w, b.reshape(depth, 1, n))


class _Mod:
    def __init__(self, mod, batch, seq, per_row):
        d = mod.shape[1] // N_MOD
        self.d = d
        self.seq = seq
        self.per_row = per_row
        if per_row:
            self.arr = jnp.repeat(mod, seq, axis=0)
        else:
            self.arr = mod.reshape(batch * N_MOD, 1, d)

    def spec(self, k, tm, grid_rank):
        d = self.d
        if self.per_row:
            if grid_rank == 1:
                return pl.BlockSpec((tm, d), lambda i: (i, k))
            return pl.BlockSpec((tm, d), lambda i, j: (i, k))
        per = self.seq // tm
        if grid_rank == 1:
            return pl.BlockSpec((1, 1, d), lambda i: ((i // per) * N_MOD + k, 0, 0))
        return pl.BlockSpec((1, 1, d), lambda i, j: ((i // per) * N_MOD + k, 0, 0))


def _row_tile(m, seq, per_row, want):
    tm = min(want, m if per_row else seq)
    assert m % tm == 0 and (per_row or seq % tm == 0)
    return tm


def _ffn_kernel(x_ref, sh_ref, sc_ref, gt_ref, gn_ref, win_ref, wout_ref, *rest, d_ff, fc, final):
    if final:
        gf_ref, o_ref, acc_ref = rest
    else:
        o_ref, acc_ref = rest
    x = x_ref[...]
    h = _norm_mod(x, gn_ref[...], _ld(sc_ref), _ld(sh_ref)).astype(BF16)
    for c in range(d_ff // fc):
        u = _dot(h, win_ref[:, c * fc:(c + 1) * fc])
        v = _dot(h, win_ref[:, d_ff + c * fc:d_ff + (c + 1) * fc])
        a = (u * jax.nn.sigmoid(u) * v).astype(BF16)
        part = _dot(a, wout_ref[c * fc:(c + 1) * fc, :])
        if c == 0:
            acc_ref[...] = part
        else:
            acc_ref[...] += part
    y = x + 0.5 * _ld(gt_ref) * acc_ref[...]
    if final:
        ms = jnp.mean(y * y, axis=-1, keepdims=True)
        y = y * lax.rsqrt(ms + NORM_EPS) * gf_ref[...]
    o_ref[...] = y


def _ffn(x, mod, k0, gn, w_in, w_out, g_final=None):
    m, d = x.shape
    d_ff = w_out.shape[0]
    tm = _row_tile(m, mod.seq, mod.per_row, FFN_TM)
    final = g_final is not None
    in_specs = [pl.BlockSpec((tm, d), lambda i: (i, 0)),
                mod.spec(k0, tm, 1), mod.spec(k0 + 1, tm, 1), mod.spec(k0 + 2, tm, 1),
                _resident((1, d)), _resident(w_in.shape), _resident(w_out.shape)]
    args = [x, mod.arr, mod.arr, mod.arr, gn.reshape(1, d), w_in, w_out]
    if final:
        in_specs.append(_resident((1, d)))
        args.append(g_final.reshape(1, d))
    return pl.pallas_call(
        functools.partial(_ffn_kernel, d_ff=d_ff, fc=256, final=final),
        grid=(m // tm,),
        in_specs=in_specs,
        out_specs=pl.BlockSpec((tm, d), lambda i: (i, 0)),
        out_shape=jax.ShapeDtypeStruct((m, d), F32),
        scratch_shapes=[pltpu.VMEM((tm, d), F32)],
        compiler_params=_cparams("parallel"),
        name="ffn",
    )(*args)


def _proj_kernel(x_ref, sh_ref, sc_ref, gn_ref, w_ref, o_ref, h_ref):
    @pl.when(pl.program_id(1) == 0)
    def _():
        h_ref[...] = _norm_mod(x_ref[...], gn_ref[...], _ld(sc_ref), _ld(sh_ref)).astype(BF16)

    tm = h_ref.shape[0]
    sub = min(tm, PROJ_SUB)
    for r in range(tm // sub):
        rs = slice(r * sub, (r + 1) * sub)
        res = _dot(h_ref[rs, :], w_ref[...]).astype(o_ref.dtype)
        if len(o_ref.shape) == 2:
            o_ref[rs, :] = res
        else:
            for s in range(o_ref.shape[0]):
                o_ref[s, rs, :] = res[:, s * LANES:(s + 1) * LANES]


def _proj(x, mod, gn, w, out_dtype, tn, slabs=False):
    m, d = x.shape
    n = w.shape[1]
    tm = _row_tile(m, mod.seq, mod.per_row, PROJ_TM)
    if slabs:
        out_spec = pl.BlockSpec((tn // LANES, tm, LANES), lambda i, j: (j, i, 0))
        out_shape = jax.ShapeDtypeStruct((n // LANES, m, LANES), out_dtype)
    else:
        out_spec = pl.BlockSpec((tm, tn), lambda i, j: (i, j))
        out_shape = jax.ShapeDtypeStruct((m, n), out_dtype)
    return pl.pallas_call(
        _proj_kernel,
        grid=(m // tm, n // tn),
        in_specs=[pl.BlockSpec((tm, d), lambda i, j: (i, 0)),
                  mod.spec(3, tm, 2), mod.spec(4, tm, 2),
                  pl.BlockSpec((1, d), lambda i, j: (0, 0)),
                  pl.BlockSpec((d, tn), lambda i, j: (0, j))],
        out_specs=out_spec,
        out_shape=out_shape,
        scratch_shapes=[pltpu.VMEM((tm, d), BF16)],
        compiler_params=_cparams("parallel", "arbitrary"),
        name="proj",
    )(x, mod.arr, mod.arr, gn.reshape(1, d), w)


def _pair_softmax(q2, k2, v2, bias2, lo, transposed=False):
    rows = q2.shape[0]
    zero = jnp.zeros_like(q2)
    qs = jnp.concatenate([jnp.where(lo, q2, zero), jnp.where(lo, zero, q2)], axis=0).astype(BF16)
    s = (_dot(qs, k2) if transposed else _dot_nt(qs, k2)) + bias2
    mx = jnp.max(s, axis=-1, keepdims=True)
    p = jnp.exp(s - mx)
    l = jnp.sum(p, axis=-1, keepdims=True)
    pb = p.astype(BF16)
    o = _dot_nt(pb, v2) if transposed else _dot(pb, v2)
    return (jnp.where(lo, o[:rows], o[rows:]), jnp.where(lo, mx[:rows], mx[rows:]),
            jnp.where(lo, l[:rows], l[rows:]))


def _merge_softmax(acc, new):
    ao, am, al = acc
    o, m, l = new
    mn = jnp.maximum(am, m)
    a1 = jnp.exp(am - mn)
    a2 = jnp.exp(m - mn)
    return ao * a1 + o * a2, mn, al * a1 + l * a2


def _attn_plan(g):
    dil = A_GROUPS[g][1]
    if dil <= ATTN_MAX_STRIDE:
        return dil, 1, A_KEYS // A_BLOCK
    assert dil % ATTN_MAX_STRIDE == 0
    step = dil // ATTN_MAX_STRIDE
    return ATTN_MAX_STRIDE, step, step * A_KEYS // A_BLOCK


def _attn_prompt_kernel(*refs, seq, n_alias):
    ng = len(A_GROUPS)
    qkv = refs[:3 * ng]
    bias_refs = refs[3 * ng:4 * ng]
    o_ref = refs[4 * ng + n_alias]
    kvt_refs = refs[4 * ng + 1 + n_alias:5 * ng + 1 + n_alias]
    acc_o, acc_m, acc_l = refs[-3:]
    blk = A_BLOCK

    for g in range(ng):
        keep = kvt_refs[g].shape[-1]
        for kv in range(2):
            src = qkv[3 * g + 1 + kv]
            for c in range(keep // blk):
                t = src[pl.ds(seq - keep + c * blk, blk), :].T
                for i in range(2):
                    kvt_refs[g][kv, i, :, c * blk:(c + 1) * blk] = t[i * A_HEAD_DIM:(i + 1) * A_HEAD_DIM]

    hp = pl.program_id(1)
    lo = lax.broadcasted_iota(jnp.int32, (blk, LANES), 1) < A_HEAD_DIM

    def tiles(g, starts, kw):
        stride = _attn_plan(g)[0]
        q_ref, k_ref, v_ref = qkv[3 * g:3 * g + 3]
        bias_ref = bias_refs[g]
        wcols = bias_ref.shape[-1]

        def rows(start, n):
            return pl.ds(start, n) if stride == 1 else pl.ds(start, n, stride=stride)

        news = []
        for qstart, kstart in starts:
            q2 = q_ref[rows(qstart, blk), :]
            k2 = k_ref[rows(kstart, kw), :].astype(BF16)
            v2 = v_ref[rows(kstart, kw), :].astype(BF16)
            news.append(_pair_softmax(q2, k2, v2, bias_ref[hp, :, wcols - kw:wcols], lo))
        for (qstart, _), new in zip(starts, news):
            sel = rows(qstart, blk)
            if g > 0:
                new = _merge_softmax((acc_o[sel, :], acc_m[sel, :], acc_l[sel, :]), new)
            acc_o[sel, :] = new[0]
            acc_m[sel, :] = new[1]
            acc_l[sel, :] = new[2]

    for g in range(ng):
        stride, _, wblk = _attn_plan(g)
        assert stride <= ATTN_UNROLL
        nb = seq // stride // blk
        span = blk * stride
        for b in range(min(wblk, nb)):
            tiles(g, [(r + b * span, r) for r in range(stride)], (b + 1) * blk)
        nfull = nb - wblk
        if nfull > 0:
            ub = max(u for u in range(1, nfull + 1) if nfull % u == 0 and u * stride <= ATTN_UNROLL_REST)

            def full(i, carry, g=g, span=span, stride=stride, ub=ub, wblk=wblk):
                b = wblk + i * ub
                tiles(g, [(r + (b + k) * span, r + (b + k - wblk) * span)
                          for k in range(ub) for r in range(stride)], (wblk + 1) * blk)
                return carry

            lax.fori_loop(0, nfull // ub, full, 0)
    o_ref[...] = (acc_o[...] / acc_l[...]).astype(o_ref.dtype)


def _attn_prompt(qkv, bias_tab, kvt_prev, layer, depth, batch, seq, o_dtype):
    npair = A_WIDTH // LANES
    ng = len(A_GROUPS)
    col = lambda j: pl.BlockSpec((None, seq, LANES), lambda b, h: (j * npair + h, b, 0))
    keeps = [min(win, seq) for win, _ in A_GROUPS]
    n_alias = 0 if kvt_prev is None else ng
    alias_specs = [pl.BlockSpec(memory_space=pl.ANY)] * n_alias
    alias_args = [] if kvt_prev is None else list(kvt_prev)
    n_in = 4 * ng
    outs = pl.pallas_call(
        functools.partial(_attn_prompt_kernel, seq=seq, n_alias=n_alias),
        grid=(batch, npair),
        in_specs=[col(j) for j in range(3 * ng)]
        + [pl.BlockSpec(tab.shape, lambda b, h: (0, 0, 0)) for tab in bias_tab] + alias_specs,
        out_specs=[pl.BlockSpec((seq, LANES), lambda b, h: (b, h))]
        + [pl.BlockSpec((None, None, 2, 2, A_HEAD_DIM, keep), lambda b, h: (layer, b, 0, h, 0, 0))
           for keep in keeps],
        out_shape=[jax.ShapeDtypeStruct((batch * seq, A_WIDTH), o_dtype)]
        + [jax.ShapeDtypeStruct((depth, batch, 2, A_HEADS, A_HEAD_DIM, keep), F32) for keep in keeps],
        input_output_aliases={n_in + g: 1 + g for g in range(n_alias)},
        scratch_shapes=[pltpu.VMEM((seq, LANES), F32)] * 3,
        compiler_params=_cparams("parallel", "parallel"),
        name="attn_prompt",
    )(*([qkv] * (3 * ng)), *bias_tab, *alias_args)
    return outs[0], list(outs[1:])


def _attn_sample_kernel(qkv_ref, buf1_ref, buf2_ref, buf3_ref, bb1, bn1, bb2, bn2, bb3, bn3,
                        o_ref, knew_scr, vnew_scr):
    t = qkv_ref.shape[1]
    npair = A_HEADS // 2
    bufs = (buf1_ref, buf2_ref, buf3_ref)
    bias = ((bb1, bn1), (bb2, bn2), (bb3, bn3))
    lo = lax.broadcasted_iota(jnp.int32, (t, LANES), 1) < A_HEAD_DIM
    knew_scr[...] = jnp.zeros_like(knew_scr)
    vnew_scr[...] = jnp.zeros_like(vnew_scr)
    for hp in range(npair):
        cols = slice(hp * LANES, (hp + 1) * LANES)
        acc = None
        for g in range(len(A_GROUPS)):
            q2 = qkv_ref[3 * g * npair + hp]
            wb = bufs[g].shape[-1]
            kb = bufs[g][0, 2 * hp:2 * hp + 2].reshape(LANES, wb).astype(BF16)
            vb = bufs[g][1, 2 * hp:2 * hp + 2].reshape(LANES, wb).astype(BF16)
            knew_scr[g, hp, 0:t, :] = qkv_ref[(3 * g + 1) * npair + hp]
            vnew_scr[g, hp, 0:t, :] = qkv_ref[(3 * g + 2) * npair + hp]
            kn = knew_scr[g, hp].astype(BF16)
            vn = vnew_scr[g, hp].astype(BF16)
            bbuf, bnew = bias[g]
            new = _pair_softmax(q2, kb, vb, bbuf[hp], lo, transposed=True)
            new = _merge_softmax(new, _pair_softmax(q2, kn, vn, bnew[hp], lo))
            acc = new if acc is None else _merge_softmax(acc, new)
        o_ref[:, cols] = acc[0] / acc[2]


def _attn_sample(qkv, caches, layer, bias_s, batch, t):
    aw = A_WIDTH
    ng = len(A_GROUPS)
    bufs = [jnp.transpose(c, (0, 1, 3, 4, 5, 2)) for c in caches]
    buf_specs = [pl.BlockSpec((None, None) + bv.shape[2:], lambda b: (layer, b, 0, 0, 0, 0)) for bv in bufs]
    tabs = [tab for pair in bias_s for tab in pair]
    return pl.pallas_call(
        _attn_sample_kernel,
        grid=(batch,),
        in_specs=[pl.BlockSpec((3 * ng * aw // LANES, t, LANES), lambda b: (0, b, 0))] + buf_specs
        + [pl.BlockSpec(tab.shape, lambda b: (0, 0, 0)) for tab in tabs],
        out_specs=pl.BlockSpec((t, aw), lambda b: (b, 0)),
        out_shape=jax.ShapeDtypeStruct((batch * t, aw), F32),
        scratch_shapes=[pltpu.VMEM((ng, aw // LANES, LANES, LANES), F32)] * 2,
        compiler_params=_cparams("parallel"),
        name="attn_sample",
    )(qkv, *bufs, *tabs)


def _ssd_kernel(xbc_ref, z_ref, dt_ref, conv0_ref, st0_ref, cw_ref, cb_ref, dtb_ref, aneg_ref,
                dsk_ref, gn_ref, e_ref, y_ref, convo_ref, sto_ref, xpad, st_t, y_scr, *, lv, nch):
    q = SSD_CHUNK
    c = pl.program_id(1)
    width = y_ref.shape[1]
    nst = B_STATE
    hpg = width // B_HEAD_DIM // B_GROUPS // 2

    @pl.when(c == 0)
    def _():
        xpad[...] = jnp.zeros_like(xpad)
        xpad[0:SUBLANES, :] = conv0_ref[0]
        for j in range(width // LANES):
            st_t[:, j * LANES:(j + 1) * LANES] = st0_ref[0, j * LANES:(j + 1) * LANES, :].T

    @pl.when(c > 0)
    def _():
        xpad[0:SUBLANES, :] = xpad[q:q + SUBLANES, :]

    xpad[SUBLANES:SUBLANES + lv, :] = xbc_ref[...].astype(F32)

    conv = _causal_conv(xpad, cw_ref, cb_ref, q)
    act = conv * jax.nn.sigmoid(conv)
    xs = act[:, :width]
    bm = [act[:, width + gi * nst:width + (gi + 1) * nst] for gi in range(B_GROUPS)]
    cm = [act[:, width + (B_GROUPS + gi) * nst:width + (B_GROUPS + gi + 1) * nst] for gi in range(B_GROUPS)]

    dt_raw = dt_ref[...]
    if lv < q:
        dt_raw = jnp.concatenate([dt_raw, jnp.zeros((q - lv, LANES), F32)], axis=0)
    dt = _softplus(dt_raw + dtb_ref[...])
    if lv < q:
        dt = jnp.where(lax.broadcasted_iota(jnp.int32, (q, LANES), 0) < lv, dt, 0.0)
    dta = dt * aneg_ref[...]

    row = lax.broadcasted_iota(jnp.int32, (q, q), 0)
    col = lax.broadcasted_iota(jnp.int32, (q, q), 1)
    causal = row >= col
    tri = jnp.where(causal, 1.0, 0.0).astype(BF16)
    acum = sum(_dot(tri, p) for p in _split_bf16(dta, 3))
    acum_t = acum.T
    e = e_ref[...]
    dt_x = sum(_dot(p, e) for p in _split_bf16(dt, 2))
    acum_x = sum(_dot(p, e) for p in _split_bf16(acum, 2))
    last_x = acum_x[q - 1:q, :]
    ea_x = jnp.exp(acum_x)
    xdt = xs * dt_x
    xdt_te = (xdt * jnp.exp(last_x - acum_x)).astype(BF16)
    xdt_b = xdt.astype(BF16)
    chunk_decay = jnp.exp(last_x)
    lo = lax.broadcasted_iota(jnp.int32, (q, LANES), 1) < B_HEAD_DIM

    for gi in range(B_GROUPS):
        cmb = cm[gi].astype(BF16)
        cb = _dot_nt(cmb, bm[gi].astype(BF16))
        bm_t = bm[gi].T.astype(BF16)
        for hp in range(gi * hpg, (gi + 1) * hpg):
            cols = slice(hp * LANES, (hp + 1) * LANES)
            ax = acum_x[:, cols]
            ax_r = pltpu.roll(ax, B_HEAD_DIM, axis=1)
            ys = []
            for col_v, h in ((jnp.where(lo, ax, ax_r), 2 * hp), (jnp.where(lo, ax_r, ax), 2 * hp + 1)):
                seg = col_v - acum_t[h:h + 1, :]
                dec = jnp.exp(jnp.where(causal, seg, -jnp.inf))
                ys.append(_dot((cb * dec).astype(BF16), xdt_b[:, cols]))
            st_old = st_t[:, cols]
            y_off = _dot(cmb, st_old.astype(BF16)) * ea_x[:, cols]
            st_t[:, cols] = st_old * chunk_decay[:, cols] + _dot(bm_t, xdt_te[:, cols])
            y_scr[:, cols] = jnp.where(lo, ys[0], ys[1]) + y_off + dsk_ref[:, cols] * xs[:, cols]

    z = z_ref[...].astype(F32)
    if lv < q:
        z = jnp.concatenate([z, jnp.zeros((q - lv, width), F32)], axis=0)
    y = y_scr[...] * (z * jax.nn.sigmoid(z))
    gw = width // B_GROUPS
    for gi in range(B_GROUPS):
        yg = y[:, gi * gw:(gi + 1) * gw]
        ms = jnp.mean(yg * yg, axis=-1, keepdims=True)
        yn = yg * lax.rsqrt(ms + NORM_EPS) * gn_ref[:, gi * gw:(gi + 1) * gw]
        y_ref[:, gi * gw:(gi + 1) * gw] = yn[0:lv].astype(y_ref.dtype)

    @pl.when(c == nch - 1)
    def _():
        convo_ref[0] = xpad[lv:lv + SUBLANES, :]
        for j in range(width // LANES):
            sto_ref[0, j * LANES:(j + 1) * LANES, :] = st_t[:, j * LANES:(j + 1) * LANES].T


def _ssd(big, qkv, dt_col, conv0, st0, lw, batch, seq, o_dtype):
    width = lw["w_b_proj"].shape[0]
    cch = lw["conv_b_w"].shape[1]
    nheads = width // B_HEAD_DIM
    lv = min(seq, SSD_CHUNK)
    nch = seq // lv
    assert lv == SSD_CHUNK or nch == 1
    pad = lambda v: jnp.pad(v.astype(F32), (0, LANES - nheads)).reshape(1, LANES)
    e = (np.arange(LANES)[:, None] == (np.arange(width)[None, :] // B_HEAD_DIM)).astype(np.float32)
    conv0p = jnp.pad(conv0, ((0, 0), (SUBLANES - (B_CONV - 1), 0), (0, 0)))
    y, convo, sto = pl.pallas_call(
        functools.partial(_ssd_kernel, lv=lv, nch=nch),
        grid=(batch, nch),
        in_specs=[pl.BlockSpec((lv, cch), lambda b, c: (b * nch + c, 4)),
                  pl.BlockSpec((lv, width), lambda b, c: (b * nch + c, 3)),
                  pl.BlockSpec((None, lv, LANES), lambda b, c: (dt_col, b * nch + c, 0)),
                  pl.BlockSpec((1, SUBLANES, cch), lambda b, c: (b, 0, 0)),
                  pl.BlockSpec((1, width, B_STATE), lambda b, c: (b, 0, 0)),
                  pl.BlockSpec((B_CONV, cch), lambda b, c: (0, 0)),
                  pl.BlockSpec((1, cch), lambda b, c: (0, 0)),
                  pl.BlockSpec((1, LANES), lambda b, c: (0, 0)),
                  pl.BlockSpec((1, LANES), lambda b, c: (0, 0)),
                  pl.BlockSpec((1, width), lambda b, c: (0, 0)),
                  pl.BlockSpec((1, width), lambda b, c: (0, 0)),
                  pl.BlockSpec((LANES, width), lambda b, c: (0, 0))],
        out_specs=[pl.BlockSpec((lv, width), lambda b, c: (b * nch + c, 0)),
                   pl.BlockSpec((1, SUBLANES, cch), lambda b, c: (b, 0, 0)),
                   pl.BlockSpec((1, width, B_STATE), lambda b, c: (b, 0, 0))],
        out_shape=[jax.ShapeDtypeStruct((batch * seq, width), o_dtype),
                   jax.ShapeDtypeStruct((batch, SUBLANES, cch), F32),
                   jax.ShapeDtypeStruct((batch, width, B_STATE), F32)],
        scratch_shapes=[pltpu.VMEM((SUBLANES + SSD_CHUNK, cch), F32),
                        pltpu.VMEM((B_STATE, width), F32),
                        pltpu.VMEM((SSD_CHUNK, width), F32)],
        compiler_params=_cparams("parallel", "arbitrary"),
        name="ssd",
    )(big, big, qkv, conv0p, st0.reshape(batch, width, B_STATE),
      lw["conv_b_w"], lw["conv_b_b"].reshape(1, cch), pad(lw["dt_bias"]),
      pad(-jnp.exp(lw["a_log"].astype(F32))),
      jnp.repeat(lw["d_skip"].astype(F32), B_HEAD_DIM).reshape(1, width),
      lw["g_ssm_norm"].reshape(1, width), jnp.asarray(e, BF16))
    return (y, convo[:, SUBLANES - (B_CONV - 1):, :],
            sto.reshape(batch, nheads, B_HEAD_DIM, B_STATE))


def _lru_chunk(x, h, wr_ref, wi_ref, br_ref, bi_ref, lam_ref):
    q, width = x.shape
    bd = width // C_BLOCKS
    rs, gs = [], []
    for j in range(C_BLOCKS):
        xb = x[:, j * bd:(j + 1) * bd].astype(BF16)
        rs.append(_dot(xb, wr_ref[j]))
        gs.append(_dot(xb, wi_ref[j]))
    rg = jax.nn.sigmoid(jnp.concatenate(rs, axis=1) + br_ref[...])
    ig = jax.nn.sigmoid(jnp.concatenate(gs, axis=1) + bi_ref[...])
    log_a = -C_POW * rg * _softplus(-lam_ref[...])
    a = jnp.exp(log_a)
    u = x * ig * jnp.sqrt(-jnp.tanh(log_a) * (a * a + 1.0))

    rowi = lax.broadcasted_iota(jnp.int32, (SUBLANES, width), 0)
    hs = []
    for g in range(q // SUBLANES):
        ag = a[g * SUBLANES:(g + 1) * SUBLANES]
        bg = u[g * SUBLANES:(g + 1) * SUBLANES]
        s = 1
        while s < SUBLANES:
            keep = rowi >= s
            a_sh = jnp.where(keep, pltpu.roll(ag, s, axis=0), 1.0)
            b_sh = jnp.where(keep, pltpu.roll(bg, s, axis=0), 0.0)
            bg = ag * b_sh + bg
            ag = ag * a_sh
            s *= 2
        hg = bg + ag * h
        hs.append(hg)
        h = hg[SUBLANES - 1:SUBLANES]
    return (hs[0] if len(hs) == 1 else jnp.concatenate(hs, axis=0)), h


def _lru_kernel(xc_ref, gc_ref, conv0_ref, h0_ref, cw_ref, cb_ref, wr_ref, wi_ref, br_ref, bi_ref,
                lam_ref, y_ref, convo_ref, ho_ref, xpad, hprev, *, q, nch):
    c = pl.program_id(1)
    width = y_ref.shape[1]

    @pl.when(c == 0)
    def _():
        xpad[0:SUBLANES, :] = conv0_ref[0]
        hprev[...] = h0_ref[0]

    @pl.when(c > 0)
    def _():
        xpad[0:SUBLANES, :] = xpad[q:q + SUBLANES, :]

    xpad[SUBLANES:SUBLANES + q, :] = xc_ref[...].astype(F32)
    x = _causal_conv(xpad, cw_ref, cb_ref, q)
    hs, h = _lru_chunk(x, hprev[...], wr_ref, wi_ref, br_ref, bi_ref, lam_ref)
    hprev[...] = h
    y_ref[...] = (hs * jax.nn.gelu(gc_ref[...].astype(F32))).astype(y_ref.dtype)

    @pl.when(c == nch - 1)
    def _():
        convo_ref[0] = xpad[q:q + SUBLANES, :]
        ho_ref[0] = h


def _lru(big, conv0, h0, lw, batch, seq, o_dtype):
    width = lw["w_c_proj"].shape[0]
    q = min(seq, 128)
    nch = seq // q
    bd = width // C_BLOCKS
    conv0p = jnp.pad(conv0, ((0, 0), (SUBLANES - (C_CONV - 1), 0), (0, 0)))
    vec = lambda v: v.astype(F32).reshape(1, width)
    y, convo, ho = pl.pallas_call(
        functools.partial(_lru_kernel, q=q, nch=nch),
        grid=(batch, nch),
        in_specs=[pl.BlockSpec((q, width), lambda b, c: (b * nch + c, 5)),
                  pl.BlockSpec((q, width), lambda b, c: (b * nch + c, 4)),
                  pl.BlockSpec((1, SUBLANES, width), lambda b, c: (b, 0, 0)),
                  pl.BlockSpec((1, 1, width), lambda b, c: (b, 0, 0)),
                  pl.BlockSpec((C_CONV, width), lambda b, c: (0, 0)),
                  pl.BlockSpec((1, width), lambda b, c: (0, 0)),
                  pl.BlockSpec((C_BLOCKS, bd, bd), lambda b, c: (0, 0, 0)),
                  pl.BlockSpec((C_BLOCKS, bd, bd), lambda b, c: (0, 0, 0)),
                  pl.BlockSpec((1, width), lambda b, c: (0, 0)),
                  pl.BlockSpec((1, width), lambda b, c: (0, 0)),
                  pl.BlockSpec((1, width), lambda b, c: (0, 0))],
        out_specs=[pl.BlockSpec((q, width), lambda b, c: (b * nch + c, 0)),
                   pl.BlockSpec((1, SUBLANES, width), lambda b, c: (b, 0, 0)),
                   pl.BlockSpec((1, 1, width), lambda b, c: (b, 0, 0))],
        out_shape=[jax.ShapeDtypeStruct((batch * seq, width), o_dtype),
                   jax.ShapeDtypeStruct((batch, SUBLANES, width), F32),
                   jax.ShapeDtypeStruct((batch, 1, width), F32)],
        scratch_shapes=[pltpu.VMEM((SUBLANES + q, width), F32),
                        pltpu.VMEM((1, width), F32)],
        compiler_params=_cparams("parallel", "arbitrary"),
        name="lru",
    )(big, big, conv0p, h0.reshape(batch, 1, width), lw["conv_c_w"], vec(lw["conv_c_b"]),
      lw["w_rgate"].astype(BF16), lw["w_igate"].astype(BF16), vec(lw["b_rgate"]), vec(lw["b_igate"]),
      vec(lw["lru_lambda"]))
    return y, convo[:, SUBLANES - (C_CONV - 1):, :], ho.reshape(batch, width)


def _merge_kernel(x_ref, gt_ref, oa_ref, yb_ref, yc_ref, gates_ref, wa_ref, wb_ref, wc_ref, wo_ref,
                  out_ref):
    tm, d = x_ref.shape
    gt = _ld(gt_ref)
    sub = min(tm, MERGE_SUB)
    for r in range(tm // sub):
        rs = slice(r * sub, (r + 1) * sub)
        ya = _dot(oa_ref[rs, :].astype(BF16), wa_ref[...])
        yb = _dot(yb_ref[rs, :].astype(BF16), wb_ref[...])
        yc = _dot(yc_ref[rs, :].astype(BF16), wc_ref[...])
        sg = jax.nn.sigmoid(gates_ref[rs, :].astype(F32))
        mixed = sg[:, :d] * ya + sg[:, d:2 * d] * yb + sg[:, 2 * d:] * yc
        g = gt if gt.shape[0] == 1 else gt[rs, :]
        out_ref[rs, :] = x_ref[rs, :] + g * _dot(mixed.astype(BF16), wo_ref[...])


def _merge(x, mod, oa, yb, yc, big, lw):
    m, d = x.shape
    aw = A_WIDTH
    tm = _row_tile(m, mod.seq, mod.per_row, MERGE_TM)
    row = lambda wd, j=0: pl.BlockSpec((tm, wd), lambda i: (i, j))
    return pl.pallas_call(
        _merge_kernel,
        grid=(m // tm,),
        in_specs=[row(d), mod.spec(5, tm, 1), row(aw), row(yb.shape[1]), row(yc.shape[1]), row(3 * d, 0)]
        + [_resident(lw[k].shape) for k in ("w_a_proj", "w_b_proj", "w_c_proj", "w_out")],
        out_specs=row(d),
        out_shape=jax.ShapeDtypeStruct((m, d), F32),
        compiler_params=_cparams("parallel"),
        name="merge",
    )(x, mod.arr, oa, yb, yc, big,
      lw["w_a_proj"], lw["w_b_proj"], lw["w_c_proj"], lw["w_out"])


def _merge_lru_kernel(x_ref, gt_ref, oa_ref, yb_ref, xc_ref, gc_ref, gates_ref, conv0_ref, h0_ref,
                      cw_ref, cb_ref, wr_ref, wi_ref, br_ref, bi_ref, lam_ref,
                      wa_ref, wb_ref, wc_ref, wo_ref, out_ref, convo_ref, ho_ref, xpad, hprev, *, nstep):
    c = pl.program_id(1)
    tm, d = x_ref.shape
    q = LRU_CHUNK

    @pl.when(c == 0)
    def _():
        xpad[0:SUBLANES, :] = conv0_ref[0]
        hprev[...] = h0_ref[0]

    @pl.when(c > 0)
    def _():
        xpad[0:SUBLANES, :] = xpad[tm:tm + SUBLANES, :]

    xpad[SUBLANES:SUBLANES + tm, :] = xc_ref[...].astype(F32)
    gt = gt_ref[0]
    h = hprev[...]
    for k in range(tm // q):
        rs = slice(k * q, (k + 1) * q)
        xk = _causal_conv(xpad, cw_ref, cb_ref, q, start=k * q)
        hs, h = _lru_chunk(xk, h, wr_ref, wi_ref, br_ref, bi_ref, lam_ref)
        yc_in = (hs * jax.nn.gelu(gc_ref[rs, :].astype(F32))).astype(BF16)
        ya = _dot(oa_ref[rs, :].astype(BF16), wa_ref[...])
        yb = _dot(yb_ref[rs, :].astype(BF16), wb_ref[...])
        yc = _dot(yc_in, wc_ref[...])
        sg = jax.nn.sigmoid(gates_ref[rs, :].astype(F32))
        mixed = sg[:, :d] * ya + sg[:, d:2 * d] * yb + sg[:, 2 * d:] * yc
        out_ref[rs, :] = x_ref[rs, :] + gt * _dot(mixed.astype(BF16), wo_ref[...])
    hprev[...] = h

    @pl.when(c == nstep - 1)
    def _():
        convo_ref[0] = xpad[tm:tm + SUBLANES, :]
        ho_ref[0] = h


def _merge_lru(x, mod, oa, yb, big, conv0, h0, lw, batch, seq):
    m, d = x.shape
    aw = A_WIDTH
    width = lw["w_c_proj"].shape[0]
    bd = width // C_BLOCKS
    tm = min(FUSE_TM, seq)
    nstep = seq // tm
    assert not mod.per_row and seq % tm == 0 and tm % LRU_CHUNK == 0
    row = lambda wd, j=0: pl.BlockSpec((tm, wd), lambda b, c: (b * nstep + c, j))
    const = lambda shape: pl.BlockSpec(shape, lambda b, c: (0,) * len(shape))
    conv0p = jnp.pad(conv0, ((0, 0), (SUBLANES - (C_CONV - 1), 0), (0, 0)))
    vec = lambda v: v.astype(F32).reshape(1, width)
    out, convo, ho = pl.pallas_call(
        functools.partial(_merge_lru_kernel, nstep=nstep),
        grid=(batch, nstep),
        in_specs=[row(d), pl.BlockSpec((1, 1, d), lambda b, c: (b * N_MOD + 5, 0, 0)),
                  row(aw), row(yb.shape[1]),
                  row(width, 5), row(width, 4), row(3 * d, 0),
                  pl.BlockSpec((1, SUBLANES, width), lambda b, c: (b, 0, 0)),
                  pl.BlockSpec((1, 1, width), lambda b, c: (b, 0, 0)),
                  const((C_CONV, width)), const((1, width)),
                  const((C_BLOCKS, bd, bd)), const((C_BLOCKS, bd, bd)),
                  const((1, width)), const((1, width)), const((1, width))]
        + [_resident(lw[k].shape) for k in ("w_a_proj", "w_b_proj", "w_c_proj", "w_out")],
        out_specs=[row(d),
                   pl.BlockSpec((1, SUBLANES, width), lambda b, c: (b, 0, 0)),
                   pl.BlockSpec((1, 1, width), lambda b, c: (b, 0, 0))],
        out_shape=[jax.ShapeDtypeStruct((m, d), F32),
                   jax.ShapeDtypeStruct((batch, SUBLANES, width), F32),
                   jax.ShapeDtypeStruct((batch, 1, width), F32)],
        scratch_shapes=[pltpu.VMEM((SUBLANES + tm, width), F32), pltpu.VMEM((1, width), F32)],
        compiler_params=_cparams("parallel", "arbitrary"),
        name="merge_lru",
    )(x, mod.arr, oa, yb, big, big, big, conv0p, h0.reshape(batch, 1, width),
      lw["conv_c_w"], vec(lw["conv_c_b"]), lw["w_rgate"].astype(BF16), lw["w_igate"].astype(BF16),
      vec(lw["b_rgate"]), vec(lw["b_igate"]), vec(lw["lru_lambda"]),
      lw["w_a_proj"], lw["w_b_proj"], lw["w_c_proj"], lw["w_out"])
    return out, convo[:, SUBLANES - (C_CONV - 1):, :], ho.reshape(batch, width)


def _t5_bucket(dist):
    dist = np.asarray(dist)
    large = REL_MAX_EXACT + (np.log(np.maximum(dist, 1) / REL_MAX_EXACT)
                             / math.log(REL_MAX_DISTANCE / REL_MAX_EXACT)
                             * (REL_BUCKETS - REL_MAX_EXACT)).astype(np.int64)
    large = np.minimum(large, REL_BUCKETS - 1)
    return np.where(dist < REL_MAX_EXACT, dist, large).astype(np.int32)


def _group_bias(rel_bias, g):
    dil = A_GROUPS[g][1]
    buckets = _t5_bucket(np.arange(A_KEYS + 1) * dil)
    return rel_bias[buckets][:, g * A_HEADS:(g + 1) * A_HEADS].T.astype(F32)


def _prompt_bias_table(bias, g):
    _, step, wblk = _attn_plan(g)
    h = bias.shape[0]
    cols = (wblk + 1) * A_BLOCK
    n = cols + A_BLOCK - 1
    x = np.arange(n + 1)
    x = np.where(x < cols, x, x - (n + 1))
    dist = wblk * A_BLOCK - x
    valid = (dist >= 0) & (dist % step == 0) & (dist // step <= A_KEYS)
    v = jnp.where(valid[None], bias[:, np.clip(dist // step, 0, A_KEYS)], -jnp.inf)
    tab = jnp.tile(v, (1, A_BLOCK))[:, :A_BLOCK * n].reshape(h, A_BLOCK, n)[:, :, :cols]
    return tab.reshape(h // 2, 2 * A_BLOCK, cols)


def _sample_bias_tables(bias, wb, dil, t):
    h = bias.shape[0]
    front = t - 1 + wb - A_KEYS * dil
    assert front >= 0
    sparse = jnp.concatenate([bias[:, ::-1, None], jnp.full((h, A_KEYS + 1, dil - 1), -jnp.inf, F32)], axis=2)
    base = jnp.concatenate([jnp.full((h, front), -jnp.inf, F32), sparse.reshape(h, (A_KEYS + 1) * dil)], axis=1)
    tab_buf = jnp.stack([base[:, t - 1 - tq:t - 1 - tq + wb] for tq in range(t)], axis=1)
    tq = np.arange(t)[:, None]
    dist = tq - np.arange(LANES)[None, :]
    valid = (dist >= 0) & (dist % dil == 0) & (np.arange(LANES)[None, :] < t)
    tab_new = jnp.where(valid[None], bias[:, np.clip(dist // dil, 0, A_KEYS)], -jnp.inf)
    return tab_buf.reshape(h // 2, 2 * t, wb), tab_new.reshape(h // 2, 2 * t, LANES)


def _mixer_weights(w_in):
    na = len(A_GROUPS) * A_WIDTH
    d = w_in.shape[0]
    offs = np.cumsum([0, na, na, na, 1024, 1536, 16, 1024, 1024, 3 * d])
    seg = lambda i: w_in[:, offs[i]:offs[i + 1]]
    qa, ka, va, zb, xbc, dtb, xc, gc, gates = [seg(i) for i in range(9)]
    cols = []
    for g in range(len(A_GROUPS)):
        sl = slice(g * A_WIDTH, (g + 1) * A_WIDTH)
        cols += [qa[:, sl] * (A_HEAD_DIM ** -0.5), ka[:, sl], va[:, sl]]
    w_qkv = jnp.concatenate(cols + [dtb, jnp.zeros((d, DT_PAD - dtb.shape[1]), w_in.dtype)], axis=1)
    w_big = jnp.concatenate([gates, zb, gc, xc, xbc], axis=1)
    return w_qkv.astype(BF16), w_big.astype(BF16)


def _mixer(x, mod, lw, st, bias, prompt, batch, seq):
    act_dtype = BF16 if prompt else F32
    qkv = _proj(x, mod, lw["g_mix"], lw["w_qkv"], F32, QKV_TN, slabs=True)
    big = _proj(x, mod, lw["g_mix"], lw["w_big"], act_dtype, BIG_TN)
    ng = len(A_GROUPS)
    npair = A_WIDTH // LANES
    if prompt:
        oa, new_kv = _attn_prompt(qkv, bias, st["kvt"], st["layer"], st["depth"], batch, seq, act_dtype)
    else:
        oa = _attn_sample(qkv, st["kv"], st["layer"], bias, batch, seq)
        new_kv = [jnp.transpose(qkv[(3 * g + 1) * npair:(3 * g + 3) * npair], (1, 0, 2))
                  .reshape(batch, seq, 2, A_HEADS, A_HEAD_DIM) for g in range(ng)]
    dt_col = 3 * ng * npair
    yb, conv_b_new, ssm_new = _ssd(big, qkv, dt_col, st["conv_b"], st["ssm"], lw, batch, seq, act_dtype)
    if prompt:
        x, conv_c_new, lru_new = _merge_lru(x, mod, oa, yb, big, st["conv_c"], st["lru"], lw, batch, seq)
    else:
        yc, conv_c_new, lru_new = _lru(big, st["conv_c"], st["lru"], lw, batch, seq, act_dtype)
        x = _merge(x, mod, oa, yb, yc, big, lw)
    return x, (new_kv[0], new_kv[1], new_kv[2], conv_b_new, ssm_new, conv_c_new, lru_new)


def _block(x, mod, lw, st, bias, prompt, batch, seq, g_final):
    x = _ffn(x, mod, 0, lw["g_ff1"], lw["w_ff1_in"], lw["w_ff1_out"])
    x, new_st = _mixer(x, mod, lw, st, bias, prompt, batch, seq)
    x = _ffn(x, mod, 6, lw["g_ff2"], lw["w_ff2_in"], lw["w_ff2_out"], g_final)
    return x, new_st


@jax.jit
def _forward(x_prompt, x_sample, c_prompt, c_sample, cache_win1_kv, cache_win2_kv, cache_win3_kv,
             state_conv_b, state_ssm, state_conv_c, state_lru, rel_bias, w_ada, b_ada, g_ff1,
             w_ff1_in, w_ff1_out, g_mix, w_in, w_a_proj, conv_b_w, conv_b_b, dt_bias, a_log, d_skip,
             g_ssm_norm, w_b_proj, conv_c_w, conv_c_b, w_rgate, b_rgate, w_igate, b_igate, lru_lambda,
             w_c_proj, w_out, g_ff2, w_ff2_in, w_ff2_out, g_final):
    bp, lp, d = x_prompt.shape
    bs, ls, _ = x_sample.shape
    depth = w_ada.shape[0]
    caches = (cache_win1_kv, cache_win2_kv, cache_win3_kv)
    biases = [_group_bias(rel_bias, g) for g in range(len(A_GROUPS))]
    bias_p = [_prompt_bias_table(b, g) for g, b in enumerate(biases)]
    bias_s = [_sample_bias_tables(b, caches[g].shape[2], A_GROUPS[g][1], ls) for g, b in enumerate(biases)]

    yp = x_prompt.reshape(bp * lp, d)
    ys = x_sample.reshape(bs * ls, d)
    c_all = jnp.concatenate([c_prompt, c_sample], axis=0)
    new_p = [[] for _ in range(7)]
    new_s = [[] for _ in range(7)]
    kvt_p = None
    for l in range(depth):
        w_qkv, w_big = _mixer_weights(w_in[l])
        lw = dict(g_ff1=g_ff1[l], w_ff1_in=w_ff1_in[l].astype(BF16), w_ff1_out=w_ff1_out[l].astype(BF16),
                  g_mix=g_mix[l], w_qkv=w_qkv, w_big=w_big, w_a_proj=w_a_proj[l].astype(BF16),
                  conv_b_w=conv_b_w[l], conv_b_b=conv_b_b[l], dt_bias=dt_bias[l], a_log=a_log[l],
                  d_skip=d_skip[l], g_ssm_norm=g_ssm_norm[l], w_b_proj=w_b_proj[l].astype(BF16),
                  conv_c_w=conv_c_w[l], conv_c_b=conv_c_b[l], w_rgate=w_rgate[l], b_rgate=b_rgate[l],
                  w_igate=w_igate[l], b_igate=b_igate[l], lru_lambda=lru_lambda[l],
                  w_c_proj=w_c_proj[l].astype(BF16), w_out=w_out[l].astype(BF16),
                  g_ff2=g_ff2[l], w_ff2_in=w_ff2_in[l].astype(BF16), w_ff2_out=w_ff2_out[l].astype(BF16))
        mod_all = _ada(c_all, w_ada, b_ada, l)
        mod_p = _Mod(mod_all[:bp], bp, lp, per_row=False)
        mod_s = _Mod(mod_all[bp:], bs, ls, per_row=True)
        gf = g_final if l == depth - 1 else None
        st_p = dict(kvt=kvt_p, layer=l, depth=depth,
                    conv_b=jnp.zeros((bp, B_CONV - 1, conv_b_w.shape[2]), F32),
                    ssm=jnp.zeros((bp,) + state_ssm.shape[2:], F32),
                    conv_c=jnp.zeros((bp, C_CONV - 1, conv_c_w.shape[2]), F32),
                    lru=jnp.zeros((bp, state_lru.shape[2]), F32))
        st_s = dict(kv=caches, layer=l, conv_b=state_conv_b[l], ssm=state_ssm[l],
                    conv_c=state_conv_c[l], lru=state_lru[l])
        yp, stp = _block(yp, mod_p, lw, st_p, bias_p, True, bp, lp, gf)
        ys, sts = _block(ys, mod_s, lw, st_s, bias_s, False, bs, ls, gf)
        kvt_p = list(stp[:3])
        for i in range(7):
            new_p[i].append(stp[i])
            new_s[i].append(sts[i])
    outs_p = [jnp.transpose(v, (0, 1, 5, 2, 3, 4)) for v in kvt_p] + [jnp.stack(v, 0) for v in new_p[3:]]
    outs_s = [jnp.stack(v, 0) for v in new_s]
    return (yp.reshape(bp, lp, d), ys.reshape(bs, ls, d), *outs_p, *outs_s)


def kernel(x_prompt, x_sample, c_prompt, c_sample, cache_win1_kv, cache_win2_kv, cache_win3_kv,
           state_conv_b, state_ssm, state_conv_c, state_lru, rel_bias, w_ada, b_ada, g_ff1,
           w_ff1_in, w_ff1_out, g_mix, w_in, w_a_proj, conv_b_w, conv_b_b, dt_bias, a_log, d_skip,
           g_ssm_norm, w_b_proj, conv_c_w, conv_c_b, w_rgate, b_rgate, w_igate, b_igate, lru_lambda,
           w_c_proj, w_out, g_ff2, w_ff2_in, w_ff2_out, g_final):
    return _forward(x_prompt, x_sample, c_prompt, c_sample, cache_win1_kv, cache_win2_kv,
                    cache_win3_kv, state_conv_b, state_ssm, state_conv_c, state_lru, rel_bias, w_ada,
                    b_ada, g_ff1, w_ff1_in, w_ff1_out, g_mix, w_in, w_a_proj, conv_b_w, conv_b_b,
                    dt_bias, a_log, d_skip, g_ssm_norm, w_b_proj, conv_c_w, conv_c_b, w_rgate,
                    b_rgate, w_igate, b_igate, lru_lambda, w_c_proj, w_out, g_ff2, w_ff2_in,
                    w_ff2_out, g_final)
```

```python
import functools
import math

import jax
import jax.numpy as jnp
import numpy as np
from jax import lax
from jax.experimental import pallas as pl
from jax.experimental.pallas import tpu as pltpu

F32 = jnp.float32
BF16 = jnp.bfloat16

NORM_EPS = 1e-6
N_MOD = 9
A_GROUPS = ((128, 1), (512, 4), (2048, 16))
A_HEADS = 8
A_HEAD_DIM = 64
A_WIDTH = A_HEADS * A_HEAD_DIM
A_KEYS = 128
A_BLOCK = 128
REL_BUCKETS = 32
REL_MAX_EXACT = 16
REL_MAX_DISTANCE = 2048
B_HEAD_DIM = 64
B_GROUPS = 2
B_STATE = 128
B_CONV = 4
C_BLOCKS = 8
C_CONV = 4
C_POW = 8.0

LANES = 128
SUBLANES = 8
SSD_CHUNK = 128
FFN_TM = 1024
PROJ_TM = 2048
PROJ_SUB = 512
MERGE_TM = 1024
MERGE_SUB = 256
SEQ_TM = 512
LRU_CHUNK = 128
QKV_TN = 1024
BIG_TN = 1536
DT_PAD = 512
ATTN_UNROLL = 8
ATTN_UNROLL_REST = 12
ATTN_MAX_STRIDE = 4
VMEM_LIMIT = 56 * 1024 * 1024


def _cparams(*sem):
    return pltpu.CompilerParams(dimension_semantics=sem, vmem_limit_bytes=VMEM_LIMIT)


def _resident(shape):
    nd = len(shape)
    return pl.BlockSpec(shape, lambda *_: (0,) * nd, pipeline_mode=pl.Buffered(1))


def _ld(ref):
    return ref[0] if len(ref.shape) == 3 else ref[...]


def _norm_mod(x, g, sc, sh):
    ms = jnp.mean(x * x, axis=-1, keepdims=True)
    return (x * lax.rsqrt(ms + NORM_EPS) * g) * (1.0 + sc) + sh


def _softplus(x):
    return jnp.maximum(x, 0.0) + jnp.log1p(jnp.exp(-jnp.abs(x)))


def _causal_conv(xpad, w_ref, b_ref, q, start=0):
    taps = w_ref.shape[0]
    assert taps - 1 <= SUBLANES and start % SUBLANES == 0
    xe = xpad[start:start + SUBLANES + q, :]
    w = w_ref[...]
    acc = xe * w[0:1, :]
    for k in range(1, taps):
        acc = xe * w[k:k + 1, :] + pltpu.roll(acc, 1, axis=0)
    return acc[SUBLANES:SUBLANES + q] + b_ref[...]


def _split_bf16(v, n):
    parts = []
    r = v
    for _ in range(n):
        p = r.astype(BF16)
        parts.append(p)
        r = r - p.astype(F32)
    return parts


def _dot(a, b):
    return jnp.dot(a, b, preferred_element_type=F32)


def _dot_nt(a, b):
    return lax.dot_general(a, b, (((1,), (1,)), ((), ())), preferred_element_type=F32)


def _ada_kernel(c_ref, w_ref, b_ref, o_ref):
    c = c_ref[...]
    a = (c * jax.nn.sigmoid(c)).astype(BF16)
    o_ref[...] = _dot(a, w_ref[...].astype(BF16)) + b_ref[...]


def _ada(c, w, b, layer):
    rows, d = c.shape
    depth, _, n = w.shape
    tn = 1024
    return pl.pallas_call(
        _ada_kernel,
        grid=(n // tn,),
        in_specs=[pl.BlockSpec((rows, d), lambda j: (0, 0)),
                  pl.BlockSpec((None, d, tn), lambda j: (layer, 0, j)),
                  pl.BlockSpec((None, 1, tn), lambda j: (layer, 0, j))],
        out_specs=pl.BlockSpec((rows, tn), lambda j: (0, j)),
        out_shape=jax.ShapeDtypeStruct((rows, n), F32),
        compiler_params=_cparams("parallel"),
        name="ada",
    )(c, w, b.reshape(depth, 1, n))


class _Mod:
    def __init__(self, mod, batch, seq, per_row):
        d = mod.shape[1] // N_MOD
        self.d = d
        self.seq = seq
        self.per_row = per_row
        if per_row:
            self.arr = jnp.repeat(mod, seq, axis=0)
        else:
            self.arr = mod.reshape(batch * N_MOD, 1, d)

    def spec(self, k, tm, grid_rank):
        d = self.d
        if self.per_row:
            if grid_rank == 1:
                return pl.BlockSpec((tm, d), lambda i: (i, k))
            return pl.BlockSpec((tm, d), lambda i, j: (i, k))
        per = self.seq // tm
        if grid_rank == 1:
            return pl.BlockSpec((1, 1, d), lambda i: ((i // per) * N_MOD + k, 0, 0))
        return pl.BlockSpec((1, 1, d), lambda i, j: ((i // per) * N_MOD + k, 0, 0))


def _row_tile(m, seq, per_row, want):
    tm = min(want, m if per_row else seq)
    assert m % tm == 0 and (per_row or seq % tm == 0)
    return tm


def _ffn_kernel(x_ref, sh_ref, sc_ref, gt_ref, gn_ref, win_ref, wout_ref, *rest, d_ff, fc, final):
    if final:
        gf_ref, o_ref, acc_ref = rest
    else:
        o_ref, acc_ref = rest
    x = x_ref[...]
    h = _norm_mod(x, gn_ref[...], _ld(sc_ref), _ld(sh_ref)).astype(BF16)
    for c in range(d_ff // fc):
        u = _dot(h, win_ref[:, c * fc:(c + 1) * fc])
        v = _dot(h, win_ref[:, d_ff + c * fc:d_ff + (c + 1) * fc])
        a = (u * jax.nn.sigmoid(u) * v).astype(BF16)
        part = _dot(a, wout_ref[c * fc:(c + 1) * fc, :])
        if c == 0:
            acc_ref[...] = part
        else:
            acc_ref[...] += part
    y = x + 0.5 * _ld(gt_ref) * acc_ref[...]
    if final:
        ms = jnp.mean(y * y, axis=-1, keepdims=True)
        y = y * lax.rsqrt(ms + NORM_EPS) * gf_ref[...]
    o_ref[...] = y


def _ffn(x, mod, k0, gn, w_in, w_out, g_final=None):
    m, d = x.shape
    d_ff = w_out.shape[0]
    tm = _row_tile(m, mod.seq, mod.per_row, FFN_TM)
    final = g_final is not None
    in_specs = [pl.BlockSpec((tm, d), lambda i: (i, 0)),
                mod.spec(k0, tm, 1), mod.spec(k0 + 1, tm, 1), mod.spec(k0 + 2, tm, 1),
                _resident((1, d)), _resident(w_in.shape), _resident(w_out.shape)]
    args = [x, mod.arr, mod.arr, mod.arr, gn.reshape(1, d), w_in, w_out]
    if final:
        in_specs.append(_resident((1, d)))
        args.append(g_final.reshape(1, d))
    return pl.pallas_call(
        functools.partial(_ffn_kernel, d_ff=d_ff, fc=256, final=final),
        grid=(m // tm,),
        in_specs=in_specs,
        out_specs=pl.BlockSpec((tm, d), lambda i: (i, 0)),
        out_shape=jax.ShapeDtypeStruct((m, d), F32),
        scratch_shapes=[pltpu.VMEM((tm, d), F32)],
        compiler_params=_cparams("parallel"),
        name="ffn",
    )(*args)


def _proj_kernel(x_ref, sh_ref, sc_ref, gn_ref, w_ref, o_ref, h_ref):
    @pl.when(pl.program_id(1) == 0)
    def _():
        h_ref[...] = _norm_mod(x_ref[...], gn_ref[...], _ld(sc_ref), _ld(sh_ref)).astype(BF16)

    tm = h_ref.shape[0]
    sub = min(tm, PROJ_SUB)
    for r in range(tm // sub):
        rs = slice(r * sub, (r + 1) * sub)
        res = _dot(h_ref[rs, :], w_ref[...]).astype(o_ref.dtype)
        if len(o_ref.shape) == 2:
            o_ref[rs, :] = res
        else:
            for s in range(o_ref.shape[0]):
                o_ref[s, rs, :] = res[:, s * LANES:(s + 1) * LANES]


def _proj(x, mod, gn, w, out_dtype, tn, slabs=False):
    m, d = x.shape
    n = w.shape[1]
    tm = _row_tile(m, mod.seq, mod.per_row, PROJ_TM)
    if slabs:
        out_spec = pl.BlockSpec((tn // LANES, tm, LANES), lambda i, j: (j, i, 0))
        out_shape = jax.ShapeDtypeStruct((n // LANES, m, LANES), out_dtype)
    else:
        out_spec = pl.BlockSpec((tm, tn), lambda i, j: (i, j))
        out_shape = jax.ShapeDtypeStruct((m, n), out_dtype)
    return pl.pallas_call(
        _proj_kernel,
        grid=(m // tm, n // tn),
        in_specs=[pl.BlockSpec((tm, d), lambda i, j: (i, 0)),
                  mod.spec(3, tm, 2), mod.spec(4, tm, 2),
                  pl.BlockSpec((1, d), lambda i, j: (0, 0)),
                  pl.BlockSpec((d, tn), lambda i, j: (0, j))],
        out_specs=out_spec,
        out_shape=out_shape,
        scratch_shapes=[pltpu.VMEM((tm, d), BF16)],
        compiler_params=_cparams("parallel", "arbitrary"),
        name="proj",
    )(x, mod.arr, mod.arr, gn.reshape(1, d), w)


def _pair_softmax(q2, k2, v2, bias2, lo, transposed=False):
    rows = q2.shape[0]
    zero = jnp.zeros_like(q2)
    qs = jnp.concatenate([jnp.where(lo, q2, zero), jnp.where(lo, zero, q2)], axis=0).astype(BF16)
    s = (_dot(qs, k2) if transposed else _dot_nt(qs, k2)) + bias2
    mx = jnp.max(s, axis=-1, keepdims=True)
    p = jnp.exp(s - mx)
    l = jnp.sum(p, axis=-1, keepdims=True)
    pb = p.astype(BF16)
    o = _dot_nt(pb, v2) if transposed else _dot(pb, v2)
    return (jnp.where(lo, o[:rows], o[rows:]), jnp.where(lo, mx[:rows], mx[rows:]),
            jnp.where(lo, l[:rows], l[rows:]))


def _merge_softmax(acc, new):
    ao, am, al = acc
    o, m, l = new
    mn = jnp.maximum(am, m)
    a1 = jnp.exp(am - mn)
    a2 = jnp.exp(m - mn)
    return ao * a1 + o * a2, mn, al * a1 + l * a2


def _attn_plan(g):
    dil = A_GROUPS[g][1]
    if dil <= ATTN_MAX_STRIDE:
        return dil, 1, A_KEYS // A_BLOCK
    assert dil % ATTN_MAX_STRIDE == 0
    step = dil // ATTN_MAX_STRIDE
    return ATTN_MAX_STRIDE, step, step * A_KEYS // A_BLOCK


def _attn_prompt_kernel(*refs, seq, n_alias):
    ng = len(A_GROUPS)
    qkv = refs[:3 * ng]
    bias_refs = refs[3 * ng:4 * ng]
    o_ref = refs[4 * ng + n_alias]
    kvt_refs = refs[4 * ng + 1 + n_alias:5 * ng + 1 + n_alias]
    acc_o, acc_m, acc_l = refs[-3:]
    blk = A_BLOCK

    for g in range(ng):
        keep = kvt_refs[g].shape[-1]
        for kv in range(2):
            src = qkv[3 * g + 1 + kv]
            for c in range(keep // blk):
                t = src[pl.ds(seq - keep + c * blk, blk), :].T
                for i in range(2):
                    kvt_refs[g][kv, i, :, c * blk:(c + 1) * blk] = t[i * A_HEAD_DIM:(i + 1) * A_HEAD_DIM]

    hp = pl.program_id(1)
    lo = lax.broadcasted_iota(jnp.int32, (blk, LANES), 1) < A_HEAD_DIM

    def tiles(g, starts, kw):
        stride = _attn_plan(g)[0]
        q_ref, k_ref, v_ref = qkv[3 * g:3 * g + 3]
        bias_ref = bias_refs[g]
        wcols = bias_ref.shape[-1]

        def rows(start, n):
            return pl.ds(start, n) if stride == 1 else pl.ds(start, n, stride=stride)

        news = []
        for qstart, kstart in starts:
            q2 = q_ref[rows(qstart, blk), :]
            k2 = k_ref[rows(kstart, kw), :].astype(BF16)
            v2 = v_ref[rows(kstart, kw), :].astype(BF16)
            news.append(_pair_softmax(q2, k2, v2, bias_ref[hp, :, wcols - kw:wcols], lo))
        for (qstart, _), new in zip(starts, news):
            sel = rows(qstart, blk)
            if g > 0:
                new = _merge_softmax((acc_o[sel, :], acc_m[sel, :], acc_l[sel, :]), new)
            acc_o[sel, :] = new[0]
            acc_m[sel, :] = new[1]
            acc_l[sel, :] = new[2]

    for g in range(ng):
        stride, _, wblk = _attn_plan(g)
        assert stride <= ATTN_UNROLL
        nb = seq // stride // blk
        span = blk * stride
        for b in range(min(wblk, nb)):
            tiles(g, [(r + b * span, r) for r in range(stride)], (b + 1) * blk)
        nfull = nb - wblk
        if nfull > 0:
            ub = max(u for u in range(1, nfull + 1) if nfull % u == 0 and u * stride <= ATTN_UNROLL_REST)

            def full(i, carry, g=g, span=span, stride=stride, ub=ub, wblk=wblk):
                b = wblk + i * ub
                tiles(g, [(r + (b + k) * span, r + (b + k - wblk) * span)
                          for k in range(ub) for r in range(stride)], (wblk + 1) * blk)
                return carry

            lax.fori_loop(0, nfull // ub, full, 0)
    o_ref[...] = (acc_o[...] / acc_l[...]).astype(o_ref.dtype)


def _attn_prompt(qkv, bias_tab, kvt_prev, layer, depth, batch, seq, o_dtype):
    npair = A_WIDTH // LANES
    ng = len(A_GROUPS)
    col = lambda j: pl.BlockSpec((None, seq, LANES), lambda b, h: (j * npair + h, b, 0))
    keeps = [min(win, seq) for win, _ in A_GROUPS]
    n_alias = 0 if kvt_prev is None else ng
    alias_specs = [pl.BlockSpec(memory_space=pl.ANY)] * n_alias
    alias_args = [] if kvt_prev is None else list(kvt_prev)
    n_in = 4 * ng
    outs = pl.pallas_call(
        functools.partial(_attn_prompt_kernel, seq=seq, n_alias=n_alias),
        grid=(batch, npair),
        in_specs=[col(j) for j in range(3 * ng)]
        + [pl.BlockSpec(tab.shape, lambda b, h: (0, 0, 0)) for tab in bias_tab] + alias_specs,
        out_specs=[pl.BlockSpec((seq, LANES), lambda b, h: (b, h))]
        + [pl.BlockSpec((None, None, 2, 2, A_HEAD_DIM, keep), lambda b, h: (layer, b, 0, h, 0, 0))
           for keep in keeps],
        out_shape=[jax.ShapeDtypeStruct((batch * seq, A_WIDTH), o_dtype)]
        + [jax.ShapeDtypeStruct((depth, batch, 2, A_HEADS, A_HEAD_DIM, keep), F32) for keep in keeps],
        input_output_aliases={n_in + g: 1 + g for g in range(n_alias)},
        scratch_shapes=[pltpu.VMEM((seq, LANES), F32)] * 3,
        compiler_params=_cparams("parallel", "parallel"),
        name="attn_prompt",
    )(*([qkv] * (3 * ng)), *bias_tab, *alias_args)
    return outs[0], list(outs[1:])


def _attn_sample_kernel(qkv_ref, buf1_ref, buf2_ref, buf3_ref, bb1, bn1, bb2, bn2, bb3, bn3,
                        o_ref, knew_scr, vnew_scr):
    t = qkv_ref.shape[1]
    npair = A_HEADS // 2
    bufs = (buf1_ref, buf2_ref, buf3_ref)
    bias = ((bb1, bn1), (bb2, bn2), (bb3, bn3))
    lo = lax.broadcasted_iota(jnp.int32, (t, LANES), 1) < A_HEAD_DIM
    knew_scr[...] = jnp.zeros_like(knew_scr)
    vnew_scr[...] = jnp.zeros_like(vnew_scr)
    for hp in range(npair):
        cols = slice(hp * LANES, (hp + 1) * LANES)
        acc = None
        for g in range(len(A_GROUPS)):
            q2 = qkv_ref[3 * g * npair + hp]
            wb = bufs[g].shape[-1]
            kb = bufs[g][0, 2 * hp:2 * hp + 2].reshape(LANES, wb).astype(BF16)
            vb = bufs[g][1, 2 * hp:2 * hp + 2].reshape(LANES, wb).astype(BF16)
            knew_scr[g, hp, 0:t, :] = qkv_ref[(3 * g + 1) * npair + hp]
            vnew_scr[g, hp, 0:t, :] = qkv_ref[(3 * g + 2) * npair + hp]
            kn = knew_scr[g, hp].astype(BF16)
            vn = vnew_scr[g, hp].astype(BF16)
            bbuf, bnew = bias[g]
            new = _pair_softmax(q2, kb, vb, bbuf[hp], lo, transposed=True)
            new = _merge_softmax(new, _pair_softmax(q2, kn, vn, bnew[hp], lo))
            acc = new if acc is None else _merge_softmax(acc, new)
        o_ref[:, cols] = acc[0] / acc[2]


def _attn_sample(qkv, caches, layer, bias_s, batch, t):
    aw = A_WIDTH
    ng = len(A_GROUPS)
    bufs = [jnp.transpose(c, (0, 1, 3, 4, 5, 2)) for c in caches]
    buf_specs = [pl.BlockSpec((None, None) + bv.shape[2:], lambda b: (layer, b, 0, 0, 0, 0)) for bv in bufs]
    tabs = [tab for pair in bias_s for tab in pair]
    return pl.pallas_call(
        _attn_sample_kernel,
        grid=(batch,),
        in_specs=[pl.BlockSpec((3 * ng * aw // LANES, t, LANES), lambda b: (0, b, 0))] + buf_specs
        + [pl.BlockSpec(tab.shape, lambda b: (0, 0, 0)) for tab in tabs],
        out_specs=pl.BlockSpec((t, aw), lambda b: (b, 0)),
        out_shape=jax.ShapeDtypeStruct((batch * t, aw), F32),
        scratch_shapes=[pltpu.VMEM((ng, aw // LANES, LANES, LANES), F32)] * 2,
        compiler_params=_cparams("parallel"),
        name="attn_sample",
    )(qkv, *bufs, *tabs)


def _ssd_kernel(xbc_ref, z_ref, dt_ref, conv0_ref, st0_ref, cw_ref, cb_ref, dtb_ref, aneg_ref,
                dsk_ref, gn_ref, e_ref, y_ref, convo_ref, sto_ref, xpad, st_t, y_scr, *, lv, nsub, nch):
    q = SSD_CHUNK
    c = pl.program_id(1)
    width = y_ref.shape[1]
    nst = B_STATE
    hpg = width // B_HEAD_DIM // B_GROUPS // 2

    @pl.when(c == 0)
    def _():
        xpad[...] = jnp.zeros_like(xpad)
        xpad[0:SUBLANES, :] = conv0_ref[0]
        for j in range(width // LANES):
            st_t[:, j * LANES:(j + 1) * LANES] = st0_ref[0, j * LANES:(j + 1) * LANES, :].T

    rows = nsub * lv

    @pl.when(c > 0)
    def _():
        xpad[0:SUBLANES, :] = xpad[rows:rows + SUBLANES, :]

    xpad[SUBLANES:SUBLANES + rows, :] = xbc_ref[...].astype(F32)

    for k in range(nsub):
        rs = slice(k * lv, (k + 1) * lv)
        conv = _causal_conv(xpad, cw_ref, cb_ref, q, start=k * q)
        act = conv * jax.nn.sigmoid(conv)
        xs = act[:, :width]
        bm = [act[:, width + gi * nst:width + (gi + 1) * nst] for gi in range(B_GROUPS)]
        cm = [act[:, width + (B_GROUPS + gi) * nst:width + (B_GROUPS + gi + 1) * nst] for gi in range(B_GROUPS)]

        dt_raw = dt_ref[rs, :]
        if lv < q:
            dt_raw = jnp.concatenate([dt_raw, jnp.zeros((q - lv, LANES), F32)], axis=0)
        dt = _softplus(dt_raw + dtb_ref[...])
        if lv < q:
            dt = jnp.where(lax.broadcasted_iota(jnp.int32, (q, LANES), 0) < lv, dt, 0.0)
        dta = dt * aneg_ref[...]

        row = lax.broadcasted_iota(jnp.int32, (q, q), 0)
        col = lax.broadcasted_iota(jnp.int32, (q, q), 1)
        causal = row >= col
        tri = jnp.where(causal, 1.0, 0.0).astype(BF16)
        acum = sum(_dot(tri, p) for p in _split_bf16(dta, 3))
        acum_t = acum.T
        e = e_ref[...]
        dt_x = sum(_dot(p, e) for p in _split_bf16(dt, 2))
        acum_x = sum(_dot(p, e) for p in _split_bf16(acum, 2))
        last_x = acum_x[q - 1:q, :]
        ea_x = jnp.exp(acum_x)
        xdt = xs * dt_x
        xdt_te = (xdt * jnp.exp(last_x - acum_x)).astype(BF16)
        xdt_b = xdt.astype(BF16)
        chunk_decay = jnp.exp(last_x)
        lo = lax.broadcasted_iota(jnp.int32, (q, LANES), 1) < B_HEAD_DIM

        for gi in range(B_GROUPS):
            cmb = cm[gi].astype(BF16)
            cb = _dot_nt(cmb, bm[gi].astype(BF16))
            bm_t = bm[gi].T.astype(BF16)
            for hp in range(gi * hpg, (gi + 1) * hpg):
                cols = slice(hp * LANES, (hp + 1) * LANES)
                ax = acum_x[:, cols]
                ax_r = pltpu.roll(ax, B_HEAD_DIM, axis=1)
                ys = []
                for col_v, h in ((jnp.where(lo, ax, ax_r), 2 * hp), (jnp.where(lo, ax_r, ax), 2 * hp + 1)):
                    seg = col_v - acum_t[h:h + 1, :]
                    dec = jnp.exp(jnp.where(causal, seg, -jnp.inf))
                    ys.append(_dot((cb * dec).astype(BF16), xdt_b[:, cols]))
                st_old = st_t[:, cols]
                y_off = _dot(cmb, st_old.astype(BF16)) * ea_x[:, cols]
                st_t[:, cols] = st_old * chunk_decay[:, cols] + _dot(bm_t, xdt_te[:, cols])
                y_scr[k, :, cols] = jnp.where(lo, ys[0], ys[1]) + y_off + dsk_ref[:, cols] * xs[:, cols]

        z = z_ref[rs, :].astype(F32)
        if lv < q:
            z = jnp.concatenate([z, jnp.zeros((q - lv, width), F32)], axis=0)
        y = y_scr[k] * (z * jax.nn.sigmoid(z))
        gw = width // B_GROUPS
        for gi in range(B_GROUPS):
            yg = y[:, gi * gw:(gi + 1) * gw]
            ms = jnp.mean(yg * yg, axis=-1, keepdims=True)
            yn = yg * lax.rsqrt(ms + NORM_EPS) * gn_ref[:, gi * gw:(gi + 1) * gw]
            y_ref[rs, gi * gw:(gi + 1) * gw] = yn[0:lv].astype(y_ref.dtype)

    @pl.when(c == nch - 1)
    def _():
        convo_ref[0] = xpad[rows:rows + SUBLANES, :]
        for j in range(width // LANES):
            sto_ref[0, j * LANES:(j + 1) * LANES, :] = st_t[:, j * LANES:(j + 1) * LANES].T


def _ssd(big, qkv, dt_col, conv0, st0, lw, batch, seq, o_dtype):
    width = lw["w_b_proj"].shape[0]
    cch = lw["conv_b_w"].shape[1]
    nheads = width // B_HEAD_DIM
    lv = min(seq, SSD_CHUNK)
    nsub = min(seq, SEQ_TM) // lv if lv == SSD_CHUNK else 1
    rows = lv * nsub
    nch = seq // rows
    assert seq % rows == 0 and (lv == SSD_CHUNK or nch == 1)
    pad = lambda v: jnp.pad(v.astype(F32), (0, LANES - nheads)).reshape(1, LANES)
    e = (np.arange(LANES)[:, None] == (np.arange(width)[None, :] // B_HEAD_DIM)).astype(np.float32)
    conv0p = jnp.pad(conv0, ((0, 0), (SUBLANES - (B_CONV - 1), 0), (0, 0)))
    y, convo, sto = pl.pallas_call(
        functools.partial(_ssd_kernel, lv=lv, nsub=nsub, nch=nch),
        grid=(batch, nch),
        in_specs=[pl.BlockSpec((rows, cch), lambda b, c: (b * nch + c, 4)),
                  pl.BlockSpec((rows, width), lambda b, c: (b * nch + c, 3)),
                  pl.BlockSpec((None, rows, LANES), lambda b, c: (dt_col, b * nch + c, 0)),
                  pl.BlockSpec((1, SUBLANES, cch), lambda b, c: (b, 0, 0)),
                  pl.BlockSpec((1, width, B_STATE), lambda b, c: (b, 0, 0)),
                  pl.BlockSpec((B_CONV, cch), lambda b, c: (0, 0)),
                  pl.BlockSpec((1, cch), lambda b, c: (0, 0)),
                  pl.BlockSpec((1, LANES), lambda b, c: (0, 0)),
                  pl.BlockSpec((1, LANES), lambda b, c: (0, 0)),
                  pl.BlockSpec((1, width), lambda b, c: (0, 0)),
                  pl.BlockSpec((1, width), lambda b, c: (0, 0)),
                  pl.BlockSpec((LANES, width), lambda b, c: (0, 0))],
        out_specs=[pl.BlockSpec((rows, width), lambda b, c: (b * nch + c, 0)),
                   pl.BlockSpec((1, SUBLANES, cch), lambda b, c: (b, 0, 0)),
                   pl.BlockSpec((1, width, B_STATE), lambda b, c: (b, 0, 0))],
        out_shape=[jax.ShapeDtypeStruct((batch * seq, width), o_dtype),
                   jax.ShapeDtypeStruct((batch, SUBLANES, cch), F32),
                   jax.ShapeDtypeStruct((batch, width, B_STATE), F32)],
        scratch_shapes=[pltpu.VMEM((SUBLANES + nsub * SSD_CHUNK, cch), F32),
                        pltpu.VMEM((B_STATE, width), F32),
                        pltpu.VMEM((nsub, SSD_CHUNK, width), F32)],
        compiler_params=_cparams("parallel", "arbitrary"),
        name="ssd",
    )(big, big, qkv, conv0p, st0.reshape(batch, width, B_STATE),
      lw["conv_b_w"], lw["conv_b_b"].reshape(1, cch), pad(lw["dt_bias"]),
      pad(-jnp.exp(lw["a_log"].astype(F32))),
      jnp.repeat(lw["d_skip"].astype(F32), B_HEAD_DIM).reshape(1, width),
      lw["g_ssm_norm"].reshape(1, width), jnp.asarray(e, BF16))
    return (y, convo[:, SUBLANES - (B_CONV - 1):, :],
            sto.reshape(batch, nheads, B_HEAD_DIM, B_STATE))


def _lru_chunk(x, h, wr_ref, wi_ref, br_ref, bi_ref, lam_ref):
    q, width = x.shape
    bd = width // C_BLOCKS
    rs, gs = [], []
    for j in range(C_BLOCKS):
        xb = x[:, j * bd:(j + 1) * bd].astype(BF16)
        rs.append(_dot(xb, wr_ref[j]))
        gs.append(_dot(xb, wi_ref[j]))
    rg = jax.nn.sigmoid(jnp.concatenate(rs, axis=1) + br_ref[...])
    ig = jax.nn.sigmoid(jnp.concatenate(gs, axis=1) + bi_ref[...])
    log_a = -C_POW * rg * _softplus(-lam_ref[...])
    a = jnp.exp(log_a)
    u = x * ig * jnp.sqrt(-jnp.tanh(log_a) * (a * a + 1.0))

    rowi = lax.broadcasted_iota(jnp.int32, (SUBLANES, width), 0)
    hs = []
    for g in range(q // SUBLANES):
        ag = a[g * SUBLANES:(g + 1) * SUBLANES]
        bg = u[g * SUBLANES:(g + 1) * SUBLANES]
        s = 1
        while s < SUBLANES:
            keep = rowi >= s
            a_sh = jnp.where(keep, pltpu.roll(ag, s, axis=0), 1.0)
            b_sh = jnp.where(keep, pltpu.roll(bg, s, axis=0), 0.0)
            bg = ag * b_sh + bg
            ag = ag * a_sh
            s *= 2
        hg = bg + ag * h
        hs.append(hg)
        h = hg[SUBLANES - 1:SUBLANES]
    return (hs[0] if len(hs) == 1 else jnp.concatenate(hs, axis=0)), h


def _lru_kernel(xc_ref, gc_ref, conv0_ref, h0_ref, cw_ref, cb_ref, wr_ref, wi_ref, br_ref, bi_ref,
                lam_ref, y_ref, convo_ref, ho_ref, xpad, hprev, *, q, nch):
    c = pl.program_id(1)
    width = y_ref.shape[1]

    @pl.when(c == 0)
    def _():
        xpad[0:SUBLANES, :] = conv0_ref[0]
        hprev[...] = h0_ref[0]

    @pl.when(c > 0)
    def _():
        xpad[0:SUBLANES, :] = xpad[q:q + SUBLANES, :]

    xpad[SUBLANES:SUBLANES + q, :] = xc_ref[...].astype(F32)
    sub = min(q, LRU_CHUNK)
    h = hprev[...]
    for k in range(q // sub):
        rs = slice(k * sub, (k + 1) * sub)
        x = _causal_conv(xpad, cw_ref, cb_ref, sub, start=k * sub)
        hs, h = _lru_chunk(x, h, wr_ref, wi_ref, br_ref, bi_ref, lam_ref)
        y_ref[rs, :] = (hs * jax.nn.gelu(gc_ref[rs, :].astype(F32))).astype(y_ref.dtype)
    hprev[...] = h

    @pl.when(c == nch - 1)
    def _():
        convo_ref[0] = xpad[q:q + SUBLANES, :]
        ho_ref[0] = h


def _lru(big, conv0, h0, lw, batch, seq, o_dtype):
    width = lw["w_c_proj"].shape[0]
    q = min(seq, SEQ_TM)
    nch = seq // q
    assert seq % q == 0 and q % min(q, LRU_CHUNK) == 0
    bd = width // C_BLOCKS
    conv0p = jnp.pad(conv0, ((0, 0), (SUBLANES - (C_CONV - 1), 0), (0, 0)))
    vec = lambda v: v.astype(F32).reshape(1, width)
    y, convo, ho = pl.pallas_call(
        functools.partial(_lru_kernel, q=q, nch=nch),
        grid=(batch, nch),
        in_specs=[pl.BlockSpec((q, width), lambda b, c: (b * nch + c, 5)),
                  pl.BlockSpec((q, width), lambda b, c: (b * nch + c, 4)),
                  pl.BlockSpec((1, SUBLANES, width), lambda b, c: (b, 0, 0)),
                  pl.BlockSpec((1, 1, width), lambda b, c: (b, 0, 0)),
                  pl.BlockSpec((C_CONV, width), lambda b, c: (0, 0)),
                  pl.BlockSpec((1, width), lambda b, c: (0, 0)),
                  pl.BlockSpec((C_BLOCKS, bd, bd), lambda b, c: (0, 0, 0)),
                  pl.BlockSpec((C_BLOCKS, bd, bd), lambda b, c: (0, 0, 0)),
                  pl.BlockSpec((1, width), lambda b, c: (0, 0)),
                  pl.BlockSpec((1, width), lambda b, c: (0, 0)),
                  pl.BlockSpec((1, width), lambda b, c: (0, 0))],
        out_specs=[pl.BlockSpec((q, width), lambda b, c: (b * nch + c, 0)),
                   pl.BlockSpec((1, SUBLANES, width), lambda b, c: (b, 0, 0)),
                   pl.BlockSpec((1, 1, width), lambda b, c: (b, 0, 0))],
        out_shape=[jax.ShapeDtypeStruct((batch * seq, width), o_dtype),
                   jax.ShapeDtypeStruct((batch, SUBLANES, width), F32),
                   jax.ShapeDtypeStruct((batch, 1, width), F32)],
        scratch_shapes=[pltpu.VMEM((SUBLANES + q, width), F32),
                        pltpu.VMEM((1, width), F32)],
        compiler_params=_cparams("parallel", "arbitrary"),
        name="lru",
    )(big, big, conv0p, h0.reshape(batch, 1, width), lw["conv_c_w"], vec(lw["conv_c_b"]),
      lw["w_rgate"].astype(BF16), lw["w_igate"].astype(BF16), vec(lw["b_rgate"]), vec(lw["b_igate"]),
      vec(lw["lru_lambda"]))
    return y, convo[:, SUBLANES - (C_CONV - 1):, :], ho.reshape(batch, width)


def _merge_kernel(x_ref, gt_ref, oa_ref, yb_ref, yc_ref, gates_ref, wa_ref, wb_ref, wc_ref, wo_ref,
                  out_ref):
    tm, d = x_ref.shape
    gt = _ld(gt_ref)
    sub = min(tm, MERGE_SUB)
    for r in range(tm // sub):
        rs = slice(r * sub, (r + 1) * sub)
        ya = _dot(oa_ref[rs, :].astype(BF16), wa_ref[...])
        yb = _dot(yb_ref[rs, :].astype(BF16), wb_ref[...])
        yc = _dot(yc_ref[rs, :].astype(BF16), wc_ref[...])
        sg = jax.nn.sigmoid(gates_ref[rs, :].astype(F32))
        mixed = sg[:, :d] * ya + sg[:, d:2 * d] * yb + sg[:, 2 * d:] * yc
        g = gt if gt.shape[0] == 1 else gt[rs, :]
        out_ref[rs, :] = x_ref[rs, :] + g * _dot(mixed.astype(BF16), wo_ref[...])


def _merge(x, mod, oa, yb, yc, big, lw):
    m, d = x.shape
    aw = A_WIDTH
    tm = _row_tile(m, mod.seq, mod.per_row, MERGE_TM)
    row = lambda wd, j=0: pl.BlockSpec((tm, wd), lambda i: (i, j))
    return pl.pallas_call(
        _merge_kernel,
        grid=(m // tm,),
        in_specs=[row(d), mod.spec(5, tm, 1), row(aw), row(yb.shape[1]), row(yc.shape[1]), row(3 * d, 0)]
        + [_resident(lw[k].shape) for k in ("w_a_proj", "w_b_proj", "w_c_proj", "w_out")],
        out_specs=row(d),
        out_shape=jax.ShapeDtypeStruct((m, d), F32),
        compiler_params=_cparams("parallel"),
        name="merge",
    )(x, mod.arr, oa, yb, yc, big,
      lw["w_a_proj"], lw["w_b_proj"], lw["w_c_proj"], lw["w_out"])


def _t5_bucket(dist):
    dist = np.asarray(dist)
    large = REL_MAX_EXACT + (np.log(np.maximum(dist, 1) / REL_MAX_EXACT)
                             / math.log(REL_MAX_DISTANCE / REL_MAX_EXACT)
                             * (REL_BUCKETS - REL_MAX_EXACT)).astype(np.int64)
    large = np.minimum(large, REL_BUCKETS - 1)
    return np.where(dist < REL_MAX_EXACT, dist, large).astype(np.int32)


def _group_bias(rel_bias, g):
    dil = A_GROUPS[g][1]
    buckets = _t5_bucket(np.arange(A_KEYS + 1) * dil)
    return rel_bias[buckets][:, g * A_HEADS:(g + 1) * A_HEADS].T.astype(F32)


def _prompt_bias_table(bias, g):
    _, step, wblk = _attn_plan(g)
    h = bias.shape[0]
    cols = (wblk + 1) * A_BLOCK
    n = cols + A_BLOCK - 1
    x = np.arange(n + 1)
    x = np.where(x < cols, x, x - (n + 1))
    dist = wblk * A_BLOCK - x
    valid = (dist >= 0) & (dist % step == 0) & (dist // step <= A_KEYS)
    v = jnp.where(valid[None], bias[:, np.clip(dist // step, 0, A_KEYS)], -jnp.inf)
    tab = jnp.tile(v, (1, A_BLOCK))[:, :A_BLOCK * n].reshape(h, A_BLOCK, n)[:, :, :cols]
    return tab.reshape(h // 2, 2 * A_BLOCK, cols)


def _sample_bias_tables(bias, wb, dil, t):
    h = bias.shape[0]
    front = t - 1 + wb - A_KEYS * dil
    assert front >= 0
    sparse = jnp.concatenate([bias[:, ::-1, None], jnp.full((h, A_KEYS + 1, dil - 1), -jnp.inf, F32)], axis=2)
    base = jnp.concatenate([jnp.full((h, front), -jnp.inf, F32), sparse.reshape(h, (A_KEYS + 1) * dil)], axis=1)
    tab_buf = jnp.stack([base[:, t - 1 - tq:t - 1 - tq + wb] for tq in range(t)], axis=1)
    tq = np.arange(t)[:, None]
    dist = tq - np.arange(LANES)[None, :]
    valid = (dist >= 0) & (dist % dil == 0) & (np.arange(LANES)[None, :] < t)
    tab_new = jnp.where(valid[None], bias[:, np.clip(dist // dil, 0, A_KEYS)], -jnp.inf)
    return tab_buf.reshape(h // 2, 2 * t, wb), tab_new.reshape(h // 2, 2 * t, LANES)


def _mixer_weights(w_in):
    na = len(A_GROUPS) * A_WIDTH
    d = w_in.shape[0]
    offs = np.cumsum([0, na, na, na, 1024, 1536, 16, 1024, 1024, 3 * d])
    seg = lambda i: w_in[:, offs[i]:offs[i + 1]]
    qa, ka, va, zb, xbc, dtb, xc, gc, gates = [seg(i) for i in range(9)]
    cols = []
    for g in range(len(A_GROUPS)):
        sl = slice(g * A_WIDTH, (g + 1) * A_WIDTH)
        cols += [qa[:, sl] * (A_HEAD_DIM ** -0.5), ka[:, sl], va[:, sl]]
    w_qkv = jnp.concatenate(cols + [dtb, jnp.zeros((d, DT_PAD - dtb.shape[1]), w_in.dtype)], axis=1)
    w_big = jnp.concatenate([gates, zb, gc, xc, xbc], axis=1)
    return w_qkv.astype(BF16), w_big.astype(BF16)


def _mixer(x, mod, lw, st, bias, prompt, batch, seq):
    act_dtype = BF16 if prompt else F32
    qkv = _proj(x, mod, lw["g_mix"], lw["w_qkv"], F32, QKV_TN, slabs=True)
    big = _proj(x, mod, lw["g_mix"], lw["w_big"], act_dtype, BIG_TN)
    ng = len(A_GROUPS)
    npair = A_WIDTH // LANES
    if prompt:
        oa, new_kv = _attn_prompt(qkv, bias, st["kvt"], st["layer"], st["depth"], batch, seq, act_dtype)
    else:
        oa = _attn_sample(qkv, st["kv"], st["layer"], bias, batch, seq)
        new_kv = [jnp.transpose(qkv[(3 * g + 1) * npair:(3 * g + 3) * npair], (1, 0, 2))
                  .reshape(batch, seq, 2, A_HEADS, A_HEAD_DIM) for g in range(ng)]
    dt_col = 3 * ng * npair
    yb, conv_b_new, ssm_new = _ssd(big, qkv, dt_col, st["conv_b"], st["ssm"], lw, batch, seq, act_dtype)
    yc, conv_c_new, lru_new = _lru(big, st["conv_c"], st["lru"], lw, batch, seq, act_dtype)
    x = _merge(x, mod, oa, yb, yc, big, lw)
    return x, (new_kv[0], new_kv[1], new_kv[2], conv_b_new, ssm_new, conv_c_new, lru_new)


def _block(x, mod, lw, st, bias, prompt, batch, seq, g_final):
    x = _ffn(x, mod, 0, lw["g_ff1"], lw["w_ff1_in"], lw["w_ff1_out"])
    x, new_st = _mixer(x, mod, lw, st, bias, prompt, batch, seq)
    x = _ffn(x, mod, 6, lw["g_ff2"], lw["w_ff2_in"], lw["w_ff2_out"], g_final)
    return x, new_st


@jax.jit
def _forward(x_prompt, x_sample, c_prompt, c_sample, cache_win1_kv, cache_win2_kv, cache_win3_kv,
             state_conv_b, state_ssm, state_conv_c, state_lru, rel_bias, w_ada, b_ada, g_ff1,
             w_ff1_in, w_ff1_out, g_mix, w_in, w_a_proj, conv_b_w, conv_b_b, dt_bias, a_log, d_skip,
             g_ssm_norm, w_b_proj, conv_c_w, conv_c_b, w_rgate, b_rgate, w_igate, b_igate, lru_lambda,
             w_c_proj, w_out, g_ff2, w_ff2_in, w_ff2_out, g_final):
    bp, lp, d = x_prompt.shape
    bs, ls, _ = x_sample.shape
    depth = w_ada.shape[0]
    caches = (cache_win1_kv, cache_win2_kv, cache_win3_kv)
    biases = [_group_bias(rel_bias, g) for g in range(len(A_GROUPS))]
    bias_p = [_prompt_bias_table(b, g) for g, b in enumerate(biases)]
    bias_s = [_sample_bias_tables(b, caches[g].shape[2], A_GROUPS[g][1], ls) for g, b in enumerate(biases)]

    yp = x_prompt.reshape(bp * lp, d)
    ys = x_sample.reshape(bs * ls, d)
    c_all = jnp.concatenate([c_prompt, c_sample], axis=0)
    new_p = [[] for _ in range(7)]
    new_s = [[] for _ in range(7)]
    kvt_p = None
    for l in range(depth):
        w_qkv, w_big = _mixer_weights(w_in[l])
        lw = dict(g_ff1=g_ff1[l], w_ff1_in=w_ff1_in[l].astype(BF16), w_ff1_out=w_ff1_out[l].astype(BF16),
                  g_mix=g_mix[l], w_qkv=w_qkv, w_big=w_big, w_a_proj=w_a_proj[l].astype(BF16),
                  conv_b_w=conv_b_w[l], conv_b_b=conv_b_b[l], dt_bias=dt_bias[l], a_log=a_log[l],
                  d_skip=d_skip[l], g_ssm_norm=g_ssm_norm[l], w_b_proj=w_b_proj[l].astype(BF16),
                  conv_c_w=conv_c_w[l], conv_c_b=conv_c_b[l], w_rgate=w_rgate[l], b_rgate=b_rgate[l],
                  w_igate=w_igate[l], b_igate=b_igate[l], lru_lambda=lru_lambda[l],
                  w_c_proj=w_c_proj[l].astype(BF16), w_out=w_out[l].astype(BF16),
                  g_ff2=g_ff2[l], w_ff2_in=w_ff2_in[l].astype(BF16), w_ff2_out=w_ff2_out[l].astype(BF16))
        mod_all = _ada(c_all, w_ada, b_ada, l)
        mod_p = _Mod(mod_all[:bp], bp, lp, per_row=False)
        mod_s = _Mod(mod_all[bp:], bs, ls, per_row=True)
        gf = g_final if l == depth - 1 else None
        st_p = dict(kvt=kvt_p, layer=l, depth=depth,
                    conv_b=jnp.zeros((bp, B_CONV - 1, conv_b_w.shape[2]), F32),
                    ssm=jnp.zeros((bp,) + state_ssm.shape[2:], F32),
                    conv_c=jnp.zeros((bp, C_CONV - 1, conv_c_w.shape[2]), F32),
                    lru=jnp.zeros((bp, state_lru.shape[2]), F32))
        st_s = dict(kv=caches, layer=l, conv_b=state_conv_b[l], ssm=state_ssm[l],
                    conv_c=state_conv_c[l], lru=state_lru[l])
        yp, stp = _block(yp, mod_p, lw, st_p, bias_p, True, bp, lp, gf)
        ys, sts = _block(ys, mod_s, lw, st_s, bias_s, False, bs, ls, gf)
        kvt_p = list(stp[:3])
        for i in range(7):
            new_p[i].append(stp[i])
            new_s[i].append(sts[i])
    outs_p = [jnp.transpose(v, (0, 1, 5, 2, 3, 4)) for v in kvt_p] + [jnp.stack(v, 0) for v in new_p[3:]]
    outs_s = [jnp.stack(v, 0) for v in new_s]
    return (yp.reshape(bp, lp, d), ys.reshape(bs, ls, d), *outs_p, *outs_s)


def kernel(x_prompt, x_sample, c_prompt, c_sample, cache_win1_kv, cache_win2_kv, cache_win3_kv,
           state_conv_b, state_ssm, state_conv_c, state_lru, rel_bias, w_ada, b_ada, g_ff1,
           w_ff1_in, w_ff1_out, g_mix, w_in, w_a_proj, conv_b_w, conv_b_b, dt_bias, a_log, d_skip,
           g_ssm_norm, w_b_proj, conv_c_w, conv_c_b, w_rgate, b_rgate, w_igate, b_igate, lru_lambda,
           w_c_proj, w_out, g_ff2, w_ff2_in, w_ff2_out, g_final):
    return _forward(x_prompt, x_sample, c_prompt, c_sample, cache_win1_kv, cache_win2_kv,
                    cache_win3_kv, state_conv_b, state_ssm, state_conv_c, state_lru, rel_bias, w_ada,
                    b_ada, g_ff1, w_ff1_in, w_ff1_out, g_mix, w_in, w_a_proj, conv_b_w, conv_b_b,
                    dt_bias, a_log, d_skip, g_ssm_norm, w_b_proj, conv_c_w, conv_c_b, w_rgate,
                    b_rgate, w_igate, b_igate, lru_lambda, w_c_proj, w_out, g_ff2, w_ff2_in,
                    w_ff2_out, g_final)
```

```python
import functools
import math

import jax
import jax.numpy as jnp
import numpy as np
from jax import lax
from jax.experimental import pallas as pl
from jax.experimental.pallas import tpu as pltpu

F32 = jnp.float32
BF16 = jnp.bfloat16

NORM_EPS = 1e-6
N_MOD = 9
A_GROUPS = ((128, 1), (512, 4), (2048, 16))
A_HEADS = 8
A_HEAD_DIM = 64
A_WIDTH = A_HEADS * A_HEAD_DIM
A_KEYS = 128
A_BLOCK = 128
REL_BUCKETS = 32
REL_MAX_EXACT = 16
REL_MAX_DISTANCE = 2048
B_HEAD_DIM = 64
B_GROUPS = 2
B_STATE = 128
B_CONV = 4
C_BLOCKS = 8
C_CONV = 4
C_POW = 8.0

LANES = 128
SUBLANES = 8
SSD_CHUNK = 128
FFN_TM = 1024
PROJ_TM = 2048
PROJ_SUB = 512
MERGE_TM = 1024
MERGE_SUB = 256
SEQ_TM = 2048
LRU_CHUNK = 128
QKV_TN = 1024
BIG_TN = 1536
DT_PAD = 512
ATTN_UNROLL = 8
ATTN_UNROLL_REST = 12
ATTN_MAX_STRIDE = 4
VMEM_LIMIT = 56 * 1024 * 1024


def _cparams(*sem):
    return pltpu.CompilerParams(dimension_semantics=sem, vmem_limit_bytes=VMEM_LIMIT)


def _resident(shape):
    nd = len(shape)
    return pl.BlockSpec(shape, lambda *_: (0,) * nd, pipeline_mode=pl.Buffered(1))


def _ld(ref):
    return ref[0] if len(ref.shape) == 3 else ref[...]


def _norm_mod(x, g, sc, sh):
    ms = jnp.mean(x * x, axis=-1, keepdims=True)
    return (x * lax.rsqrt(ms + NORM_EPS) * g) * (1.0 + sc) + sh


def _softplus(x):
    return jnp.maximum(x, 0.0) + jnp.log1p(jnp.exp(-jnp.abs(x)))


def _causal_conv(xpad, w_ref, b_ref, q, start=0):
    taps = w_ref.shape[0]
    assert taps - 1 <= SUBLANES and start % SUBLANES == 0
    xe = xpad[start:start + SUBLANES + q, :]
    w = w_ref[...]
    acc = xe * w[0:1, :]
    for k in range(1, taps):
        acc = xe * w[k:k + 1, :] + pltpu.roll(acc, 1, axis=0)
    return acc[SUBLANES:SUBLANES + q] + b_ref[...]


def _split_bf16(v, n):
    parts = []
    r = v
    for _ in range(n):
        p = r.astype(BF16)
        parts.append(p)
        r = r - p.astype(F32)
    return parts


def _dot(a, b):
    return jnp.dot(a, b, preferred_element_type=F32)


def _dot_nt(a, b):
    return lax.dot_general(a, b, (((1,), (1,)), ((), ())), preferred_element_type=F32)


def _ada_kernel(c_ref, w_ref, b_ref, o_ref):
    c = c_ref[...]
    a = (c * jax.nn.sigmoid(c)).astype(BF16)
    o_ref[...] = _dot(a, w_ref[...].astype(BF16)) + b_ref[...]


def _ada(c, w, b, layer):
    rows, d = c.shape
    depth, _, n = w.shape
    tn = 1024
    return pl.pallas_call(
        _ada_kernel,
        grid=(n // tn,),
        in_specs=[pl.BlockSpec((rows, d), lambda j: (0, 0)),
                  pl.BlockSpec((None, d, tn), lambda j: (layer, 0, j)),
                  pl.BlockSpec((None, 1, tn), lambda j: (layer, 0, j))],
        out_specs=pl.BlockSpec((rows, tn), lambda j: (0, j)),
        out_shape=jax.ShapeDtypeStruct((rows, n), F32),
        compiler_params=_cparams("parallel"),
        name="ada",
    )(c, w, b.reshape(depth, 1, n))


class _Mod:
    def __init__(self, mod, batch, seq, per_row):
        d = mod.shape[1] // N_MOD
        self.d = d
        self.seq = seq
        self.per_row = per_row
        if per_row:
            self.arr = jnp.repeat(mod, seq, axis=0)
        else:
            self.arr = mod.reshape(batch * N_MOD, 1, d)

    def spec(self, k, tm, grid_rank):
        d = self.d
        if self.per_row:
            if grid_rank == 1:
                return pl.BlockSpec((tm, d), lambda i: (i, k))
            return pl.BlockSpec((tm, d), lambda i, j: (i, k))
        per = self.seq // tm
        if grid_rank == 1:
            return pl.BlockSpec((1, 1, d), lambda i: ((i // per) * N_MOD + k, 0, 0))
        return pl.BlockSpec((1, 1, d), lambda i, j: ((i // per) * N_MOD + k, 0, 0))


def _row_tile(m, seq, per_row, want):
    tm = min(want, m if per_row else seq)
    assert m % tm == 0 and (per_row or seq % tm == 0)
    return tm


def _ffn_kernel(x_ref, sh_ref, sc_ref, gt_ref, gn_ref, win_ref, wout_ref, *rest, d_ff, fc, final):
    if final:
        gf_ref, o_ref, acc_ref = rest
    else:
        o_ref, acc_ref = rest
    x = x_ref[...]
    h = _norm_mod(x, gn_ref[...], _ld(sc_ref), _ld(sh_ref)).astype(BF16)
    for c in range(d_ff // fc):
        u = _dot(h, win_ref[:, c * fc:(c + 1) * fc])
        v = _dot(h, win_ref[:, d_ff + c * fc:d_ff + (c + 1) * fc])
        a = (u * jax.nn.sigmoid(u) * v).astype(BF16)
        part = _dot(a, wout_ref[c * fc:(c + 1) * fc, :])
        if c == 0:
            acc_ref[...] = part
        else:
            acc_ref[...] += part
    y = x + 0.5 * _ld(gt_ref) * acc_ref[...]
    if final:
        ms = jnp.mean(y * y, axis=-1, keepdims=True)
        y = y * lax.rsqrt(ms + NORM_EPS) * gf_ref[...]
    o_ref[...] = y


def _ffn(x, mod, k0, gn, w_in, w_out, g_final=None):
    m, d = x.shape
    d_ff = w_out.shape[0]
    tm = _row_tile(m, mod.seq, mod.per_row, FFN_TM)
    final = g_final is not None
    in_specs = [pl.BlockSpec((tm, d), lambda i: (i, 0)),
                mod.spec(k0, tm, 1), mod.spec(k0 + 1, tm, 1), mod.spec(k0 + 2, tm, 1),
                _resident((1, d)), _resident(w_in.shape), _resident(w_out.shape)]
    args = [x, mod.arr, mod.arr, mod.arr, gn.reshape(1, d), w_in, w_out]
    if final:
        in_specs.append(_resident((1, d)))
        args.append(g_final.reshape(1, d))
    return pl.pallas_call(
        functools.partial(_ffn_kernel, d_ff=d_ff, fc=256, final=final),
        grid=(m // tm,),
        in_specs=in_specs,
        out_specs=pl.BlockSpec((tm, d), lambda i: (i, 0)),
        out_shape=jax.ShapeDtypeStruct((m, d), F32),
        scratch_shapes=[pltpu.VMEM((tm, d), F32)],
        compiler_params=_cparams("parallel"),
        name="ffn",
    )(*args)


def _proj_kernel(x_ref, sh_ref, sc_ref, gn_ref, w_ref, o_ref, h_ref):
    @pl.when(pl.program_id(1) == 0)
    def _():
        h_ref[...] = _norm_mod(x_ref[...], gn_ref[...], _ld(sc_ref), _ld(sh_ref)).astype(BF16)

    tm = h_ref.shape[0]
    sub = min(tm, PROJ_SUB)
    for r in range(tm // sub):
        rs = slice(r * sub, (r + 1) * sub)
        res = _dot(h_ref[rs, :], w_ref[...]).astype(o_ref.dtype)
        if len(o_ref.shape) == 2:
            o_ref[rs, :] = res
        else:
            for s in range(o_ref.shape[0]):
                o_ref[s, rs, :] = res[:, s * LANES:(s + 1) * LANES]


def _proj(x, mod, gn, w, out_dtype, tn, slabs=False):
    m, d = x.shape
    n = w.shape[1]
    tm = _row_tile(m, mod.seq, mod.per_row, PROJ_TM)
    if slabs:
        out_spec = pl.BlockSpec((tn // LANES, tm, LANES), lambda i, j: (j, i, 0))
        out_shape = jax.ShapeDtypeStruct((n // LANES, m, LANES), out_dtype)
    else:
        out_spec = pl.BlockSpec((tm, tn), lambda i, j: (i, j))
        out_shape = jax.ShapeDtypeStruct((m, n), out_dtype)
    return pl.pallas_call(
        _proj_kernel,
        grid=(m // tm, n // tn),
        in_specs=[pl.BlockSpec((tm, d), lambda i, j: (i, 0)),
                  mod.spec(3, tm, 2), mod.spec(4, tm, 2),
                  pl.BlockSpec((1, d), lambda i, j: (0, 0)),
                  pl.BlockSpec((d, tn), lambda i, j: (0, j))],
        out_specs=out_spec,
        out_shape=out_shape,
        scratch_shapes=[pltpu.VMEM((tm, d), BF16)],
        compiler_params=_cparams("parallel", "arbitrary"),
        name="proj",
    )(x, mod.arr, mod.arr, gn.reshape(1, d), w)


def _pair_softmax(q2, k2, v2, bias2, lo, transposed=False):
    rows = q2.shape[0]
    zero = jnp.zeros_like(q2)
    qs = jnp.concatenate([jnp.where(lo, q2, zero), jnp.where(lo, zero, q2)], axis=0).astype(BF16)
    s = (_dot(qs, k2) if transposed else _dot_nt(qs, k2)) + bias2
    mx = jnp.max(s, axis=-1, keepdims=True)
    p = jnp.exp(s - mx)
    l = jnp.sum(p, axis=-1, keepdims=True)
    pb = p.astype(BF16)
    o = _dot_nt(pb, v2) if transposed else _dot(pb, v2)
    return (jnp.where(lo, o[:rows], o[rows:]), jnp.where(lo, mx[:rows], mx[rows:]),
            jnp.where(lo, l[:rows], l[rows:]))


def _merge_softmax(acc, new):
    ao, am, al = acc
    o, m, l = new
    mn = jnp.maximum(am, m)
    a1 = jnp.exp(am - mn)
    a2 = jnp.exp(m - mn)
    return ao * a1 + o * a2, mn, al * a1 + l * a2


def _attn_plan(g):
    dil = A_GROUPS[g][1]
    if dil <= ATTN_MAX_STRIDE:
        return dil, 1, A_KEYS // A_BLOCK
    assert dil % ATTN_MAX_STRIDE == 0
    step = dil // ATTN_MAX_STRIDE
    return ATTN_MAX_STRIDE, step, step * A_KEYS // A_BLOCK


def _attn_prompt_kernel(*refs, seq, n_alias):
    ng = len(A_GROUPS)
    qkv = refs[:3 * ng]
    bias_refs = refs[3 * ng:4 * ng]
    o_ref = refs[4 * ng + n_alias]
    kvt_refs = refs[4 * ng + 1 + n_alias:5 * ng + 1 + n_alias]
    acc_o, acc_m, acc_l = refs[-3:]
    blk = A_BLOCK

    for g in range(ng):
        keep = kvt_refs[g].shape[-1]
        for kv in range(2):
            src = qkv[3 * g + 1 + kv]
            for c in range(keep // blk):
                t = src[pl.ds(seq - keep + c * blk, blk), :].T
                for i in range(2):
                    kvt_refs[g][kv, i, :, c * blk:(c + 1) * blk] = t[i * A_HEAD_DIM:(i + 1) * A_HEAD_DIM]

    hp = pl.program_id(1)
    lo = lax.broadcasted_iota(jnp.int32, (blk, LANES), 1) < A_HEAD_DIM

    def tiles(g, starts, kw):
        stride = _attn_plan(g)[0]
        q_ref, k_ref, v_ref = qkv[3 * g:3 * g + 3]
        bias_ref = bias_refs[g]
        wcols = bias_ref.shape[-1]

        def rows(start, n):
            return pl.ds(start, n) if stride == 1 else pl.ds(start, n, stride=stride)

        news = []
        for qstart, kstart in starts:
            q2 = q_ref[rows(qstart, blk), :]
            k2 = k_ref[rows(kstart, kw), :].astype(BF16)
            v2 = v_ref[rows(kstart, kw), :].astype(BF16)
            news.append(_pair_softmax(q2, k2, v2, bias_ref[hp, :, wcols - kw:wcols], lo))
        for (qstart, _), new in zip(starts, news):
            sel = rows(qstart, blk)
            if g > 0:
                new = _merge_softmax((acc_o[sel, :], acc_m[sel, :], acc_l[sel, :]), new)
            acc_o[sel, :] = new[0]
            acc_m[sel, :] = new[1]
            acc_l[sel, :] = new[2]

    for g in range(ng):
        stride, _, wblk = _attn_plan(g)
        assert stride <= ATTN_UNROLL
        nb = seq // stride // blk
        span = blk * stride
        for b in range(min(wblk, nb)):
            tiles(g, [(r + b * span, r) for r in range(stride)], (b + 1) * blk)
        nfull = nb - wblk
        if nfull > 0:
            ub = max(u for u in range(1, nfull + 1) if nfull % u == 0 and u * stride <= ATTN_UNROLL_REST)

            def full(i, carry, g=g, span=span, stride=stride, ub=ub, wblk=wblk):
                b = wblk + i * ub
                tiles(g, [(r + (b + k) * span, r + (b + k - wblk) * span)
                          for k in range(ub) for r in range(stride)], (wblk + 1) * blk)
                return carry

            lax.fori_loop(0, nfull // ub, full, 0)
    o_ref[...] = (acc_o[...] / acc_l[...]).astype(o_ref.dtype)


def _attn_prompt(qkv, bias_tab, kvt_prev, layer, depth, batch, seq, o_dtype):
    npair = A_WIDTH // LANES
    ng = len(A_GROUPS)
    col = lambda j: pl.BlockSpec((None, seq, LANES), lambda b, h: (j * npair + h, b, 0))
    keeps = [min(win, seq) for win, _ in A_GROUPS]
    n_alias = 0 if kvt_prev is None else ng
    alias_specs = [pl.BlockSpec(memory_space=pl.ANY)] * n_alias
    alias_args = [] if kvt_prev is None else list(kvt_prev)
    n_in = 4 * ng
    outs = pl.pallas_call(
        functools.partial(_attn_prompt_kernel, seq=seq, n_alias=n_alias),
        grid=(batch, npair),
        in_specs=[col(j) for j in range(3 * ng)]
        + [pl.BlockSpec(tab.shape, lambda b, h: (0, 0, 0)) for tab in bias_tab] + alias_specs,
        out_specs=[pl.BlockSpec((seq, LANES), lambda b, h: (b, h))]
        + [pl.BlockSpec((None, None, 2, 2, A_HEAD_DIM, keep), lambda b, h: (layer, b, 0, h, 0, 0))
           for keep in keeps],
        out_shape=[jax.ShapeDtypeStruct((batch * seq, A_WIDTH), o_dtype)]
        + [jax.ShapeDtypeStruct((depth, batch, 2, A_HEADS, A_HEAD_DIM, keep), F32) for keep in keeps],
        input_output_aliases={n_in + g: 1 + g for g in range(n_alias)},
        scratch_shapes=[pltpu.VMEM((seq, LANES), F32)] * 3,
        compiler_params=_cparams("parallel", "parallel"),
        name="attn_prompt",
    )(*([qkv] * (3 * ng)), *bias_tab, *alias_args)
    return outs[0], list(outs[1:])


def _attn_sample_kernel(qkv_ref, buf1_ref, buf2_ref, buf3_ref, bb1, bn1, bb2, bn2, bb3, bn3,
                        o_ref, knew_scr, vnew_scr):
    t = qkv_ref.shape[1]
    npair = A_HEADS // 2
    bufs = (buf1_ref, buf2_ref, buf3_ref)
    bias = ((bb1, bn1), (bb2, bn2), (bb3, bn3))
    lo = lax.broadcasted_iota(jnp.int32, (t, LANES), 1) < A_HEAD_DIM
    knew_scr[...] = jnp.zeros_like(knew_scr)
    vnew_scr[...] = jnp.zeros_like(vnew_scr)
    for hp in range(npair):
        cols = slice(hp * LANES, (hp + 1) * LANES)
        acc = None
        for g in range(len(A_GROUPS)):
            q2 = qkv_ref[3 * g * npair + hp]
            wb = bufs[g].shape[-1]
            kb = bufs[g][0, 2 * hp:2 * hp + 2].reshape(LANES, wb).astype(BF16)
            vb = bufs[g][1, 2 * hp:2 * hp + 2].reshape(LANES, wb).astype(BF16)
            knew_scr[g, hp, 0:t, :] = qkv_ref[(3 * g + 1) * npair + hp]
            vnew_scr[g, hp, 0:t, :] = qkv_ref[(3 * g + 2) * npair + hp]
            kn = knew_scr[g, hp].astype(BF16)
            vn = vnew_scr[g, hp].astype(BF16)
            bbuf, bnew = bias[g]
            new = _pair_softmax(q2, kb, vb, bbuf[hp], lo, transposed=True)
            new = _merge_softmax(new, _pair_softmax(q2, kn, vn, bnew[hp], lo))
            acc = new if acc is None else _merge_softmax(acc, new)
        o_ref[:, cols] = acc[0] / acc[2]


def _attn_sample(qkv, caches, layer, bias_s, batch, t):
    aw = A_WIDTH
    ng = len(A_GROUPS)
    bufs = [jnp.transpose(c, (0, 1, 3, 4, 5, 2)) for c in caches]
    buf_specs = [pl.BlockSpec((None, None) + bv.shape[2:], lambda b: (layer, b, 0, 0, 0, 0)) for bv in bufs]
    tabs = [tab for pair in bias_s for tab in pair]
    return pl.pallas_call(
        _attn_sample_kernel,
        grid=(batch,),
        in_specs=[pl.BlockSpec((3 * ng * aw // LANES, t, LANES), lambda b: (0, b, 0))] + buf_specs
        + [pl.BlockSpec(tab.shape, lambda b: (0, 0, 0)) for tab in tabs],
        out_specs=pl.BlockSpec((t, aw), lambda b: (b, 0)),
        out_shape=jax.ShapeDtypeStruct((batch * t, aw), F32),
        scratch_shapes=[pltpu.VMEM((ng, aw // LANES, LANES, LANES), F32)] * 2,
        compiler_params=_cparams("parallel"),
        name="attn_sample",
    )(qkv, *bufs, *tabs)


def _ssd_kernel(xbc_ref, z_ref, dt_ref, conv0_ref, st0_ref, cw_ref, cb_ref, dtb_ref, aneg_ref,
                dsk_ref, gn_ref, e_ref, y_ref, convo_ref, sto_ref, xpad, st_t, y_scr, *, lv, nsub, nch):
    q = SSD_CHUNK
    c = pl.program_id(1)
    width = y_ref.shape[1]
    nst = B_STATE
    hpg = width // B_HEAD_DIM // B_GROUPS // 2

    @pl.when(c == 0)
    def _():
        xpad[...] = jnp.zeros_like(xpad)
        xpad[0:SUBLANES, :] = conv0_ref[0]
        for j in range(width // LANES):
            st_t[:, j * LANES:(j + 1) * LANES] = st0_ref[0, j * LANES:(j + 1) * LANES, :].T

    rows = nsub * lv

    @pl.when(c > 0)
    def _():
        xpad[0:SUBLANES, :] = xpad[rows:rows + SUBLANES, :]

    xpad[SUBLANES:SUBLANES + rows, :] = xbc_ref[...].astype(F32)

    for k in range(nsub):
        rs = slice(k * lv, (k + 1) * lv)
        conv = _causal_conv(xpad, cw_ref, cb_ref, q, start=k * q)
        act = conv * jax.nn.sigmoid(conv)
        xs = act[:, :width]
        bm = [act[:, width + gi * nst:width + (gi + 1) * nst] for gi in range(B_GROUPS)]
        cm = [act[:, width + (B_GROUPS + gi) * nst:width + (B_GROUPS + gi + 1) * nst] for gi in range(B_GROUPS)]

        dt_raw = dt_ref[rs, :]
        if lv < q:
            dt_raw = jnp.concatenate([dt_raw, jnp.zeros((q - lv, LANES), F32)], axis=0)
        dt = _softplus(dt_raw + dtb_ref[...])
        if lv < q:
            dt = jnp.where(lax.broadcasted_iota(jnp.int32, (q, LANES), 0) < lv, dt, 0.0)
        dta = dt * aneg_ref[...]

        row = lax.broadcasted_iota(jnp.int32, (q, q), 0)
        col = lax.broadcasted_iota(jnp.int32, (q, q), 1)
        causal = row >= col
        tri = jnp.where(causal, 1.0, 0.0).astype(BF16)
        acum = sum(_dot(tri, p) for p in _split_bf16(dta, 3))
        acum_t = acum.T
        e = e_ref[...]
        dt_x = sum(_dot(p, e) for p in _split_bf16(dt, 2))
        acum_x = sum(_dot(p, e) for p in _split_bf16(acum, 2))
        last_x = acum_x[q - 1:q, :]
        ea_x = jnp.exp(acum_x)
        xdt = xs * dt_x
        xdt_te = (xdt * jnp.exp(last_x - acum_x)).astype(BF16)
        xdt_b = xdt.astype(BF16)
        chunk_decay = jnp.exp(last_x)
        lo = lax.broadcasted_iota(jnp.int32, (q, LANES), 1) < B_HEAD_DIM

        for gi in range(B_GROUPS):
            cmb = cm[gi].astype(BF16)
            cb = _dot_nt(cmb, bm[gi].astype(BF16))
            bm_t = bm[gi].T.astype(BF16)
            for hp in range(gi * hpg, (gi + 1) * hpg):
                cols = slice(hp * LANES, (hp + 1) * LANES)
                ax = acum_x[:, cols]
                ax_r = pltpu.roll(ax, B_HEAD_DIM, axis=1)
                ys = []
                for col_v, h in ((jnp.where(lo, ax, ax_r), 2 * hp), (jnp.where(lo, ax_r, ax), 2 * hp + 1)):
                    seg = col_v - acum_t[h:h + 1, :]
                    dec = jnp.exp(jnp.where(causal, seg, -jnp.inf))
                    ys.append(_dot((cb * dec).astype(BF16), xdt_b[:, cols]))
                st_old = st_t[:, cols]
                y_off = _dot(cmb, st_old.astype(BF16)) * ea_x[:, cols]
                st_t[:, cols] = st_old * chunk_decay[:, cols] + _dot(bm_t, xdt_te[:, cols])
                y_scr[k, :, cols] = jnp.where(lo, ys[0], ys[1]) + y_off + dsk_ref[:, cols] * xs[:, cols]

        z = z_ref[rs, :].astype(F32)
        if lv < q:
            z = jnp.concatenate([z, jnp.zeros((q - lv, width), F32)], axis=0)
        y = y_scr[k] * (z * jax.nn.sigmoid(z))
        gw = width // B_GROUPS
        for gi in range(B_GROUPS):
            yg = y[:, gi * gw:(gi + 1) * gw]
            ms = jnp.mean(yg * yg, axis=-1, keepdims=True)
            yn = yg * lax.rsqrt(ms + NORM_EPS) * gn_ref[:, gi * gw:(gi + 1) * gw]
            y_ref[rs, gi * gw:(gi + 1) * gw] = yn[0:lv].astype(y_ref.dtype)

    @pl.when(c == nch - 1)
    def _():
        convo_ref[0] = xpad[rows:rows + SUBLANES, :]
        for j in range(width // LANES):
            sto_ref[0, j * LANES:(j + 1) * LANES, :] = st_t[:, j * LANES:(j + 1) * LANES].T


def _ssd(big, qkv, dt_col, conv0, st0, lw, batch, seq, o_dtype):
    width = lw["w_b_proj"].shape[0]
    cch = lw["conv_b_w"].shape[1]
    nheads = width // B_HEAD_DIM
    lv = min(seq, SSD_CHUNK)
    nsub = min(seq, SEQ_TM) // lv if lv == SSD_CHUNK else 1
    rows = lv * nsub
    nch = seq // rows
    assert seq % rows == 0 and (lv == SSD_CHUNK or nch == 1)
    pad = lambda v: jnp.pad(v.astype(F32), (0, LANES - nheads)).reshape(1, LANES)
    e = (np.arange(LANES)[:, None] == (np.arange(width)[None, :] // B_HEAD_DIM)).astype(np.float32)
    conv0p = jnp.pad(conv0, ((0, 0), (SUBLANES - (B_CONV - 1), 0), (0, 0)))
    y, convo, sto = pl.pallas_call(
        functools.partial(_ssd_kernel, lv=lv, nsub=nsub, nch=nch),
        grid=(batch, nch),
        in_specs=[pl.BlockSpec((rows, cch), lambda b, c: (b * nch + c, 4)),
                  pl.BlockSpec((rows, width), lambda b, c: (b * nch + c, 3)),
                  pl.BlockSpec((None, rows, LANES), lambda b, c: (dt_col, b * nch + c, 0)),
                  pl.BlockSpec((1, SUBLANES, cch), lambda b, c: (b, 0, 0)),
                  pl.BlockSpec((1, width, B_STATE), lambda b, c: (b, 0, 0)),
                  pl.BlockSpec((B_CONV, cch), lambda b, c: (0, 0)),
                  pl.BlockSpec((1, cch), lambda b, c: (0, 0)),
                  pl.BlockSpec((1, LANES), lambda b, c: (0, 0)),
                  pl.BlockSpec((1, LANES), lambda b, c: (0, 0)),
                  pl.BlockSpec((1, width), lambda b, c: (0, 0)),
                  pl.BlockSpec((1, width), lambda b, c: (0, 0)),
                  pl.BlockSpec((LANES, width), lambda b, c: (0, 0))],
        out_specs=[pl.BlockSpec((rows, width), lambda b, c: (b * nch + c, 0)),
                   pl.BlockSpec((1, SUBLANES, cch), lambda b, c: (b, 0, 0)),
                   pl.BlockSpec((1, width, B_STATE), lambda b, c: (b, 0, 0))],
        out_shape=[jax.ShapeDtypeStruct((batch * seq, width), o_dtype),
                   jax.ShapeDtypeStruct((batch, SUBLANES, cch), F32),
                   jax.ShapeDtypeStruct((batch, width, B_STATE), F32)],
        scratch_shapes=[pltpu.VMEM((SUBLANES + nsub * SSD_CHUNK, cch), F32),
                        pltpu.VMEM((B_STATE, width), F32),
                        pltpu.VMEM((nsub, SSD_CHUNK, width), F32)],
        compiler_params=_cparams("parallel", "arbitrary"),
        name="ssd",
    )(big, big, qkv, conv0p, st0.reshape(batch, width, B_STATE),
      lw["conv_b_w"], lw["conv_b_b"].reshape(1, cch), pad(lw["dt_bias"]),
      pad(-jnp.exp(lw["a_log"].astype(F32))),
      jnp.repeat(lw["d_skip"].astype(F32), B_HEAD_DIM).reshape(1, width),
      lw["g_ssm_norm"].reshape(1, width), jnp.asarray(e, BF16))
    return (y, convo[:, SUBLANES - (B_CONV - 1):, :],
            sto.reshape(batch, nheads, B_HEAD_DIM, B_STATE))


def _lru_chunk(x, h, wr_ref, wi_ref, br_ref, bi_ref, lam_ref):
    q, width = x.shape
    bd = width // C_BLOCKS
    rs, gs = [], []
    for j in range(C_BLOCKS):
        xb = x[:, j * bd:(j + 1) * bd].astype(BF16)
        rs.append(_dot(xb, wr_ref[j]))
        gs.append(_dot(xb, wi_ref[j]))
    rg = jax.nn.sigmoid(jnp.concatenate(rs, axis=1) + br_ref[...])
    ig = jax.nn.sigmoid(jnp.concatenate(gs, axis=1) + bi_ref[...])
    log_a = -C_POW * rg * _softplus(-lam_ref[...])
    a = jnp.exp(log_a)
    u = x * ig * jnp.sqrt(-jnp.tanh(log_a) * (a * a + 1.0))

    rowi = lax.broadcasted_iota(jnp.int32, (SUBLANES, width), 0)
    hs = []
    for g in range(q // SUBLANES):
        ag = a[g * SUBLANES:(g + 1) * SUBLANES]
        bg = u[g * SUBLANES:(g + 1) * SUBLANES]
        s = 1
        while s < SUBLANES:
            keep = rowi >= s
            a_sh = jnp.where(keep, pltpu.roll(ag, s, axis=0), 1.0)
            b_sh = jnp.where(keep, pltpu.roll(bg, s, axis=0), 0.0)
            bg = ag * b_sh + bg
            ag = ag * a_sh
            s *= 2
        hg = bg + ag * h
        hs.append(hg)
        h = hg[SUBLANES - 1:SUBLANES]
    return (hs[0] if len(hs) == 1 else jnp.concatenate(hs, axis=0)), h


def _lru_kernel(xc_ref, gc_ref, conv0_ref, h0_ref, cw_ref, cb_ref, wr_ref, wi_ref, br_ref, bi_ref,
                lam_ref, y_ref, convo_ref, ho_ref, xpad, hprev, *, q, nch):
    c = pl.program_id(1)
    width = y_ref.shape[1]

    @pl.when(c == 0)
    def _():
        xpad[0:SUBLANES, :] = conv0_ref[0]
        hprev[...] = h0_ref[0]

    @pl.when(c > 0)
    def _():
        xpad[0:SUBLANES, :] = xpad[q:q + SUBLANES, :]

    xpad[SUBLANES:SUBLANES + q, :] = xc_ref[...].astype(F32)
    sub = min(q, LRU_CHUNK)
    h = hprev[...]
    for k in range(q // sub):
        rs = slice(k * sub, (k + 1) * sub)
        x = _causal_conv(xpad, cw_ref, cb_ref, sub, start=k * sub)
        hs, h = _lru_chunk(x, h, wr_ref, wi_ref, br_ref, bi_ref, lam_ref)
        y_ref[rs, :] = (hs * jax.nn.gelu(gc_ref[rs, :].astype(F32))).astype(y_ref.dtype)
    hprev[...] = h

    @pl.when(c == nch - 1)
    def _():
        convo_ref[0] = xpad[q:q + SUBLANES, :]
        ho_ref[0] = h


def _lru(big, conv0, h0, lw, batch, seq, o_dtype):
    width = lw["w_c_proj"].shape[0]
    q = min(seq, SEQ_TM)
    nch = seq // q
    assert seq % q == 0 and q % min(q, LRU_CHUNK) == 0
    bd = width // C_BLOCKS
    conv0p = jnp.pad(conv0, ((0, 0), (SUBLANES - (C_CONV - 1), 0), (0, 0)))
    vec = lambda v: v.astype(F32).reshape(1, width)
    y, convo, ho = pl.pallas_call(
        functools.partial(_lru_kernel, q=q, nch=nch),
        grid=(batch, nch),
        in_specs=[pl.BlockSpec((q, width), lambda b, c: (b * nch + c, 5)),
                  pl.BlockSpec((q, width), lambda b, c: (b * nch + c, 4)),
                  pl.BlockSpec((1, SUBLANES, width), lambda b, c: (b, 0, 0)),
                  pl.BlockSpec((1, 1, width), lambda b, c: (b, 0, 0)),
                  pl.BlockSpec((C_CONV, width), lambda b, c: (0, 0)),
                  pl.BlockSpec((1, width), lambda b, c: (0, 0)),
                  pl.BlockSpec((C_BLOCKS, bd, bd), lambda b, c: (0, 0, 0)),
                  pl.BlockSpec((C_BLOCKS, bd, bd), lambda b, c: (0, 0, 0)),
                  pl.BlockSpec((1, width), lambda b, c: (0, 0)),
                  pl.BlockSpec((1, width), lambda b, c: (0, 0)),
                  pl.BlockSpec((1, width), lambda b, c: (0, 0))],
        out_specs=[pl.BlockSpec((q, width), lambda b, c: (b * nch + c, 0)),
                   pl.BlockSpec((1, SUBLANES, width), lambda b, c: (b, 0, 0)),
                   pl.BlockSpec((1, 1, width), lambda b, c: (b, 0, 0))],
        out_shape=[jax.ShapeDtypeStruct((batch * seq, width), o_dtype),
                   jax.ShapeDtypeStruct((batch, SUBLANES, width), F32),
                   jax.ShapeDtypeStruct((batch, 1, width), F32)],
        scratch_shapes=[pltpu.VMEM((SUBLANES + q, width), F32),
                        pltpu.VMEM((1, width), F32)],
        compiler_params=_cparams("parallel", "arbitrary"),
        name="lru",
    )(big, big, conv0p, h0.reshape(batch, 1, width), lw["conv_c_w"], vec(lw["conv_c_b"]),
      lw["w_rgate"].astype(BF16), lw["w_igate"].astype(BF16), vec(lw["b_rgate"]), vec(lw["b_igate"]),
      vec(lw["lru_lambda"]))
    return y, convo[:, SUBLANES - (C_CONV - 1):, :], ho.reshape(batch, width)


def _merge_kernel(x_ref, gt_ref, oa_ref, yb_ref, yc_ref, gates_ref, wa_ref, wb_ref, wc_ref, wo_ref,
                  out_ref):
    tm, d = x_ref.shape
    gt = _ld(gt_ref)
    sub = min(tm, MERGE_SUB)
    for r in range(tm // sub):
        rs = slice(r * sub, (r + 1) * sub)
        ya = _dot(oa_ref[rs, :].astype(BF16), wa_ref[...])
        yb = _dot(yb_ref[rs, :].astype(BF16), wb_ref[...])
        yc = _dot(yc_ref[rs, :].astype(BF16), wc_ref[...])
        sg = jax.nn.sigmoid(gates_ref[rs, :].astype(F32))
        mixed = sg[:, :d] * ya + sg[:, d:2 * d] * yb + sg[:, 2 * d:] * yc
        g = gt if gt.shape[0] == 1 else gt[rs, :]
        out_ref[rs, :] = x_ref[rs, :] + g * _dot(mixed.astype(BF16), wo_ref[...])


def _merge(x, mod, oa, yb, yc, big, lw):
    m, d = x.shape
    aw = A_WIDTH
    tm = _row_tile(m, mod.seq, mod.per_row, MERGE_TM)
    row = lambda wd, j=0: pl.BlockSpec((tm, wd), lambda i: (i, j))
    return pl.pallas_call(
        _merge_kernel,
        grid=(m // tm,),
        in_specs=[row(d), mod.spec(5, tm, 1), row(aw), row(yb.shape[1]), row(yc.shape[1]), row(3 * d, 0)]
        + [_resident(lw[k].shape) for k in ("w_a_proj", "w_b_proj", "w_c_proj", "w_out")],
        out_specs=row(d),
        out_shape=jax.ShapeDtypeStruct((m, d), F32),
        compiler_params=_cparams("parallel"),
        name="merge",
    )(x, mod.arr, oa, yb, yc, big,
      lw["w_a_proj"], lw["w_b_proj"], lw["w_c_proj"], lw["w_out"])


def _t5_bucket(dist):
    dist = np.asarray(dist)
    large = REL_MAX_EXACT + (np.log(np.maximum(dist, 1) / REL_MAX_EXACT)
                             / math.log(REL_MAX_DISTANCE / REL_MAX_EXACT)
                             * (REL_BUCKETS - REL_MAX_EXACT)).astype(np.int64)
    large = np.minimum(large, REL_BUCKETS - 1)
    return np.where(dist < REL_MAX_EXACT, dist, large).astype(np.int32)


def _group_bias(rel_bias, g):
    dil = A_GROUPS[g][1]
    buckets = _t5_bucket(np.arange(A_KEYS + 1) * dil)
    return rel_bias[buckets][:, g * A_HEADS:(g + 1) * A_HEADS].T.astype(F32)


def _prompt_bias_table(bias, g):
    _, step, wblk = _attn_plan(g)
    h = bias.shape[0]
    cols = (wblk + 1) * A_BLOCK
    n = cols + A_BLOCK - 1
    x = np.arange(n + 1)
    x = np.where(x < cols, x, x - (n + 1))
    dist = wblk * A_BLOCK - x
    valid = (dist >= 0) & (dist % step == 0) & (dist // step <= A_KEYS)
    v = jnp.where(valid[None], bias[:, np.clip(dist // step, 0, A_KEYS)], -jnp.inf)
    tab = jnp.tile(v, (1, A_BLOCK))[:, :A_BLOCK * n].reshape(h, A_BLOCK, n)[:, :, :cols]
    return tab.reshape(h // 2, 2 * A_BLOCK, cols)


def _sample_bias_tables(bias, wb, dil, t):
    h = bias.shape[0]
    front = t - 1 + wb - A_KEYS * dil
    assert front >= 0
    sparse = jnp.concatenate([bias[:, ::-1, None], jnp.full((h, A_KEYS + 1, dil - 1), -jnp.inf, F32)], axis=2)
    base = jnp.concatenate([jnp.full((h, front), -jnp.inf, F32), sparse.reshape(h, (A_KEYS + 1) * dil)], axis=1)
    tab_buf = jnp.stack([base[:, t - 1 - tq:t - 1 - tq + wb] for tq in range(t)], axis=1)
    tq = np.arange(t)[:, None]
    dist = tq - np.arange(LANES)[None, :]
    valid = (dist >= 0) & (dist % dil == 0) & (np.arange(LANES)[None, :] < t)
    tab_new = jnp.where(valid[None], bias[:, np.clip(dist // dil, 0, A_KEYS)], -jnp.inf)
    return tab_buf.reshape(h // 2, 2 * t, wb), tab_new.reshape(h // 2, 2 * t, LANES)


def _mixer_weights(w_in):
    na = len(A_GROUPS) * A_WIDTH
    d = w_in.shape[0]
    offs = np.cumsum([0, na, na, na, 1024, 1536, 16, 1024, 1024, 3 * d])
    seg = lambda i: w_in[:, offs[i]:offs[i + 1]]
    qa, ka, va, zb, xbc, dtb, xc, gc, gates = [seg(i) for i in range(9)]
    cols = []
    for g in range(len(A_GROUPS)):
        sl = slice(g * A_WIDTH, (g + 1) * A_WIDTH)
        cols += [qa[:, sl] * (A_HEAD_DIM ** -0.5), ka[:, sl], va[:, sl]]
    w_qkv = jnp.concatenate(cols + [dtb, jnp.zeros((d, DT_PAD - dtb.shape[1]), w_in.dtype)], axis=1)
    w_big = jnp.concatenate([gates, zb, gc, xc, xbc], axis=1)
    return w_qkv.astype(BF16), w_big.astype(BF16)


def _mixer(x, mod, lw, st, bias, prompt, batch, seq):
    act_dtype = BF16 if prompt else F32
    qkv = _proj(x, mod, lw["g_mix"], lw["w_qkv"], F32, QKV_TN, slabs=True)
    big = _proj(x, mod, lw["g_mix"], lw["w_big"], act_dtype, BIG_TN)
    ng = len(A_GROUPS)
    npair = A_WIDTH // LANES
    if prompt:
        oa, new_kv = _attn_prompt(qkv, bias, st["kvt"], st["layer"], st["depth"], batch, seq, act_dtype)
    else:
        oa = _attn_sample(qkv, st["kv"], st["layer"], bias, batch, seq)
        new_kv = [jnp.transpose(qkv[(3 * g + 1) * npair:(3 * g + 3) * npair], (1, 0, 2))
                  .reshape(batch, seq, 2, A_HEADS, A_HEAD_DIM) for g in range(ng)]
    dt_col = 3 * ng * npair
    yb, conv_b_new, ssm_new = _ssd(big, qkv, dt_col, st["conv_b"], st["ssm"], lw, batch, seq, act_dtype)
    yc, conv_c_new, lru_new = _lru(big, st["conv_c"], st["lru"], lw, batch, seq, act_dtype)
    x = _merge(x, mod, oa, yb, yc, big, lw)
    return x, (new_kv[0], new_kv[1], new_kv[2], conv_b_new, ssm_new, conv_c_new, lru_new)


def _block(x, mod, lw, st, bias, prompt, batch, seq, g_final):
    x = _ffn(x, mod, 0, lw["g_ff1"], lw["w_ff1_in"], lw["w_ff1_out"])
    x, new_st = _mixer(x, mod, lw, st, bias, prompt, batch, seq)
    x = _ffn(x, mod, 6, lw["g_ff2"], lw["w_ff2_in"], lw["w_ff2_out"], g_final)
    return x, new_st


@jax.jit
def _forward(x_prompt, x_sample, c_prompt, c_sample, cache_win1_kv, cache_win2_kv, cache_win3_kv,
             state_conv_b, state_ssm, state_conv_c, state_lru, rel_bias, w_ada, b_ada, g_ff1,
             w_ff1_in, w_ff1_out, g_mix, w_in, w_a_proj, conv_b_w, conv_b_b, dt_bias, a_log, d_skip,
             g_ssm_norm, w_b_proj, conv_c_w, conv_c_b, w_rgate, b_rgate, w_igate, b_igate, lru_lambda,
             w_c_proj, w_out, g_ff2, w_ff2_in, w_ff2_out, g_final):
    bp, lp, d = x_prompt.shape
    bs, ls, _ = x_sample.shape
    depth = w_ada.shape[0]
    caches = (cache_win1_kv, cache_win2_kv, cache_win3_kv)
    biases = [_group_bias(rel_bias, g) for g in range(len(A_GROUPS))]
    bias_p = [_prompt_bias_table(b, g) for g, b in enumerate(biases)]
    bias_s = [_sample_bias_tables(b, caches[g].shape[2], A_GROUPS[g][1], ls) for g, b in enumerate(biases)]

    yp = x_prompt.reshape(bp * lp, d)
    ys = x_sample.reshape(bs * ls, d)
    c_all = jnp.concatenate([c_prompt, c_sample], axis=0)
    new_p = [[] for _ in range(7)]
    new_s = [[] for _ in range(7)]
    kvt_p = None
    for l in range(depth):
        w_qkv, w_big = _mixer_weights(w_in[l])
        lw = dict(g_ff1=g_ff1[l], w_ff1_in=w_ff1_in[l].astype(BF16), w_ff1_out=w_ff1_out[l].astype(BF16),
                  g_mix=g_mix[l], w_qkv=w_qkv, w_big=w_big, w_a_proj=w_a_proj[l].astype(BF16),
                  conv_b_w=conv_b_w[l], conv_b_b=conv_b_b[l], dt_bias=dt_bias[l], a_log=a_log[l],
                  d_skip=d_skip[l], g_ssm_norm=g_ssm_norm[l], w_b_proj=w_b_proj[l].astype(BF16),
                  conv_c_w=conv_c_w[l], conv_c_b=conv_c_b[l], w_rgate=w_rgate[l], b_rgate=b_rgate[l],
                  w_igate=w_igate[l], b_igate=b_igate[l], lru_lambda=lru_lambda[l],
                  w_c_proj=w_c_proj[l].astype(BF16), w_out=w_out[l].astype(BF16),
                  g_ff2=g_ff2[l], w_ff2_in=w_ff2_in[l].astype(BF16), w_ff2_out=w_ff2_out[l].astype(BF16))
        mod_all = _ada(c_all, w_ada, b_ada, l)
        mod_p = _Mod(mod_all[:bp], bp, lp, per_row=False)
        mod_s = _Mod(mod_all[bp:], bs, ls, per_row=True)
        gf = g_final if l == depth - 1 else None
        st_p = dict(kvt=kvt_p, layer=l, depth=depth,
                    conv_b=jnp.zeros((bp, B_CONV - 1, conv_b_w.shape[2]), F32),
                    ssm=jnp.zeros((bp,) + state_ssm.shape[2:], F32),
                    conv_c=jnp.zeros((bp, C_CONV - 1, conv_c_w.shape[2]), F32),
                    lru=jnp.zeros((bp, state_lru.shape[2]), F32))
        st_s = dict(kv=caches, layer=l, conv_b=state_conv_b[l], ssm=state_ssm[l],
                    conv_c=state_conv_c[l], lru=state_lru[l])
        yp, stp = _block(yp, mod_p, lw, st_p, bias_p, True, bp, lp, gf)
        ys, sts = _block(ys, mod_s, lw, st_s, bias_s, False, bs, ls, gf)
        kvt_p = list(stp[:3])
        for i in range(7):
            new_p[i].append(stp[i])
            new_s[i].append(sts[i])
    outs_p = [jnp.transpose(v, (0, 1, 5, 2, 3, 4)) for v in kvt_p] + [jnp.stack(v, 0) for v in new_p[3:]]
    outs_s = [jnp.stack(v, 0) for v in new_s]
    return (yp.reshape(bp, lp, d), ys.reshape(bs, ls, d), *outs_p, *outs_s)


def kernel(x_prompt, x_sample, c_prompt, c_sample, cache_win1_kv, cache_win2_kv, cache_win3_kv,
           state_conv_b, state_ssm, state_conv_c, state_lru, rel_bias, w_ada, b_ada, g_ff1,
           w_ff1_in, w_ff1_out, g_mix, w_in, w_a_proj, conv_b_w, conv_b_b, dt_bias, a_log, d_skip,
           g_ssm_norm, w_b_proj, conv_c_w, conv_c_b, w_rgate, b_rgate, w_igate, b_igate, lru_lambda,
           w_c_proj, w_out, g_ff2, w_ff2_in, w_ff2_out, g_final):
    return _forward(x_prompt, x_sample, c_prompt, c_sample, cache_win1_kv, cache_win2_kv,
                    cache_win3_kv, state_conv_b, state_ssm, state_conv_c, state_lru, rel_bias, w_ada,
                    b_ada, g_ff1, w_ff1_in, w_ff1_out, g_mix, w_in, w_a_proj, conv_b_w, conv_b_b,
                    dt_bias, a_log, d_skip, g_ssm_norm, w_b_proj, conv_c_w, conv_c_b, w_rgate,
                    b_rgate, w_igate, b_igate, lru_lambda, w_c_proj, w_out, g_ff2, w_ff2_in,
                    w_ff2_out, g_final)
```

```python
import functools
import math

import jax
import jax.numpy as jnp
import numpy as np
from jax import lax
from jax.experimental import pallas as pl
from jax.experimental.pallas import tpu as pltpu

F32 = jnp.float32
BF16 = jnp.bfloat16

NORM_EPS = 1e-6
N_MOD = 9
A_GROUPS = ((128, 1), (512, 4), (2048, 16))
A_HEADS = 8
A_HEAD_DIM = 64
A_WIDTH = A_HEADS * A_HEAD_DIM
A_KEYS = 128
A_BLOCK = 128
REL_BUCKETS = 32
REL_MAX_EXACT = 16
REL_MAX_DISTANCE = 2048
B_HEAD_DIM = 64
B_GROUPS = 2
B_STATE = 128
B_CONV = 4
C_BLOCKS = 8
C_CONV = 4
C_POW = 8.0

LANES = 128
SUBLANES = 8
SSD_CHUNK = 128
FFN_TM = 1024
PROJ_TM = 2048
PROJ_SUB = 512
MERGE_TM = 1024
MERGE_SUB = 256
SEQ_TM = 512
LRU_CHUNK = 128
QKV_TN = 1024
BIG_TN = 1536
DT_PAD = 512
ATTN_UNROLL = 8
ATTN_UNROLL_REST = 15
ATTN_MAX_STRIDE = 4
VMEM_LIMIT = 56 * 1024 * 1024


def _cparams(*sem):
    return pltpu.CompilerParams(dimension_semantics=sem, vmem_limit_bytes=VMEM_LIMIT)


def _resident(shape):
    nd = len(shape)
    return pl.BlockSpec(shape, lambda *_: (0,) * nd, pipeline_mode=pl.Buffered(1))


def _ld(ref):
    return ref[0] if len(ref.shape) == 3 else ref[...]


def _norm_mod(x, g, sc, sh):
    ms = jnp.mean(x * x, axis=-1, keepdims=True)
    return (x * lax.rsqrt(ms + NORM_EPS) * g) * (1.0 + sc) + sh


def _softplus(x):
    return jnp.maximum(x, 0.0) + jnp.log1p(jnp.exp(-jnp.abs(x)))


def _causal_conv(xpad, w_ref, b_ref, q, start=0):
    taps = w_ref.shape[0]
    assert taps - 1 <= SUBLANES and start % SUBLANES == 0
    xe = xpad[start:start + SUBLANES + q, :]
    w = w_ref[...]
    acc = xe * w[0:1, :]
    for k in range(1, taps):
        acc = xe * w[k:k + 1, :] + pltpu.roll(acc, 1, axis=0)
    return acc[SUBLANES:SUBLANES + q] + b_ref[...]


def _split_bf16(v, n):
    parts = []
    r = v
    for _ in range(n):
        p = r.astype(BF16)
        parts.append(p)
        r = r - p.astype(F32)
    return parts


def _dot(a, b):
    return jnp.dot(a, b, preferred_element_type=F32)


def _dot_nt(a, b):
    return lax.dot_general(a, b, (((1,), (1,)), ((), ())), preferred_element_type=F32)


def _ada_kernel(c_ref, w_ref, b_ref, o_ref):
    c = c_ref[...]
    a = (c * jax.nn.sigmoid(c)).astype(BF16)
    o_ref[...] = _dot(a, w_ref[...].astype(BF16)) + b_ref[...]


def _ada(c, w, b, layer):
    rows, d = c.shape
    depth, _, n = w.shape
    tn = 1024
    return pl.pallas_call(
        _ada_kernel,
        grid=(n // tn,),
        in_specs=[pl.BlockSpec((rows, d), lambda j: (0, 0)),
                  pl.BlockSpec((None, d, tn), lambda j: (layer, 0, j)),
                  pl.BlockSpec((None, 1, tn), lambda j: (layer, 0, j))],
        out_specs=pl.BlockSpec((rows, tn), lambda j: (0, j)),
        out_shape=jax.ShapeDtypeStruct((rows, n), F32),
        compiler_params=_cparams("parallel"),
        name="ada",
    )(c, w, b.reshape(depth, 1, n))


class _Mod:
    def __init__(self, mod, batch, seq, per_row):
        d = mod.shape[1] // N_MOD
        self.d = d
        self.seq = seq
        self.per_row = per_row
        if per_row:
            self.arr = jnp.repeat(mod, seq, axis=0)
        else:
            self.arr = mod.reshape(batch * N_MOD, 1, d)

    def spec(self, k, tm, grid_rank):
        d = self.d
        if self.per_row:
            if grid_rank == 1:
                return pl.BlockSpec((tm, d), lambda i: (i, k))
            return pl.BlockSpec((tm, d), lambda i, j: (i, k))
        per = self.seq // tm
        if grid_rank == 1:
            return pl.BlockSpec((1, 1, d), lambda i: ((i // per) * N_MOD + k, 0, 0))
        return pl.BlockSpec((1, 1, d), lambda i, j: ((i // per) * N_MOD + k, 0, 0))


def _row_tile(m, seq, per_row, want):
    tm = min(want, m if per_row else seq)
    assert m % tm == 0 and (per_row or seq % tm == 0)
    return tm


def _ffn_kernel(x_ref, sh_ref, sc_ref, gt_ref, gn_ref, win_ref, wout_ref, *rest, d_ff, fc, final):
    if final:
        gf_ref, o_ref, acc_ref = rest
    else:
        o_ref, acc_ref = rest
    x = x_ref[...]
    h = _norm_mod(x, gn_ref[...], _ld(sc_ref), _ld(sh_ref)).astype(BF16)
    for c in range(d_ff // fc):
        u = _dot(h, win_ref[:, c * fc:(c + 1) * fc])
        v = _dot(h, win_ref[:, d_ff + c * fc:d_ff + (c + 1) * fc])
        a = (u * jax.nn.sigmoid(u) * v).astype(BF16)
        part = _dot(a, wout_ref[c * fc:(c + 1) * fc, :])
        if c == 0:
            acc_ref[...] = part
        else:
            acc_ref[...] += part
    y = x + 0.5 * _ld(gt_ref) * acc_ref[...]
    if final:
        ms = jnp.mean(y * y, axis=-1, keepdims=True)
        y = y * lax.rsqrt(ms + NORM_EPS) * gf_ref[...]
    o_ref[...] = y


def _ffn(x, mod, k0, gn, w_in, w_out, g_final=None):
    m, d = x.shape
    d_ff = w_out.shape[0]
    tm = _row_tile(m, mod.seq, mod.per_row, FFN_TM)
    final = g_final is not None
    in_specs = [pl.BlockSpec((tm, d), lambda i: (i, 0)),
                mod.spec(k0, tm, 1), mod.spec(k0 + 1, tm, 1), mod.spec(k0 + 2, tm, 1),
                _resident((1, d)), _resident(w_in.shape), _resident(w_out.shape)]
    args = [x, mod.arr, mod.arr, mod.arr, gn.reshape(1, d), w_in, w_out]
    if final:
        in_specs.append(_resident((1, d)))
        args.append(g_final.reshape(1, d))
    return pl.pallas_call(
        functools.partial(_ffn_kernel, d_ff=d_ff, fc=256, final=final),
        grid=(m // tm,),
        in_specs=in_specs,
        out_specs=pl.BlockSpec((tm, d), lambda i: (i, 0)),
        out_shape=jax.ShapeDtypeStruct((m, d), F32),
        scratch_shapes=[pltpu.VMEM((tm, d), F32)],
        compiler_params=_cparams("parallel"),
        name="ffn",
    )(*args)


def _proj_kernel(x_ref, sh_ref, sc_ref, gn_ref, w_ref, o_ref, h_ref):
    @pl.when(pl.program_id(1) == 0)
    def _():
        h_ref[...] = _norm_mod(x_ref[...], gn_ref[...], _ld(sc_ref), _ld(sh_ref)).astype(BF16)

    tm = h_ref.shape[0]
    sub = min(tm, PROJ_SUB)
    for r in range(tm // sub):
        rs = slice(r * sub, (r + 1) * sub)
        res = _dot(h_ref[rs, :], w_ref[...]).astype(o_ref.dtype)
        if len(o_ref.shape) == 2:
            o_ref[rs, :] = res
        else:
            for s in range(o_ref.shape[0]):
                o_ref[s, rs, :] = res[:, s * LANES:(s + 1) * LANES]


def _proj(x, mod, gn, w, out_dtype, tn, slabs=False):
    m, d = x.shape
    n = w.shape[1]
    tm = _row_tile(m, mod.seq, mod.per_row, PROJ_TM)
    if slabs:
        out_spec = pl.BlockSpec((tn // LANES, tm, LANES), lambda i, j: (j, i, 0))
        out_shape = jax.ShapeDtypeStruct((n // LANES, m, LANES), out_dtype)
    else:
        out_spec = pl.BlockSpec((tm, tn), lambda i, j: (i, j))
        out_shape = jax.ShapeDtypeStruct((m, n), out_dtype)
    return pl.pallas_call(
        _proj_kernel,
        grid=(m // tm, n // tn),
        in_specs=[pl.BlockSpec((tm, d), lambda i, j: (i, 0)),
                  mod.spec(3, tm, 2), mod.spec(4, tm, 2),
                  pl.BlockSpec((1, d), lambda i, j: (0, 0)),
                  pl.BlockSpec((d, tn), lambda i, j: (0, j))],
        out_specs=out_spec,
        out_shape=out_shape,
        scratch_shapes=[pltpu.VMEM((tm, d), BF16)],
        compiler_params=_cparams("parallel", "arbitrary"),
        name="proj",
    )(x, mod.arr, mod.arr, gn.reshape(1, d), w)


def _pair_softmax(q2, k2, v2, bias2, lo, transposed=False):
    rows = q2.shape[0]
    zero = jnp.zeros_like(q2)
    qs = jnp.concatenate([jnp.where(lo, q2, zero), jnp.where(lo, zero, q2)], axis=0).astype(BF16)
    s = (_dot(qs, k2) if transposed else _dot_nt(qs, k2)) + bias2
    mx = jnp.max(s, axis=-1, keepdims=True)
    p = jnp.exp(s - mx)
    l = jnp.sum(p, axis=-1, keepdims=True)
    pb = p.astype(BF16)
    o = _dot_nt(pb, v2) if transposed else _dot(pb, v2)
    return (jnp.where(lo, o[:rows], o[rows:]), jnp.where(lo, mx[:rows], mx[rows:]),
            jnp.where(lo, l[:rows], l[rows:]))


def _merge_softmax(acc, new):
    ao, am, al = acc
    o, m, l = new
    mn = jnp.maximum(am, m)
    a1 = jnp.exp(am - mn)
    a2 = jnp.exp(m - mn)
    return ao * a1 + o * a2, mn, al * a1 + l * a2


def _attn_plan(g):
    dil = A_GROUPS[g][1]
    if dil <= ATTN_MAX_STRIDE:
        return dil, 1, A_KEYS // A_BLOCK
    assert dil % ATTN_MAX_STRIDE == 0
    step = dil // ATTN_MAX_STRIDE
    return ATTN_MAX_STRIDE, step, step * A_KEYS // A_BLOCK


def _attn_prompt_kernel(*refs, seq, n_alias):
    ng = len(A_GROUPS)
    qkv = refs[:3 * ng]
    bias_refs = refs[3 * ng:4 * ng]
    o_ref = refs[4 * ng + n_alias]
    kvt_refs = refs[4 * ng + 1 + n_alias:5 * ng + 1 + n_alias]
    acc_o, acc_m, acc_l = refs[-3:]
    blk = A_BLOCK

    for g in range(ng):
        keep = kvt_refs[g].shape[-1]
        for kv in range(2):
            src = qkv[3 * g + 1 + kv]
            for c in range(keep // blk):
                t = src[pl.ds(seq - keep + c * blk, blk), :].T
                for i in range(2):
                    kvt_refs[g][kv, i, :, c * blk:(c + 1) * blk] = t[i * A_HEAD_DIM:(i + 1) * A_HEAD_DIM]

    hp = pl.program_id(1)
    lo = lax.broadcasted_iota(jnp.int32, (blk, LANES), 1) < A_HEAD_DIM

    def tiles(g, starts, kw):
        stride = _attn_plan(g)[0]
        q_ref, k_ref, v_ref = qkv[3 * g:3 * g + 3]
        bias_ref = bias_refs[g]
        wcols = bias_ref.shape[-1]

        def rows(start, n):
            return pl.ds(start, n) if stride == 1 else pl.ds(start, n, stride=stride)

        news = []
        for qstart, kstart in starts:
            q2 = q_ref[rows(qstart, blk), :]
            k2 = k_ref[rows(kstart, kw), :].astype(BF16)
            v2 = v_ref[rows(kstart, kw), :].astype(BF16)
            news.append(_pair_softmax(q2, k2, v2, bias_ref[hp, :, wcols - kw:wcols], lo))
        for (qstart, _), new in zip(starts, news):
            sel = rows(qstart, blk)
            if g > 0:
                new = _merge_softmax((acc_o[sel, :], acc_m[sel, :], acc_l[sel, :]), new)
            if g == ng - 1:
                acc_o[sel, :] = new[0] / new[2]
            else:
                acc_o[sel, :] = new[0]
                acc_m[sel, :] = new[1]
                acc_l[sel, :] = new[2]

    for g in range(ng):
        stride, _, wblk = _attn_plan(g)
        assert stride <= ATTN_UNROLL
        nb = seq // stride // blk
        span = blk * stride
        for b in range(min(wblk, nb)):
            tiles(g, [(r + b * span, r) for r in range(stride)], (b + 1) * blk)
        nfull = nb - wblk
        if nfull > 0:
            ub = max(u for u in range(1, nfull + 1) if nfull % u == 0 and u * stride <= ATTN_UNROLL_REST)

            def full(i, carry, g=g, span=span, stride=stride, ub=ub, wblk=wblk):
                b = wblk + i * ub
                tiles(g, [(r + (b + k) * span, r + (b + k - wblk) * span)
                          for k in range(ub) for r in range(stride)], (wblk + 1) * blk)
                return carry

            lax.fori_loop(0, nfull // ub, full, 0)
    o_ref[...] = acc_o[...].astype(o_ref.dtype)


def _attn_prompt(qkv, bias_tab, kvt_prev, layer, depth, batch, seq, o_dtype):
    npair = A_WIDTH // LANES
    ng = len(A_GROUPS)
    col = lambda j: pl.BlockSpec((None, seq, LANES), lambda b, h: (j * npair + h, b, 0))
    keeps = [min(win, seq) for win, _ in A_GROUPS]
    n_alias = 0 if kvt_prev is None else ng
    alias_specs = [pl.BlockSpec(memory_space=pl.ANY)] * n_alias
    alias_args = [] if kvt_prev is None else list(kvt_prev)
    n_in = 4 * ng
    outs = pl.pallas_call(
        functools.partial(_attn_prompt_kernel, seq=seq, n_alias=n_alias),
        grid=(batch, npair),
        in_specs=[col(j) for j in range(3 * ng)]
        + [pl.BlockSpec(tab.shape, lambda b, h: (0, 0, 0)) for tab in bias_tab] + alias_specs,
        out_specs=[pl.BlockSpec((seq, LANES), lambda b, h: (b, h))]
        + [pl.BlockSpec((None, None, 2, 2, A_HEAD_DIM, keep), lambda b, h: (layer, b, 0, h, 0, 0))
           for keep in keeps],
        out_shape=[jax.ShapeDtypeStruct((batch * seq, A_WIDTH), o_dtype)]
        + [jax.ShapeDtypeStruct((depth, batch, 2, A_HEADS, A_HEAD_DIM, keep), F32) for keep in keeps],
        input_output_aliases={n_in + g: 1 + g for g in range(n_alias)},
        scratch_shapes=[pltpu.VMEM((seq, LANES), F32)] * 3,
        compiler_params=_cparams("parallel", "parallel"),
        name="attn_prompt",
    )(*([qkv] * (3 * ng)), *bias_tab, *alias_args)
    return outs[0], list(outs[1:])


def _attn_sample_kernel(qkv_ref, buf1_ref, buf2_ref, buf3_ref, bb1, bn1, bb2, bn2, bb3, bn3,
                        o_ref, knew_scr, vnew_scr):
    t = qkv_ref.shape[1]
    npair = A_HEADS // 2
    bufs = (buf1_ref, buf2_ref, buf3_ref)
    bias = ((bb1, bn1), (bb2, bn2), (bb3, bn3))
    lo = lax.broadcasted_iota(jnp.int32, (t, LANES), 1) < A_HEAD_DIM
    knew_scr[...] = jnp.zeros_like(knew_scr)
    vnew_scr[...] = jnp.zeros_like(vnew_scr)
    for hp in range(npair):
        cols = slice(hp * LANES, (hp + 1) * LANES)
        acc = None
        for g in range(len(A_GROUPS)):
            q2 = qkv_ref[3 * g * npair + hp]
            wb = bufs[g].shape[-1]
            kb = bufs[g][0, 2 * hp:2 * hp + 2].reshape(LANES, wb).astype(BF16)
            vb = bufs[g][1, 2 * hp:2 * hp + 2].reshape(LANES, wb).astype(BF16)
            knew_scr[g, hp, 0:t, :] = qkv_ref[(3 * g + 1) * npair + hp]
            vnew_scr[g, hp, 0:t, :] = qkv_ref[(3 * g + 2) * npair + hp]
            kn = knew_scr[g, hp].astype(BF16)
            vn = vnew_scr[g, hp].astype(BF16)
            bbuf, bnew = bias[g]
            new = _pair_softmax(q2, kb, vb, bbuf[hp], lo, transposed=True)
            new = _merge_softmax(new, _pair_softmax(q2, kn, vn, bnew[hp], lo))
            acc = new if acc is None else _merge_softmax(acc, new)
        o_ref[:, cols] = acc[0] / acc[2]


def _attn_sample(qkv, caches, layer, bias_s, batch, t):
    aw = A_WIDTH
    ng = len(A_GROUPS)
    bufs = [jnp.transpose(c, (0, 1, 3, 4, 5, 2)) for c in caches]
    buf_specs = [pl.BlockSpec((None, None) + bv.shape[2:], lambda b: (layer, b, 0, 0, 0, 0)) for bv in bufs]
    tabs = [tab for pair in bias_s for tab in pair]
    return pl.pallas_call(
        _attn_sample_kernel,
        grid=(batch,),
        in_specs=[pl.BlockSpec((3 * ng * aw // LANES, t, LANES), lambda b: (0, b, 0))] + buf_specs
        + [pl.BlockSpec(tab.shape, lambda b: (0, 0, 0)) for tab in tabs],
        out_specs=pl.BlockSpec((t, aw), lambda b: (b, 0)),
        out_shape=jax.ShapeDtypeStruct((batch * t, aw), F32),
        scratch_shapes=[pltpu.VMEM((ng, aw // LANES, LANES, LANES), F32)] * 2,
        compiler_params=_cparams("parallel"),
        name="attn_sample",
    )(qkv, *bufs, *tabs)


def _ssd_kernel(xbc_ref, z_ref, dt_ref, conv0_ref, st0_ref, cw_ref, cb_ref, dtb_ref, aneg_ref,
                dsk_ref, gn_ref, e_ref, y_ref, convo_ref, sto_ref, xpad, st_t, y_scr, *, lv, nsub, nch):
    q = SSD_CHUNK
    c = pl.program_id(1)
    width = y_ref.shape[1]
    nst = B_STATE
    hpg = width // B_HEAD_DIM // B_GROUPS // 2

    @pl.when(c == 0)
    def _():
        xpad[...] = jnp.zeros_like(xpad)
        xpad[0:SUBLANES, :] = conv0_ref[0]
        for j in range(width // LANES):
            st_t[:, j * LANES:(j + 1) * LANES] = st0_ref[0, j * LANES:(j + 1) * LANES, :].T

    rows = nsub * lv

    @pl.when(c > 0)
    def _():
        xpad[0:SUBLANES, :] = xpad[rows:rows + SUBLANES, :]

    xpad[SUBLANES:SUBLANES + rows, :] = xbc_ref[...].astype(F32)

    for k in range(nsub):
        rs = slice(k * lv, (k + 1) * lv)
        conv = _causal_conv(xpad, cw_ref, cb_ref, q, start=k * q)
        act = conv * jax.nn.sigmoid(conv)
        xs = act[:, :width]
        bm = [act[:, width + gi * nst:width + (gi + 1) * nst] for gi in range(B_GROUPS)]
        cm = [act[:, width + (B_GROUPS + gi) * nst:width + (B_GROUPS + gi + 1) * nst] for gi in range(B_GROUPS)]

        dt_raw = dt_ref[rs, :]
        if lv < q:
            dt_raw = jnp.concatenate([dt_raw, jnp.zeros((q - lv, LANES), F32)], axis=0)
        dt = _softplus(dt_raw + dtb_ref[...])
        if lv < q:
            dt = jnp.where(lax.broadcasted_iota(jnp.int32, (q, LANES), 0) < lv, dt, 0.0)
        dta = dt * aneg_ref[...]

        row = lax.broadcasted_iota(jnp.int32, (q, q), 0)
        col = lax.broadcasted_iota(jnp.int32, (q, q), 1)
        causal = row >= col
        tri = jnp.where(causal, 1.0, 0.0).astype(BF16)
        acum = sum(_dot(tri, p) for p in _split_bf16(dta, 3))
        acum_t = acum.T
        e = e_ref[...]
        dt_x = sum(_dot(p, e) for p in _split_bf16(dt, 2))
        acum_x = sum(_dot(p, e) for p in _split_bf16(acum, 2))
        last_x = acum_x[q - 1:q, :]
        ea_x = jnp.exp(acum_x)
        xdt = xs * dt_x
        xdt_te = (xdt * jnp.exp(last_x - acum_x)).astype(BF16)
        xdt_b = xdt.astype(BF16)
        chunk_decay = jnp.exp(last_x)
        lo = lax.broadcasted_iota(jnp.int32, (q, LANES), 1) < B_HEAD_DIM

        for gi in range(B_GROUPS):
            cmb = cm[gi].astype(BF16)
            cb = _dot_nt(cmb, bm[gi].astype(BF16))
            bm_t = bm[gi].T.astype(BF16)
            for hp in range(gi * hpg, (gi + 1) * hpg):
                cols = slice(hp * LANES, (hp + 1) * LANES)
                ax = acum_x[:, cols]
                ax_r = pltpu.roll(ax, B_HEAD_DIM, axis=1)
                ys = []
                for col_v, h in ((jnp.where(lo, ax, ax_r), 2 * hp), (jnp.where(lo, ax_r, ax), 2 * hp + 1)):
                    seg = col_v - acum_t[h:h + 1, :]
                    dec = jnp.exp(jnp.where(causal, seg, -jnp.inf))
                    ys.append(_dot((cb * dec).astype(BF16), xdt_b[:, cols]))
                st_old = st_t[:, cols]
                y_off = _dot(cmb, st_old.astype(BF16)) * ea_x[:, cols]
                st_t[:, cols] = st_old * chunk_decay[:, cols] + _dot(bm_t, xdt_te[:, cols])
                y_scr[k, :, cols] = jnp.where(lo, ys[0], ys[1]) + y_off + dsk_ref[:, cols] * xs[:, cols]

        z = z_ref[rs, :].astype(F32)
        if lv < q:
            z = jnp.concatenate([z, jnp.zeros((q - lv, width), F32)], axis=0)
        y = y_scr[k] * (z * jax.nn.sigmoid(z))
        gw = width // B_GROUPS
        for gi in range(B_GROUPS):
            yg = y[:, gi * gw:(gi + 1) * gw]
            ms = jnp.mean(yg * yg, axis=-1, keepdims=True)
            yn = yg * lax.rsqrt(ms + NORM_EPS) * gn_ref[:, gi * gw:(gi + 1) * gw]
            y_ref[rs, gi * gw:(gi + 1) * gw] = yn[0:lv].astype(y_ref.dtype)

    @pl.when(c == nch - 1)
    def _():
        convo_ref[0] = xpad[rows:rows + SUBLANES, :]
        for j in range(width // LANES):
            sto_ref[0, j * LANES:(j + 1) * LANES, :] = st_t[:, j * LANES:(j + 1) * LANES].T


def _ssd(big, qkv, dt_col, conv0, st0, lw, batch, seq, o_dtype):
    width = lw["w_b_proj"].shape[0]
    cch = lw["conv_b_w"].shape[1]
    nheads = width // B_HEAD_DIM
    lv = min(seq, SSD_CHUNK)
    nsub = min(seq, SEQ_TM) // lv if lv == SSD_CHUNK else 1
    rows = lv * nsub
    nch = seq // rows
    assert seq % rows == 0 and (lv == SSD_CHUNK or nch == 1)
    pad = lambda v: jnp.pad(v.astype(F32), (0, LANES - nheads)).reshape(1, LANES)
    e = (np.arange(LANES)[:, None] == (np.arange(width)[None, :] // B_HEAD_DIM)).astype(np.float32)
    conv0p = jnp.pad(conv0, ((0, 0), (SUBLANES - (B_CONV - 1), 0), (0, 0)))
    y, convo, sto = pl.pallas_call(
        functools.partial(_ssd_kernel, lv=lv, nsub=nsub, nch=nch),
        grid=(batch, nch),
        in_specs=[pl.BlockSpec((rows, cch), lambda b, c: (b * nch + c, 4)),
                  pl.BlockSpec((rows, width), lambda b, c: (b * nch + c, 3)),
                  pl.BlockSpec((None, rows, LANES), lambda b, c: (dt_col, b * nch + c, 0)),
                  pl.BlockSpec((1, SUBLANES, cch), lambda b, c: (b, 0, 0)),
                  pl.BlockSpec((1, width, B_STATE), lambda b, c: (b, 0, 0)),
                  pl.BlockSpec((B_CONV, cch), lambda b, c: (0, 0)),
                  pl.BlockSpec((1, cch), lambda b, c: (0, 0)),
                  pl.BlockSpec((1, LANES), lambda b, c: (0, 0)),
                  pl.BlockSpec((1, LANES), lambda b, c: (0, 0)),
                  pl.BlockSpec((1, width), lambda b, c: (0, 0)),
                  pl.BlockSpec((1, width), lambda b, c: (0, 0)),
                  pl.BlockSpec((LANES, width), lambda b, c: (0, 0))],
        out_specs=[pl.BlockSpec((rows, width), lambda b, c: (b * nch + c, 0)),
                   pl.BlockSpec((1, SUBLANES, cch), lambda b, c: (b, 0, 0)),
                   pl.BlockSpec((1, width, B_STATE), lambda b, c: (b, 0, 0))],
        out_shape=[jax.ShapeDtypeStruct((batch * seq, width), o_dtype),
                   jax.ShapeDtypeStruct((batch, SUBLANES, cch), F32),
                   jax.ShapeDtypeStruct((batch, width, B_STATE), F32)],
        scratch_shapes=[pltpu.VMEM((SUBLANES + nsub * SSD_CHUNK, cch), F32),
                        pltpu.VMEM((B_STATE, width), F32),
                        pltpu.VMEM((nsub, SSD_CHUNK, width), F32)],
        compiler_params=_cparams("parallel", "arbitrary"),
        name="ssd",
    )(big, big, qkv, conv0p, st0.reshape(batch, width, B_STATE),
      lw["conv_b_w"], lw["conv_b_b"].reshape(1, cch), pad(lw["dt_bias"]),
      pad(-jnp.exp(lw["a_log"].astype(F32))),
      jnp.repeat(lw["d_skip"].astype(F32), B_HEAD_DIM).reshape(1, width),
      lw["g_ssm_norm"].reshape(1, width), jnp.asarray(e, BF16))
    return (y, convo[:, SUBLANES - (B_CONV - 1):, :],
            sto.reshape(batch, nheads, B_HEAD_DIM, B_STATE))


def _lru_chunk(x, h, wr_ref, wi_ref, br_ref, bi_ref, lam_ref):
    q, width = x.shape
    bd = width // C_BLOCKS
    rs, gs = [], []
    for j in range(C_BLOCKS):
        xb = x[:, j * bd:(j + 1) * bd].astype(BF16)
        rs.append(_dot(xb, wr_ref[j]))
        gs.append(_dot(xb, wi_ref[j]))
    rg = jax.nn.sigmoid(jnp.concatenate(rs, axis=1) + br_ref[...])
    ig = jax.nn.sigmoid(jnp.concatenate(gs, axis=1) + bi_ref[...])
    log_a = -C_POW * rg * _softplus(-lam_ref[...])
    a = jnp.exp(log_a)
    u = x * ig * jnp.sqrt(-jnp.tanh(log_a) * (a * a + 1.0))

    rowi = lax.broadcasted_iota(jnp.int32, (SUBLANES, width), 0)
    hs = []
    for g in range(q // SUBLANES):
        ag = a[g * SUBLANES:(g + 1) * SUBLANES]
        bg = u[g * SUBLANES:(g + 1) * SUBLANES]
        s = 1
        while s < SUBLANES:
            keep = rowi >= s
            a_sh = jnp.where(keep, pltpu.roll(ag, s, axis=0), 1.0)
            b_sh = jnp.where(keep, pltpu.roll(bg, s, axis=0), 0.0)
            bg = ag * b_sh + bg
            ag = ag * a_sh
            s *= 2
        hg = bg + ag * h
        hs.append(hg)
        h = hg[SUBLANES - 1:SUBLANES]
    return (hs[0] if len(hs) == 1 else jnp.concatenate(hs, axis=0)), h


def _lru_kernel(xc_ref, gc_ref, conv0_ref, h0_ref, cw_ref, cb_ref, wr_ref, wi_ref, br_ref, bi_ref,
                lam_ref, y_ref, convo_ref, ho_ref, xpad, hprev, *, q, nch):
    c = pl.program_id(1)
    width = y_ref.shape[1]

    @pl.when(c == 0)
    def _():
        xpad[0:SUBLANES, :] = conv0_ref[0]
        hprev[...] = h0_ref[0]

    @pl.when(c > 0)
    def _():
        xpad[0:SUBLANES, :] = xpad[q:q + SUBLANES, :]

    xpad[SUBLANES:SUBLANES + q, :] = xc_ref[...].astype(F32)
    sub = min(q, LRU_CHUNK)
    h = hprev[...]
    for k in range(q // sub):
        rs = slice(k * sub, (k + 1) * sub)
        x = _causal_conv(xpad, cw_ref, cb_ref, sub, start=k * sub)
        hs, h = _lru_chunk(x, h, wr_ref, wi_ref, br_ref, bi_ref, lam_ref)
        y_ref[rs, :] = (hs * jax.nn.gelu(gc_ref[rs, :].astype(F32))).astype(y_ref.dtype)
    hprev[...] = h

    @pl.when(c == nch - 1)
    def _():
        convo_ref[0] = xpad[q:q + SUBLANES, :]
        ho_ref[0] = h


def _lru(big, conv0, h0, lw, batch, seq, o_dtype):
    width = lw["w_c_proj"].shape[0]
    q = min(seq, SEQ_TM)
    nch = seq // q
    assert seq % q == 0 and q % min(q, LRU_CHUNK) == 0
    bd = width // C_BLOCKS
    conv0p = jnp.pad(conv0, ((0, 0), (SUBLANES - (C_CONV - 1), 0), (0, 0)))
    vec = lambda v: v.astype(F32).reshape(1, width)
    y, convo, ho = pl.pallas_call(
        functools.partial(_lru_kernel, q=q, nch=nch),
        grid=(batch, nch),
        in_specs=[pl.BlockSpec((q, width), lambda b, c: (b * nch + c, 5)),
                  pl.BlockSpec((q, width), lambda b, c: (b * nch + c, 4)),
                  pl.BlockSpec((1, SUBLANES, width), lambda b, c: (b, 0, 0)),
                  pl.BlockSpec((1, 1, width), lambda b, c: (b, 0, 0)),
                  pl.BlockSpec((C_CONV, width), lambda b, c: (0, 0)),
                  pl.BlockSpec((1, width), lambda b, c: (0, 0)),
                  pl.BlockSpec((C_BLOCKS, bd, bd), lambda b, c: (0, 0, 0)),
                  pl.BlockSpec((C_BLOCKS, bd, bd), lambda b, c: (0, 0, 0)),
                  pl.BlockSpec((1, width), lambda b, c: (0, 0)),
                  pl.BlockSpec((1, width), lambda b, c: (0, 0)),
                  pl.BlockSpec((1, width), lambda b, c: (0, 0))],
        out_specs=[pl.BlockSpec((q, width), lambda b, c: (b * nch + c, 0)),
                   pl.BlockSpec((1, SUBLANES, width), lambda b, c: (b, 0, 0)),
                   pl.BlockSpec((1, 1, width), lambda b, c: (b, 0, 0))],
        out_shape=[jax.ShapeDtypeStruct((batch * seq, width), o_dtype),
                   jax.ShapeDtypeStruct((batch, SUBLANES, width), F32),
                   jax.ShapeDtypeStruct((batch, 1, width), F32)],
        scratch_shapes=[pltpu.VMEM((SUBLANES + q, width), F32),
                        pltpu.VMEM((1, width), F32)],
        compiler_params=_cparams("parallel", "arbitrary"),
        name="lru",
    )(big, big, conv0p, h0.reshape(batch, 1, width), lw["conv_c_w"], vec(lw["conv_c_b"]),
      lw["w_rgate"].astype(BF16), lw["w_igate"].astype(BF16), vec(lw["b_rgate"]), vec(lw["b_igate"]),
      vec(lw["lru_lambda"]))
    return y, convo[:, SUBLANES - (C_CONV - 1):, :], ho.reshape(batch, width)


def _merge_kernel(x_ref, gt_ref, oa_ref, yb_ref, yc_ref, gates_ref, wa_ref, wb_ref, wc_ref, wo_ref,
                  out_ref):
    tm, d = x_ref.shape
    gt = _ld(gt_ref)
    sub = min(tm, MERGE_SUB)
    for r in range(tm // sub):
        rs = slice(r * sub, (r + 1) * sub)
        ya = _dot(oa_ref[rs, :].astype(BF16), wa_ref[...])
        yb = _dot(yb_ref[rs, :].astype(BF16), wb_ref[...])
        yc = _dot(yc_ref[rs, :].astype(BF16), wc_ref[...])
        sg = jax.nn.sigmoid(gates_ref[rs, :].astype(F32))
        mixed = sg[:, :d] * ya + sg[:, d:2 * d] * yb + sg[:, 2 * d:] * yc
        g = gt if gt.shape[0] == 1 else gt[rs, :]
        out_ref[rs, :] = x_ref[rs, :] + g * _dot(mixed.astype(BF16), wo_ref[...])


def _merge(x, mod, oa, yb, yc, big, lw):
    m, d = x.shape
    aw = A_WIDTH
    tm = _row_tile(m, mod.seq, mod.per_row, MERGE_TM)
    row = lambda wd, j=0: pl.BlockSpec((tm, wd), lambda i: (i, j))
    return pl.pallas_call(
        _merge_kernel,
        grid=(m // tm,),
        in_specs=[row(d), mod.spec(5, tm, 1), row(aw), row(yb.shape[1]), row(yc.shape[1]), row(3 * d, 0)]
        + [_resident(lw[k].shape) for k in ("w_a_proj", "w_b_proj", "w_c_proj", "w_out")],
        out_specs=row(d),
        out_shape=jax.ShapeDtypeStruct((m, d), F32),
        compiler_params=_cparams("parallel"),
        name="merge",
    )(x, mod.arr, oa, yb, yc, big,
      lw["w_a_proj"], lw["w_b_proj"], lw["w_c_proj"], lw["w_out"])


def _t5_bucket(dist):
    dist = np.asarray(dist)
    large = REL_MAX_EXACT + (np.log(np.maximum(dist, 1) / REL_MAX_EXACT)
                             / math.log(REL_MAX_DISTANCE / REL_MAX_EXACT)
                             * (REL_BUCKETS - REL_MAX_EXACT)).astype(np.int64)
    large = np.minimum(large, REL_BUCKETS - 1)
    return np.where(dist < REL_MAX_EXACT, dist, large).astype(np.int32)


def _group_bias(rel_bias, g):
    dil = A_GROUPS[g][1]
    buckets = _t5_bucket(np.arange(A_KEYS + 1) * dil)
    return rel_bias[buckets][:, g * A_HEADS:(g + 1) * A_HEADS].T.astype(F32)


def _prompt_bias_table(bias, g):
    _, step, wblk = _attn_plan(g)
    h = bias.shape[0]
    cols = (wblk + 1) * A_BLOCK
    n = cols + A_BLOCK - 1
    x = np.arange(n + 1)
    x = np.where(x < cols, x, x - (n + 1))
    dist = wblk * A_BLOCK - x
    valid = (dist >= 0) & (dist % step == 0) & (dist // step <= A_KEYS)
    v = jnp.where(valid[None], bias[:, np.clip(dist // step, 0, A_KEYS)], -jnp.inf)
    tab = jnp.tile(v, (1, A_BLOCK))[:, :A_BLOCK * n].reshape(h, A_BLOCK, n)[:, :, :cols]
    return tab.reshape(h // 2, 2 * A_BLOCK, cols)


def _sample_bias_tables(bias, wb, dil, t):
    h = bias.shape[0]
    front = t - 1 + wb - A_KEYS * dil
    assert front >= 0
    sparse = jnp.concatenate([bias[:, ::-1, None], jnp.full((h, A_KEYS + 1, dil - 1), -jnp.inf, F32)], axis=2)
    base = jnp.concatenate([jnp.full((h, front), -jnp.inf, F32), sparse.reshape(h, (A_KEYS + 1) * dil)], axis=1)
    tab_buf = jnp.stack([base[:, t - 1 - tq:t - 1 - tq + wb] for tq in range(t)], axis=1)
    tq = np.arange(t)[:, None]
    dist = tq - np.arange(LANES)[None, :]
    valid = (dist >= 0) & (dist % dil == 0) & (np.arange(LANES)[None, :] < t)
    tab_new = jnp.where(valid[None], bias[:, np.clip(dist // dil, 0, A_KEYS)], -jnp.inf)
    return tab_buf.reshape(h // 2, 2 * t, wb), tab_new.reshape(h // 2, 2 * t, LANES)


def _mixer_weights(w_in):
    na = len(A_GROUPS) * A_WIDTH
    d = w_in.shape[0]
    offs = np.cumsum([0, na, na, na, 1024, 1536, 16, 1024, 1024, 3 * d])
    seg = lambda i: w_in[:, offs[i]:offs[i + 1]]
    qa, ka, va, zb, xbc, dtb, xc, gc, gates = [seg(i) for i in range(9)]
    cols = []
    for g in range(len(A_GROUPS)):
        sl = slice(g * A_WIDTH, (g + 1) * A_WIDTH)
        cols += [qa[:, sl] * (A_HEAD_DIM ** -0.5), ka[:, sl], va[:, sl]]
    w_qkv = jnp.concatenate(cols + [dtb, jnp.zeros((d, DT_PAD - dtb.shape[1]), w_in.dtype)], axis=1)
    w_big = jnp.concatenate([gates, zb, gc, xc, xbc], axis=1)
    return w_qkv.astype(BF16), w_big.astype(BF16)


def _mixer(x, mod, lw, st, bias, prompt, batch, seq):
    act_dtype = BF16 if prompt else F32
    qkv = _proj(x, mod, lw["g_mix"], lw["w_qkv"], F32, QKV_TN, slabs=True)
    big = _proj(x, mod, lw["g_mix"], lw["w_big"], act_dtype, BIG_TN)
    ng = len(A_GROUPS)
    npair = A_WIDTH // LANES
    if prompt:
        oa, new_kv = _attn_prompt(qkv, bias, st["kvt"], st["layer"], st["depth"], batch, seq, act_dtype)
    else:
        oa = _attn_sample(qkv, st["kv"], st["layer"], bias, batch, seq)
        new_kv = [jnp.transpose(qkv[(3 * g + 1) * npair:(3 * g + 3) * npair], (1, 0, 2))
                  .reshape(batch, seq, 2, A_HEADS, A_HEAD_DIM) for g in range(ng)]
    dt_col = 3 * ng * npair
    yb, conv_b_new, ssm_new = _ssd(big, qkv, dt_col, st["conv_b"], st["ssm"], lw, batch, seq, act_dtype)
    yc, conv_c_new, lru_new = _lru(big, st["conv_c"], st["lru"], lw, batch, seq, act_dtype)
    x = _merge(x, mod, oa, yb, yc, big, lw)
    return x, (new_kv[0], new_kv[1], new_kv[2], conv_b_new, ssm_new, conv_c_new, lru_new)


def _block(x, mod, lw, st, bias, prompt, batch, seq, g_final):
    x = _ffn(x, mod, 0, lw["g_ff1"], lw["w_ff1_in"], lw["w_ff1_out"])
    x, new_st = _mixer(x, mod, lw, st, bias, prompt, batch, seq)
    x = _ffn(x, mod, 6, lw["g_ff2"], lw["w_ff2_in"], lw["w_ff2_out"], g_final)
    return x, new_st


@jax.jit
def _forward(x_prompt, x_sample, c_prompt, c_sample, cache_win1_kv, cache_win2_kv, cache_win3_kv,
             state_conv_b, state_ssm, state_conv_c, state_lru, rel_bias, w_ada, b_ada, g_ff1,
             w_ff1_in, w_ff1_out, g_mix, w_in, w_a_proj, conv_b_w, conv_b_b, dt_bias, a_log, d_skip,
             g_ssm_norm, w_b_proj, conv_c_w, conv_c_b, w_rgate, b_rgate, w_igate, b_igate, lru_lambda,
             w_c_proj, w_out, g_ff2, w_ff2_in, w_ff2_out, g_final):
    bp, lp, d = x_prompt.shape
    bs, ls, _ = x_sample.shape
    depth = w_ada.shape[0]
    caches = (cache_win1_kv, cache_win2_kv, cache_win3_kv)
    biases = [_group_bias(rel_bias, g) for g in range(len(A_GROUPS))]
    bias_p = [_prompt_bias_table(b, g) for g, b in enumerate(biases)]
    bias_s = [_sample_bias_tables(b, caches[g].shape[2], A_GROUPS[g][1], ls) for g, b in enumerate(biases)]

    yp = x_prompt.reshape(bp * lp, d)
    ys = x_sample.reshape(bs * ls, d)
    c_all = jnp.concatenate([c_prompt, c_sample], axis=0)
    new_p = [[] for _ in range(7)]
    new_s = [[] for _ in range(7)]
    kvt_p = None
    for l in range(depth):
        w_qkv, w_big = _mixer_weights(w_in[l])
        lw = dict(g_ff1=g_ff1[l], w_ff1_in=w_ff1_in[l].astype(BF16), w_ff1_out=w_ff1_out[l].astype(BF16),
                  g_mix=g_mix[l], w_qkv=w_qkv, w_big=w_big, w_a_proj=w_a_proj[l].astype(BF16),
                  conv_b_w=conv_b_w[l], conv_b_b=conv_b_b[l], dt_bias=dt_bias[l], a_log=a_log[l],
                  d_skip=d_skip[l], g_ssm_norm=g_ssm_norm[l], w_b_proj=w_b_proj[l].astype(BF16),
                  conv_c_w=conv_c_w[l], conv_c_b=conv_c_b[l], w_rgate=w_rgate[l], b_rgate=b_rgate[l],
                  w_igate=w_igate[l], b_igate=b_igate[l], lru_lambda=lru_lambda[l],
                  w_c_proj=w_c_proj[l].astype(BF16), w_out=w_out[l].astype(BF16),
                  g_ff2=g_ff2[l], w_ff2_in=w_ff2_in[l].astype(BF16), w_ff2_out=w_ff2_out[l].astype(BF16))
        mod_all = _ada(c_all, w_ada, b_ada, l)
        mod_p = _Mod(mod_all[:bp], bp, lp, per_row=False)
        mod_s = _Mod(mod_all[bp:], bs, ls, per_row=True)
        gf = g_final if l == depth - 1 else None
        st_p = dict(kvt=kvt_p, layer=l, depth=depth,
                    conv_b=jnp.zeros((bp, B_CONV - 1, conv_b_w.shape[2]), F32),
                    ssm=jnp.zeros((bp,) + state_ssm.shape[2:], F32),
                    conv_c=jnp.zeros((bp, C_CONV - 1, conv_c_w.shape[2]), F32),
                    lru=jnp.zeros((bp, state_lru.shape[2]), F32))
        st_s = dict(kv=caches, layer=l, conv_b=state_conv_b[l], ssm=state_ssm[l],
                    conv_c=state_conv_c[l], lru=state_lru[l])
        yp, stp = _block(yp, mod_p, lw, st_p, bias_p, True, bp, lp, gf)
        ys, sts = _block(ys, mod_s, lw, st_s, bias_s, False, bs, ls, gf)
        kvt_p = list(stp[:3])
        for i in range(7):
            new_p[i].append(stp[i])
            new_s[i].append(sts[i])
    outs_p = [jnp.transpose(v, (0, 1, 5, 2, 3, 4)) for v in kvt_p] + [jnp.stack(v, 0) for v in new_p[3:]]
    outs_s = [jnp.stack(v, 0) for v in new_s]
    return (yp.reshape(bp, lp, d), ys.reshape(bs, ls, d), *outs_p, *outs_s)


def kernel(x_prompt, x_sample, c_prompt, c_sample, cache_win1_kv, cache_win2_kv, cache_win3_kv,
           state_conv_b, state_ssm, state_conv_c, state_lru, rel_bias, w_ada, b_ada, g_ff1,
           w_ff1_in, w_ff1_out, g_mix, w_in, w_a_proj, conv_b_w, conv_b_b, dt_bias, a_log, d_skip,
           g_ssm_norm, w_b_proj, conv_c_w, conv_c_b, w_rgate, b_rgate, w_igate, b_igate, lru_lambda,
           w_c_proj, w_out, g_ff2, w_ff2_in, w_ff2_out, g_final):
    return _forward(x_prompt, x_sample, c_prompt, c_sample, cache_win1_kv, cache_win2_kv,
                    cache_win3_kv, state_conv_b, state_ssm, state_conv_c, state_lru, rel_bias, w_ada,
                    b_ada, g_ff1, w_ff1_in, w_ff1_out, g_mix, w_in, w_a_proj, conv_b_w, conv_b_b,
                    dt_bias, a_log, d_skip, g_ssm_norm, w_b_proj, conv_c_w, conv_c_b, w_rgate,
                    b_rgate, w_igate, b_igate, lru_lambda, w_c_proj, w_out, g_ff2, w_ff2_in,
                    w_ff2_out, g_final)
```
